```python
import math
import jax, jax.numpy as jnp
from jax import lax
import numpy as np

D_MODEL = 1024
BATCH = 8
SEQ = 4096
DEPTH = 4

HG_HEADS = 4
HG_DK = 128
HG_DV = 128
HG_W = HG_HEADS * HG_DV
HG_CHUNK = 64
HG_F_MIN = 1e-6
DA_HEADS = 4
DA_DQK = 64
DA_DV = 2 * DA_DQK
DA_W = DA_HEADS * DA_DV
DA_QBLOCK = 128
MASK_VALUE = -1e30
RW_HEADS = 8
RW_DH = 64
RW_W = RW_HEADS * RW_DH
RW_LORA_W = 64
RW_LORA_A = 64
RW_LORA_G = 128
RW_IN_SIZES = (RW_W, RW_W, RW_W, RW_LORA_W, RW_LORA_A, RW_LORA_G)
RW_IN_W = 1792
RW_GN_EPS = 64e-5
IN_SIZES = (HG_HEADS * HG_DK, HG_HEADS * HG_DK, HG_W, HG_W,
            DA_HEADS * 2 * DA_DQK, DA_HEADS * 2 * DA_DQK, DA_W, RW_IN_W)
IN_W = 5376
N_BRANCH = 3
MIX_W = HG_W + DA_W + RW_W
N_EXPERTS = 16
N_GROUPS = 4
EXPERTS_PER_GROUP = 4
TOP_K = 2
D_EXPERT = 512
ALPHA = (2.0 * DEPTH) ** 0.25
BETA = (8.0 * DEPTH) ** -0.25
LN_EPS = 1e-5
RMS_EPS = 1e-6

kernel_name = "hybrid_hgrn2_diffattn_rwkv7_groupmoe_deepnorm"


def _split_points(sizes):
    return [int(s) for s in np.cumsum(np.asarray(sizes))[:-1]]


def layer_norm(x, g, b):
    xf = x.astype(jnp.float32)
    mu = jnp.mean(xf, axis=-1, keepdims=True)
    var = jnp.mean(jnp.square(xf - mu), axis=-1, keepdims=True)
    return ((xf - mu) * lax.rsqrt(var + LN_EPS) * g.astype(jnp.float32) + b.astype(jnp.float32)).astype(x.dtype)


def rms_norm(x, g):
    xf = x.astype(jnp.float32)
    return xf * lax.rsqrt(jnp.mean(jnp.square(xf), axis=-1, keepdims=True) + RMS_EPS) * g.astype(jnp.float32)


def _token_shift(a):
    return jnp.pad(a, ((0, 0), (1, 0), (0, 0)))[:, :-1]


def _alibi_slopes(n):
    return jnp.asarray([2.0 ** (-8.0 * (h + 1) / n) for h in range(n)], jnp.float32)


def hgrn2_mixer(q, f_pre, i, og, lb, norm_g):
    B, T, _ = q.shape
    n = T // HG_CHUNK
    z = f_pre.astype(jnp.float32)
    lb = lb.astype(jnp.float32)
    f = lb + (1.0 - lb) * jax.nn.sigmoid(z)
    log_f = jnp.log(jnp.maximum(f, HG_F_MIN))
    k = (1.0 - lb) * jax.nn.sigmoid(-z)

    def to_chunks(a, d):
        return a.reshape(B, n, HG_CHUNK, HG_HEADS, d).transpose(1, 0, 3, 2, 4)

    qc = to_chunks(q.astype(jnp.float32), HG_DK)
    kc = to_chunks(k, HG_DK)
    vc = to_chunks(i.astype(jnp.float32), HG_DV)
    gc = to_chunks(log_f, HG_DK)
    causal = jnp.tril(jnp.ones((HG_CHUNK, HG_CHUNK), bool))[:, :, None]

    def step(S, inp):
        qb, kb, vb, gb = inp
        b = jnp.cumsum(gb, axis=2)
        diff = b[:, :, :, None, :] - b[:, :, None, :, :]
        decay = jnp.where(causal, jnp.exp(jnp.minimum(diff, 0.0)), 0.0)
        a = jnp.einsum('bhtk,bhtsk,bhsk->bhts', qb, decay, kb)
        o = jnp.einsum('bhts,bhsv->bhtv', a, vb) + jnp.einsum('bhtk,bhkv->bhtv', qb * jnp.exp(b), S)
        b_last = b[:, :, -1:, :]
        S = jnp.exp(b_last[:, :, 0, :])[..., None] * S + jnp.einsum('bhsk,bhsv->bhkv', kb * jnp.exp(b_last - b), vb)
        return S, o

    S0 = jnp.zeros((B, HG_HEADS, HG_DK, HG_DV), jnp.float32)
    _, o = lax.scan(step, S0, (qc, kc, vc, gc))
    o = o.transpose(1, 0, 3, 2, 4).reshape(B, T, HG_HEADS, HG_DV)
    o = rms_norm(o, norm_g) * jax.nn.silu(og.reshape(B, T, HG_HEADS, HG_DV).astype(jnp.float32))
    return o.reshape(B, T, HG_W).astype(q.dtype)


def diff_attention(q, k, v, lam, lam_init, subln_g):
    B, T, _ = q.shape
    q = q.reshape(B, T, DA_HEADS, 2, DA_DQK).transpose(0, 3, 2, 1, 4) * (DA_DQK ** -0.5)
    k = k.reshape(B, T, DA_HEADS, 2, DA_DQK).transpose(0, 3, 2, 1, 4)
    v = v.reshape(B, T, DA_HEADS, DA_DV).transpose(0, 2, 1, 3)
    slopes = _alibi_slopes(DA_HEADS)
    outs = []
    for blk in range(T // DA_QBLOCK):
        q0, q1 = blk * DA_QBLOCK, (blk + 1) * DA_QBLOCK
        s = jnp.einsum('bmhqd,bmhkd->bmhqk', q[:, :, :, q0:q1], k[:, :, :, :q1]).astype(jnp.float32)
        dist = (jnp.arange(q0, q1)[:, None] - jnp.arange(q1)[None, :]).astype(jnp.float32)
        s = jnp.where(dist >= 0, s - slopes[:, None, None] * dist, MASK_VALUE)
        p = jax.nn.softmax(s, axis=-1)
        p = p[:, 0] - lam * p[:, 1]
        outs.append(jnp.einsum('bhqk,bhkd->bhqd', p.astype(v.dtype), v[:, :, :q1]))
    o = jnp.concatenate(outs, axis=2)
    o = rms_norm(o, subln_g) * (1.0 - lam_init)
    return o.transpose(0, 2, 1, 3).reshape(B, T, DA_W).astype(v.dtype)


def rwkv7_mixer(xs, mu, w0, w_up, a0, a_up, g_up, k_k, k_a, r_k, gn_g, gn_b):
    B, T, _ = xs.shape
    mu, w0, w_up, a0, a_up, g_up, k_k, k_a, r_k, gn_g, gn_b = [
        p.astype(jnp.float32) for p in (mu, w0, w_up, a0, a_up, g_up, k_k, k_a, r_k, gn_g, gn_b)]
    xs = xs.astype(jnp.float32)
    xs = xs + (_token_shift(xs) - xs) * mu
    r, k, v, wd, ad, gd = jnp.split(xs, _split_points(RW_IN_SIZES), axis=-1)
    w_log = -jax.nn.softplus(-(w0 + jnp.tanh(wd) @ w_up)) - 0.5
    decay = jnp.exp(-jnp.exp(w_log))
    a = jax.nn.sigmoid(a0 + ad @ a_up)
    g = jax.nn.sigmoid(gd) @ g_up

    def hd(t):
        return t.reshape(B, T, RW_HEADS, RW_DH)

    kk = hd(k * k_k)
    kk = kk * lax.rsqrt(jnp.maximum(jnp.sum(jnp.square(kk), axis=-1, keepdims=True), 1e-12))
    k = k * (1.0 + (a - 1.0) * k_a)
    r, decay, k, v, a = hd(r), hd(decay), hd(k), hd(v), hd(a)

    def step(S, inp):
        r_t, w_t, k_t, v_t, a_t, b_t = inp
        S = (S * w_t[:, :, None, :]
             + jnp.einsum('bhvk,bhk->bhv', S, a_t)[..., None] * b_t[:, :, None, :]
             + v_t[..., None] * k_t[:, :, None, :])
        return S, jnp.einsum('bhvk,bhk->bhv', S, r_t)

    def tm(t):
        return jnp.swapaxes(t, 0, 1)

    S0 = jnp.zeros((B, RW_HEADS, RW_DH, RW_DH), jnp.float32)
    _, o = lax.scan(step, S0, (tm(r), tm(decay), tm(k), tm(v), tm(-kk), tm(kk * a)))
    o = tm(o)
    m = jnp.mean(o, axis=-1, keepdims=True)
    var = jnp.mean(jnp.square(o - m), axis=-1, keepdims=True)
    o = (o - m) * lax.rsqrt(var + RW_GN_EPS) * gn_g.reshape(RW_HEADS, RW_DH) + gn_b.reshape(RW_HEADS, RW_DH)
    o = o + jnp.sum(r * k * r_k, axis=-1, keepdims=True) * v
    return o.reshape(B, T, RW_W) * g


def hybrid_mixer(u, w_in, hg_lb, hg_norm_g, lam, lam_init, da_subln_g,
                 rw_mu, rw_w0, rw_w_up, rw_a0, rw_a_up, rw_g_up, rw_k_k, rw_k_a, rw_r_k, rw_gn_g, rw_gn_b,
                 w_merge, b_merge, w_branch, w_out):
    hg_q, hg_f, hg_i, hg_og, da_q, da_k, da_v, rw_x = jnp.split(
        jnp.einsum('btd,de->bte', u, w_in), _split_points(IN_SIZES), axis=-1)
    o_hg = hgrn2_mixer(hg_q, hg_f, hg_i, hg_og, hg_lb, hg_norm_g)
    o_da = diff_attention(da_q, da_k, da_v, lam, lam_init, da_subln_g)
    o_rw = rwkv7_mixer(rw_x, rw_mu, rw_w0, rw_w_up, rw_a0, rw_a_up, rw_g_up,
                       rw_k_k, rw_k_a, rw_r_k, rw_gn_g, rw_gn_b)
    wb_hg, wb_da, wb_rw = jnp.split(w_branch, _split_points((HG_W, DA_W, RW_W)), axis=0)
    merged = jnp.zeros_like(u)
    for br, (o, wb) in enumerate(((o_hg, wb_hg), (o_da, wb_da), (o_rw, wb_rw))):
        gate = jax.nn.sigmoid(u @ w_merge[br] + b_merge[br])
        merged = merged + gate * (o.astype(u.dtype) @ wb)
    return merged @ w_out


def grouped_moe(u, w_router, router_bias, w_gate, w_up, w_down):
    B, T, D = u.shape
    t = u.reshape(B * T, D)
    scores = jax.nn.softmax((t @ w_router).astype(jnp.float32), axis=-1)
    sel = scores + router_bias.astype(jnp.float32)
    grp_score = jnp.sum(lax.top_k(sel.reshape(-1, N_GROUPS, EXPERTS_PER_GROUP), TOP_K)[0], axis=-1)
    gmask = jax.nn.one_hot(jnp.argmax(grp_score, axis=-1), N_GROUPS, dtype=jnp.bool_)
    emask = jnp.repeat(gmask, EXPERTS_PER_GROUP, axis=-1)
    _, idx = lax.top_k(jnp.where(emask, sel, MASK_VALUE), TOP_K)
    w = jnp.take_along_axis(scores, idx, axis=-1)
    w = w / jnp.sum(w, axis=-1, keepdims=True)
    combine = jnp.sum(jax.nn.one_hot(idx, N_EXPERTS, dtype=jnp.float32) * w[..., None], axis=1).astype(t.dtype)
    y = jnp.zeros_like(t)
    for e in range(N_EXPERTS):
        h = jax.nn.silu(t @ w_gate[e]) * (t @ w_up[e])
        y = y + combine[:, e:e + 1] * (h @ w_down[e])
    return y.reshape(B, T, D)


def setup_inputs(seed: int = 0) -> dict:
    key = jax.random.key(seed)
    ks = iter(jax.random.split(key, 40))
    L, D = DEPTH, D_MODEL

    def nrm(shape, std):
        return std * jax.random.normal(next(ks), shape, jnp.float32)

    return {
        "x": nrm((BATCH, SEQ, D), 1.0),
        "c": nrm((BATCH, D), 1.0),
        "w_ada": nrm((L, D, 6 * D), 0.5 * D ** -0.5),
        "b_ada": nrm((L, 6 * D), 0.02),
        "w_in": nrm((L, D, IN_W), D ** -0.5),
        "hg_lb_logits": nrm((L, HG_HEADS * HG_DK), 1.0),
        "hg_norm_g": 1.0 + nrm((L, HG_DV), 0.02),
        "da_lambda": nrm((L, 4, DA_DQK), 0.1),
        "da_subln_g": 1.0 + nrm((L, DA_DV), 0.02),
        "rw_mu": jax.random.uniform(next(ks), (L, RW_IN_W), jnp.float32),
        "rw_w0": -1.0 + nrm((L, RW_W), 0.5),
        "rw_w_up": nrm((L, RW_LORA_W, RW_W), 0.1),
        "rw_a0": nrm((L, RW_W), 0.1),
        "rw_a_up": nrm((L, RW_LORA_A, RW_W), RW_LORA_A ** -0.5),
        "rw_g_up": nrm((L, RW_LORA_G, RW_W), RW_LORA_G ** -0.5),
        "rw_k_k": 0.85 + nrm((L, RW_W), 0.02),
        "rw_k_a": 1.0 + nrm((L, RW_W), 0.02),
        "rw_r_k": nrm((L, RW_HEADS, RW_DH), 0.1),
        "rw_gn_g": 1.0 + nrm((L, RW_W), 0.02),
        "rw_gn_b": nrm((L, RW_W), 0.02),
        "w_merge": nrm((L, N_BRANCH, D, D), D ** -0.5),
        "b_merge": nrm((L, N_BRANCH, D), 0.02),
        "w_branch": nrm((L, MIX_W, D), BETA * HG_W ** -0.5),
        "w_out": nrm((L, D, D), BETA * D ** -0.5),
        "ln_g": 1.0 + nrm((L, 2, D), 0.02),
        "ln_b": nrm((L, 2, D), 0.02),
        "w_router": nrm((D, N_EXPERTS), D ** -0.5),
        "router_bias": nrm((N_EXPERTS,), 0.01),
        "w_exp_gate": nrm((L, N_EXPERTS, D, D_EXPERT), D ** -0.5),
        "w_exp_up": nrm((L, N_EXPERTS, D, D_EXPERT), D ** -0.5),
        "w_exp_down": nrm((L, N_EXPERTS, D_EXPERT, D), BETA * D_EXPERT ** -0.5),
    }


def reference(x, c, w_ada, b_ada, w_in, hg_lb_logits, hg_norm_g, da_lambda, da_subln_g,
              rw_mu, rw_w0, rw_w_up, rw_a0, rw_a_up, rw_g_up, rw_k_k, rw_k_a, rw_r_k, rw_gn_g, rw_gn_b,
              w_merge, b_merge, w_branch, w_out, ln_g, ln_b, w_router, router_bias,
              w_exp_gate, w_exp_up, w_exp_down):
    sm = jax.nn.softmax(hg_lb_logits.astype(jnp.float32), axis=0)
    hg_lb = jnp.cumsum(sm, axis=0) - sm[0:1]
    cond = jax.nn.silu(c)
    for l in range(DEPTH):
        mod = (cond @ w_ada[l] + b_ada[l])[:, None, :]
        sh1, sc1, g1, sh2, sc2, g2 = jnp.split(mod, 6, axis=-1)
        lq1, lk1, lq2, lk2 = da_lambda[l].astype(jnp.float32)
        lam_init = 0.8 - 0.6 * math.exp(-0.3 * l)
        lam = jnp.exp(jnp.sum(lq1 * lk1)) - jnp.exp(jnp.sum(lq2 * lk2)) + lam_init
        u = x * (1.0 + sc1) + sh1
        mix = hybrid_mixer(u, w_in[l], hg_lb[l], hg_norm_g[l], lam, lam_init, da_subln_g[l],
                           rw_mu[l], rw_w0[l], rw_w_up[l], rw_a0[l], rw_a_up[l], rw_g_up[l],
                           rw_k_k[l], rw_k_a[l], rw_r_k[l], rw_gn_g[l], rw_gn_b[l],
                           w_merge[l], b_merge[l], w_branch[l], w_out[l])
        x = layer_norm(ALPHA * x + (1.0 + g1) * mix, ln_g[l, 0], ln_b[l, 0])
        u = x * (1.0 + sc2) + sh2
        ffn = grouped_moe(u, w_router, router_bias, w_exp_gate[l], w_exp_up[l], w_exp_down[l])
        x = layer_norm(ALPHA * x + (1.0 + g2) * ffn, ln_g[l, 1], ln_b[l, 1])
    return x
```

```python
import functools
import math

import jax
import jax.numpy as jnp
from jax import lax
from jax.experimental import pallas as pl
from jax.experimental.pallas import tpu as pltpu

D_MODEL = 1024
DEPTH = 4
HG_HEADS, HG_DK, HG_DV, HG_CHUNK, HG_SUB = 4, 128, 128, 64, 16
HG_W = HG_HEADS * HG_DV
HG_F_MIN = 1e-6
DA_HEADS, DA_DQK = 4, 64
DA_DV = 2 * DA_DQK
DA_W = DA_HEADS * DA_DV
MASK_VALUE = -1e30
RW_HEADS, RW_DH, RW_CHUNK, RW_SUB = 8, 64, 64, 16
RW_W = RW_HEADS * RW_DH
RW_IN_W = 1792
RW_GN_EPS = 64e-5
IN_W = 5376
HG_COL, DA_COL, RW_COL = 0, 2048, 3584
N_EXPERTS, N_GROUPS, EXPERTS_PER_GROUP, D_EXPERT = 16, 4, 4, 512
ALPHA = (2.0 * DEPTH) ** 0.25
LN_EPS = 1e-5
RMS_EPS = 1e-6
LANES = 128

F32 = jnp.float32
BF16 = jnp.bfloat16
HIGHEST = lax.Precision.HIGHEST
VMEM_LIMIT = 48 * 1024 * 1024

_NT = (((1,), (1,)), ((), ()))
_TN = (((0,), (0,)), ((), ()))


def _mm(a, b):
    return jnp.dot(a.astype(BF16), b.astype(BF16), preferred_element_type=F32)


def _mm_nt(a, b):
    return lax.dot_general(a.astype(BF16), b.astype(BF16), _NT, preferred_element_type=F32)


def _mm_tn(a, b):
    return lax.dot_general(a.astype(BF16), b.astype(BF16), _TN, preferred_element_type=F32)


def _mmh(a, b):
    return jnp.dot(a, b, precision=HIGHEST, preferred_element_type=F32)


def _mmh_nt(a, b):
    return lax.dot_general(a, b, _NT, precision=HIGHEST, preferred_element_type=F32)


def _split_dot(x, w_bf16):
    hi = x.astype(BF16)
    lo = (x - hi.astype(F32)).astype(BF16)
    return (jnp.dot(hi, w_bf16, preferred_element_type=F32)
            + jnp.dot(lo, w_bf16, preferred_element_type=F32))


def _sigmoid(x):
    return 1.0 / (1.0 + jnp.exp(-x))


def _softplus(x):
    return jnp.maximum(x, 0.0) + jnp.log(1.0 + jnp.exp(-jnp.abs(x)))


def _iota(shape, dim):
    return lax.broadcasted_iota(jnp.int32, shape, dim)


def _params(*sem):
    return pltpu.CompilerParams(dimension_semantics=sem, vmem_limit_bytes=VMEM_LIMIT)


def _layer_norm(y, g, b):
    mu = jnp.mean(y, axis=-1, keepdims=True)
    d = y - mu
    var = jnp.mean(d * d, axis=-1, keepdims=True)
    return d * lax.rsqrt(var + LN_EPS) * g + b


def _ada_body(c_ref, w_ref, b_ref, o_ref):
    c = c_ref[...]
    o_ref[0] = _mmh(c * _sigmoid(c), w_ref[0]) + b_ref[0]


def _ada(c, w_ada, b_ada):
    depth, d, _ = w_ada.shape
    bsz = c.shape[0]
    return pl.pallas_call(
        _ada_body,
        grid=(depth, 6),
        in_specs=[pl.BlockSpec((bsz, d), lambda l, j: (0, 0)),
                  pl.BlockSpec((1, d, d), lambda l, j: (l, 0, j)),
                  pl.BlockSpec((1, 1, d), lambda l, j: (l, 0, j))],
        out_specs=pl.BlockSpec((1, bsz, d), lambda l, j: (l, 0, j)),
        out_shape=jax.ShapeDtypeStruct((depth, bsz, 6 * d), F32),
        compiler_params=_params("arbitrary", "arbitrary"),
        name="ada",
    )(c, w_ada, b_ada.reshape(depth, 1, 6 * d))


def _modmm_body(x_ref, mod_ref, w_ref, *rest, gate):
    if gate:
        b_ref, o_ref, u_ref = rest
    else:
        o_ref, u_ref = rest

    @pl.when(pl.program_id(1) == 0)
    def _():
        sh = mod_ref[0, 0:1, :]
        sc = mod_ref[0, 1:2, :]
        u_ref[...] = (x_ref[...] * (1.0 + sc) + sh).astype(BF16)

    y = jnp.dot(u_ref[...], w_ref[...], preferred_element_type=F32)
    if gate:
        y = _sigmoid(y + b_ref[...])
    o_ref[...] = y.astype(o_ref.dtype)


def _modmm(x, mod, w, bias, seq, tm, tn):
    n, d = x.shape
    width = w.shape[1]
    per_seq = seq // tm
    in_specs = [pl.BlockSpec((tm, d), lambda i, j: (i, 0)),
                pl.BlockSpec((1, 6, d), lambda i, j: (i // per_seq, 0, 0)),
                pl.BlockSpec((d, tn), lambda i, j: (0, j))]
    args = [x, mod, w]
    if bias is not None:
        in_specs.append(pl.BlockSpec((1, tn), lambda i, j: (0, j)))
        args.append(bias)
    return pl.pallas_call(
        functools.partial(_modmm_body, gate=bias is not None),
        grid=(n // tm, width // tn),
        in_specs=in_specs,
        out_specs=pl.BlockSpec((tm, tn), lambda i, j: (i, j)),
        out_shape=jax.ShapeDtypeStruct((n, width), BF16),
        scratch_shapes=[pltpu.VMEM((tm, d), BF16)],
        compiler_params=_params("arbitrary", "arbitrary"),
        name="gates" if bias is not None else "inproj",
    )(*args)


def _hgrn2_body(y_ref, lb_ref, ng_ref, o_ref, st_ref):
    c, sub = HG_CHUNK, HG_SUB

    @pl.when(pl.program_id(1) == 0)
    def _():
        st_ref[...] = jnp.zeros_like(st_ref)

    row = _iota((c, c), 0)
    col = _iota((c, c), 1)
    tril = (row >= col).astype(F32)
    ones = jnp.ones((2 * HG_DK, LANES), BF16)
    row_s = _iota((sub, c), 0)
    col_s = _iota((sub, c), 1)

    for h in range(HG_HEADS):
        q = y_ref[:, h * HG_DK:(h + 1) * HG_DK].astype(F32)
        z = y_ref[:, HG_W + h * HG_DK:HG_W + (h + 1) * HG_DK].astype(F32)
        v = y_ref[:, 2 * HG_W + h * HG_DV:2 * HG_W + (h + 1) * HG_DV]
        og = y_ref[:, 3 * HG_W + h * HG_DV:3 * HG_W + (h + 1) * HG_DV].astype(F32)
        lb = lb_ref[:, h * HG_DK:(h + 1) * HG_DK]
        f = lb + (1.0 - lb) * _sigmoid(z)
        logf = jnp.log(jnp.maximum(f, HG_F_MIN))
        kin = (1.0 - lb) * _sigmoid(-z)
        b = _mmh(tril, logf)

        a_rows = []
        for blk in range(c // sub):
            r0 = blk * sub
            b_i = b[r0:r0 + sub]
            q_i = q[r0:r0 + sub]
            k_i = kin[r0:r0 + sub]
            terms = []
            for s in range(sub):
                e = jnp.exp(jnp.minimum(b_i - b_i[s:s + 1], 0.0))
                terms.append(q_i * e * k_i[s:s + 1])
            w = jnp.concatenate(terms, axis=0)
            w_hi = w.astype(BF16)
            w_lo = (w - w_hi.astype(F32)).astype(BF16)
            rs = jnp.dot(jnp.concatenate([w_hi, w_lo], axis=1), ones,
                         preferred_element_type=F32)
            a_blk = jnp.zeros((sub, c), F32)
            for s in range(sub):
                a_blk = jnp.where(col_s == r0 + s, rs[s * sub:(s + 1) * sub, :c], a_blk)
            if blk > 0:
                beta = b[r0 - 1:r0]
                q_t = q_i * jnp.exp(b_i - beta)
                k_h = kin * jnp.exp(jnp.minimum(beta - b, 0.0))
                a_blk = jnp.where(col_s < r0, _mm_nt(q_t, k_h), a_blk)
            a_rows.append(jnp.where(col_s <= row_s + r0, a_blk, 0.0))
        a = jnp.concatenate(a_rows, axis=0)

        st = st_ref[h]
        o = _mm(a, v) + _mm_nt(q * jnp.exp(b), st)
        b_last = b[c - 1:c]
        st_ref[h] = st * jnp.exp(b_last) + _mm_tn(v, kin * jnp.exp(b_last - b))

        o = o * lax.rsqrt(jnp.mean(o * o, axis=-1, keepdims=True) + RMS_EPS) * ng_ref[...]
        o = o * (og * _sigmoid(og))
        o_ref[:, h * HG_DV:(h + 1) * HG_DV] = o.astype(o_ref.dtype)


def _hgrn2(y, lb, norm_g, bsz, seq):
    n = y.shape[0]
    nc = seq // HG_CHUNK
    width = 4 * HG_W
    return pl.pallas_call(
        _hgrn2_body,
        grid=(bsz, nc),
        in_specs=[pl.BlockSpec((HG_CHUNK, width), lambda b, c: (b * nc + c, HG_COL // width)),
                  pl.BlockSpec((1, HG_W), lambda b, c: (0, 0)),
                  pl.BlockSpec((1, HG_DV), lambda b, c: (0, 0))],
        out_specs=pl.BlockSpec((HG_CHUNK, HG_W), lambda b, c: (b * nc + c, 0)),
        out_shape=jax.ShapeDtypeStruct((n, HG_W), BF16),
        scratch_shapes=[pltpu.VMEM((HG_HEADS, HG_DV, HG_DK), F32)],
        compiler_params=_params("arbitrary", "arbitrary"),
        name="hgrn2",
    )(y, lb, norm_g)


def _diffattn_body(scal_ref, q_ref, k_ref, v_ref, g_ref, o_ref, *, blk):
    h = pl.program_id(1)
    i = pl.program_id(2)
    lam = scal_ref[0]
    out_scale = scal_ref[1]
    slope = scal_ref[2 + h]
    q0 = i * blk

    q = q_ref[...].astype(F32) * (DA_DQK ** -0.5)
    lane = _iota(q.shape, 1)
    qq = jnp.concatenate([jnp.where(lane < DA_DQK, q, 0.0), jnp.where(lane >= DA_DQK, q, 0.0)],
                         axis=0).astype(BF16)
    kpos0 = _iota((1, blk), 1)
    qpos = q0 + (_iota((2 * blk, 1), 0) & (blk - 1))

    def step(j, carry, masked):
        m, l, acc = carry
        k = k_ref[pl.ds(j * blk, blk), :]
        v = v_ref[pl.ds(j * blk, blk), :]
        kpos = kpos0 + j * blk
        s = lax.dot_general(qq, k, _NT, preferred_element_type=F32)
        s = s + slope * (kpos - q0).astype(F32)
        if masked:
            s = jnp.where(kpos <= qpos, s, MASK_VALUE)
        m_new = jnp.maximum(m, jnp.max(s, axis=-1, keepdims=True))
        p = jnp.exp(s - m_new)
        scale = jnp.exp(m - m_new)
        l = scale * l + jnp.sum(p, axis=-1, keepdims=True)
        acc = scale * acc + jnp.dot(p.astype(BF16), v, preferred_element_type=F32)
        return m_new, l, acc

    init = (jnp.full((2 * blk, 1), MASK_VALUE, F32), jnp.zeros((2 * blk, 1), F32),
            jnp.zeros((2 * blk, DA_DV), F32))
    carry = lax.fori_loop(0, i, functools.partial(step, masked=False), init)
    _, l, acc = step(i, carry, True)
    o = acc / l
    d = o[:blk] - lam * o[blk:]
    d = d * lax.rsqrt(jnp.mean(d * d, axis=-1, keepdims=True) + RMS_EPS) * g_ref[...] * out_scale
    o_ref[...] = d.astype(o_ref.dtype)


def _diffattn(y, scal, subln_g, bsz, seq, blk):
    n = y.shape[0]
    nq = seq // blk
    qc, kc, vc = (DA_COL // DA_DV, (DA_COL + DA_W) // DA_DV, (DA_COL + 2 * DA_W) // DA_DV)
    return pl.pallas_call(
        functools.partial(_diffattn_body, blk=blk),
        grid=(bsz, DA_HEADS, nq),
        in_specs=[pl.BlockSpec(memory_space=pltpu.SMEM),
                  pl.BlockSpec((blk, DA_DV), lambda b, h, i: (b * nq + i, qc + h)),
                  pl.BlockSpec((seq, DA_DV), lambda b, h, i: (b, kc + h)),
                  pl.BlockSpec((seq, DA_DV), lambda b, h, i: (b, vc + h)),
                  pl.BlockSpec((1, DA_DV), lambda b, h, i: (0, 0))],
        out_specs=pl.BlockSpec((blk, DA_DV), lambda b, h, i: (b * nq + i, h)),
        out_shape=jax.ShapeDtypeStruct((n, DA_W), BF16),
        compiler_params=_params("arbitrary", "arbitrary", "arbitrary"),
        name="diffattn",
    )(scal, y, y, y, subln_g)


def _rwkv_body(y_ref, mu_ref, w0_ref, wup_ref, a0_ref, aup_ref, gup_ref, kk_ref, ka_ref, rk_ref,
               gng_ref, gnb_ref, seg_ref, o_ref, st_ref, prev_ref, osc_ref):
    c, sub, dh = RW_CHUNK, RW_SUB, RW_DH

    @pl.when(pl.program_id(1) == 0)
    def _():
        st_ref[...] = jnp.zeros_like(st_ref)
        prev_ref[...] = jnp.zeros_like(prev_ref)

    x = y_ref[...].astype(F32)
    x_prev = pltpu.roll(x, 1, axis=0)
    x_prev = jnp.where(_iota(x.shape, 0) == 0, prev_ref[...], x_prev)
    prev_ref[...] = x[c - 1:c]
    xs = x + (x_prev - x) * mu_ref[...]
    r = xs[:, 0:RW_W]
    k = xs[:, RW_W:2 * RW_W]
    v = xs[:, 2 * RW_W:3 * RW_W]
    wd = xs[:, 3 * RW_W:3 * RW_W + 64]
    ad = xs[:, 3 * RW_W + 64:3 * RW_W + 128]
    gd = xs[:, 3 * RW_W + 128:RW_IN_W]

    w_log = -_softplus(-(w0_ref[...] + _mmh(jnp.tanh(wd), wup_ref[...]))) - 0.5
    g = -jnp.exp(w_log)
    a = _sigmoid(a0_ref[...] + _mmh(ad, aup_ref[...]))
    gate = _mmh(_sigmoid(gd), gup_ref[...])
    seg = seg_ref[...]
    kk = k * kk_ref[...]
    kk = kk * lax.rsqrt(jnp.maximum(_split_dot(kk * kk, seg), 1e-12))
    k2 = k * (1.0 + (a - 1.0) * ka_ref[...])
    bb = kk * a
    bonus = _split_dot(r * k2 * rk_ref[...], seg) * v

    row = _iota((c, c), 0)
    col = _iota((c, c), 1)
    gc = _mmh((row >= col).astype(F32), g)
    g_last = gc[c - 1:c]
    e_inv = jnp.exp(-gc)
    e_tail = jnp.exp(g_last - gc)
    gam = jnp.exp(g_last)
    r_t = r * jnp.exp(gc)
    a_t = -kk * jnp.exp(gc - g)
    k_h = k2 * e_inv
    b_h = bb * e_inv
    k_bar = k2 * e_tail
    b_bar = bb * e_tail

    strict = row > col
    incl = row >= col
    same_blk = (row // sub) == (col // sub)
    eye = (row == col).astype(F32)

    for h in range(RW_HEADS):
        sl = slice(h * dh, (h + 1) * dh)
        at_h, rt_h, kh_h, bh_h, v_h = a_t[:, sl], r_t[:, sl], k_h[:, sl], b_h[:, sl], v[:, sl]
        a_ab = jnp.where(strict, _mmh_nt(at_h, bh_h), 0.0)
        a_ak = jnp.where(strict, _mmh_nt(at_h, kh_h), 0.0)
        a_rb = jnp.where(incl, _mmh_nt(rt_h, bh_h), 0.0)
        a_rk = jnp.where(incl, _mmh_nt(rt_h, kh_h), 0.0)

        a_d = jnp.where(same_blk, a_ab, 0.0)
        a_o = a_ab - a_d
        p2 = _mmh(a_d, a_d)
        p4 = _mmh(p2, p2)
        p8 = _mmh(p4, p4)
        t_d = eye + a_d
        t_d = t_d + _mmh(t_d, p2)
        t_d = t_d + _mmh(t_d, p4)
        t_d = t_d + _mmh(t_d, p8)
        nn = _mmh(t_d, a_o)
        n2 = _mmh(nn, nn)
        n3 = _mmh(nn, n2)
        t_m = _mmh(eye + nn + n2 + n3, t_d)

        s0 = st_ref[h]
        u = _mmh_nt(_mmh(t_m, at_h), s0) + _mmh(t_m, _mmh(a_ak, v_h))
        o = _mmh_nt(rt_h, s0) + _mmh(a_rk, v_h) + _mmh(a_rb, u)
        st_ref[h] = (s0 * gam[:, sl]
                     + lax.dot_general(v_h, k_bar[:, sl], _TN, precision=HIGHEST, preferred_element_type=F32)
                     + lax.dot_general(u, b_bar[:, sl], _TN, precision=HIGHEST, preferred_element_type=F32))
        osc_ref[:, sl] = o

    o = osc_ref[...]
    mean = _split_dot(o, seg) * (1.0 / dh)
    d = o - mean
    var = _split_dot(d * d, seg) * (1.0 / dh)
    o = d * lax.rsqrt(var + RW_GN_EPS) * gng_ref[...] + gnb_ref[...]
    o_ref[...] = ((o + bonus) * gate).astype(o_ref.dtype)


def _rwkv(y, p, bsz, seq):
    n = y.shape[0]
    nc = seq // RW_CHUNK
    head = _iota((RW_W, RW_W), 0) // RW_DH == _iota((RW_W, RW_W), 1) // RW_DH
    seg = head.astype(BF16)
    rows = [p["mu"], p["w0"], p["w_up"], p["a0"], p["a_up"], p["g_up"], p["k_k"], p["k_a"], p["r_k"],
            p["gn_g"], p["gn_b"], seg]
    full = lambda b, c: (0, 0)
    return pl.pallas_call(
        _rwkv_body,
        grid=(bsz, nc),
        in_specs=[pl.BlockSpec((RW_CHUNK, RW_IN_W), lambda b, c: (b * nc + c, RW_COL // RW_IN_W))]
        + [pl.BlockSpec(a.shape, full) for a in rows],
        out_specs=pl.BlockSpec((RW_CHUNK, RW_W), lambda b, c: (b * nc + c, 0)),
        out_shape=jax.ShapeDtypeStruct((n, RW_W), BF16),
        scratch_shapes=[pltpu.VMEM((RW_HEADS, RW_DH, RW_DH), F32),
                        pltpu.VMEM((1, RW_IN_W), F32),
                        pltpu.VMEM((RW_CHUNK, RW_W), F32)],
        compiler_params=_params("arbitrary", "arbitrary"),
        name="rwkv7",
    )(y, *rows)


def _first_argmax(vals, row):
    top = jnp.max(vals, axis=0, keepdims=True)
    idx = jnp.min(jnp.where(vals == top, row, N_EXPERTS), axis=0, keepdims=True)
    return top, idx


def _merge_body(ohg_ref, oda_ref, orw_ref, gt_ref, x_ref, mod_ref, wb_ref, wo_ref, lng_ref, lnb_ref,
                wrt_ref, rb_ref, x1_ref, u2_ref, comb_ref):
    d = D_MODEL
    merged = (gt_ref[:, 0:d].astype(F32) * jnp.dot(ohg_ref[...], wb_ref[0:HG_W, :], preferred_element_type=F32)
              + gt_ref[:, d:2 * d].astype(F32)
              * jnp.dot(oda_ref[...], wb_ref[HG_W:HG_W + DA_W, :], preferred_element_type=F32)
              + gt_ref[:, 2 * d:3 * d].astype(F32)
              * jnp.dot(orw_ref[...], wb_ref[HG_W + DA_W:, :], preferred_element_type=F32))
    mix = _mm(merged, wo_ref[...])
    x1 = _layer_norm(ALPHA * x_ref[...] + (1.0 + mod_ref[0, 2:3, :]) * mix, lng_ref[...], lnb_ref[...])
    x1_ref[...] = x1
    u2 = x1 * (1.0 + mod_ref[0, 4:5, :]) + mod_ref[0, 3:4, :]
    u2_ref[...] = u2.astype(BF16)

    logits = _mmh_nt(wrt_ref[...], u2)
    ex = jnp.exp(logits - jnp.max(logits, axis=0, keepdims=True))
    scores = ex / jnp.sum(ex, axis=0, keepdims=True)
    sel = scores + rb_ref[...]
    row = _iota(sel.shape, 0)
    best = None
    for grp in range(N_GROUPS):
        a, b, c2, d2 = (sel[grp * EXPERTS_PER_GROUP + i:grp * EXPERTS_PER_GROUP + i + 1] for i in range(4))
        hi1, lo1, hi2, lo2 = jnp.maximum(a, b), jnp.minimum(a, b), jnp.maximum(c2, d2), jnp.minimum(c2, d2)
        top2 = jnp.maximum(hi1, hi2) + jnp.maximum(jnp.minimum(hi1, hi2), jnp.maximum(lo1, lo2))
        if best is None:
            best, best_grp = top2, jnp.zeros_like(top2, dtype=jnp.int32)
        else:
            better = top2 > best
            best = jnp.where(better, top2, best)
            best_grp = jnp.where(better, grp, best_grp)
    masked = jnp.where(row // EXPERTS_PER_GROUP == best_grp, sel, MASK_VALUE)
    _, idx1 = _first_argmax(masked, row)
    _, idx2 = _first_argmax(jnp.where(row == idx1, -jnp.inf, masked), row)
    w1 = jnp.sum(jnp.where(row == idx1, scores, 0.0), axis=0, keepdims=True)
    w2 = jnp.sum(jnp.where(row == idx2, scores, 0.0), axis=0, keepdims=True)
    comb = (jnp.where(row == idx1, w1, 0.0) + jnp.where(row == idx2, w2, 0.0)) / (w1 + w2)
    pad = jnp.zeros((LANES - N_EXPERTS, comb.shape[1]), F32)
    comb_ref[...] = jnp.concatenate([comb, pad], axis=0).T


def _merge(o_hg, o_da, o_rw, gates, x, mod, w_branch, w_out, ln_g, ln_b, w_router_t, router_bias, seq, tm):
    n, d = x.shape
    per_seq = seq // tm
    tile = lambda i: (i, 0)
    full = lambda i: (0, 0)
    return pl.pallas_call(
        _merge_body,
        grid=(n // tm,),
        in_specs=[pl.BlockSpec((tm, HG_W), tile), pl.BlockSpec((tm, DA_W), tile), pl.BlockSpec((tm, RW_W), tile),
                  pl.BlockSpec((tm, 3 * d), tile), pl.BlockSpec((tm, d), tile),
                  pl.BlockSpec((1, 6, d), lambda i: (i // per_seq, 0, 0)),
                  pl.BlockSpec(w_branch.shape, full), pl.BlockSpec(w_out.shape, full),
                  pl.BlockSpec((1, d), full), pl.BlockSpec((1, d), full),
                  pl.BlockSpec((N_EXPERTS, d), full), pl.BlockSpec((N_EXPERTS, 1), full)],
        out_specs=[pl.BlockSpec((tm, d), tile), pl.BlockSpec((tm, d), tile), pl.BlockSpec((tm, LANES), tile)],
        out_shape=[jax.ShapeDtypeStruct((n, d), F32), jax.ShapeDtypeStruct((n, d), BF16),
                   jax.ShapeDtypeStruct((n, LANES), F32)],
        compiler_params=_params("arbitrary"),
        name="merge",
    )(o_hg, o_da, o_rw, gates, x, mod, w_branch, w_out, ln_g, ln_b, w_router_t, router_bias)


def _moe_body(u_ref, comb_ref, wgu_ref, wd_ref, x1_ref, mod_ref, lng_ref, lnb_ref, o_ref, acc_ref):
    e = pl.program_id(1)

    @pl.when(e == 0)
    def _():
        acc_ref[...] = jnp.zeros_like(acc_ref)

    comb = comb_ref[...]
    weight = jnp.sum(jnp.where(_iota(comb.shape, 1) == e, comb, 0.0), axis=1, keepdims=True)
    hidden = jnp.dot(u_ref[...], wgu_ref[0], preferred_element_type=F32)
    hg = hidden[:, :D_EXPERT]
    act = hg * _sigmoid(hg) * hidden[:, D_EXPERT:] * weight
    acc_ref[...] += _mm(act, wd_ref[0])

    @pl.when(e == N_EXPERTS - 1)
    def _():
        y = ALPHA * x1_ref[...] + (1.0 + mod_ref[0, 5:6, :]) * acc_ref[...]
        o_ref[...] = _layer_norm(y, lng_ref[...], lnb_ref[...])


def _moe(u2, comb, w_gu, w_down, x1, mod, ln_g, ln_b, seq, tm):
    n, d = x1.shape
    per_seq = seq // tm
    tile = lambda i, e: (i, 0)
    full = lambda i, e: (0, 0)
    return pl.pallas_call(
        _moe_body,
        grid=(n // tm, N_EXPERTS),
        in_specs=[pl.BlockSpec((tm, d), tile), pl.BlockSpec((tm, LANES), tile),
                  pl.BlockSpec((1, d, 2 * D_EXPERT), lambda i, e: (e, 0, 0)),
                  pl.BlockSpec((1, D_EXPERT, d), lambda i, e: (e, 0, 0)),
                  pl.BlockSpec((tm, d), tile),
                  pl.BlockSpec((1, 6, d), lambda i, e: (i // per_seq, 0, 0)),
                  pl.BlockSpec((1, d), full), pl.BlockSpec((1, d), full)],
        out_specs=pl.BlockSpec((tm, d), tile),
        out_shape=jax.ShapeDtypeStruct((n, d), F32),
        scratch_shapes=[pltpu.VMEM((tm, d), F32)],
        compiler_params=_params("arbitrary", "arbitrary"),
        name="moe",
    )(u2, comb, w_gu, w_down, x1, mod, ln_g, ln_b)


def _tiles(seq):
    return min(512, seq), min(256, seq)


def kernel(x, c, w_ada, b_ada, w_in, hg_lb_logits, hg_norm_g, da_lambda, da_subln_g, rw_mu, rw_w0, rw_w_up,
           rw_a0, rw_a_up, rw_g_up, rw_k_k, rw_k_a, rw_r_k, rw_gn_g, rw_gn_b, w_merge, b_merge, w_branch, w_out,
           ln_g, ln_b, w_router, router_bias, w_exp_gate, w_exp_up, w_exp_down):
    bsz, seq, d = x.shape
    depth = w_in.shape[0]
    n = bsz * seq
    tm, blk = _tiles(seq)

    sm = jax.nn.softmax(hg_lb_logits.astype(F32), axis=0)
    hg_lb = jnp.cumsum(sm, axis=0) - sm[0:1]
    slopes = jnp.asarray([2.0 ** (-8.0 * (h + 1) / DA_HEADS) for h in range(DA_HEADS)], F32)

    mod_all = _ada(c, w_ada, b_ada).reshape(depth, bsz, 6, d)
    w_router_t = w_router.T
    router_bias = router_bias.reshape(N_EXPERTS, 1)

    xf = x.reshape(n, d)
    for l in range(depth):
        mod = mod_all[l]
        lq1, lk1, lq2, lk2 = da_lambda[l].astype(F32)
        lam_init = 0.8 - 0.6 * math.exp(-0.3 * l)
        lam = jnp.exp(jnp.sum(lq1 * lk1)) - jnp.exp(jnp.sum(lq2 * lk2)) + lam_init
        scal = jnp.concatenate([jnp.stack([lam, jnp.asarray(1.0 - lam_init, F32)]), slopes])

        y = _modmm(xf, mod, w_in[l].astype(BF16), None, seq, tm, 768)
        w_gates = jnp.concatenate([w_merge[l, br] for br in range(3)], axis=1).astype(BF16)
        gates = _modmm(xf, mod, w_gates, b_merge[l].reshape(1, 3 * d), seq, tm, 768)

        o_hg = _hgrn2(y, hg_lb[l].reshape(1, HG_W), hg_norm_g[l].reshape(1, HG_DV), bsz, seq)
        o_da = _diffattn(y, scal, da_subln_g[l].reshape(1, DA_DV), bsz, seq, blk)
        rw = dict(mu=rw_mu[l].reshape(1, -1), w0=rw_w0[l].reshape(1, -1), w_up=rw_w_up[l],
                  a0=rw_a0[l].reshape(1, -1), a_up=rw_a_up[l], g_up=rw_g_up[l],
                  k_k=rw_k_k[l].reshape(1, -1), k_a=rw_k_a[l].reshape(1, -1), r_k=rw_r_k[l].reshape(1, -1),
                  gn_g=rw_gn_g[l].reshape(1, -1), gn_b=rw_gn_b[l].reshape(1, -1))
        o_rw = _rwkv(y, rw, bsz, seq)

        x1, u2, comb = _merge(o_hg, o_da, o_rw, gates, xf, mod, w_branch[l].astype(BF16), w_out[l].astype(BF16),
                              ln_g[l, 0].reshape(1, d), ln_b[l, 0].reshape(1, d), w_router_t, router_bias, seq, tm)
        w_gu = jnp.concatenate([w_exp_gate[l], w_exp_up[l]], axis=-1).astype(BF16)
        xf = _moe(u2, comb, w_gu, w_exp_down[l].astype(BF16), x1, mod,
                  ln_g[l, 1].reshape(1, d), ln_b[l, 1].reshape(1, d), seq, tm)
    return xf.reshape(bsz, seq, d)
```

```python
import functools
import math

import jax
import jax.numpy as jnp
from jax import lax
from jax.experimental import pallas as pl
from jax.experimental.pallas import tpu as pltpu

D_MODEL = 1024
DEPTH = 4
HG_HEADS, HG_DK, HG_DV, HG_CHUNK, HG_SUB = 4, 128, 128, 64, 16
HG_W = HG_HEADS * HG_DV
HG_F_MIN = 1e-6
DA_HEADS, DA_DQK = 4, 64
DA_DV = 2 * DA_DQK
DA_W = DA_HEADS * DA_DV
DA_ROWS = 128
MASK_VALUE = -1e30
RW_HEADS, RW_DH, RW_CHUNK, RW_SUB = 8, 64, 64, 16
RW_W = RW_HEADS * RW_DH
RW_IN_W = 1792
RW_GN_EPS = 64e-5
IN_W = 5376
HG_COL, DA_COL, RW_COL = 0, 2048, 3584
N_EXPERTS, N_GROUPS, EXPERTS_PER_GROUP, D_EXPERT = 16, 4, 4, 512
ALPHA = (2.0 * DEPTH) ** 0.25
LN_EPS = 1e-5
RMS_EPS = 1e-6
LANES = 128

F32 = jnp.float32
BF16 = jnp.bfloat16
HIGHEST = lax.Precision.HIGHEST
VMEM_LIMIT = 48 * 1024 * 1024

_NT = (((1,), (1,)), ((), ()))
_TN = (((0,), (0,)), ((), ()))


def _mm(a, b):
    return jnp.dot(a.astype(BF16), b.astype(BF16), preferred_element_type=F32)


def _mm_nt(a, b):
    return lax.dot_general(a.astype(BF16), b.astype(BF16), _NT, preferred_element_type=F32)


def _mm_tn(a, b):
    return lax.dot_general(a.astype(BF16), b.astype(BF16), _TN, preferred_element_type=F32)


def _mmh(a, b):
    return jnp.dot(a, b, precision=HIGHEST, preferred_element_type=F32)


def _mmh_nt(a, b):
    return lax.dot_general(a, b, _NT, precision=HIGHEST, preferred_element_type=F32)


def _split_dot(x, w_bf16):
    hi = x.astype(BF16)
    lo = (x - hi.astype(F32)).astype(BF16)
    return (jnp.dot(hi, w_bf16, preferred_element_type=F32)
            + jnp.dot(lo, w_bf16, preferred_element_type=F32))


def _sigmoid(x):
    return 1.0 / (1.0 + jnp.exp(-x))


def _softplus(x):
    return jnp.maximum(x, 0.0) + jnp.log(1.0 + jnp.exp(-jnp.abs(x)))


def _iota(shape, dim):
    return lax.broadcasted_iota(jnp.int32, shape, dim)


def _params(*sem):
    return pltpu.CompilerParams(dimension_semantics=sem, vmem_limit_bytes=VMEM_LIMIT)


def _layer_norm(y, g, b):
    mu = jnp.mean(y, axis=-1, keepdims=True)
    d = y - mu
    var = jnp.mean(d * d, axis=-1, keepdims=True)
    return d * lax.rsqrt(var + LN_EPS) * g + b


def _ada_body(c_ref, w_ref, b_ref, o_ref):
    c = c_ref[...]
    o_ref[0] = _mmh(c * _sigmoid(c), w_ref[0]) + b_ref[0]


def _ada(c, w_ada, b_ada):
    depth, d, _ = w_ada.shape
    bsz = c.shape[0]
    return pl.pallas_call(
        _ada_body,
        grid=(depth, 6),
        in_specs=[pl.BlockSpec((bsz, d), lambda l, j: (0, 0)),
                  pl.BlockSpec((1, d, d), lambda l, j: (l, 0, j)),
                  pl.BlockSpec((1, 1, d), lambda l, j: (l, 0, j))],
        out_specs=pl.BlockSpec((1, bsz, d), lambda l, j: (l, 0, j)),
        out_shape=jax.ShapeDtypeStruct((depth, bsz, 6 * d), F32),
        compiler_params=_params("arbitrary", "arbitrary"),
        name="ada",
    )(c, w_ada, b_ada.reshape(depth, 1, 6 * d))


def _modmm_body(x_ref, mod_ref, w_ref, *rest, gate):
    if gate:
        b_ref, o_ref, u_ref = rest
    else:
        o_ref, u_ref = rest

    @pl.when(pl.program_id(1) == 0)
    def _():
        sh = mod_ref[0, 0:1, :]
        sc = mod_ref[0, 1:2, :]
        u_ref[...] = (x_ref[...] * (1.0 + sc) + sh).astype(BF16)

    y = jnp.dot(u_ref[...], w_ref[...], preferred_element_type=F32)
    if gate:
        y = _sigmoid(y + b_ref[...])
    o_ref[...] = y.astype(o_ref.dtype)


def _modmm(x, mod, w, bias, seq, tm, tn):
    n, d = x.shape
    width = w.shape[1]
    per_seq = seq // tm
    in_specs = [pl.BlockSpec((tm, d), lambda i, j: (i, 0)),
                pl.BlockSpec((1, 6, d), lambda i, j: (i // per_seq, 0, 0)),
                pl.BlockSpec((d, tn), lambda i, j: (0, j))]
    args = [x, mod, w]
    if bias is not None:
        in_specs.append(pl.BlockSpec((1, tn), lambda i, j: (0, j)))
        args.append(bias)
    return pl.pallas_call(
        functools.partial(_modmm_body, gate=bias is not None),
        grid=(n // tm, width // tn),
        in_specs=in_specs,
        out_specs=pl.BlockSpec((tm, tn), lambda i, j: (i, j)),
        out_shape=jax.ShapeDtypeStruct((n, width), BF16),
        scratch_shapes=[pltpu.VMEM((tm, d), BF16)],
        compiler_params=_params("arbitrary", "arbitrary"),
        name="gates" if bias is not None else "inproj",
    )(*args)


def _hgrn2_body(y_ref, lb_ref, ng_ref, o_ref, st_ref):
    c, sub = HG_CHUNK, HG_SUB

    @pl.when(pl.program_id(1) == 0)
    def _():
        st_ref[...] = jnp.zeros_like(st_ref)

    row = _iota((c, c), 0)
    col = _iota((c, c), 1)
    tril = (row >= col).astype(F32)
    ones = jnp.ones((2 * HG_DK, LANES), BF16)
    row_s = _iota((sub, c), 0)
    col_s = _iota((sub, c), 1)

    for h in range(HG_HEADS):
        q = y_ref[:, h * HG_DK:(h + 1) * HG_DK].astype(F32)
        z = y_ref[:, HG_W + h * HG_DK:HG_W + (h + 1) * HG_DK].astype(F32)
        v = y_ref[:, 2 * HG_W + h * HG_DV:2 * HG_W + (h + 1) * HG_DV]
        og = y_ref[:, 3 * HG_W + h * HG_DV:3 * HG_W + (h + 1) * HG_DV].astype(F32)
        lb = lb_ref[:, h * HG_DK:(h + 1) * HG_DK]
        f = lb + (1.0 - lb) * _sigmoid(z)
        logf = jnp.log(jnp.maximum(f, HG_F_MIN))
        kin = (1.0 - lb) * _sigmoid(-z)
        b = _mmh(tril, logf)

        a_rows = []
        for blk in range(c // sub):
            r0 = blk * sub
            b_i = b[r0:r0 + sub]
            q_i = q[r0:r0 + sub]
            k_i = kin[r0:r0 + sub]
            terms = []
            for s in range(sub):
                e = jnp.exp(jnp.minimum(b_i - b_i[s:s + 1], 0.0))
                terms.append(q_i * e * k_i[s:s + 1])
            w = jnp.concatenate(terms, axis=0)
            w_hi = w.astype(BF16)
            w_lo = (w - w_hi.astype(F32)).astype(BF16)
            rs = jnp.dot(jnp.concatenate([w_hi, w_lo], axis=1), ones,
                         preferred_element_type=F32)
            a_blk = jnp.zeros((sub, c), F32)
            for s in range(sub):
                a_blk = jnp.where(col_s == r0 + s, rs[s * sub:(s + 1) * sub, :c], a_blk)
            if blk > 0:
                beta = b[r0 - 1:r0]
                q_t = q_i * jnp.exp(b_i - beta)
                k_h = kin * jnp.exp(jnp.minimum(beta - b, 0.0))
                a_blk = jnp.where(col_s < r0, _mm_nt(q_t, k_h), a_blk)
            a_rows.append(jnp.where(col_s <= row_s + r0, a_blk, 0.0))
        a = jnp.concatenate(a_rows, axis=0)

        st = st_ref[h]
        o = _mm(a, v) + _mm_nt(q * jnp.exp(b), st)
        b_last = b[c - 1:c]
        st_ref[h] = st * jnp.exp(b_last) + _mm_tn(v, kin * jnp.exp(b_last - b))

        o = o * lax.rsqrt(jnp.mean(o * o, axis=-1, keepdims=True) + RMS_EPS) * ng_ref[...]
        o = o * (og * _sigmoid(og))
        o_ref[:, h * HG_DV:(h + 1) * HG_DV] = o.astype(o_ref.dtype)


def _hgrn2(y, lb, norm_g, bsz, seq):
    n = y.shape[0]
    nc = seq // HG_CHUNK
    width = 4 * HG_W
    return pl.pallas_call(
        _hgrn2_body,
        grid=(bsz, nc),
        in_specs=[pl.BlockSpec((HG_CHUNK, width), lambda b, c: (b * nc + c, HG_COL // width)),
                  pl.BlockSpec((1, HG_W), lambda b, c: (0, 0)),
                  pl.BlockSpec((1, HG_DV), lambda b, c: (0, 0))],
        out_specs=pl.BlockSpec((HG_CHUNK, HG_W), lambda b, c: (b * nc + c, 0)),
        out_shape=jax.ShapeDtypeStruct((n, HG_W), BF16),
        scratch_shapes=[pltpu.VMEM((HG_HEADS, HG_DV, HG_DK), F32)],
        compiler_params=_params("arbitrary", "arbitrary"),
        name="hgrn2",
    )(y, lb, norm_g)


def _diffattn_body(scal_ref, q_ref, k_ref, vt_ref, g_ref, o_ref, qq_ref, sa_ref, sb_ref, p_ref, m_ref, l_ref,
                   sc_ref, acc_ref, *, blk):
    h = pl.program_id(1)
    i = pl.program_id(2)
    lam = scal_ref[0]
    out_scale = scal_ref[1]
    slope = scal_ref[2 + h]
    q0 = i * blk

    q = q_ref[...].astype(F32) * (DA_DQK ** -0.5)
    lane = _iota(q.shape, 1)
    qq_ref[0:blk] = jnp.where(lane < DA_DQK, q, 0.0).astype(BF16)
    qq_ref[blk:2 * blk] = jnp.where(lane >= DA_DQK, q, 0.0).astype(BF16)
    m_ref[...] = jnp.full(m_ref.shape, MASK_VALUE, F32)
    l_ref[...] = jnp.zeros(l_ref.shape, F32)
    sc_ref[...] = jnp.ones(sc_ref.shape, F32)
    acc_ref[...] = jnp.zeros(acc_ref.shape, F32)
    p_ref[...] = jnp.zeros(p_ref.shape, BF16)
    key_off = _iota((blk, LANES), 0)
    rel = slope * key_off.astype(F32)

    def scores(j):
        return lax.dot_general(k_ref[pl.ds(j * blk, blk), :], qq_ref[...], _NT, preferred_element_type=F32)

    def softmax(j, src_ref, masked):
        bias = rel + slope * (j * blk - q0).astype(F32)
        for t in range(2 * blk // LANES):
            cols = slice(t * LANES, (t + 1) * LANES)
            s = src_ref[:, cols] + bias
            if masked:
                s = jnp.where(key_off <= ((_iota((blk, LANES), 1) + t * LANES) & (blk - 1)), s, MASK_VALUE)
            m_old = m_ref[:, cols]
            m_new = jnp.maximum(m_old, jnp.max(s, axis=0, keepdims=True))
            p = jnp.exp(s - m_new)
            sc = jnp.exp(m_old - m_new)
            l_ref[:, cols] = sc * l_ref[:, cols] + jnp.sum(p, axis=0, keepdims=True)
            p_ref[:, cols] = p.astype(BF16)
            sc_ref[:, cols] = sc
            m_ref[:, cols] = m_new

    def iteration(j, src_ref, dst_ref):
        sc_prev = sc_ref[...]
        pv = jnp.dot(vt_ref[jnp.maximum(j - 1, 0)], p_ref[...], preferred_element_type=F32)
        if dst_ref is not None:
            dst_ref[...] = scores(j + 1)
        softmax(j, src_ref, dst_ref is None)
        acc_ref[...] = acc_ref[...] * sc_prev + pv

    def body(pair, carry):
        iteration(2 * pair, sa_ref, sb_ref)
        iteration(2 * pair + 1, sb_ref, sa_ref)
        return carry

    sa_ref[...] = scores(0)
    lax.fori_loop(0, i // 2, body, 0)

    @pl.when(i % 2 == 0)
    def _():
        iteration(i, sa_ref, None)

    @pl.when(i % 2 == 1)
    def _():
        iteration(i - 1, sa_ref, sb_ref)
        iteration(i, sb_ref, None)

    acc = acc_ref[...] * sc_ref[...] + jnp.dot(vt_ref[i], p_ref[...], preferred_element_type=F32)
    o = acc / l_ref[...]
    d = o[:, :blk] - lam * o[:, blk:]
    g = jnp.concatenate([g_ref[...]] * (blk // LANES), axis=1)
    d = d * lax.rsqrt(jnp.mean(d * d, axis=0, keepdims=True) + RMS_EPS) * g * out_scale
    o_ref[...] = d.T.astype(o_ref.dtype)


def _diffattn(y, scal, subln_g, bsz, seq, blk):
    n = y.shape[0]
    nq = seq // blk
    qc, kc = DA_COL // DA_DV, (DA_COL + DA_W) // DA_DV
    v_t = y[:, DA_COL + 2 * DA_W:DA_COL + 3 * DA_W].reshape(bsz, nq, blk, DA_HEADS, DA_DV)
    v_t = v_t.transpose(0, 3, 1, 4, 2).reshape(bsz * DA_HEADS * nq, DA_DV, blk)
    g_col = jnp.broadcast_to(subln_g.reshape(DA_DV, 1), (DA_DV, LANES))
    row = (1, 2 * blk)
    return pl.pallas_call(
        functools.partial(_diffattn_body, blk=blk),
        scratch_shapes=[pltpu.VMEM((2 * blk, DA_DV), BF16), pltpu.VMEM((blk, 2 * blk), F32),
                        pltpu.VMEM((blk, 2 * blk), F32), pltpu.VMEM((blk, 2 * blk), BF16), pltpu.VMEM(row, F32), pltpu.VMEM(row, F32),
                        pltpu.VMEM(row, F32), pltpu.VMEM((DA_DV, 2 * blk), F32)],
        grid=(bsz, DA_HEADS, nq),
        in_specs=[pl.BlockSpec(memory_space=pltpu.SMEM),
                  pl.BlockSpec((blk, DA_DV), lambda b, h, i: (b * nq + i, qc + h)),
                  pl.BlockSpec((seq, DA_DV), lambda b, h, i: (b, kc + h)),
                  pl.BlockSpec((nq, DA_DV, blk), lambda b, h, i: (b * DA_HEADS + h, 0, 0)),
                  pl.BlockSpec((DA_DV, LANES), lambda b, h, i: (0, 0))],
        out_specs=pl.BlockSpec((blk, DA_DV), lambda b, h, i: (b * nq + i, h)),
        out_shape=jax.ShapeDtypeStruct((n, DA_W), BF16),
        compiler_params=_params("arbitrary", "arbitrary", "arbitrary"),
        name="diffattn",
    )(scal, y, y, v_t, g_col)


def _split_f32(x):
    hi = x.astype(BF16)
    hi_f = hi.astype(F32)
    return hi, hi_f, x - hi_f


def _dup_lhs(hi_f, lo_f, low_half):
    packed = jnp.where(low_half, hi_f, lo_f).astype(BF16)
    return jnp.concatenate([packed, packed], axis=1)


def _dup_rhs(hi, lo_f):
    lo = lo_f.astype(BF16)
    return jnp.concatenate([hi, hi, lo, lo], axis=0)


def _rwkv_body(y_ref, mu_ref, w0_ref, wup_ref, a0_ref, aup_ref, gup_ref, kk_ref, ka_ref, rk_ref,
               gng_ref, gnb_ref, seg_ref, o_ref, st_ref, prev_ref, osc_ref):
    c, sub, dh = RW_CHUNK, RW_SUB, RW_DH

    @pl.when(pl.program_id(1) == 0)
    def _():
        st_ref[...] = jnp.zeros_like(st_ref)
        prev_ref[...] = jnp.zeros_like(prev_ref)

    x = y_ref[...].astype(F32)
    x_prev = pltpu.roll(x, 1, axis=0)
    x_prev = jnp.where(_iota(x.shape, 0) == 0, prev_ref[...], x_prev)
    prev_ref[...] = x[c - 1:c]
    xs = x + (x_prev - x) * mu_ref[...]
    r = xs[:, 0:RW_W]
    k = xs[:, RW_W:2 * RW_W]
    v = xs[:, 2 * RW_W:3 * RW_W]
    wd = xs[:, 3 * RW_W:3 * RW_W + 64]
    ad = xs[:, 3 * RW_W + 64:3 * RW_W + 128]
    gd = xs[:, 3 * RW_W + 128:RW_IN_W]

    w_log = -_softplus(-(w0_ref[...] + _mmh(jnp.tanh(wd), wup_ref[...]))) - 0.5
    g = -jnp.exp(w_log)
    a = _sigmoid(a0_ref[...] + _mmh(ad, aup_ref[...]))
    gate = _mmh(_sigmoid(gd), gup_ref[...])
    seg = seg_ref[...]
    kk = k * kk_ref[...]
    kk = kk * lax.rsqrt(jnp.maximum(_split_dot(kk * kk, seg), 1e-12))
    k2 = k * (1.0 + (a - 1.0) * ka_ref[...])
    bb = kk * a
    bonus = _split_dot(r * k2 * rk_ref[...], seg) * v

    row = _iota((c, c), 0)
    col = _iota((c, c), 1)
    gc = _mmh((row >= col).astype(F32), g)
    g_last = gc[c - 1:c]
    e_inv = jnp.exp(-gc)
    e_tail = jnp.exp(g_last - gc)
    gam = jnp.exp(g_last)
    ar = jnp.concatenate([-kk * jnp.exp(gc - g), r * jnp.exp(gc)], axis=0).astype(BF16)
    bk = jnp.concatenate([bb * e_inv, k2 * e_inv], axis=0).astype(BF16)
    kb_bar = jnp.concatenate([k2 * e_tail, bb * e_tail], axis=0).astype(BF16)
    v_bf = v.astype(BF16)

    row2 = _iota((c, 2 * c), 0)
    lane2 = _iota((c, 2 * c), 1)
    col2 = lane2 & (c - 1)
    low_half = lane2 < c
    strict = row2 > col2
    incl = row2 >= col2
    same_blk = (row2 // sub) == (col2 // sub)
    eye = (row2 == col2).astype(F32)

    def dot(p, q):
        return jnp.dot(p, q, preferred_element_type=F32)

    heads = range(RW_HEADS)
    sls = [slice(h * dh, (h + 1) * dh) for h in heads]
    ar_h = [ar[:, sl] for sl in sls]
    quad = [lax.dot_general(ar_h[h], bk[:, sls[h]], _NT, preferred_element_type=F32)
            for h in heads]
    top = [jnp.where(strict, q[:c], 0.0) for q in quad]
    a_ak = [t[:, c:].astype(BF16) for t in top]
    a_r = [jnp.where(incl, q[c:], 0.0).astype(BF16) for q in quad]
    a_ab = [jnp.where(low_half, t, pltpu.roll(t, c, axis=1)) for t in top]
    a_d = [jnp.where(same_blk, x, 0.0) for x in a_ab]
    a_o = [(x - y).astype(BF16) for x, y in zip(a_ab, a_d)]

    s1 = [_split_f32(x) for x in a_d]
    p2 = [dot(_dup_lhs(hf, lf, low_half), _dup_rhs(hi, lf)) for hi, hf, lf in s1]
    s2 = [_split_f32(x) for x in p2]
    rhs2 = [_dup_rhs(hi, lf) for hi, _, lf in s2]
    p4 = [dot(_dup_lhs(hf, lf, low_half), rhs) for (_, hf, lf), rhs in zip(s2, rhs2)]
    s4 = [_split_f32(x) for x in p4]
    rhs4 = [_dup_rhs(hi, lf) for hi, _, lf in s4]
    p8 = [dot(_dup_lhs(hf, lf, low_half), rhs) for (_, hf, lf), rhs in zip(s4, rhs4)]
    rhs8 = [_dup_rhs(hi, lf) for hi, _, lf in (_split_f32(x) for x in p8)]
    t_d = [eye + x for x in a_d]
    for rhs_all in (rhs2, rhs4, rhs8):
        st = [_split_f32(x) for x in t_d]
        t_d = [x + dot(_dup_lhs(hf, lf, low_half), rhs) for x, (_, hf, lf), rhs in zip(t_d, st, rhs_all)]
    t_d = [x.astype(BF16) for x in t_d]

    nn = [dot(t[:, :c], x) for t, x in zip(t_d, a_o)]
    nn_bf = [x.astype(BF16) for x in nn]
    n2 = [dot(x[:, :c], x) for x in nn_bf]
    n3 = [dot(x[:, :c], y.astype(BF16)) for x, y in zip(nn_bf, n2)]
    t_m = [dot((eye + x + y + z).astype(BF16)[:, :c], t).astype(BF16)[:, :c]
           for x, y, z, t in zip(nn, n2, n3, t_d)]

    v_h = [v_bf[:, sl] for sl in sls]
    akv = [dot(x, y).astype(BF16) for x, y in zip(a_ak, v_h)]
    at_m = [dot(t, x[:c]).astype(BF16) for t, x in zip(t_m, ar_h)]
    v_p = [dot(t, x) for t, x in zip(t_m, akv)]
    s0 = [st_ref[h] for h in heads]
    proj = [lax.dot_general(jnp.concatenate([x, y[c:]], axis=0), s.astype(BF16), _NT, preferred_element_type=F32)
            for x, y, s in zip(at_m, ar_h, s0)]
    u = [(p[:c] + x).astype(BF16) for p, x in zip(proj, v_p)]
    for h in heads:
        osc_ref[:, sls[h]] = proj[h][c:] + dot(a_r[h], jnp.concatenate([u[h], v_h[h]], axis=0))
    for h in heads:
        st_ref[h] = s0[h] * gam[:, sls[h]] + lax.dot_general(
            jnp.concatenate([v_h[h], u[h]], axis=0), kb_bar[:, sls[h]], _TN, preferred_element_type=F32)

    o = osc_ref[...]
    mean = _split_dot(o, seg) * (1.0 / dh)
    d = o - mean
    var = _split_dot(d * d, seg) * (1.0 / dh)
    o = d * lax.rsqrt(var + RW_GN_EPS) * gng_ref[...] + gnb_ref[...]
    o_ref[...] = ((o + bonus) * gate).astype(o_ref.dtype)


def _rwkv(y, p, bsz, seq):
    n = y.shape[0]
    nc = seq // RW_CHUNK
    head = _iota((RW_W, RW_W), 0) // RW_DH == _iota((RW_W, RW_W), 1) // RW_DH
    seg = head.astype(BF16)
    rows = [p["mu"], p["w0"], p["w_up"], p["a0"], p["a_up"], p["g_up"], p["k_k"], p["k_a"], p["r_k"],
            p["gn_g"], p["gn_b"], seg]
    full = lambda b, c: (0, 0)
    return pl.pallas_call(
        _rwkv_body,
        grid=(bsz, nc),
        in_specs=[pl.BlockSpec((RW_CHUNK, RW_IN_W), lambda b, c: (b * nc + c, RW_COL // RW_IN_W))]
        + [pl.BlockSpec(a.shape, full) for a in rows],
        out_specs=pl.BlockSpec((RW_CHUNK, RW_W), lambda b, c: (b * nc + c, 0)),
        out_shape=jax.ShapeDtypeStruct((n, RW_W), BF16),
        scratch_shapes=[pltpu.VMEM((RW_HEADS, RW_DH, RW_DH), F32),
                        pltpu.VMEM((1, RW_IN_W), F32),
                        pltpu.VMEM((RW_CHUNK, RW_W), F32)],
        compiler_params=_params("arbitrary", "arbitrary"),
        name="rwkv7",
    )(y, *rows)


def _first_argmax(vals, row):
    top = jnp.max(vals, axis=0, keepdims=True)
    idx = jnp.min(jnp.where(vals == top, row, N_EXPERTS), axis=0, keepdims=True)
    return top, idx


def _merge_body(ohg_ref, oda_ref, orw_ref, gt_ref, x_ref, mod_ref, wb_ref, wo_ref, lng_ref, lnb_ref,
                wrt_ref, rb_ref, x1_ref, u2_ref, comb_ref):
    d = D_MODEL
    merged = (gt_ref[:, 0:d].astype(F32) * jnp.dot(ohg_ref[...], wb_ref[0:HG_W, :], preferred_element_type=F32)
              + gt_ref[:, d:2 * d].astype(F32)
              * jnp.dot(oda_ref[...], wb_ref[HG_W:HG_W + DA_W, :], preferred_element_type=F32)
              + gt_ref[:, 2 * d:3 * d].astype(F32)
              * jnp.dot(orw_ref[...], wb_ref[HG_W + DA_W:, :], preferred_element_type=F32))
    mix = _mm(merged, wo_ref[...])
    x1 = _layer_norm(ALPHA * x_ref[...] + (1.0 + mod_ref[0, 2:3, :]) * mix, lng_ref[...], lnb_ref[...])
    x1_ref[...] = x1
    u2 = x1 * (1.0 + mod_ref[0, 4:5, :]) + mod_ref[0, 3:4, :]
    u2_ref[...] = u2.astype(BF16)

    logits = _mmh_nt(wrt_ref[...], u2)
    ex = jnp.exp(logits - jnp.max(logits, axis=0, keepdims=True))
    scores = ex / jnp.sum(ex, axis=0, keepdims=True)
    sel = scores + rb_ref[...]
    row = _iota(sel.shape, 0)
    best = None
    for grp in range(N_GROUPS):
        a, b, c2, d2 = (sel[grp * EXPERTS_PER_GROUP + i:grp * EXPERTS_PER_GROUP + i + 1] for i in range(4))
        hi1, lo1, hi2, lo2 = jnp.maximum(a, b), jnp.minimum(a, b), jnp.maximum(c2, d2), jnp.minimum(c2, d2)
        top2 = jnp.maximum(hi1, hi2) + jnp.maximum(jnp.minimum(hi1, hi2), jnp.maximum(lo1, lo2))
        if best is None:
            best, best_grp = top2, jnp.zeros_like(top2, dtype=jnp.int32)
        else:
            better = top2 > best
            best = jnp.where(better, top2, best)
            best_grp = jnp.where(better, grp, best_grp)
    masked = jnp.where(row // EXPERTS_PER_GROUP == best_grp, sel, MASK_VALUE)
    _, idx1 = _first_argmax(masked, row)
    _, idx2 = _first_argmax(jnp.where(row == idx1, -jnp.inf, masked), row)
    w1 = jnp.sum(jnp.where(row == idx1, scores, 0.0), axis=0, keepdims=True)
    w2 = jnp.sum(jnp.where(row == idx2, scores, 0.0), axis=0, keepdims=True)
    comb = (jnp.where(row == idx1, w1, 0.0) + jnp.where(row == idx2, w2, 0.0)) / (w1 + w2)
    pad = jnp.zeros((LANES - N_EXPERTS, comb.shape[1]), F32)
    comb_ref[...] = jnp.concatenate([comb, pad], axis=0).T


def _merge(o_hg, o_da, o_rw, gates, x, mod, w_branch, w_out, ln_g, ln_b, w_router_t, router_bias, seq, tm):
    n, d = x.shape
    per_seq = seq // tm
    tile = lambda i: (i, 0)
    full = lambda i: (0, 0)
    return pl.pallas_call(
        _merge_body,
        grid=(n // tm,),
        in_specs=[pl.BlockSpec((tm, HG_W), tile), pl.BlockSpec((tm, DA_W), tile), pl.BlockSpec((tm, RW_W), tile),
                  pl.BlockSpec((tm, 3 * d), tile), pl.BlockSpec((tm, d), tile),
                  pl.BlockSpec((1, 6, d), lambda i: (i // per_seq, 0, 0)),
                  pl.BlockSpec(w_branch.shape, full), pl.BlockSpec(w_out.shape, full),
                  pl.BlockSpec((1, d), full), pl.BlockSpec((1, d), full),
                  pl.BlockSpec((N_EXPERTS, d), full), pl.BlockSpec((N_EXPERTS, 1), full)],
        out_specs=[pl.BlockSpec((tm, d), tile), pl.BlockSpec((tm, d), tile), pl.BlockSpec((tm, LANES), tile)],
        out_shape=[jax.ShapeDtypeStruct((n, d), F32), jax.ShapeDtypeStruct((n, d), BF16),
                   jax.ShapeDtypeStruct((n, LANES), F32)],
        compiler_params=_params("arbitrary"),
        name="merge",
    )(o_hg, o_da, o_rw, gates, x, mod, w_branch, w_out, ln_g, ln_b, w_router_t, router_bias)


def _moe_body(u_ref, comb_ref, wgu_ref, wd_ref, x1_ref, mod_ref, lng_ref, lnb_ref, o_ref, acc_ref):
    e = pl.program_id(1)

    @pl.when(e == 0)
    def _():
        acc_ref[...] = jnp.zeros_like(acc_ref)

    comb = comb_ref[...]
    weight = jnp.sum(jnp.where(_iota(comb.shape, 1) == e, comb, 0.0), axis=1, keepdims=True)
    hidden = jnp.dot(u_ref[...], wgu_ref[0], preferred_element_type=F32)
    hg = hidden[:, :D_EXPERT]
    act = hg * _sigmoid(hg) * hidden[:, D_EXPERT:] * weight
    acc_ref[...] += _mm(act, wd_ref[0])

    @pl.when(e == N_EXPERTS - 1)
    def _():
        y = ALPHA * x1_ref[...] + (1.0 + mod_ref[0, 5:6, :]) * acc_ref[...]
        o_ref[...] = _layer_norm(y, lng_ref[...], lnb_ref[...])


def _moe(u2, comb, w_gu, w_down, x1, mod, ln_g, ln_b, seq, tm):
    n, d = x1.shape
    per_seq = seq // tm
    tile = lambda i, e: (i, 0)
    full = lambda i, e: (0, 0)
    return pl.pallas_call(
        _moe_body,
        grid=(n // tm, N_EXPERTS),
        in_specs=[pl.BlockSpec((tm, d), tile), pl.BlockSpec((tm, LANES), tile),
                  pl.BlockSpec((1, d, 2 * D_EXPERT), lambda i, e: (e, 0, 0)),
                  pl.BlockSpec((1, D_EXPERT, d), lambda i, e: (e, 0, 0)),
                  pl.BlockSpec((tm, d), tile),
                  pl.BlockSpec((1, 6, d), lambda i, e: (i // per_seq, 0, 0)),
                  pl.BlockSpec((1, d), full), pl.BlockSpec((1, d), full)],
        out_specs=pl.BlockSpec((tm, d), tile),
        out_shape=jax.ShapeDtypeStruct((n, d), F32),
        scratch_shapes=[pltpu.VMEM((tm, d), F32)],
        compiler_params=_params("arbitrary", "arbitrary"),
        name="moe",
    )(u2, comb, w_gu, w_down, x1, mod, ln_g, ln_b)


def _tiles(seq):
    return min(512, seq), min(256, seq)


def kernel(x, c, w_ada, b_ada, w_in, hg_lb_logits, hg_norm_g, da_lambda, da_subln_g, rw_mu, rw_w0, rw_w_up,
           rw_a0, rw_a_up, rw_g_up, rw_k_k, rw_k_a, rw_r_k, rw_gn_g, rw_gn_b, w_merge, b_merge, w_branch, w_out,
           ln_g, ln_b, w_router, router_bias, w_exp_gate, w_exp_up, w_exp_down):
    bsz, seq, d = x.shape
    depth = w_in.shape[0]
    n = bsz * seq
    tm, blk = _tiles(seq)

    sm = jax.nn.softmax(hg_lb_logits.astype(F32), axis=0)
    hg_lb = jnp.cumsum(sm, axis=0) - sm[0:1]
    slopes = jnp.asarray([2.0 ** (-8.0 * (h + 1) / DA_HEADS) for h in range(DA_HEADS)], F32)

    mod_all = _ada(c, w_ada, b_ada).reshape(depth, bsz, 6, d)
    w_router_t = w_router.T
    router_bias = router_bias.reshape(N_EXPERTS, 1)

    xf = x.reshape(n, d)
    for l in range(depth):
        mod = mod_all[l]
        lq1, lk1, lq2, lk2 = da_lambda[l].astype(F32)
        lam_init = 0.8 - 0.6 * math.exp(-0.3 * l)
        lam = jnp.exp(jnp.sum(lq1 * lk1)) - jnp.exp(jnp.sum(lq2 * lk2)) + lam_init
        scal = jnp.concatenate([jnp.stack([lam, jnp.asarray(1.0 - lam_init, F32)]), slopes])

        y = _modmm(xf, mod, w_in[l].astype(BF16), None, seq, tm, 768)
        w_gates = jnp.concatenate([w_merge[l, br] for br in range(3)], axis=1).astype(BF16)
        gates = _modmm(xf, mod, w_gates, b_merge[l].reshape(1, 3 * d), seq, tm, 768)

        o_hg = _hgrn2(y, hg_lb[l].reshape(1, HG_W), hg_norm_g[l].reshape(1, HG_DV), bsz, seq)
        o_da = _diffattn(y, scal, da_subln_g[l].reshape(1, DA_DV), bsz, seq, blk)
        rw = dict(mu=rw_mu[l].reshape(1, -1), w0=rw_w0[l].reshape(1, -1), w_up=rw_w_up[l],
                  a0=rw_a0[l].reshape(1, -1), a_up=rw_a_up[l], g_up=rw_g_up[l],
                  k_k=rw_k_k[l].reshape(1, -1), k_a=rw_k_a[l].reshape(1, -1), r_k=rw_r_k[l].reshape(1, -1),
                  gn_g=rw_gn_g[l].reshape(1, -1), gn_b=rw_gn_b[l].reshape(1, -1))
        o_rw = _rwkv(y, rw, bsz, seq)

        x1, u2, comb = _merge(o_hg, o_da, o_rw, gates, xf, mod, w_branch[l].astype(BF16), w_out[l].astype(BF16),
                              ln_g[l, 0].reshape(1, d), ln_b[l, 0].reshape(1, d), w_router_t, router_bias, seq, tm)
        w_gu = jnp.concatenate([w_exp_gate[l], w_exp_up[l]], axis=-1).astype(BF16)
        xf = _moe(u2, comb, w_gu, w_exp_down[l].astype(BF16), x1, mod,
                  ln_g[l, 1].reshape(1, d), ln_b[l, 1].reshape(1, d), seq, tm)
    return xf.reshape(bsz, seq, d)
```

```python
import functools
import math

import jax
import jax.numpy as jnp
from jax import lax
from jax.experimental import pallas as pl
from jax.experimental.pallas import tpu as pltpu

D_MODEL = 1024
DEPTH = 4
HG_HEADS, HG_DK, HG_DV, HG_CHUNK, HG_SUB = 4, 128, 128, 64, 16
HG_W = HG_HEADS * HG_DV
HG_F_MIN = 1e-6
DA_HEADS, DA_DQK = 4, 64
DA_DV = 2 * DA_DQK
DA_W = DA_HEADS * DA_DV
MASK_VALUE = -1e30
PROJ_TN = 768
RW_HEADS, RW_DH, RW_CHUNK, RW_SUB = 8, 64, 64, 16
RW_W = RW_HEADS * RW_DH
RW_IN_W = 1792
RW_GN_EPS = 64e-5
IN_W = 5376
HG_COL, DA_COL, RW_COL = 0, 2048, 3584
N_EXPERTS, N_GROUPS, EXPERTS_PER_GROUP, D_EXPERT = 16, 4, 4, 512
ALPHA = (2.0 * DEPTH) ** 0.25
LN_EPS = 1e-5
RMS_EPS = 1e-6
LANES = 128

F32 = jnp.float32
BF16 = jnp.bfloat16
HIGHEST = lax.Precision.HIGHEST
VMEM_LIMIT = 48 * 1024 * 1024

_NT = (((1,), (1,)), ((), ()))
_TN = (((0,), (0,)), ((), ()))


def _mm(a, b):
    return jnp.dot(a.astype(BF16), b.astype(BF16), preferred_element_type=F32)


def _mm_nt(a, b):
    return lax.dot_general(a.astype(BF16), b.astype(BF16), _NT, preferred_element_type=F32)


def _mm_tn(a, b):
    return lax.dot_general(a.astype(BF16), b.astype(BF16), _TN, preferred_element_type=F32)


def _mmh(a, b):
    return jnp.dot(a, b, precision=HIGHEST, preferred_element_type=F32)


def _mmh_nt(a, b):
    return lax.dot_general(a, b, _NT, precision=HIGHEST, preferred_element_type=F32)


def _split_dot(x, w_bf16):
    hi = x.astype(BF16)
    lo = (x - hi.astype(F32)).astype(BF16)
    return (jnp.dot(hi, w_bf16, preferred_element_type=F32)
            + jnp.dot(lo, w_bf16, preferred_element_type=F32))


def _chunk_cumsum(x):
    c = x.shape[0]
    hi = x.astype(BF16)
    rest = x - hi.astype(F32)
    mid = rest.astype(BF16)
    lo = (rest - mid.astype(F32)).astype(BF16)
    col = _iota((c, 4 * c), 1)
    tri = ((col & (c - 1)) <= _iota((c, 4 * c), 0)) & (col < 3 * c)
    return jnp.dot(jnp.where(tri, 1.0, 0.0).astype(BF16), jnp.concatenate([hi, mid, lo, lo], axis=0),
                   preferred_element_type=F32)


def _sigmoid(x):
    return 1.0 / (1.0 + jnp.exp(-x))


def _softplus(x):
    return jnp.maximum(x, 0.0) + jnp.log(1.0 + jnp.exp(-jnp.abs(x)))


def _iota(shape, dim):
    return lax.broadcasted_iota(jnp.int32, shape, dim)


def _params(*sem):
    return pltpu.CompilerParams(dimension_semantics=sem, vmem_limit_bytes=VMEM_LIMIT)


def _layer_norm(y, g, b):
    mu = jnp.mean(y, axis=-1, keepdims=True)
    d = y - mu
    var = jnp.mean(d * d, axis=-1, keepdims=True)
    return d * lax.rsqrt(var + LN_EPS) * g + b


def _ada_body(c_ref, w_ref, b_ref, o_ref):
    c = c_ref[...]
    o_ref[0] = _mmh(c * _sigmoid(c), w_ref[0]) + b_ref[0]


def _ada(c, w_ada, b_ada):
    depth, d, _ = w_ada.shape
    bsz = c.shape[0]
    return pl.pallas_call(
        _ada_body,
        grid=(depth, 6),
        in_specs=[pl.BlockSpec((bsz, d), lambda l, j: (0, 0)),
                  pl.BlockSpec((1, d, d), lambda l, j: (l, 0, j)),
                  pl.BlockSpec((1, 1, d), lambda l, j: (l, 0, j))],
        out_specs=pl.BlockSpec((1, bsz, d), lambda l, j: (l, 0, j)),
        out_shape=jax.ShapeDtypeStruct((depth, bsz, 6 * d), F32),
        compiler_params=_params("arbitrary", "arbitrary"),
        name="ada",
    )(c, w_ada, b_ada.reshape(depth, 1, 6 * d))


def _proj_body(x_ref, mod_ref, win_ref, wg_ref, bg_ref, y_ref, g_ref):
    u = (x_ref[...] * (1.0 + mod_ref[0, 1:2, :]) + mod_ref[0, 0:1, :]).astype(BF16)
    for c0 in range(0, IN_W, PROJ_TN):
        cols = slice(c0, c0 + PROJ_TN)
        y_ref[:, cols] = jnp.dot(u, win_ref[:, cols], preferred_element_type=F32).astype(BF16)
    for c0 in range(0, 3 * D_MODEL, PROJ_TN):
        cols = slice(c0, c0 + PROJ_TN)
        g = jnp.dot(u, wg_ref[:, cols], preferred_element_type=F32) + bg_ref[:, cols]
        g_ref[:, cols] = _sigmoid(g).astype(BF16)


def _proj(x, mod, w_in, w_gates, b_gates, seq, tm):
    n, d = x.shape
    per_seq = seq // tm
    tile = lambda i: (i, 0)
    resident = dict(index_map=lambda i: (0, 0), pipeline_mode=pl.Buffered(1))
    return pl.pallas_call(
        _proj_body,
        grid=(n // tm,),
        in_specs=[pl.BlockSpec((tm, d), tile),
                  pl.BlockSpec((1, 6, d), lambda i: (i // per_seq, 0, 0)),
                  pl.BlockSpec(w_in.shape, **resident),
                  pl.BlockSpec(w_gates.shape, **resident),
                  pl.BlockSpec(b_gates.shape, **resident)],
        out_specs=[pl.BlockSpec((tm, IN_W), tile), pl.BlockSpec((tm, 3 * d), tile)],
        out_shape=[jax.ShapeDtypeStruct((n, IN_W), BF16), jax.ShapeDtypeStruct((n, 3 * d), BF16)],
        compiler_params=_params("arbitrary"),
        name="proj",
    )(x, mod, w_in, w_gates, b_gates)


def _hgrn2_body(y_ref, lb_ref, ng_ref, o_ref, st_ref):
    c, sub = HG_CHUNK, HG_SUB

    @pl.when(pl.program_id(1) == 0)
    def _():
        st_ref[...] = jnp.zeros_like(st_ref)

    q = y_ref[:, 0:HG_W].astype(F32)
    z = y_ref[:, HG_W:2 * HG_W].astype(F32)
    v = y_ref[:, 2 * HG_W:3 * HG_W]
    og = y_ref[:, 3 * HG_W:4 * HG_W].astype(F32)
    lb = lb_ref[...]
    f = lb + (1.0 - lb) * _sigmoid(z)
    kin = (1.0 - lb) * _sigmoid(-z)
    b = _chunk_cumsum(jnp.log(jnp.maximum(f, HG_F_MIN)))
    log_k = jnp.log(kin)
    rel = log_k - b
    b_last = b[c - 1:c]
    q_dec = (q * jnp.exp(b)).astype(BF16)
    k_tail = (kin * jnp.exp(b_last - b)).astype(BF16)
    decay = jnp.exp(b_last)

    ones = jnp.ones((HG_DK, LANES), BF16)
    row_s = _iota((sub, c), 0)
    col_s = _iota((sub, c), 1)
    heads = range(HG_HEADS)
    blks = range(c // sub)
    hs = [slice(h * HG_DK, (h + 1) * HG_DK) for h in heads]

    diag = {}
    for h in heads:
        for blk in blks:
            rows = slice(blk * sub, (blk + 1) * sub)
            b_i, q_i, rel_i, lk_i = b[rows, hs[h]], q[rows, hs[h]], rel[rows, hs[h]], log_k[rows, hs[h]]
            terms = [q_i * jnp.exp(jnp.minimum(b_i + rel_i[s:s + 1], lk_i[s:s + 1])) for s in range(sub)]
            w = jnp.concatenate(terms, axis=0).astype(BF16)
            diag[h, blk] = jnp.dot(w, ones, preferred_element_type=F32)
    below = {}
    for h in heads:
        for blk in blks[1:]:
            r0 = blk * sub
            beta = b[r0 - 1:r0, hs[h]]
            q_t = q[r0:r0 + sub, hs[h]] * jnp.exp(b[r0:r0 + sub, hs[h]] - beta)
            k_h = kin[:, hs[h]] * jnp.exp(jnp.minimum(beta - b[:, hs[h]], 0.0))
            below[h, blk] = _mm_nt(q_t, k_h)
    scores = []
    for h in heads:
        a_rows = []
        for blk in blks:
            r0 = blk * sub
            a_blk = jnp.zeros((sub, c), F32)
            for s in range(sub):
                a_blk = jnp.where(col_s == r0 + s, diag[h, blk][s * sub:(s + 1) * sub, :c], a_blk)
            if blk > 0:
                a_blk = jnp.where(col_s < r0, below[h, blk], a_blk)
            a_rows.append(jnp.where(col_s <= row_s + r0, a_blk, 0.0))
        scores.append(jnp.concatenate(a_rows, axis=0).astype(BF16))

    st = [st_ref[h] for h in heads]
    intra = [jnp.dot(scores[h], v[:, hs[h]], preferred_element_type=F32) for h in heads]
    inter = [lax.dot_general(q_dec[:, hs[h]], st[h].astype(BF16), _NT, preferred_element_type=F32) for h in heads]
    for h in heads:
        st_ref[h] = st[h] * decay[:, hs[h]] + lax.dot_general(v[:, hs[h]], k_tail[:, hs[h]], _TN,
                                                             preferred_element_type=F32)
    for h in heads:
        o = intra[h] + inter[h]
        o = o * lax.rsqrt(jnp.mean(o * o, axis=-1, keepdims=True) + RMS_EPS) * ng_ref[...]
        o_ref[:, hs[h]] = (o * (og[:, hs[h]] * _sigmoid(og[:, hs[h]]))).astype(o_ref.dtype)


def _hgrn2(y, lb, norm_g, bsz, seq):
    n = y.shape[0]
    nc = seq // HG_CHUNK
    width = 4 * HG_W
    return pl.pallas_call(
        _hgrn2_body,
        grid=(bsz, nc),
        in_specs=[pl.BlockSpec((HG_CHUNK, width), lambda b, c: (b * nc + c, HG_COL // width)),
                  pl.BlockSpec((1, HG_W), lambda b, c: (0, 0)),
                  pl.BlockSpec((1, HG_DV), lambda b, c: (0, 0))],
        out_specs=pl.BlockSpec((HG_CHUNK, HG_W), lambda b, c: (b * nc + c, 0)),
        out_shape=jax.ShapeDtypeStruct((n, HG_W), BF16),
        scratch_shapes=[pltpu.VMEM((HG_HEADS, HG_DV, HG_DK), F32)],
        compiler_params=_params("arbitrary", "arbitrary"),
        name="hgrn2",
    )(y, lb, norm_g)


def _diffattn_body(scal_ref, q_ref, k_ref, vt_ref, g_ref, o_ref, qq_ref, sa_ref, sb_ref, p_ref, m_ref, l_ref,
                   sc_ref, acc_ref, *, blk):
    h = pl.program_id(1)
    i = pl.program_id(2)
    lam = scal_ref[0]
    out_scale = scal_ref[1]
    slope = scal_ref[2 + h]
    q0 = i * blk

    q = q_ref[...].astype(F32) * (DA_DQK ** -0.5)
    lane = _iota(q.shape, 1)
    qq_ref[0:blk] = jnp.where(lane < DA_DQK, q, 0.0).astype(BF16)
    qq_ref[blk:2 * blk] = jnp.where(lane >= DA_DQK, q, 0.0).astype(BF16)
    m_ref[...] = jnp.full(m_ref.shape, MASK_VALUE, F32)
    l_ref[...] = jnp.zeros(l_ref.shape, F32)
    sc_ref[...] = jnp.ones(sc_ref.shape, F32)
    acc_ref[...] = jnp.zeros(acc_ref.shape, F32)
    p_ref[...] = jnp.zeros(p_ref.shape, BF16)
    key_off = _iota((blk, LANES), 0)
    rel = slope * key_off.astype(F32)

    def scores(j):
        return lax.dot_general(k_ref[pl.ds(j * blk, blk), :], qq_ref[...], _NT, preferred_element_type=F32)

    def softmax(j, src_ref, masked):
        bias = rel + slope * (j * blk - q0).astype(F32)
        for t in range(2 * blk // LANES):
            cols = slice(t * LANES, (t + 1) * LANES)
            s = src_ref[:, cols] + bias
            if masked:
                s = jnp.where(key_off <= ((_iota((blk, LANES), 1) + t * LANES) & (blk - 1)), s, MASK_VALUE)
            m_old = m_ref[:, cols]
            m_new = jnp.maximum(m_old, jnp.max(s, axis=0, keepdims=True))
            p = jnp.exp(s - m_new)
            sc = jnp.exp(m_old - m_new)
            l_ref[:, cols] = sc * l_ref[:, cols] + jnp.sum(p, axis=0, keepdims=True)
            p_ref[:, cols] = p.astype(BF16)
            sc_ref[:, cols] = sc
            m_ref[:, cols] = m_new

    def iteration(j, src_ref, dst_ref):
        sc_prev = sc_ref[...]
        pv = jnp.dot(vt_ref[jnp.maximum(j - 1, 0)], p_ref[...], preferred_element_type=F32)
        if dst_ref is not None:
            dst_ref[...] = scores(j + 1)
        softmax(j, src_ref, dst_ref is None)
        acc_ref[...] = acc_ref[...] * sc_prev + pv

    def body(pair, carry):
        iteration(2 * pair, sa_ref, sb_ref)
        iteration(2 * pair + 1, sb_ref, sa_ref)
        return carry

    sa_ref[...] = scores(0)
    lax.fori_loop(0, i // 2, body, 0)

    @pl.when(i % 2 == 0)
    def _():
        iteration(i, sa_ref, None)

    @pl.when(i % 2 == 1)
    def _():
        iteration(i - 1, sa_ref, sb_ref)
        iteration(i, sb_ref, None)

    acc = acc_ref[...] * sc_ref[...] + jnp.dot(vt_ref[i], p_ref[...], preferred_element_type=F32)
    o = acc / l_ref[...]
    d = o[:, :blk] - lam * o[:, blk:]
    g = jnp.concatenate([g_ref[...]] * (blk // LANES), axis=1)
    d = d * lax.rsqrt(jnp.mean(d * d, axis=0, keepdims=True) + RMS_EPS) * g * out_scale
    o_ref[...] = d.T.astype(o_ref.dtype)


def _diffattn(y, scal, subln_g, bsz, seq, blk):
    n = y.shape[0]
    nq = seq // blk
    qc, kc = DA_COL // DA_DV, (DA_COL + DA_W) // DA_DV
    v_t = y[:, DA_COL + 2 * DA_W:DA_COL + 3 * DA_W].reshape(bsz, nq, blk, DA_HEADS, DA_DV)
    v_t = v_t.transpose(0, 3, 1, 4, 2).reshape(bsz * DA_HEADS * nq, DA_DV, blk)
    g_col = jnp.broadcast_to(subln_g.reshape(DA_DV, 1), (DA_DV, LANES))
    row = (1, 2 * blk)
    return pl.pallas_call(
        functools.partial(_diffattn_body, blk=blk),
        scratch_shapes=[pltpu.VMEM((2 * blk, DA_DV), BF16), pltpu.VMEM((blk, 2 * blk), F32),
                        pltpu.VMEM((blk, 2 * blk), F32), pltpu.VMEM((blk, 2 * blk), BF16), pltpu.VMEM(row, F32), pltpu.VMEM(row, F32),
                        pltpu.VMEM(row, F32), pltpu.VMEM((DA_DV, 2 * blk), F32)],
        grid=(bsz, DA_HEADS, nq),
        in_specs=[pl.BlockSpec(memory_space=pltpu.SMEM),
                  pl.BlockSpec((blk, DA_DV), lambda b, h, i: (b * nq + i, qc + h)),
                  pl.BlockSpec((seq, DA_DV), lambda b, h, i: (b, kc + h)),
                  pl.BlockSpec((nq, DA_DV, blk), lambda b, h, i: (b * DA_HEADS + h, 0, 0)),
                  pl.BlockSpec((DA_DV, LANES), lambda b, h, i: (0, 0))],
        out_specs=pl.BlockSpec((blk, DA_DV), lambda b, h, i: (b * nq + i, h)),
        out_shape=jax.ShapeDtypeStruct((n, DA_W), BF16),
        compiler_params=_params("arbitrary", "arbitrary", "arbitrary"),
        name="diffattn",
    )(scal, y, y, v_t, g_col)


def _split_f32(x):
    hi = x.astype(BF16)
    hi_f = hi.astype(F32)
    return hi, hi_f, x - hi_f


def _dup_lhs(hi_f, lo_f, low_half):
    packed = jnp.where(low_half, hi_f, lo_f).astype(BF16)
    return jnp.concatenate([packed, packed], axis=1)


def _dup_rhs(hi, lo_f):
    lo = lo_f.astype(BF16)
    return jnp.concatenate([hi, hi, lo, lo], axis=0)


def _rwkv_body(y_ref, mu_ref, w0_ref, wup_ref, a0_ref, aup_ref, gup_ref, kk_ref, ka_ref, rk_ref,
               gng_ref, gnb_ref, seg_ref, o_ref, st_ref, prev_ref, osc_ref):
    c, sub, dh = RW_CHUNK, RW_SUB, RW_DH

    @pl.when(pl.program_id(1) == 0)
    def _():
        st_ref[...] = jnp.zeros_like(st_ref)
        prev_ref[...] = jnp.zeros_like(prev_ref)

    x = y_ref[...].astype(F32)
    x_prev = pltpu.roll(x, 1, axis=0)
    x_prev = jnp.where(_iota(x.shape, 0) == 0, prev_ref[...], x_prev)
    prev_ref[...] = x[c - 1:c]
    xs = x + (x_prev - x) * mu_ref[...]
    r = xs[:, 0:RW_W]
    k = xs[:, RW_W:2 * RW_W]
    v = xs[:, 2 * RW_W:3 * RW_W]
    wd = xs[:, 3 * RW_W:3 * RW_W + 64]
    ad = xs[:, 3 * RW_W + 64:3 * RW_W + 128]
    gd = xs[:, 3 * RW_W + 128:RW_IN_W]

    w_log = -_softplus(-(w0_ref[...] + _mmh(jnp.tanh(wd), wup_ref[...]))) - 0.5
    g = -jnp.exp(w_log)
    a = _sigmoid(a0_ref[...] + _mmh(ad, aup_ref[...]))
    gate = _mmh(_sigmoid(gd), gup_ref[...])
    seg = seg_ref[...]
    kk = k * kk_ref[...]
    kk = kk * lax.rsqrt(jnp.maximum(_split_dot(kk * kk, seg), 1e-12))
    k2 = k * (1.0 + (a - 1.0) * ka_ref[...])
    bb = kk * a
    bonus = _split_dot(r * k2 * rk_ref[...], seg) * v

    gc = _chunk_cumsum(g)
    g_last = gc[c - 1:c]
    e_inv = jnp.exp(-gc)
    e_tail = jnp.exp(g_last - gc)
    gam = jnp.exp(g_last)
    ar = jnp.concatenate([-kk * jnp.exp(gc - g), r * jnp.exp(gc)], axis=0).astype(BF16)
    bk = jnp.concatenate([bb * e_inv, k2 * e_inv], axis=0).astype(BF16)
    kb_bar = jnp.concatenate([k2 * e_tail, bb * e_tail], axis=0).astype(BF16)
    v_bf = v.astype(BF16)

    row2 = _iota((c, 2 * c), 0)
    lane2 = _iota((c, 2 * c), 1)
    col2 = lane2 & (c - 1)
    low_half = lane2 < c
    strict = row2 > col2
    incl = row2 >= col2
    same_blk = (row2 // sub) == (col2 // sub)
    eye = (row2 == col2).astype(F32)

    def dot(p, q):
        return jnp.dot(p, q, preferred_element_type=F32)

    heads = range(RW_HEADS)
    sls = [slice(h * dh, (h + 1) * dh) for h in heads]
    ar_h = [ar[:, sl] for sl in sls]
    quad = [lax.dot_general(ar_h[h], bk[:, sls[h]], _NT, preferred_element_type=F32)
            for h in heads]
    top = [jnp.where(strict, q[:c], 0.0) for q in quad]
    a_ak = [t[:, c:].astype(BF16) for t in top]
    a_r = [jnp.where(incl, q[c:], 0.0).astype(BF16) for q in quad]
    a_ab = [jnp.where(low_half, t, pltpu.roll(t, c, axis=1)) for t in top]
    a_d = [jnp.where(same_blk, x, 0.0) for x in a_ab]
    a_o = [(x - y).astype(BF16) for x, y in zip(a_ab, a_d)]

    s1 = [_split_f32(x) for x in a_d]
    p2 = [dot(_dup_lhs(hf, lf, low_half), _dup_rhs(hi, lf)) for hi, hf, lf in s1]
    s2 = [_split_f32(x) for x in p2]
    rhs2 = [_dup_rhs(hi, lf) for hi, _, lf in s2]
    p4 = [dot(_dup_lhs(hf, lf, low_half), rhs) for (_, hf, lf), rhs in zip(s2, rhs2)]
    s4 = [_split_f32(x) for x in p4]
    rhs4 = [_dup_rhs(hi, lf) for hi, _, lf in s4]
    p8 = [dot(_dup_lhs(hf, lf, low_half), rhs) for (_, hf, lf), rhs in zip(s4, rhs4)]
    rhs8 = [_dup_rhs(hi, lf) for hi, _, lf in (_split_f32(x) for x in p8)]
    t_d = [eye + x for x in a_d]
    for rhs_all in (rhs2, rhs4, rhs8):
        st = [_split_f32(x) for x in t_d]
        t_d = [x + dot(_dup_lhs(hf, lf, low_half), rhs) for x, (_, hf, lf), rhs in zip(t_d, st, rhs_all)]
    t_d = [x.astype(BF16) for x in t_d]

    nn = [dot(t[:, :c], x) for t, x in zip(t_d, a_o)]
    nn_bf = [x.astype(BF16) for x in nn]
    n2 = [dot(x[:, :c], x) for x in nn_bf]
    n3 = [dot(x[:, :c], y.astype(BF16)) for x, y in zip(nn_bf, n2)]
    t_m = [dot((eye + x + y + z).astype(BF16)[:, :c], t).astype(BF16)[:, :c]
           for x, y, z, t in zip(nn, n2, n3, t_d)]

    v_h = [v_bf[:, sl] for sl in sls]
    akv = [dot(x, y).astype(BF16) for x, y in zip(a_ak, v_h)]
    at_m = [dot(t, x[:c]).astype(BF16) for t, x in zip(t_m, ar_h)]
    v_p = [dot(t, x) for t, x in zip(t_m, akv)]
    s0 = [st_ref[h] for h in heads]
    proj = [lax.dot_general(jnp.concatenate([x, y[c:]], axis=0), s.astype(BF16), _NT, preferred_element_type=F32)
            for x, y, s in zip(at_m, ar_h, s0)]
    u = [(p[:c] + x).astype(BF16) for p, x in zip(proj, v_p)]
    for h in heads:
        osc_ref[:, sls[h]] = proj[h][c:] + dot(a_r[h], jnp.concatenate([u[h], v_h[h]], axis=0))
    for h in heads:
        st_ref[h] = s0[h] * gam[:, sls[h]] + lax.dot_general(
            jnp.concatenate([v_h[h], u[h]], axis=0), kb_bar[:, sls[h]], _TN, preferred_element_type=F32)

    o = osc_ref[...]
    mean = _split_dot(o, seg) * (1.0 / dh)
    d = o - mean
    var = _split_dot(d * d, seg) * (1.0 / dh)
    o = d * lax.rsqrt(var + RW_GN_EPS) * gng_ref[...] + gnb_ref[...]
    o_ref[...] = ((o + bonus) * gate).astype(o_ref.dtype)


def _rwkv(y, p, bsz, seq):
    n = y.shape[0]
    nc = seq // RW_CHUNK
    head = _iota((RW_W, RW_W), 0) // RW_DH == _iota((RW_W, RW_W), 1) // RW_DH
    seg = head.astype(BF16)
    rows = [p["mu"], p["w0"], p["w_up"], p["a0"], p["a_up"], p["g_up"], p["k_k"], p["k_a"], p["r_k"],
            p["gn_g"], p["gn_b"], seg]
    full = lambda b, c: (0, 0)
    return pl.pallas_call(
        _rwkv_body,
        grid=(bsz, nc),
        in_specs=[pl.BlockSpec((RW_CHUNK, RW_IN_W), lambda b, c: (b * nc + c, RW_COL // RW_IN_W))]
        + [pl.BlockSpec(a.shape, full) for a in rows],
        out_specs=pl.BlockSpec((RW_CHUNK, RW_W), lambda b, c: (b * nc + c, 0)),
        out_shape=jax.ShapeDtypeStruct((n, RW_W), BF16),
        scratch_shapes=[pltpu.VMEM((RW_HEADS, RW_DH, RW_DH), F32),
                        pltpu.VMEM((1, RW_IN_W), F32),
                        pltpu.VMEM((RW_CHUNK, RW_W), F32)],
        compiler_params=_params("arbitrary", "arbitrary"),
        name="rwkv7",
    )(y, *rows)


def _first_argmax(vals, row):
    top = jnp.max(vals, axis=0, keepdims=True)
    idx = jnp.min(jnp.where(vals == top, row, N_EXPERTS), axis=0, keepdims=True)
    return top, idx


def _merge_body(ohg_ref, oda_ref, orw_ref, gt_ref, x_ref, mod_ref, wb_ref, wo_ref, lng_ref, lnb_ref,
                wrt_ref, rb_ref, x1_ref, u2_ref, comb_ref):
    d = D_MODEL
    merged = (gt_ref[:, 0:d].astype(F32) * jnp.dot(ohg_ref[...], wb_ref[0:HG_W, :], preferred_element_type=F32)
              + gt_ref[:, d:2 * d].astype(F32)
              * jnp.dot(oda_ref[...], wb_ref[HG_W:HG_W + DA_W, :], preferred_element_type=F32)
              + gt_ref[:, 2 * d:3 * d].astype(F32)
              * jnp.dot(orw_ref[...], wb_ref[HG_W + DA_W:, :], preferred_element_type=F32))
    mix = _mm(merged, wo_ref[...])
    x1 = _layer_norm(ALPHA * x_ref[...] + (1.0 + mod_ref[0, 2:3, :]) * mix, lng_ref[...], lnb_ref[...])
    x1_ref[...] = x1
    u2 = x1 * (1.0 + mod_ref[0, 4:5, :]) + mod_ref[0, 3:4, :]
    u2_ref[...] = u2.astype(BF16)

    logits = _mmh_nt(wrt_ref[...], u2)
    ex = jnp.exp(logits - jnp.max(logits, axis=0, keepdims=True))
    scores = ex / jnp.sum(ex, axis=0, keepdims=True)
    sel = scores + rb_ref[...]
    row = _iota(sel.shape, 0)
    best = None
    for grp in range(N_GROUPS):
        a, b, c2, d2 = (sel[grp * EXPERTS_PER_GROUP + i:grp * EXPERTS_PER_GROUP + i + 1] for i in range(4))
        hi1, lo1, hi2, lo2 = jnp.maximum(a, b), jnp.minimum(a, b), jnp.maximum(c2, d2), jnp.minimum(c2, d2)
        top2 = jnp.maximum(hi1, hi2) + jnp.maximum(jnp.minimum(hi1, hi2), jnp.maximum(lo1, lo2))
        if best is None:
            best, best_grp = top2, jnp.zeros_like(top2, dtype=jnp.int32)
        else:
            better = top2 > best
            best = jnp.where(better, top2, best)
            best_grp = jnp.where(better, grp, best_grp)
    masked = jnp.where(row // EXPERTS_PER_GROUP == best_grp, sel, MASK_VALUE)
    _, idx1 = _first_argmax(masked, row)
    _, idx2 = _first_argmax(jnp.where(row == idx1, -jnp.inf, masked), row)
    w1 = jnp.sum(jnp.where(row == idx1, scores, 0.0), axis=0, keepdims=True)
    w2 = jnp.sum(jnp.where(row == idx2, scores, 0.0), axis=0, keepdims=True)
    comb = (jnp.where(row == idx1, w1, 0.0) + jnp.where(row == idx2, w2, 0.0)) / (w1 + w2)
    pad = jnp.zeros((LANES - N_EXPERTS, comb.shape[1]), F32)
    comb_ref[...] = jnp.concatenate([comb, pad], axis=0).T


def _merge(o_hg, o_da, o_rw, gates, x, mod, w_branch, w_out, ln_g, ln_b, w_router_t, router_bias, seq, tm):
    n, d = x.shape
    per_seq = seq // tm
    tile = lambda i: (i, 0)
    full = lambda i: (0, 0)
    return pl.pallas_call(
        _merge_body,
        grid=(n // tm,),
        in_specs=[pl.BlockSpec((tm, HG_W), tile), pl.BlockSpec((tm, DA_W), tile), pl.BlockSpec((tm, RW_W), tile),
                  pl.BlockSpec((tm, 3 * d), tile), pl.BlockSpec((tm, d), tile),
                  pl.BlockSpec((1, 6, d), lambda i: (i // per_seq, 0, 0)),
                  pl.BlockSpec(w_branch.shape, full), pl.BlockSpec(w_out.shape, full),
                  pl.BlockSpec((1, d), full), pl.BlockSpec((1, d), full),
                  pl.BlockSpec((N_EXPERTS, d), full), pl.BlockSpec((N_EXPERTS, 1), full)],
        out_specs=[pl.BlockSpec((tm, d), tile), pl.BlockSpec((tm, d), tile), pl.BlockSpec((tm, LANES), tile)],
        out_shape=[jax.ShapeDtypeStruct((n, d), F32), jax.ShapeDtypeStruct((n, d), BF16),
                   jax.ShapeDtypeStruct((n, LANES), F32)],
        compiler_params=_params("arbitrary"),
        name="merge",
    )(o_hg, o_da, o_rw, gates, x, mod, w_branch, w_out, ln_g, ln_b, w_router_t, router_bias)


def _moe_body(u_ref, comb_ref, wgu_ref, wd_ref, x1_ref, mod_ref, lng_ref, lnb_ref, o_ref, acc_ref):
    e = pl.program_id(1)

    @pl.when(e == 0)
    def _():
        acc_ref[...] = jnp.zeros_like(acc_ref)

    comb = comb_ref[...]
    weight = jnp.sum(jnp.where(_iota(comb.shape, 1) == e, comb, 0.0), axis=1, keepdims=True)
    hidden = jnp.dot(u_ref[...], wgu_ref[0], preferred_element_type=F32)
    hg = hidden[:, :D_EXPERT]
    act = hg * _sigmoid(hg) * hidden[:, D_EXPERT:] * weight
    acc_ref[...] += _mm(act, wd_ref[0])

    @pl.when(e == N_EXPERTS - 1)
    def _():
        y = ALPHA * x1_ref[...] + (1.0 + mod_ref[0, 5:6, :]) * acc_ref[...]
        o_ref[...] = _layer_norm(y, lng_ref[...], lnb_ref[...])


def _moe(u2, comb, w_gu, w_down, x1, mod, ln_g, ln_b, seq, tm):
    n, d = x1.shape
    per_seq = seq // tm
    tile = lambda i, e: (i, 0)
    full = lambda i, e: (0, 0)
    return pl.pallas_call(
        _moe_body,
        grid=(n // tm, N_EXPERTS),
        in_specs=[pl.BlockSpec((tm, d), tile), pl.BlockSpec((tm, LANES), tile),
                  pl.BlockSpec((1, d, 2 * D_EXPERT), lambda i, e: (e, 0, 0)),
                  pl.BlockSpec((1, D_EXPERT, d), lambda i, e: (e, 0, 0)),
                  pl.BlockSpec((tm, d), tile),
                  pl.BlockSpec((1, 6, d), lambda i, e: (i // per_seq, 0, 0)),
                  pl.BlockSpec((1, d), full), pl.BlockSpec((1, d), full)],
        out_specs=pl.BlockSpec((tm, d), tile),
        out_shape=jax.ShapeDtypeStruct((n, d), F32),
        scratch_shapes=[pltpu.VMEM((tm, d), F32)],
        compiler_params=_params("arbitrary", "arbitrary"),
        name="moe",
    )(u2, comb, w_gu, w_down, x1, mod, ln_g, ln_b)


def _tiles(seq):
    return min(512, seq), min(256, seq)


def kernel(x, c, w_ada, b_ada, w_in, hg_lb_logits, hg_norm_g, da_lambda, da_subln_g, rw_mu, rw_w0, rw_w_up,
           rw_a0, rw_a_up, rw_g_up, rw_k_k, rw_k_a, rw_r_k, rw_gn_g, rw_gn_b, w_merge, b_merge, w_branch, w_out,
           ln_g, ln_b, w_router, router_bias, w_exp_gate, w_exp_up, w_exp_down):
    bsz, seq, d = x.shape
    depth = w_in.shape[0]
    n = bsz * seq
    tm, blk = _tiles(seq)

    sm = jax.nn.softmax(hg_lb_logits.astype(F32), axis=0)
    hg_lb = jnp.cumsum(sm, axis=0) - sm[0:1]
    slopes = jnp.asarray([2.0 ** (-8.0 * (h + 1) / DA_HEADS) for h in range(DA_HEADS)], F32)

    mod_all = _ada(c, w_ada, b_ada).reshape(depth, bsz, 6, d)
    w_router_t = w_router.T
    router_bias = router_bias.reshape(N_EXPERTS, 1)

    xf = x.reshape(n, d)
    for l in range(depth):
        mod = mod_all[l]
        lq1, lk1, lq2, lk2 = da_lambda[l].astype(F32)
        lam_init = 0.8 - 0.6 * math.exp(-0.3 * l)
        lam = jnp.exp(jnp.sum(lq1 * lk1)) - jnp.exp(jnp.sum(lq2 * lk2)) + lam_init
        scal = jnp.concatenate([jnp.stack([lam, jnp.asarray(1.0 - lam_init, F32)]), slopes])

        w_gates = jnp.concatenate([w_merge[l, br] for br in range(3)], axis=1).astype(BF16)
        y, gates = _proj(xf, mod, w_in[l].astype(BF16), w_gates, b_merge[l].reshape(1, 3 * d), seq, tm)

        o_hg = _hgrn2(y, hg_lb[l].reshape(1, HG_W), hg_norm_g[l].reshape(1, HG_DV), bsz, seq)
        o_da = _diffattn(y, scal, da_subln_g[l].reshape(1, DA_DV), bsz, seq, blk)
        rw = dict(mu=rw_mu[l].reshape(1, -1), w0=rw_w0[l].reshape(1, -1), w_up=rw_w_up[l],
                  a0=rw_a0[l].reshape(1, -1), a_up=rw_a_up[l], g_up=rw_g_up[l],
                  k_k=rw_k_k[l].reshape(1, -1), k_a=rw_k_a[l].reshape(1, -1), r_k=rw_r_k[l].reshape(1, -1),
                  gn_g=rw_gn_g[l].reshape(1, -1), gn_b=rw_gn_b[l].reshape(1, -1))
        o_rw = _rwkv(y, rw, bsz, seq)

        x1, u2, comb = _merge(o_hg, o_da, o_rw, gates, xf, mod, w_branch[l].astype(BF16), w_out[l].astype(BF16),
                              ln_g[l, 0].reshape(1, d), ln_b[l, 0].reshape(1, d), w_router_t, router_bias, seq, tm)
        w_gu = jnp.concatenate([w_exp_gate[l], w_exp_up[l]], axis=-1).astype(BF16)
        xf = _moe(u2, comb, w_gu, w_exp_down[l].astype(BF16), x1, mod,
                  ln_g[l, 1].reshape(1, d), ln_b[l, 1].reshape(1, d), seq, tm)
    return xf.reshape(bsz, seq, d)
```

```python
import functools
import math

import jax
import jax.numpy as jnp
from jax import lax
from jax.experimental import pallas as pl
from jax.experimental.pallas import tpu as pltpu

D_MODEL = 1024
DEPTH = 4
HG_HEADS, HG_DK, HG_DV, HG_CHUNK, HG_SUB = 4, 128, 128, 64, 16
HG_W = HG_HEADS * HG_DV
HG_F_MIN = 1e-6
DA_HEADS, DA_DQK = 4, 64
DA_DV = 2 * DA_DQK
DA_W = DA_HEADS * DA_DV
MASK_VALUE = -1e30
LOG2E = math.log2(math.e)
PROJ_TN = 768
RW_HEADS, RW_DH, RW_CHUNK, RW_SUB = 8, 64, 64, 16
RW_W = RW_HEADS * RW_DH
RW_IN_W = 1792
RW_GN_EPS = 64e-5
IN_W = 5376
HG_COL, DA_COL, RW_COL = 0, 2048, 3584
N_EXPERTS, N_GROUPS, EXPERTS_PER_GROUP, D_EXPERT = 16, 4, 4, 512
MOE_CHUNK = 16
MOE_TM = 512
ALPHA = (2.0 * DEPTH) ** 0.25
LN_EPS = 1e-5
RMS_EPS = 1e-6
LANES = 128

F32 = jnp.float32
BF16 = jnp.bfloat16
HIGHEST = lax.Precision.HIGHEST
VMEM_LIMIT = 48 * 1024 * 1024

_NT = (((1,), (1,)), ((), ()))
_TN = (((0,), (0,)), ((), ()))


def _mm(a, b):
    return jnp.dot(a.astype(BF16), b.astype(BF16), preferred_element_type=F32)


def _mm_nt(a, b):
    return lax.dot_general(a.astype(BF16), b.astype(BF16), _NT, preferred_element_type=F32)


def _mm_tn(a, b):
    return lax.dot_general(a.astype(BF16), b.astype(BF16), _TN, preferred_element_type=F32)


def _mmh(a, b):
    return jnp.dot(a, b, precision=HIGHEST, preferred_element_type=F32)


def _mmh_nt(a, b):
    return lax.dot_general(a, b, _NT, precision=HIGHEST, preferred_element_type=F32)


def _split_dot(x, w_bf16):
    hi = x.astype(BF16)
    lo = (x - hi.astype(F32)).astype(BF16)
    return (jnp.dot(hi, w_bf16, preferred_element_type=F32)
            + jnp.dot(lo, w_bf16, preferred_element_type=F32))


def _chunk_cumsum(x):
    c = x.shape[0]
    hi = x.astype(BF16)
    rest = x - hi.astype(F32)
    mid = rest.astype(BF16)
    lo = (rest - mid.astype(F32)).astype(BF16)
    col = _iota((c, 4 * c), 1)
    tri = ((col & (c - 1)) <= _iota((c, 4 * c), 0)) & (col < 3 * c)
    return jnp.dot(jnp.where(tri, 1.0, 0.0).astype(BF16), jnp.concatenate([hi, mid, lo, lo], axis=0),
                   preferred_element_type=F32)


def _sigmoid(x):
    return 1.0 / (1.0 + jnp.exp(-x))


def _softplus(x):
    return jnp.maximum(x, 0.0) + jnp.log(1.0 + jnp.exp(-jnp.abs(x)))


def _iota(shape, dim):
    return lax.broadcasted_iota(jnp.int32, shape, dim)


def _params(*sem):
    return pltpu.CompilerParams(dimension_semantics=sem, vmem_limit_bytes=VMEM_LIMIT)


def _layer_norm(y, g, b):
    mu = jnp.mean(y, axis=-1, keepdims=True)
    d = y - mu
    var = jnp.mean(d * d, axis=-1, keepdims=True)
    return d * lax.rsqrt(var + LN_EPS) * g + b


def _ada_body(c_ref, w_ref, b_ref, o_ref):
    c = c_ref[...]
    o_ref[0] = _mmh(c * _sigmoid(c), w_ref[0]) + b_ref[0]


def _ada(c, w_ada, b_ada):
    depth, d, _ = w_ada.shape
    bsz = c.shape[0]
    return pl.pallas_call(
        _ada_body,
        grid=(depth, 6),
        in_specs=[pl.BlockSpec((bsz, d), lambda l, j: (0, 0)),
                  pl.BlockSpec((1, d, d), lambda l, j: (l, 0, j)),
                  pl.BlockSpec((1, 1, d), lambda l, j: (l, 0, j))],
        out_specs=pl.BlockSpec((1, bsz, d), lambda l, j: (l, 0, j)),
        out_shape=jax.ShapeDtypeStruct((depth, bsz, 6 * d), F32),
        compiler_params=_params("arbitrary", "arbitrary"),
        name="ada",
    )(c, w_ada, b_ada.reshape(depth, 1, 6 * d))


def _proj_body(x_ref, mod_ref, win_ref, wg_ref, bg_ref, y_ref, g_ref):
    u = (x_ref[...] * (1.0 + mod_ref[0, 1:2, :]) + mod_ref[0, 0:1, :]).astype(BF16)
    for c0 in range(0, IN_W, PROJ_TN):
        cols = slice(c0, c0 + PROJ_TN)
        y_ref[:, cols] = jnp.dot(u, win_ref[:, cols], preferred_element_type=F32).astype(BF16)
    for c0 in range(0, 3 * D_MODEL, PROJ_TN):
        cols = slice(c0, c0 + PROJ_TN)
        g = jnp.dot(u, wg_ref[:, cols], preferred_element_type=F32) + bg_ref[:, cols]
        g_ref[:, cols] = _sigmoid(g).astype(BF16)


def _proj(x, mod, w_in, w_gates, b_gates, seq, tm):
    n, d = x.shape
    per_seq = seq // tm
    tile = lambda i: (i, 0)
    resident = dict(index_map=lambda i: (0, 0), pipeline_mode=pl.Buffered(1))
    return pl.pallas_call(
        _proj_body,
        grid=(n // tm,),
        in_specs=[pl.BlockSpec((tm, d), tile),
                  pl.BlockSpec((1, 6, d), lambda i: (i // per_seq, 0, 0)),
                  pl.BlockSpec(w_in.shape, **resident),
                  pl.BlockSpec(w_gates.shape, **resident),
                  pl.BlockSpec(b_gates.shape, **resident)],
        out_specs=[pl.BlockSpec((tm, IN_W), tile), pl.BlockSpec((tm, 3 * d), tile)],
        out_shape=[jax.ShapeDtypeStruct((n, IN_W), BF16), jax.ShapeDtypeStruct((n, 3 * d), BF16)],
        compiler_params=_params("arbitrary"),
        name="proj",
    )(x, mod, w_in, w_gates, b_gates)


def _hgrn2_body(y_ref, lb_ref, ng_ref, o_ref, st_ref):
    c, sub = HG_CHUNK, HG_SUB

    @pl.when(pl.program_id(1) == 0)
    def _():
        st_ref[...] = jnp.zeros_like(st_ref)

    q = y_ref[:, 0:HG_W].astype(F32)
    z = y_ref[:, HG_W:2 * HG_W].astype(F32)
    v = y_ref[:, 2 * HG_W:3 * HG_W]
    og = y_ref[:, 3 * HG_W:4 * HG_W].astype(F32)
    lb = lb_ref[...]
    f = lb + (1.0 - lb) * _sigmoid(z)
    kin = (1.0 - lb) * _sigmoid(-z)
    b = _chunk_cumsum(jnp.log(jnp.maximum(f, HG_F_MIN)))
    log_k = jnp.log(kin)
    rel = log_k - b
    b_last = b[c - 1:c]
    q_dec = (q * jnp.exp(b)).astype(BF16)
    k_tail = (kin * jnp.exp(b_last - b)).astype(BF16)
    decay = jnp.exp(b_last)

    ones = jnp.ones((HG_DK, LANES), BF16)
    row_s = _iota((sub, c), 0)
    col_s = _iota((sub, c), 1)
    heads = range(HG_HEADS)
    blks = range(c // sub)
    hs = [slice(h * HG_DK, (h + 1) * HG_DK) for h in heads]

    diag = {}
    for h in heads:
        for blk in blks:
            rows = slice(blk * sub, (blk + 1) * sub)
            b_i, q_i, rel_i, lk_i = b[rows, hs[h]], q[rows, hs[h]], rel[rows, hs[h]], log_k[rows, hs[h]]
            terms = [q_i * jnp.exp(jnp.minimum(b_i + rel_i[s:s + 1], lk_i[s:s + 1])) for s in range(sub)]
            w = jnp.concatenate(terms, axis=0).astype(BF16)
            diag[h, blk] = jnp.dot(w, ones, preferred_element_type=F32)
    below = {}
    for h in heads:
        for blk in blks[1:]:
            r0 = blk * sub
            beta = b[r0 - 1:r0, hs[h]]
            q_t = q[r0:r0 + sub, hs[h]] * jnp.exp(b[r0:r0 + sub, hs[h]] - beta)
            k_h = kin[:, hs[h]] * jnp.exp(jnp.minimum(beta - b[:, hs[h]], 0.0))
            below[h, blk] = _mm_nt(q_t, k_h)
    scores = []
    for h in heads:
        a_rows = []
        for blk in blks:
            r0 = blk * sub
            a_blk = jnp.zeros((sub, c), F32)
            for s in range(sub):
                a_blk = jnp.where(col_s == r0 + s, diag[h, blk][s * sub:(s + 1) * sub, :c], a_blk)
            if blk > 0:
                a_blk = jnp.where(col_s < r0, below[h, blk], a_blk)
            a_rows.append(jnp.where(col_s <= row_s + r0, a_blk, 0.0))
        scores.append(jnp.concatenate(a_rows, axis=0).astype(BF16))

    st = [st_ref[h] for h in heads]
    intra = [jnp.dot(scores[h], v[:, hs[h]], preferred_element_type=F32) for h in heads]
    inter = [lax.dot_general(q_dec[:, hs[h]], st[h].astype(BF16), _NT, preferred_element_type=F32) for h in heads]
    for h in heads:
        st_ref[h] = st[h] * decay[:, hs[h]] + lax.dot_general(v[:, hs[h]], k_tail[:, hs[h]], _TN,
                                                             preferred_element_type=F32)
    for h in heads:
        o = intra[h] + inter[h]
        o = o * lax.rsqrt(jnp.mean(o * o, axis=-1, keepdims=True) + RMS_EPS) * ng_ref[...]
        o_ref[:, hs[h]] = (o * (og[:, hs[h]] * _sigmoid(og[:, hs[h]]))).astype(o_ref.dtype)


def _hgrn2(y, lb, norm_g, bsz, seq):
    n = y.shape[0]
    nc = seq // HG_CHUNK
    width = 4 * HG_W
    return pl.pallas_call(
        _hgrn2_body,
        grid=(bsz, nc),
        in_specs=[pl.BlockSpec((HG_CHUNK, width), lambda b, c: (b * nc + c, HG_COL // width)),
                  pl.BlockSpec((1, HG_W), lambda b, c: (0, 0)),
                  pl.BlockSpec((1, HG_DV), lambda b, c: (0, 0))],
        out_specs=pl.BlockSpec((HG_CHUNK, HG_W), lambda b, c: (b * nc + c, 0)),
        out_shape=jax.ShapeDtypeStruct((n, HG_W), BF16),
        scratch_shapes=[pltpu.VMEM((HG_HEADS, HG_DV, HG_DK), F32)],
        compiler_params=_params("arbitrary", "arbitrary"),
        name="hgrn2",
    )(y, lb, norm_g)


def _diffattn_body(scal_ref, q_ref, k_ref, vt_ref, g_ref, o_ref, qq_ref, sa_ref, sb_ref, p_ref, m_ref, l_ref,
                   sc_ref, acc_ref, *, blk):
    h = pl.program_id(1)
    i = pl.program_id(2)
    lam = scal_ref[0]
    out_scale = scal_ref[1]
    slope = scal_ref[2 + h] * LOG2E
    q0 = i * blk

    q = q_ref[...].astype(F32) * (DA_DQK ** -0.5 * LOG2E)
    lane = _iota(q.shape, 1)
    stacked = jnp.concatenate([jnp.where(lane < DA_DQK, q, 0.0), jnp.where(lane >= DA_DQK, q, 0.0)], axis=0)
    qq_ref[...] = stacked.T.astype(BF16)
    m_ref[...] = jnp.full(m_ref.shape, MASK_VALUE, F32)
    l_ref[...] = jnp.zeros(l_ref.shape, F32)
    sc_ref[...] = jnp.ones(sc_ref.shape, F32)
    acc_ref[...] = jnp.zeros(acc_ref.shape, F32)
    p_ref[...] = jnp.zeros(p_ref.shape, BF16)
    key_off = _iota((blk, LANES), 0)
    rel = slope * key_off.astype(F32)

    def scores(j):
        return jnp.dot(k_ref[pl.ds(j * blk, blk), :], qq_ref[...], preferred_element_type=F32)

    def softmax(j, src_ref, masked):
        bias = rel + slope * (j * blk - q0).astype(F32)
        for t in range(2 * blk // LANES):
            cols = slice(t * LANES, (t + 1) * LANES)
            s = src_ref[:, cols] + bias
            if masked:
                s = jnp.where(key_off <= ((_iota((blk, LANES), 1) + t * LANES) & (blk - 1)), s, MASK_VALUE)
            m_old = m_ref[:, cols]
            m_new = jnp.maximum(m_old, jnp.max(s, axis=0, keepdims=True))
            p = jnp.exp2(s - m_new)
            sc = jnp.exp2(m_old - m_new)
            l_ref[:, cols] = sc * l_ref[:, cols] + jnp.sum(p, axis=0, keepdims=True)
            p_ref[:, cols] = p.astype(BF16)
            sc_ref[:, cols] = sc
            m_ref[:, cols] = m_new

    def iteration(j, src_ref, dst_ref):
        sc_prev = sc_ref[...]
        pv = jnp.dot(vt_ref[jnp.maximum(j - 1, 0)], p_ref[...], preferred_element_type=F32)
        if dst_ref is not None:
            dst_ref[...] = scores(j + 1)
        softmax(j, src_ref, dst_ref is None)
        acc_ref[...] = acc_ref[...] * sc_prev + pv

    def body(pair, carry):
        iteration(2 * pair, sa_ref, sb_ref)
        iteration(2 * pair + 1, sb_ref, sa_ref)
        return carry

    sa_ref[...] = scores(0)
    lax.fori_loop(0, i // 2, body, 0)

    @pl.when(i % 2 == 0)
    def _():
        iteration(i, sa_ref, None)

    @pl.when(i % 2 == 1)
    def _():
        iteration(i - 1, sa_ref, sb_ref)
        iteration(i, sb_ref, None)

    acc = acc_ref[...] * sc_ref[...] + jnp.dot(vt_ref[i], p_ref[...], preferred_element_type=F32)
    o = acc / l_ref[...]
    d = o[:, :blk] - lam * o[:, blk:]
    g = jnp.concatenate([g_ref[...]] * (blk // LANES), axis=1)
    d = d * lax.rsqrt(jnp.mean(d * d, axis=0, keepdims=True) + RMS_EPS) * g * out_scale
    o_ref[...] = d.T.astype(o_ref.dtype)


def _diffattn(y, scal, subln_g, bsz, seq, blk):
    n = y.shape[0]
    nq = seq // blk
    qc, kc = DA_COL // DA_DV, (DA_COL + DA_W) // DA_DV
    v_t = y[:, DA_COL + 2 * DA_W:DA_COL + 3 * DA_W].reshape(bsz, nq, blk, DA_HEADS, DA_DV)
    v_t = v_t.transpose(0, 3, 1, 4, 2).reshape(bsz * DA_HEADS * nq, DA_DV, blk)
    g_col = jnp.broadcast_to(subln_g.reshape(DA_DV, 1), (DA_DV, LANES))
    row = (1, 2 * blk)
    return pl.pallas_call(
        functools.partial(_diffattn_body, blk=blk),
        scratch_shapes=[pltpu.VMEM((DA_DV, 2 * blk), BF16), pltpu.VMEM((blk, 2 * blk), F32),
                        pltpu.VMEM((blk, 2 * blk), F32), pltpu.VMEM((blk, 2 * blk), BF16), pltpu.VMEM(row, F32), pltpu.VMEM(row, F32),
                        pltpu.VMEM(row, F32), pltpu.VMEM((DA_DV, 2 * blk), F32)],
        grid=(bsz, DA_HEADS, nq),
        in_specs=[pl.BlockSpec(memory_space=pltpu.SMEM),
                  pl.BlockSpec((blk, DA_DV), lambda b, h, i: (b * nq + i, qc + h)),
                  pl.BlockSpec((seq, DA_DV), lambda b, h, i: (b, kc + h)),
                  pl.BlockSpec((nq, DA_DV, blk), lambda b, h, i: (b * DA_HEADS + h, 0, 0)),
                  pl.BlockSpec((DA_DV, LANES), lambda b, h, i: (0, 0))],
        out_specs=pl.BlockSpec((blk, DA_DV), lambda b, h, i: (b * nq + i, h)),
        out_shape=jax.ShapeDtypeStruct((n, DA_W), BF16),
        compiler_params=_params("arbitrary", "arbitrary", "arbitrary"),
        name="diffattn",
    )(scal, y, y, v_t, g_col)


def _split_f32(x):
    hi = x.astype(BF16)
    hi_f = hi.astype(F32)
    return hi, hi_f, x - hi_f


def _dup_lhs(hi_f, lo_f, low_half):
    packed = jnp.where(low_half, hi_f, lo_f).astype(BF16)
    return jnp.concatenate([packed, packed], axis=1)


def _dup_rhs(hi, lo_f):
    lo = lo_f.astype(BF16)
    return jnp.concatenate([hi, hi, lo, lo], axis=0)


def _rwkv_body(y_ref, mu_ref, w0_ref, wup_ref, a0_ref, aup_ref, gup_ref, kk_ref, ka_ref, rk_ref,
               gng_ref, gnb_ref, seg_ref, o_ref, st_ref, prev_ref, osc_ref):
    c, sub, dh = RW_CHUNK, RW_SUB, RW_DH

    @pl.when(pl.program_id(1) == 0)
    def _():
        st_ref[...] = jnp.zeros_like(st_ref)
        prev_ref[...] = jnp.zeros_like(prev_ref)

    x = y_ref[...].astype(F32)
    x_prev = pltpu.roll(x, 1, axis=0)
    x_prev = jnp.where(_iota(x.shape, 0) == 0, prev_ref[...], x_prev)
    prev_ref[...] = x[c - 1:c]
    xs = x + (x_prev - x) * mu_ref[...]
    r = xs[:, 0:RW_W]
    k = xs[:, RW_W:2 * RW_W]
    v = xs[:, 2 * RW_W:3 * RW_W]
    wd = xs[:, 3 * RW_W:3 * RW_W + 64]
    ad = xs[:, 3 * RW_W + 64:3 * RW_W + 128]
    gd = xs[:, 3 * RW_W + 128:RW_IN_W]

    w_log = -_softplus(-(w0_ref[...] + _mmh(jnp.tanh(wd), wup_ref[...]))) - 0.5
    g = -jnp.exp(w_log)
    a = _sigmoid(a0_ref[...] + _mmh(ad, aup_ref[...]))
    gate = _mmh(_sigmoid(gd), gup_ref[...])
    seg = seg_ref[...]
    kk = k * kk_ref[...]
    kk = kk * lax.rsqrt(jnp.maximum(_split_dot(kk * kk, seg), 1e-12))
    k2 = k * (1.0 + (a - 1.0) * ka_ref[...])
    bb = kk * a
    bonus = _split_dot(r * k2 * rk_ref[...], seg) * v

    gc = _chunk_cumsum(g)
    g_last = gc[c - 1:c]
    e_inv = jnp.exp(-gc)
    e_tail = jnp.exp(g_last - gc)
    gam = jnp.exp(g_last)
    ar = jnp.concatenate([-kk * jnp.exp(gc - g), r * jnp.exp(gc)], axis=0).astype(BF16)
    bk = jnp.concatenate([bb * e_inv, k2 * e_inv], axis=0).astype(BF16)
    kb_bar = jnp.concatenate([k2 * e_tail, bb * e_tail], axis=0).astype(BF16)
    v_bf = v.astype(BF16)

    row2 = _iota((c, 2 * c), 0)
    lane2 = _iota((c, 2 * c), 1)
    col2 = lane2 & (c - 1)
    low_half = lane2 < c
    strict = row2 > col2
    incl = row2 >= col2
    same_blk = (row2 // sub) == (col2 // sub)
    eye = (row2 == col2).astype(F32)

    def dot(p, q):
        return jnp.dot(p, q, preferred_element_type=F32)

    heads = range(RW_HEADS)
    sls = [slice(h * dh, (h + 1) * dh) for h in heads]
    ar_h = [ar[:, sl] for sl in sls]
    quad = [lax.dot_general(ar_h[h], bk[:, sls[h]], _NT, preferred_element_type=F32)
            for h in heads]
    top = [jnp.where(strict, q[:c], 0.0) for q in quad]
    a_ak = [t[:, c:].astype(BF16) for t in top]
    a_r = [jnp.where(incl, q[c:], 0.0).astype(BF16) for q in quad]
    a_ab = [jnp.where(low_half, t, pltpu.roll(t, c, axis=1)) for t in top]
    a_d = [jnp.where(same_blk, x, 0.0) for x in a_ab]
    a_o = [(x - y).astype(BF16) for x, y in zip(a_ab, a_d)]

    s1 = [_split_f32(x) for x in a_d]
    p2 = [dot(_dup_lhs(hf, lf, low_half), _dup_rhs(hi, lf)) for hi, hf, lf in s1]
    s2 = [_split_f32(x) for x in p2]
    rhs2 = [_dup_rhs(hi, lf) for hi, _, lf in s2]
    p4 = [dot(_dup_lhs(hf, lf, low_half), rhs) for (_, hf, lf), rhs in zip(s2, rhs2)]
    s4 = [_split_f32(x) for x in p4]
    rhs4 = [_dup_rhs(hi, lf) for hi, _, lf in s4]
    p8 = [dot(_dup_lhs(hf, lf, low_half), rhs) for (_, hf, lf), rhs in zip(s4, rhs4)]
    rhs8 = [_dup_rhs(hi, lf) for hi, _, lf in (_split_f32(x) for x in p8)]
    t_d = [eye + x for x in a_d]
    for rhs_all in (rhs2, rhs4, rhs8):
        st = [_split_f32(x) for x in t_d]
        t_d = [x + dot(_dup_lhs(hf, lf, low_half), rhs) for x, (_, hf, lf), rhs in zip(t_d, st, rhs_all)]
    t_d = [x.astype(BF16) for x in t_d]

    nn = [dot(t[:, :c], x) for t, x in zip(t_d, a_o)]
    nn_bf = [x.astype(BF16) for x in nn]
    n2 = [dot(x[:, :c], x) for x in nn_bf]
    n3 = [dot(x[:, :c], y.astype(BF16)) for x, y in zip(nn_bf, n2)]
    t_m = [dot((eye + x + y + z).astype(BF16)[:, :c], t).astype(BF16)[:, :c]
           for x, y, z, t in zip(nn, n2, n3, t_d)]

    v_h = [v_bf[:, sl] for sl in sls]
    akv = [dot(x, y).astype(BF16) for x, y in zip(a_ak, v_h)]
    at_m = [dot(t, x[:c]).astype(BF16) for t, x in zip(t_m, ar_h)]
    v_p = [dot(t, x) for t, x in zip(t_m, akv)]
    s0 = [st_ref[h] for h in heads]
    proj = [lax.dot_general(jnp.concatenate([x, y[c:]], axis=0), s.astype(BF16), _NT, preferred_element_type=F32)
            for x, y, s in zip(at_m, ar_h, s0)]
    u = [(p[:c] + x).astype(BF16) for p, x in zip(proj, v_p)]
    for h in heads:
        osc_ref[:, sls[h]] = proj[h][c:] + dot(a_r[h], jnp.concatenate([u[h], v_h[h]], axis=0))
    for h in heads:
        st_ref[h] = s0[h] * gam[:, sls[h]] + lax.dot_general(
            jnp.concatenate([v_h[h], u[h]], axis=0), kb_bar[:, sls[h]], _TN, preferred_element_type=F32)

    o = osc_ref[...]
    mean = _split_dot(o, seg) * (1.0 / dh)
    d = o - mean
    var = _split_dot(d * d, seg) * (1.0 / dh)
    o = d * lax.rsqrt(var + RW_GN_EPS) * gng_ref[...] + gnb_ref[...]
    o_ref[...] = ((o + bonus) * gate).astype(o_ref.dtype)


def _rwkv(y, p, bsz, seq):
    n = y.shape[0]
    nc = seq // RW_CHUNK
    head = _iota((RW_W, RW_W), 0) // RW_DH == _iota((RW_W, RW_W), 1) // RW_DH
    seg = head.astype(BF16)
    rows = [p["mu"], p["w0"], p["w_up"], p["a0"], p["a_up"], p["g_up"], p["k_k"], p["k_a"], p["r_k"],
            p["gn_g"], p["gn_b"], seg]
    full = lambda b, c: (0, 0)
    return pl.pallas_call(
        _rwkv_body,
        grid=(bsz, nc),
        in_specs=[pl.BlockSpec((RW_CHUNK, RW_IN_W), lambda b, c: (b * nc + c, RW_COL // RW_IN_W))]
        + [pl.BlockSpec(a.shape, full) for a in rows],
        out_specs=pl.BlockSpec((RW_CHUNK, RW_W), lambda b, c: (b * nc + c, 0)),
        out_shape=jax.ShapeDtypeStruct((n, RW_W), BF16),
        scratch_shapes=[pltpu.VMEM((RW_HEADS, RW_DH, RW_DH), F32),
                        pltpu.VMEM((1, RW_IN_W), F32),
                        pltpu.VMEM((RW_CHUNK, RW_W), F32)],
        compiler_params=_params("arbitrary", "arbitrary"),
        name="rwkv7",
    )(y, *rows)


def _first_argmax(vals, row):
    top = jnp.max(vals, axis=0, keepdims=True)
    idx = jnp.min(jnp.where(vals == top, row, N_EXPERTS), axis=0, keepdims=True)
    return top, idx


def _merge_body(ohg_ref, oda_ref, orw_ref, gt_ref, x_ref, mod_ref, wb_ref, wo_ref, lng_ref, lnb_ref,
                wrt_ref, rb_ref, tri_ref, tri16_ref, x1_ref, u2_ref, route_ref, cnt_ref):
    d = D_MODEL
    merged = (gt_ref[:, 0:d].astype(F32) * jnp.dot(ohg_ref[...], wb_ref[0:HG_W, :], preferred_element_type=F32)
              + gt_ref[:, d:2 * d].astype(F32)
              * jnp.dot(oda_ref[...], wb_ref[HG_W:HG_W + DA_W, :], preferred_element_type=F32)
              + gt_ref[:, 2 * d:3 * d].astype(F32)
              * jnp.dot(orw_ref[...], wb_ref[HG_W + DA_W:, :], preferred_element_type=F32))
    mix = _mm(merged, wo_ref[...])
    x1 = _layer_norm(ALPHA * x_ref[...] + (1.0 + mod_ref[0, 2:3, :]) * mix, lng_ref[...], lnb_ref[...])
    x1_ref[...] = x1
    u2 = x1 * (1.0 + mod_ref[0, 4:5, :]) + mod_ref[0, 3:4, :]
    u2_ref[...] = u2.astype(BF16)

    logits = _mmh_nt(wrt_ref[...], u2)
    ex = jnp.exp(logits - jnp.max(logits, axis=0, keepdims=True))
    scores = ex / jnp.sum(ex, axis=0, keepdims=True)
    sel = scores + rb_ref[...]
    row = _iota(sel.shape, 0)
    best = None
    for grp in range(N_GROUPS):
        a, b, c2, d2 = (sel[grp * EXPERTS_PER_GROUP + i:grp * EXPERTS_PER_GROUP + i + 1] for i in range(4))
        hi1, lo1, hi2, lo2 = jnp.maximum(a, b), jnp.minimum(a, b), jnp.maximum(c2, d2), jnp.minimum(c2, d2)
        top2 = jnp.maximum(hi1, hi2) + jnp.maximum(jnp.minimum(hi1, hi2), jnp.maximum(lo1, lo2))
        if best is None:
            best, best_grp = top2, jnp.zeros_like(top2, dtype=jnp.int32)
        else:
            better = top2 > best
            best = jnp.where(better, top2, best)
            best_grp = jnp.where(better, grp, best_grp)
    masked = jnp.where(row // EXPERTS_PER_GROUP == best_grp, sel, MASK_VALUE)
    _, idx1 = _first_argmax(masked, row)
    _, idx2 = _first_argmax(jnp.where(row == idx1, -jnp.inf, masked), row)
    w1 = jnp.sum(jnp.where(row == idx1, scores, 0.0), axis=0, keepdims=True)
    w2 = jnp.sum(jnp.where(row == idx2, scores, 0.0), axis=0, keepdims=True)
    total = w1 + w2

    pick1 = row == idx1
    pick2 = row == idx2
    onehot = jnp.where(pick1 | pick2, 1.0, 0.0)
    cnt = jnp.sum(onehot, axis=1, keepdims=True)
    earlier = jnp.dot(onehot.astype(BF16), tri_ref[...], preferred_element_type=F32)
    chunks = jnp.floor((cnt + (MOE_CHUNK - 1)) * (1.0 / MOE_CHUNK))
    seg_start = MOE_CHUNK * jnp.dot(tri16_ref[...], jnp.broadcast_to(chunks, (N_EXPERTS, LANES)).astype(BF16),
                                    preferred_element_type=F32)[:, 0:1]
    pos = seg_start + earlier
    pos1 = jnp.sum(jnp.where(pick1, pos, 0.0), axis=0, keepdims=True)
    pos2 = jnp.sum(jnp.where(pick2, pos, 0.0), axis=0, keepdims=True)
    route_ref[...] = jnp.concatenate([pos1, pos2, w1 / total, w2 / total, jnp.zeros((4, pos1.shape[1]), F32)], axis=0)
    cnt_ref[0] = jnp.broadcast_to(cnt, (N_EXPERTS, LANES))


def _merge(o_hg, o_da, o_rw, gates, x, mod, w_branch, w_out, ln_g, ln_b, w_router_t, router_bias, seq, tm):
    n, d = x.shape
    per_seq = seq // tm
    tile = lambda i: (i, 0)
    full = lambda i: (0, 0)
    before = (_iota((tm, tm), 0) < _iota((tm, tm), 1)).astype(BF16)
    before16 = (_iota((N_EXPERTS, N_EXPERTS), 1) < _iota((N_EXPERTS, N_EXPERTS), 0)).astype(BF16)
    return pl.pallas_call(
        _merge_body,
        grid=(n // tm,),
        in_specs=[pl.BlockSpec((tm, HG_W), tile), pl.BlockSpec((tm, DA_W), tile), pl.BlockSpec((tm, RW_W), tile),
                  pl.BlockSpec((tm, 3 * d), tile), pl.BlockSpec((tm, d), tile),
                  pl.BlockSpec((1, 6, d), lambda i: (i // per_seq, 0, 0)),
                  pl.BlockSpec(w_branch.shape, full), pl.BlockSpec(w_out.shape, full),
                  pl.BlockSpec((1, d), full), pl.BlockSpec((1, d), full),
                  pl.BlockSpec((N_EXPERTS, d), full), pl.BlockSpec((N_EXPERTS, 1), full),
                  pl.BlockSpec((tm, tm), full), pl.BlockSpec((N_EXPERTS, N_EXPERTS), full)],
        out_specs=[pl.BlockSpec((tm, d), tile), pl.BlockSpec((tm, d), tile), pl.BlockSpec((8, tm), lambda i: (0, i)),
                   pl.BlockSpec((1, N_EXPERTS, LANES), lambda i: (i, 0, 0))],
        out_shape=[jax.ShapeDtypeStruct((n, d), F32), jax.ShapeDtypeStruct((n, d), BF16),
                   jax.ShapeDtypeStruct((8, n), F32), jax.ShapeDtypeStruct((n // tm, N_EXPERTS, LANES), F32)],
        compiler_params=_params("arbitrary"),
        name="merge",
    )(o_hg, o_da, o_rw, gates, x, mod, w_branch, w_out, ln_g, ln_b, w_router_t, router_bias, before, before16)


def _local_rows(tm):
    return -(-(2 * tm + N_EXPERTS * (MOE_CHUNK - 1)) // LANES) * LANES


def _token_columns(route):
    return jnp.concatenate([route, jnp.zeros((LANES - route.shape[0], route.shape[1]), F32)], axis=0).T


def _segment_copies(i, nch_ref, loc_ref, glob_ref, local_buf, global_buf, sem, to_global):
    def run(action):
        for e in range(N_EXPERTS):
            seg = i * N_EXPERTS + e
            loc0, glob0 = loc_ref[seg], glob_ref[seg]

            def one(c, carry):
                loc = local_buf.at[pl.ds(pl.multiple_of(loc0 + c * MOE_CHUNK, MOE_CHUNK), MOE_CHUNK), :]
                glob = global_buf.at[pl.ds(pl.multiple_of(glob0 + c * MOE_CHUNK, MOE_CHUNK), MOE_CHUNK), :]
                copy = pltpu.make_async_copy(loc, glob, sem) if to_global else pltpu.make_async_copy(glob, loc, sem)
                getattr(copy, action)()
                return carry

            lax.fori_loop(0, nch_ref[seg], one, 0)
    return run


def _dispatch_body(nch_ref, loc_ref, glob_ref, u_ref, route_ref, xs_in_ref, xs_ref, stage_ref, sem):
    del xs_in_ref
    i = pl.program_id(0)
    tm, d = u_ref.shape
    route = route_ref[...]
    local_row = _iota((stage_ref.shape[0], tm), 0)
    take1 = local_row == route[0:1].astype(jnp.int32)
    take2 = local_row == route[1:2].astype(jnp.int32)
    perm = jnp.where(take1 | take2, 1.0, 0.0).astype(BF16)
    stage_ref[:, 0:d] = jnp.dot(perm, u_ref[...], preferred_element_type=F32).astype(BF16)

    cols = _token_columns(route)
    lane = _iota((tm, LANES), 1)

    def weight_cols(w):
        hi = w.astype(BF16).astype(F32)
        return jnp.where(lane == 0, hi, jnp.where(lane == 1, w - hi, 0.0)).astype(BF16)

    stage_ref[:, d:d + LANES] = (
        jnp.dot(jnp.where(take1, 1.0, 0.0).astype(BF16), weight_cols(cols[:, 2:3]), preferred_element_type=F32)
        + jnp.dot(jnp.where(take2, 1.0, 0.0).astype(BF16), weight_cols(cols[:, 3:4]), preferred_element_type=F32)
    ).astype(BF16)

    copies = _segment_copies(i, nch_ref, loc_ref, glob_ref, stage_ref, xs_ref, sem, to_global=True)
    copies("start")
    copies("wait")


def _dispatch(u2, route, nch, loc, glob, rows, tm):
    n, d = u2.shape
    width = d + LANES
    return pl.pallas_call(
        _dispatch_body,
        grid_spec=pltpu.PrefetchScalarGridSpec(
            num_scalar_prefetch=3,
            grid=(n // tm,),
            in_specs=[pl.BlockSpec((tm, d), lambda i, *_: (i, 0)),
                      pl.BlockSpec((8, tm), lambda i, *_: (0, i)),
                      pl.BlockSpec(memory_space=pl.ANY)],
            out_specs=pl.BlockSpec(memory_space=pl.ANY),
            scratch_shapes=[pltpu.VMEM((_local_rows(tm), width), BF16), pltpu.SemaphoreType.DMA],
        ),
        out_shape=jax.ShapeDtypeStruct((rows, width), BF16),
        input_output_aliases={5: 0},
        compiler_params=_params("arbitrary"),
        name="moe_dispatch",
    )(nch, loc, glob, u2, route, jnp.zeros((rows, width), BF16))


def _experts_body(te_ref, x_ref, wgu_ref, wd_ref, y_ref):
    used = te_ref[pl.program_id(0)] < N_EXPERTS

    @pl.when(used)
    def _():
        d = wgu_ref.shape[1]
        weight = x_ref[:, d:d + 1].astype(F32) + x_ref[:, d + 1:d + 2].astype(F32)
        hidden = jnp.dot(x_ref[:, 0:d], wgu_ref[0], preferred_element_type=F32)
        hg = hidden[:, :D_EXPERT]
        act = hg * _sigmoid(hg) * hidden[:, D_EXPERT:] * weight
        y_ref[...] = _mm(act, wd_ref[0]).astype(BF16)

    @pl.when(jnp.logical_not(used))
    def _():
        y_ref[...] = jnp.zeros_like(y_ref)


def _experts(xs, tile_expert, w_gu, w_down):
    rows, width = xs.shape
    d = w_down.shape[2]
    expert = lambda g, te: (jnp.minimum(te[g], N_EXPERTS - 1), 0, 0)
    return pl.pallas_call(
        _experts_body,
        grid_spec=pltpu.PrefetchScalarGridSpec(
            num_scalar_prefetch=1,
            grid=(rows // MOE_TM,),
            in_specs=[pl.BlockSpec((MOE_TM, width), lambda g, te: (g, 0)),
                      pl.BlockSpec((1, d, 2 * D_EXPERT), expert),
                      pl.BlockSpec((1, D_EXPERT, d), expert)],
            out_specs=pl.BlockSpec((MOE_TM, d), lambda g, te: (g, 0)),
        ),
        out_shape=jax.ShapeDtypeStruct((rows, d), BF16),
        compiler_params=_params("arbitrary"),
        name="moe_experts",
    )(tile_expert, xs, w_gu, w_down)


def _combine_body(nch_ref, loc_ref, glob_ref, route_ref, x1_ref, mod_ref, lng_ref, lnb_ref, ys_ref, o_ref,
                  back_ref, sem):
    i = pl.program_id(0)

    @pl.when(i == 0)
    def _():
        back_ref[...] = jnp.zeros_like(back_ref)

    copies = _segment_copies(i, nch_ref, loc_ref, glob_ref, back_ref, ys_ref, sem, to_global=False)
    copies("start")
    cols = _token_columns(route_ref[...]).astype(jnp.int32)
    local_row = _iota((x1_ref.shape[0], back_ref.shape[0]), 1)
    unperm = jnp.where((local_row == cols[:, 0:1]) | (local_row == cols[:, 1:2]), 1.0, 0.0).astype(BF16)
    copies("wait")
    ffn = jnp.dot(unperm, back_ref[...], preferred_element_type=F32)
    y = ALPHA * x1_ref[...] + (1.0 + mod_ref[0, 5:6, :]) * ffn
    o_ref[...] = _layer_norm(y, lng_ref[...], lnb_ref[...])


def _combine(ys, route, nch, loc, glob, x1, mod, ln_g, ln_b, seq, tm):
    n, d = x1.shape
    per_seq = seq // tm
    full = lambda i, *_: (0, 0)
    return pl.pallas_call(
        _combine_body,
        grid_spec=pltpu.PrefetchScalarGridSpec(
            num_scalar_prefetch=3,
            grid=(n // tm,),
            in_specs=[pl.BlockSpec((8, tm), lambda i, *_: (0, i)),
                      pl.BlockSpec((tm, d), lambda i, *_: (i, 0)),
                      pl.BlockSpec((1, 6, d), lambda i, *_: (i // per_seq, 0, 0)),
                      pl.BlockSpec((1, d), full), pl.BlockSpec((1, d), full),
                      pl.BlockSpec(memory_space=pl.ANY)],
            out_specs=pl.BlockSpec((tm, d), lambda i, *_: (i, 0)),
            scratch_shapes=[pltpu.VMEM((_local_rows(tm), d), BF16), pltpu.SemaphoreType.DMA],
        ),
        out_shape=jax.ShapeDtypeStruct((n, d), F32),
        compiler_params=_params("arbitrary"),
        name="moe_combine",
    )(nch, loc, glob, route, x1, mod, ln_g, ln_b, ys)


def _moe(u2, route, counts, w_gu, w_down, x1, mod, ln_g, ln_b, seq, tm):
    n = u2.shape[0]
    n_tiles = n // tm
    seg_rows = (counts + MOE_CHUNK - 1) // MOE_CHUNK * MOE_CHUNK
    loc = jnp.cumsum(seg_rows, axis=1) - seg_rows
    region = (jnp.sum(seg_rows, axis=0) + MOE_TM - 1) // MOE_TM * MOE_TM
    region_end = jnp.cumsum(region)
    glob = (region_end - region)[None, :] + jnp.cumsum(seg_rows, axis=0) - seg_rows
    rows = -(-(2 * n + n_tiles * N_EXPERTS * (MOE_CHUNK - 1) + N_EXPERTS * (MOE_TM - 1)) // MOE_TM) * MOE_TM
    tile_expert = jnp.sum(jnp.arange(rows // MOE_TM, dtype=jnp.int32)[:, None] * MOE_TM >= region_end[None, :],
                          axis=1).astype(jnp.int32)
    flat = lambda a: a.reshape(-1).astype(jnp.int32)
    nch, loc, glob = flat(seg_rows // MOE_CHUNK), flat(loc), flat(glob)
    xs = _dispatch(u2, route, nch, loc, glob, rows, tm)
    ys = _experts(xs, tile_expert, w_gu, w_down)
    return _combine(ys, route, nch, loc, glob, x1, mod, ln_g, ln_b, seq, tm)


def _tiles(seq):
    return min(512, seq), min(256, seq)


def kernel(x, c, w_ada, b_ada, w_in, hg_lb_logits, hg_norm_g, da_lambda, da_subln_g, rw_mu, rw_w0, rw_w_up,
           rw_a0, rw_a_up, rw_g_up, rw_k_k, rw_k_a, rw_r_k, rw_gn_g, rw_gn_b, w_merge, b_merge, w_branch, w_out,
           ln_g, ln_b, w_router, router_bias, w_exp_gate, w_exp_up, w_exp_down):
    bsz, seq, d = x.shape
    depth = w_in.shape[0]
    n = bsz * seq
    tm, blk = _tiles(seq)

    sm = jax.nn.softmax(hg_lb_logits.astype(F32), axis=0)
    hg_lb = jnp.cumsum(sm, axis=0) - sm[0:1]
    slopes = jnp.asarray([2.0 ** (-8.0 * (h + 1) / DA_HEADS) for h in range(DA_HEADS)], F32)

    mod_all = _ada(c, w_ada, b_ada).reshape(depth, bsz, 6, d)
    w_router_t = w_router.T
    router_bias = router_bias.reshape(N_EXPERTS, 1)

    xf = x.reshape(n, d)
    for l in range(depth):
        mod = mod_all[l]
        lq1, lk1, lq2, lk2 = da_lambda[l].astype(F32)
        lam_init = 0.8 - 0.6 * math.exp(-0.3 * l)
        lam = jnp.exp(jnp.sum(lq1 * lk1)) - jnp.exp(jnp.sum(lq2 * lk2)) + lam_init
        scal = jnp.concatenate([jnp.stack([lam, jnp.asarray(1.0 - lam_init, F32)]), slopes])

        w_gates = jnp.concatenate([w_merge[l, br] for br in range(3)], axis=1).astype(BF16)
        y, gates = _proj(xf, mod, w_in[l].astype(BF16), w_gates, b_merge[l].reshape(1, 3 * d), seq, tm)

        o_hg = _hgrn2(y, hg_lb[l].reshape(1, HG_W), hg_norm_g[l].reshape(1, HG_DV), bsz, seq)
        o_da = _diffattn(y, scal, da_subln_g[l].reshape(1, DA_DV), bsz, seq, blk)
        rw = dict(mu=rw_mu[l].reshape(1, -1), w0=rw_w0[l].reshape(1, -1), w_up=rw_w_up[l],
                  a0=rw_a0[l].reshape(1, -1), a_up=rw_a_up[l], g_up=rw_g_up[l],
                  k_k=rw_k_k[l].reshape(1, -1), k_a=rw_k_a[l].reshape(1, -1), r_k=rw_r_k[l].reshape(1, -1),
                  gn_g=rw_gn_g[l].reshape(1, -1), gn_b=rw_gn_b[l].reshape(1, -1))
        o_rw = _rwkv(y, rw, bsz, seq)

        x1, u2, route, counts = _merge(o_hg, o_da, o_rw, gates, xf, mod, w_branch[l].astype(BF16),
                                       w_out[l].astype(BF16), ln_g[l, 0].reshape(1, d), ln_b[l, 0].reshape(1, d),
                                       w_router_t, router_bias, seq, tm)
        w_gu = jnp.concatenate([w_exp_gate[l], w_exp_up[l]], axis=-1).astype(BF16)
        xf = _moe(u2, route, counts[:, :, 0].astype(jnp.int32), w_gu, w_exp_down[l].astype(BF16), x1, mod,
                  ln_g[l, 1].reshape(1, d), ln_b[l, 1].reshape(1, d), seq, tm)
    return xf.reshape(bsz, seq, d)
```

```python
import functools
import math

import jax
import jax.numpy as jnp
from jax import lax
from jax.experimental import pallas as pl
from jax.experimental.pallas import tpu as pltpu

D_MODEL = 1024
DEPTH = 4
HG_HEADS, HG_DK, HG_DV, HG_CHUNK, HG_SUB = 4, 128, 128, 64, 16
HG_W = HG_HEADS * HG_DV
HG_F_MIN = 1e-6
DA_HEADS, DA_DQK = 4, 64
DA_DV = 2 * DA_DQK
DA_W = DA_HEADS * DA_DV
MASK_VALUE = -1e30
LOG2E = math.log2(math.e)
PROJ_TN = 768
RW_HEADS, RW_DH, RW_CHUNK, RW_SUB = 8, 64, 64, 16
RW_STEP_CHUNKS = 2
RW_STEP_SEQS = 2
RW_W = RW_HEADS * RW_DH
RW_IN_W = 1792
RW_GN_EPS = 64e-5
IN_W = 5376
HG_COL, DA_COL, RW_COL = 0, 2048, 3584
N_EXPERTS, N_GROUPS, EXPERTS_PER_GROUP, D_EXPERT = 16, 4, 4, 512
MOE_CHUNK = 16
MOE_TM = 512
ALPHA = (2.0 * DEPTH) ** 0.25
LN_EPS = 1e-5
RMS_EPS = 1e-6
LANES = 128

F32 = jnp.float32
BF16 = jnp.bfloat16
HIGHEST = lax.Precision.HIGHEST
VMEM_LIMIT = 48 * 1024 * 1024

_NT = (((1,), (1,)), ((), ()))
_TN = (((0,), (0,)), ((), ()))


def _mm(a, b):
    return jnp.dot(a.astype(BF16), b.astype(BF16), preferred_element_type=F32)


def _mm_nt(a, b):
    return lax.dot_general(a.astype(BF16), b.astype(BF16), _NT, preferred_element_type=F32)


def _mm_tn(a, b):
    return lax.dot_general(a.astype(BF16), b.astype(BF16), _TN, preferred_element_type=F32)


def _mmh(a, b):
    return jnp.dot(a, b, precision=HIGHEST, preferred_element_type=F32)


def _mmh_nt(a, b):
    return lax.dot_general(a, b, _NT, precision=HIGHEST, preferred_element_type=F32)


def _seg_sum(x, seg):
    rows = x.shape[0]
    hi = x.astype(BF16)
    lo = (x - hi.astype(F32)).astype(BF16)
    halves = []
    for c0 in range(0, x.shape[1], seg.shape[0]):
        cols = slice(c0, c0 + seg.shape[0])
        both = jnp.dot(jnp.concatenate([hi[:, cols], lo[:, cols]], axis=0), seg, preferred_element_type=F32)
        halves.append(both[:rows] + both[rows:])
    return jnp.concatenate(halves, axis=1)


def _split_mm(x, w_twice, w_lo):
    hi = x.astype(BF16)
    lo = (x - hi.astype(F32)).astype(BF16)
    return (jnp.dot(jnp.concatenate([hi, lo], axis=1), w_twice, preferred_element_type=F32)
            + jnp.dot(hi, w_lo, preferred_element_type=F32))


def _chunk_cumsum(x):
    c = x.shape[0]
    hi = x.astype(BF16)
    rest = x - hi.astype(F32)
    mid = rest.astype(BF16)
    lo = (rest - mid.astype(F32)).astype(BF16)
    col = _iota((c, 4 * c), 1)
    tri = ((col & (c - 1)) <= _iota((c, 4 * c), 0)) & (col < 3 * c)
    return jnp.dot(jnp.where(tri, 1.0, 0.0).astype(BF16), jnp.concatenate([hi, mid, lo, lo], axis=0),
                   preferred_element_type=F32)


def _sigmoid(x):
    return 1.0 / (1.0 + jnp.exp(-x))


def _softplus(x):
    return jnp.maximum(x, 0.0) + jnp.log(1.0 + jnp.exp(-jnp.abs(x)))


def _iota(shape, dim):
    return lax.broadcasted_iota(jnp.int32, shape, dim)


def _params(*sem):
    return pltpu.CompilerParams(dimension_semantics=sem, vmem_limit_bytes=VMEM_LIMIT)


def _layer_norm(y, g, b):
    mu = jnp.mean(y, axis=-1, keepdims=True)
    d = y - mu
    var = jnp.mean(d * d, axis=-1, keepdims=True)
    return d * lax.rsqrt(var + LN_EPS) * g + b


def _ada_body(c_ref, w_ref, b_ref, o_ref):
    c = c_ref[...]
    o_ref[0] = _mmh(c * _sigmoid(c), w_ref[0]) + b_ref[0]


def _ada(c, w_ada, b_ada):
    depth, d, _ = w_ada.shape
    bsz = c.shape[0]
    return pl.pallas_call(
        _ada_body,
        grid=(depth, 6),
        in_specs=[pl.BlockSpec((bsz, d), lambda l, j: (0, 0)),
                  pl.BlockSpec((1, d, d), lambda l, j: (l, 0, j)),
                  pl.BlockSpec((1, 1, d), lambda l, j: (l, 0, j))],
        out_specs=pl.BlockSpec((1, bsz, d), lambda l, j: (l, 0, j)),
        out_shape=jax.ShapeDtypeStruct((depth, bsz, 6 * d), F32),
        compiler_params=_params("arbitrary", "arbitrary"),
        name="ada",
    )(c, w_ada, b_ada.reshape(depth, 1, 6 * d))


def _proj_body(x_ref, mod_ref, win_ref, wg_ref, bg_ref, y_ref, g_ref):
    u = (x_ref[...] * (1.0 + mod_ref[0, 1:2, :]) + mod_ref[0, 0:1, :]).astype(BF16)
    for c0 in range(0, IN_W, PROJ_TN):
        cols = slice(c0, c0 + PROJ_TN)
        y_ref[:, cols] = jnp.dot(u, win_ref[:, cols], preferred_element_type=F32).astype(BF16)
    for c0 in range(0, 3 * D_MODEL, PROJ_TN):
        cols = slice(c0, c0 + PROJ_TN)
        g = jnp.dot(u, wg_ref[:, cols], preferred_element_type=F32) + bg_ref[:, cols]
        g_ref[:, cols] = _sigmoid(g).astype(BF16)


def _proj(x, mod, w_in, w_gates, b_gates, seq, tm):
    n, d = x.shape
    per_seq = seq // tm
    tile = lambda i: (i, 0)
    resident = dict(index_map=lambda i: (0, 0), pipeline_mode=pl.Buffered(1))
    return pl.pallas_call(
        _proj_body,
        grid=(n // tm,),
        in_specs=[pl.BlockSpec((tm, d), tile),
                  pl.BlockSpec((1, 6, d), lambda i: (i // per_seq, 0, 0)),
                  pl.BlockSpec(w_in.shape, **resident),
                  pl.BlockSpec(w_gates.shape, **resident),
                  pl.BlockSpec(b_gates.shape, **resident)],
        out_specs=[pl.BlockSpec((tm, IN_W), tile), pl.BlockSpec((tm, 3 * d), tile)],
        out_shape=[jax.ShapeDtypeStruct((n, IN_W), BF16), jax.ShapeDtypeStruct((n, 3 * d), BF16)],
        compiler_params=_params("arbitrary"),
        name="proj",
    )(x, mod, w_in, w_gates, b_gates)


def _hgrn2_body(y_ref, lb_ref, ng_ref, o_ref, st_ref):
    c, sub = HG_CHUNK, HG_SUB

    @pl.when(pl.program_id(1) == 0)
    def _():
        st_ref[...] = jnp.zeros_like(st_ref)

    q = y_ref[:, 0:HG_W].astype(F32)
    z = y_ref[:, HG_W:2 * HG_W].astype(F32)
    v = y_ref[:, 2 * HG_W:3 * HG_W]
    og = y_ref[:, 3 * HG_W:4 * HG_W].astype(F32)
    lb = lb_ref[...]
    f = lb + (1.0 - lb) * _sigmoid(z)
    kin = (1.0 - lb) * _sigmoid(-z)
    b = _chunk_cumsum(jnp.log(jnp.maximum(f, HG_F_MIN)))
    log_k = jnp.log(kin)
    rel = log_k - b
    b_last = b[c - 1:c]
    q_dec = (q * jnp.exp(b)).astype(BF16)
    k_tail = (kin * jnp.exp(b_last - b)).astype(BF16)
    decay = jnp.exp(b_last)

    ones = jnp.ones((HG_DK, LANES), BF16)
    row_s = _iota((sub, c), 0)
    col_s = _iota((sub, c), 1)
    heads = range(HG_HEADS)
    blks = range(c // sub)
    hs = [slice(h * HG_DK, (h + 1) * HG_DK) for h in heads]

    diag = {}
    for h in heads:
        for blk in blks:
            rows = slice(blk * sub, (blk + 1) * sub)
            b_i, q_i, rel_i, lk_i = b[rows, hs[h]], q[rows, hs[h]], rel[rows, hs[h]], log_k[rows, hs[h]]
            terms = [q_i * jnp.exp(jnp.minimum(b_i + rel_i[s:s + 1], lk_i[s:s + 1])) for s in range(sub)]
            w = jnp.concatenate(terms, axis=0).astype(BF16)
            diag[h, blk] = jnp.dot(w, ones, preferred_element_type=F32)
    below = {}
    for h in heads:
        for blk in blks[1:]:
            r0 = blk * sub
            beta = b[r0 - 1:r0, hs[h]]
            q_t = q[r0:r0 + sub, hs[h]] * jnp.exp(b[r0:r0 + sub, hs[h]] - beta)
            k_h = kin[:, hs[h]] * jnp.exp(jnp.minimum(beta - b[:, hs[h]], 0.0))
            below[h, blk] = _mm_nt(q_t, k_h)
    scores = []
    for h in heads:
        a_rows = []
        for blk in blks:
            r0 = blk * sub
            a_blk = jnp.zeros((sub, c), F32)
            for s in range(sub):
                a_blk = jnp.where(col_s == r0 + s, diag[h, blk][s * sub:(s + 1) * sub, :c], a_blk)
            if blk > 0:
                a_blk = jnp.where(col_s < r0, below[h, blk], a_blk)
            a_rows.append(jnp.where(col_s <= row_s + r0, a_blk, 0.0))
        scores.append(jnp.concatenate(a_rows, axis=0).astype(BF16))

    st = [st_ref[h] for h in heads]
    intra = [jnp.dot(scores[h], v[:, hs[h]], preferred_element_type=F32) for h in heads]
    inter = [lax.dot_general(q_dec[:, hs[h]], st[h].astype(BF16), _NT, preferred_element_type=F32) for h in heads]
    for h in heads:
        st_ref[h] = st[h] * decay[:, hs[h]] + lax.dot_general(v[:, hs[h]], k_tail[:, hs[h]], _TN,
                                                             preferred_element_type=F32)
    for h in heads:
        o = intra[h] + inter[h]
        o = o * lax.rsqrt(jnp.mean(o * o, axis=-1, keepdims=True) + RMS_EPS) * ng_ref[...]
        o_ref[:, hs[h]] = (o * (og[:, hs[h]] * _sigmoid(og[:, hs[h]]))).astype(o_ref.dtype)


def _hgrn2(y, lb, norm_g, bsz, seq):
    n = y.shape[0]
    nc = seq // HG_CHUNK
    width = 4 * HG_W
    return pl.pallas_call(
        _hgrn2_body,
        grid=(bsz, nc),
        in_specs=[pl.BlockSpec((HG_CHUNK, width), lambda b, c: (b * nc + c, HG_COL // width)),
                  pl.BlockSpec((1, HG_W), lambda b, c: (0, 0)),
                  pl.BlockSpec((1, HG_DV), lambda b, c: (0, 0))],
        out_specs=pl.BlockSpec((HG_CHUNK, HG_W), lambda b, c: (b * nc + c, 0)),
        out_shape=jax.ShapeDtypeStruct((n, HG_W), BF16),
        scratch_shapes=[pltpu.VMEM((HG_HEADS, HG_DV, HG_DK), F32)],
        compiler_params=_params("arbitrary", "arbitrary"),
        name="hgrn2",
    )(y, lb, norm_g)


def _diffattn_body(scal_ref, q_ref, k_ref, vt_ref, g_ref, o_ref, qq_ref, sa_ref, sb_ref, p_ref, m_ref, l_ref,
                   sc_ref, acc_ref, *, blk):
    h = pl.program_id(1)
    i = pl.program_id(2)
    lam = scal_ref[0]
    out_scale = scal_ref[1]
    slope = scal_ref[2 + h] * LOG2E
    q0 = i * blk

    q = q_ref[...].astype(F32) * (DA_DQK ** -0.5 * LOG2E)
    lane = _iota(q.shape, 1)
    stacked = jnp.concatenate([jnp.where(lane < DA_DQK, q, 0.0), jnp.where(lane >= DA_DQK, q, 0.0)], axis=0)
    qq_ref[...] = stacked.T.astype(BF16)
    m_ref[...] = jnp.full(m_ref.shape, MASK_VALUE, F32)
    l_ref[...] = jnp.zeros(l_ref.shape, F32)
    sc_ref[...] = jnp.ones(sc_ref.shape, F32)
    acc_ref[...] = jnp.zeros(acc_ref.shape, F32)
    p_ref[...] = jnp.zeros(p_ref.shape, BF16)
    key_off = _iota((blk, LANES), 0)
    rel = slope * key_off.astype(F32)

    def scores(j):
        return jnp.dot(k_ref[pl.ds(j * blk, blk), :], qq_ref[...], preferred_element_type=F32)

    def softmax(j, src_ref, masked):
        bias = rel + slope * (j * blk - q0).astype(F32)
        for t in range(2 * blk // LANES):
            cols = slice(t * LANES, (t + 1) * LANES)
            s = src_ref[:, cols] + bias
            if masked:
                s = jnp.where(key_off <= ((_iota((blk, LANES), 1) + t * LANES) & (blk - 1)), s, MASK_VALUE)
            m_old = m_ref[:, cols]
            m_new = jnp.maximum(m_old, jnp.max(s, axis=0, keepdims=True))
            p = jnp.exp2(s - m_new)
            sc = jnp.exp2(m_old - m_new)
            l_ref[:, cols] = sc * l_ref[:, cols] + jnp.sum(p, axis=0, keepdims=True)
            p_ref[:, cols] = p.astype(BF16)
            sc_ref[:, cols] = sc
            m_ref[:, cols] = m_new

    def iteration(j, src_ref, dst_ref):
        sc_prev = sc_ref[...]
        pv = jnp.dot(vt_ref[jnp.maximum(j - 1, 0)], p_ref[...], preferred_element_type=F32)
        if dst_ref is not None:
            dst_ref[...] = scores(j + 1)
        softmax(j, src_ref, dst_ref is None)
        acc_ref[...] = acc_ref[...] * sc_prev + pv

    def body(pair, carry):
        iteration(2 * pair, sa_ref, sb_ref)
        iteration(2 * pair + 1, sb_ref, sa_ref)
        return carry

    sa_ref[...] = scores(0)
    lax.fori_loop(0, i // 2, body, 0)

    @pl.when(i % 2 == 0)
    def _():
        iteration(i, sa_ref, None)

    @pl.when(i % 2 == 1)
    def _():
        iteration(i - 1, sa_ref, sb_ref)
        iteration(i, sb_ref, None)

    acc = acc_ref[...] * sc_ref[...] + jnp.dot(vt_ref[i], p_ref[...], preferred_element_type=F32)
    o = acc / l_ref[...]
    d = o[:, :blk] - lam * o[:, blk:]
    g = jnp.concatenate([g_ref[...]] * (blk // LANES), axis=1)
    d = d * lax.rsqrt(jnp.mean(d * d, axis=0, keepdims=True) + RMS_EPS) * g * out_scale
    o_ref[...] = d.T.astype(o_ref.dtype)


def _diffattn(y, scal, subln_g, bsz, seq, blk):
    n = y.shape[0]
    nq = seq // blk
    qc, kc = DA_COL // DA_DV, (DA_COL + DA_W) // DA_DV
    v_t = y[:, DA_COL + 2 * DA_W:DA_COL + 3 * DA_W].reshape(bsz, nq, blk, DA_HEADS, DA_DV)
    v_t = v_t.transpose(0, 3, 1, 4, 2).reshape(bsz * DA_HEADS * nq, DA_DV, blk)
    g_col = jnp.broadcast_to(subln_g.reshape(DA_DV, 1), (DA_DV, LANES))
    row = (1, 2 * blk)
    return pl.pallas_call(
        functools.partial(_diffattn_body, blk=blk),
        scratch_shapes=[pltpu.VMEM((DA_DV, 2 * blk), BF16), pltpu.VMEM((blk, 2 * blk), F32),
                        pltpu.VMEM((blk, 2 * blk), F32), pltpu.VMEM((blk, 2 * blk), BF16), pltpu.VMEM(row, F32), pltpu.VMEM(row, F32),
                        pltpu.VMEM(row, F32), pltpu.VMEM((DA_DV, 2 * blk), F32)],
        grid=(bsz, DA_HEADS, nq),
        in_specs=[pl.BlockSpec(memory_space=pltpu.SMEM),
                  pl.BlockSpec((blk, DA_DV), lambda b, h, i: (b * nq + i, qc + h)),
                  pl.BlockSpec((seq, DA_DV), lambda b, h, i: (b, kc + h)),
                  pl.BlockSpec((nq, DA_DV, blk), lambda b, h, i: (b * DA_HEADS + h, 0, 0)),
                  pl.BlockSpec((DA_DV, LANES), lambda b, h, i: (0, 0))],
        out_specs=pl.BlockSpec((blk, DA_DV), lambda b, h, i: (b * nq + i, h)),
        out_shape=jax.ShapeDtypeStruct((n, DA_W), BF16),
        compiler_params=_params("arbitrary", "arbitrary", "arbitrary"),
        name="diffattn",
    )(scal, y, y, v_t, g_col)


def _split_f32(x):
    hi = x.astype(BF16)
    hi_f = hi.astype(F32)
    return hi, hi_f, x - hi_f


def _dup_lhs(hi_f, lo_f, low_half):
    packed = jnp.where(low_half, hi_f, lo_f).astype(BF16)
    return jnp.concatenate([packed, packed], axis=1)


def _dup_rhs(hi, lo_f):
    lo = lo_f.astype(BF16)
    return jnp.concatenate([hi, hi, lo, lo], axis=0)


def _rwkv_body(y_ref, mu_ref, w0_ref, a0_ref, wa2_ref, walo_ref, gu2_ref, gulo_ref, kk_ref, ka_ref, rk_ref,
               gng_ref, gnb_ref, seg_ref, o_ref, st_ref, prev_ref, osc_ref):
    c, sub, dh = RW_CHUNK, RW_SUB, RW_DH

    @pl.when(pl.program_id(1) == 0)
    def _():
        st_ref[...] = jnp.zeros_like(st_ref)
        prev_ref[...] = jnp.zeros_like(prev_ref)

    n_seq, seq_rows = y_ref.shape[0], y_ref.shape[1]
    rows = n_seq * seq_rows
    shifted = []
    for si in range(n_seq):
        x = y_ref[si].astype(F32)
        x_prev = jnp.where(_iota(x.shape, 0) == 0, prev_ref[si:si + 1], pltpu.roll(x, 1, axis=0))
        prev_ref[si:si + 1] = x[seq_rows - 1:seq_rows]
        shifted.append(x + (x_prev - x) * mu_ref[...])
    xs = jnp.concatenate(shifted, axis=0)
    r = xs[:, 0:RW_W]
    k = xs[:, RW_W:2 * RW_W]
    v = xs[:, 2 * RW_W:3 * RW_W]
    wa = xs[:, 3 * RW_W:3 * RW_W + LANES]
    gd = xs[:, 3 * RW_W + LANES:RW_IN_W]

    lora = _split_mm(jnp.where(_iota(wa.shape, 1) < 64, jnp.tanh(wa), wa), wa2_ref[...], walo_ref[...])
    w_log = -_softplus(-(w0_ref[...] + lora[:, :RW_W])) - 0.5
    g = -jnp.exp(w_log)
    a = _sigmoid(a0_ref[...] + lora[:, RW_W:])
    gate = _split_mm(_sigmoid(gd), gu2_ref[...], gulo_ref[...])
    seg = seg_ref[...]
    kk = k * kk_ref[...]
    k2 = k * (1.0 + (a - 1.0) * ka_ref[...])
    sums = _seg_sum(jnp.concatenate([kk * kk, r * k2 * rk_ref[...]], axis=0), seg)
    kk = kk * lax.rsqrt(jnp.maximum(sums[:rows], 1e-12))
    bb = kk * a
    bonus = sums[rows:] * v

    chunks = range(rows // c)
    cr = [slice(ci * c, (ci + 1) * c) for ci in chunks]
    gc = jnp.concatenate([_chunk_cumsum(g[s]) for s in cr], axis=0)
    g_last = [gc[s][c - 1:c] for s in cr]
    e_inv = jnp.exp(-gc)
    e_tail = jnp.exp(jnp.concatenate([jnp.broadcast_to(gl, (c, RW_W)) for gl in g_last], axis=0) - gc)
    gam = [jnp.exp(gl) for gl in g_last]
    a_t = (-kk * jnp.exp(gc - g)).astype(BF16)
    r_t = (r * jnp.exp(gc)).astype(BF16)
    b_h = (bb * e_inv).astype(BF16)
    k_h = (k2 * e_inv).astype(BF16)
    k_bar = (k2 * e_tail).astype(BF16)
    b_bar = (bb * e_tail).astype(BF16)
    v_bf = v.astype(BF16)

    row2 = _iota((c, 2 * c), 0)
    lane2 = _iota((c, 2 * c), 1)
    col2 = lane2 & (c - 1)
    low_half = lane2 < c
    strict = row2 > col2
    incl = row2 >= col2
    same_blk = (row2 // sub) == (col2 // sub)
    eye = (row2 == col2).astype(F32)

    def dot(p, q):
        return jnp.dot(p, q, preferred_element_type=F32)

    heads = range(RW_HEADS)
    sls = [slice(h * dh, (h + 1) * dh) for h in heads]
    items = [(ci, h) for ci in chunks for h in heads]
    ar_h = [jnp.concatenate([a_t[cr[ci], sls[h]], r_t[cr[ci], sls[h]]], axis=0) for ci, h in items]
    quad = [lax.dot_general(x, jnp.concatenate([b_h[cr[ci], sls[h]], k_h[cr[ci], sls[h]]], axis=0), _NT,
                            preferred_element_type=F32)
            for x, (ci, h) in zip(ar_h, items)]
    top = [jnp.where(strict, q[:c], 0.0) for q in quad]
    a_ak = [t[:, c:].astype(BF16) for t in top]
    a_r = [jnp.where(incl, q[c:], 0.0).astype(BF16) for q in quad]
    a_ab = [jnp.where(low_half, t, pltpu.roll(t, c, axis=1)) for t in top]
    a_d = [jnp.where(same_blk, x, 0.0) for x in a_ab]
    a_o = [(x - y).astype(BF16) for x, y in zip(a_ab, a_d)]

    s1 = [_split_f32(x) for x in a_d]
    p2 = [dot(_dup_lhs(hf, lf, low_half), _dup_rhs(hi, lf)) for hi, hf, lf in s1]
    s2 = [_split_f32(x) for x in p2]
    rhs2 = [_dup_rhs(hi, lf) for hi, _, lf in s2]
    p4 = [dot(_dup_lhs(hf, lf, low_half), rhs) for (_, hf, lf), rhs in zip(s2, rhs2)]
    s4 = [_split_f32(x) for x in p4]
    rhs4 = [_dup_rhs(hi, lf) for hi, _, lf in s4]
    p8 = [dot(_dup_lhs(hf, lf, low_half), rhs) for (_, hf, lf), rhs in zip(s4, rhs4)]
    rhs8 = [_dup_rhs(hi, lf) for hi, _, lf in (_split_f32(x) for x in p8)]
    t_d = [eye + x for x in a_d]
    for rhs_all in (rhs2, rhs4, rhs8):
        st = [_split_f32(x) for x in t_d]
        t_d = [x + dot(_dup_lhs(hf, lf, low_half), rhs) for x, (_, hf, lf), rhs in zip(t_d, st, rhs_all)]
    t_d = [x.astype(BF16) for x in t_d]

    nn = [dot(t[:, :c], x) for t, x in zip(t_d, a_o)]
    nn_bf = [x.astype(BF16) for x in nn]
    n2 = [dot(x[:, :c], x) for x in nn_bf]
    n3 = [dot(x[:, :c], y.astype(BF16)) for x, y in zip(nn_bf, n2)]
    t_m = [dot((eye + x + y + z).astype(BF16)[:, :c], t).astype(BF16)[:, :c]
           for x, y, z, t in zip(nn, n2, n3, t_d)]

    v_h = [v_bf[cr[ci], sls[h]] for ci, h in items]
    akv = [dot(x, y).astype(BF16) for x, y in zip(a_ak, v_h)]
    at_m = [dot(t, x[:c]).astype(BF16) for t, x in zip(t_m, ar_h)]
    v_p = [dot(t, x) for t, x in zip(t_m, akv)]

    per_seq = seq_rows // c
    chains = [(si, h) for si in range(n_seq) for h in heads]
    state = [st_ref[si, h] for si, h in chains]
    for t in range(per_seq):
        it = [(si * per_seq + t) * RW_HEADS + h for si, h in chains]
        ck = [cr[si * per_seq + t] for si, _ in chains]
        proj = [lax.dot_general(jnp.concatenate([at_m[j], ar_h[j][c:]], axis=0), s.astype(BF16), _NT,
                                preferred_element_type=F32) for j, s in zip(it, state)]
        u = [(p[:c] + v_p[j]).astype(BF16) for p, j in zip(proj, it)]
        for p, uu, j, rws, (_, h) in zip(proj, u, it, ck, chains):
            osc_ref[rws, sls[h]] = p[c:] + dot(a_r[j], jnp.concatenate([uu, v_h[j]], axis=0))
        state = [s * gam[j // RW_HEADS][:, sls[h]] + lax.dot_general(
            jnp.concatenate([v_h[j], uu], axis=0),
            jnp.concatenate([k_bar[rws, sls[h]], b_bar[rws, sls[h]]], axis=0), _TN, preferred_element_type=F32)
            for s, uu, j, rws, (_, h) in zip(state, u, it, ck, chains)]
    for s, (si, h) in zip(state, chains):
        st_ref[si, h] = s

    o = osc_ref[...]
    mean = _seg_sum(o, seg) * (1.0 / dh)
    d = o - mean
    var = _seg_sum(d * d, seg) * (1.0 / dh)
    o = d * lax.rsqrt(var + RW_GN_EPS) * gng_ref[...] + gnb_ref[...]
    o = ((o + bonus) * gate).astype(o_ref.dtype)
    for si in range(n_seq):
        o_ref[si] = o[si * seq_rows:(si + 1) * seq_rows]


def _rwkv(y, p, bsz, seq):
    n = y.shape[0]
    half = RW_W // 2
    seg = (_iota((half, half), 0) // RW_DH == _iota((half, half), 1) // RW_DH).astype(BF16)

    def two_terms(w):
        hi = w.astype(BF16)
        return jnp.concatenate([hi, hi], axis=0), (w - hi.astype(F32)).astype(BF16)

    zeros = jnp.zeros_like(p["w_up"])
    wa2, wa_lo = two_terms(jnp.concatenate([jnp.concatenate([p["w_up"], zeros], axis=1),
                                            jnp.concatenate([zeros, p["a_up"]], axis=1)], axis=0))
    gu2, gu_lo = two_terms(p["g_up"])
    rows = [p["mu"], p["w0"], p["a0"], wa2, wa_lo, gu2, gu_lo, p["k_k"], p["k_a"], p["r_k"],
            p["gn_g"], p["gn_b"], seg]
    full = lambda b, c: (0, 0)
    step = RW_STEP_CHUNKS * RW_CHUNK
    n_seq = RW_STEP_SEQS if bsz % RW_STEP_SEQS == 0 else 1
    out = pl.pallas_call(
        _rwkv_body,
        grid=(bsz // n_seq, seq // step),
        in_specs=[pl.BlockSpec((n_seq, step, RW_IN_W), lambda b, c: (b, c, RW_COL // RW_IN_W))]
        + [pl.BlockSpec(a.shape, full) for a in rows],
        out_specs=pl.BlockSpec((n_seq, step, RW_W), lambda b, c: (b, c, 0)),
        out_shape=jax.ShapeDtypeStruct((bsz, seq, RW_W), BF16),
        scratch_shapes=[pltpu.VMEM((n_seq, RW_HEADS, RW_DH, RW_DH), F32),
                        pltpu.VMEM((n_seq, RW_IN_W), F32),
                        pltpu.VMEM((n_seq * step, RW_W), F32)],
        compiler_params=_params("arbitrary", "arbitrary"),
        name="rwkv7",
    )(y.reshape(bsz, seq, -1), *rows)
    return out.reshape(n, RW_W)


def _first_argmax(vals, row):
    top = jnp.max(vals, axis=0, keepdims=True)
    idx = jnp.min(jnp.where(vals == top, row, N_EXPERTS), axis=0, keepdims=True)
    return top, idx


def _merge_body(ohg_ref, oda_ref, orw_ref, gt_ref, x_ref, mod_ref, wb_ref, wo_ref, lng_ref, lnb_ref,
                wrt_ref, rb_ref, tri_ref, tri16_ref, x1_ref, u2_ref, route_ref, cnt_ref):
    d = D_MODEL
    merged = (gt_ref[:, 0:d].astype(F32) * jnp.dot(ohg_ref[...], wb_ref[0:HG_W, :], preferred_element_type=F32)
              + gt_ref[:, d:2 * d].astype(F32)
              * jnp.dot(oda_ref[...], wb_ref[HG_W:HG_W + DA_W, :], preferred_element_type=F32)
              + gt_ref[:, 2 * d:3 * d].astype(F32)
              * jnp.dot(orw_ref[...], wb_ref[HG_W + DA_W:, :], preferred_element_type=F32))
    mix = _mm(merged, wo_ref[...])
    x1 = _layer_norm(ALPHA * x_ref[...] + (1.0 + mod_ref[0, 2:3, :]) * mix, lng_ref[...], lnb_ref[...])
    x1_ref[...] = x1
    u2 = x1 * (1.0 + mod_ref[0, 4:5, :]) + mod_ref[0, 3:4, :]
    u2_ref[...] = u2.astype(BF16)

    logits = _mmh_nt(wrt_ref[...], u2)
    ex = jnp.exp(logits - jnp.max(logits, axis=0, keepdims=True))
    scores = ex / jnp.sum(ex, axis=0, keepdims=True)
    sel = scores + rb_ref[...]
    row = _iota(sel.shape, 0)
    best = None
    for grp in range(N_GROUPS):
        a, b, c2, d2 = (sel[grp * EXPERTS_PER_GROUP + i:grp * EXPERTS_PER_GROUP + i + 1] for i in range(4))
        hi1, lo1, hi2, lo2 = jnp.maximum(a, b), jnp.minimum(a, b), jnp.maximum(c2, d2), jnp.minimum(c2, d2)
        top2 = jnp.maximum(hi1, hi2) + jnp.maximum(jnp.minimum(hi1, hi2), jnp.maximum(lo1, lo2))
        if best is None:
            best, best_grp = top2, jnp.zeros_like(top2, dtype=jnp.int32)
        else:
            better = top2 > best
            best = jnp.where(better, top2, best)
            best_grp = jnp.where(better, grp, best_grp)
    masked = jnp.where(row // EXPERTS_PER_GROUP == best_grp, sel, MASK_VALUE)
    _, idx1 = _first_argmax(masked, row)
    _, idx2 = _first_argmax(jnp.where(row == idx1, -jnp.inf, masked), row)
    w1 = jnp.sum(jnp.where(row == idx1, scores, 0.0), axis=0, keepdims=True)
    w2 = jnp.sum(jnp.where(row == idx2, scores, 0.0), axis=0, keepdims=True)
    total = w1 + w2

    pick1 = row == idx1
    pick2 = row == idx2
    onehot = jnp.where(pick1 | pick2, 1.0, 0.0)
    cnt = jnp.sum(onehot, axis=1, keepdims=True)
    earlier = jnp.dot(onehot.astype(BF16), tri_ref[...], preferred_element_type=F32)
    chunks = jnp.floor((cnt + (MOE_CHUNK - 1)) * (1.0 / MOE_CHUNK))
    seg_start = MOE_CHUNK * jnp.dot(tri16_ref[...], jnp.broadcast_to(chunks, (N_EXPERTS, LANES)).astype(BF16),
                                    preferred_element_type=F32)[:, 0:1]
    pos = seg_start + earlier
    pos1 = jnp.sum(jnp.where(pick1, pos, 0.0), axis=0, keepdims=True)
    pos2 = jnp.sum(jnp.where(pick2, pos, 0.0), axis=0, keepdims=True)
    route_ref[...] = jnp.concatenate([pos1, pos2, w1 / total, w2 / total, jnp.zeros((4, pos1.shape[1]), F32)], axis=0)
    cnt_ref[0] = jnp.broadcast_to(cnt, (N_EXPERTS, LANES))


def _merge(o_hg, o_da, o_rw, gates, x, mod, w_branch, w_out, ln_g, ln_b, w_router_t, router_bias, seq, tm):
    n, d = x.shape
    per_seq = seq // tm
    tile = lambda i: (i, 0)
    full = lambda i: (0, 0)
    before = (_iota((tm, tm), 0) < _iota((tm, tm), 1)).astype(BF16)
    before16 = (_iota((N_EXPERTS, N_EXPERTS), 1) < _iota((N_EXPERTS, N_EXPERTS), 0)).astype(BF16)
    return pl.pallas_call(
        _merge_body,
        grid=(n // tm,),
        in_specs=[pl.BlockSpec((tm, HG_W), tile), pl.BlockSpec((tm, DA_W), tile), pl.BlockSpec((tm, RW_W), tile),
                  pl.BlockSpec((tm, 3 * d), tile), pl.BlockSpec((tm, d), tile),
                  pl.BlockSpec((1, 6, d), lambda i: (i // per_seq, 0, 0)),
                  pl.BlockSpec(w_branch.shape, full), pl.BlockSpec(w_out.shape, full),
                  pl.BlockSpec((1, d), full), pl.BlockSpec((1, d), full),
                  pl.BlockSpec((N_EXPERTS, d), full), pl.BlockSpec((N_EXPERTS, 1), full),
                  pl.BlockSpec((tm, tm), full), pl.BlockSpec((N_EXPERTS, N_EXPERTS), full)],
        out_specs=[pl.BlockSpec((tm, d), tile), pl.BlockSpec((tm, d), tile), pl.BlockSpec((8, tm), lambda i: (0, i)),
                   pl.BlockSpec((1, N_EXPERTS, LANES), lambda i: (i, 0, 0))],
        out_shape=[jax.ShapeDtypeStruct((n, d), F32), jax.ShapeDtypeStruct((n, d), BF16),
                   jax.ShapeDtypeStruct((8, n), F32), jax.ShapeDtypeStruct((n // tm, N_EXPERTS, LANES), F32)],
        compiler_params=_params("arbitrary"),
        name="merge",
    )(o_hg, o_da, o_rw, gates, x, mod, w_branch, w_out, ln_g, ln_b, w_router_t, router_bias, before, before16)


def _local_rows(tm):
    return -(-(2 * tm + N_EXPERTS * (MOE_CHUNK - 1)) // LANES) * LANES


def _token_columns(route):
    return jnp.concatenate([route, jnp.zeros((LANES - route.shape[0], route.shape[1]), F32)], axis=0).T


def _segment_copies(i, nch_ref, loc_ref, glob_ref, local_buf, global_buf, sem, to_global):
    def run(action):
        for e in range(N_EXPERTS):
            seg = i * N_EXPERTS + e
            loc0, glob0 = loc_ref[seg], glob_ref[seg]

            def one(c, carry):
                loc = local_buf.at[pl.ds(pl.multiple_of(loc0 + c * MOE_CHUNK, MOE_CHUNK), MOE_CHUNK), :]
                glob = global_buf.at[pl.ds(pl.multiple_of(glob0 + c * MOE_CHUNK, MOE_CHUNK), MOE_CHUNK), :]
                copy = pltpu.make_async_copy(loc, glob, sem) if to_global else pltpu.make_async_copy(glob, loc, sem)
                getattr(copy, action)()
                return carry

            lax.fori_loop(0, nch_ref[seg], one, 0)
    return run


def _dispatch_body(nch_ref, loc_ref, glob_ref, u_ref, route_ref, xs_in_ref, xs_ref, stage_ref, sem):
    del xs_in_ref
    i = pl.program_id(0)
    tm, d = u_ref.shape
    route = route_ref[...]
    local_row = _iota((stage_ref.shape[0], tm), 0)
    take1 = local_row == route[0:1].astype(jnp.int32)
    take2 = local_row == route[1:2].astype(jnp.int32)
    perm = jnp.where(take1 | take2, 1.0, 0.0).astype(BF16)
    stage_ref[:, 0:d] = jnp.dot(perm, u_ref[...], preferred_element_type=F32).astype(BF16)

    cols = _token_columns(route)
    lane = _iota((tm, LANES), 1)

    def weight_cols(w):
        hi = w.astype(BF16).astype(F32)
        return jnp.where(lane == 0, hi, jnp.where(lane == 1, w - hi, 0.0)).astype(BF16)

    stage_ref[:, d:d + LANES] = (
        jnp.dot(jnp.where(take1, 1.0, 0.0).astype(BF16), weight_cols(cols[:, 2:3]), preferred_element_type=F32)
        + jnp.dot(jnp.where(take2, 1.0, 0.0).astype(BF16), weight_cols(cols[:, 3:4]), preferred_element_type=F32)
    ).astype(BF16)

    copies = _segment_copies(i, nch_ref, loc_ref, glob_ref, stage_ref, xs_ref, sem, to_global=True)
    copies("start")
    copies("wait")


def _dispatch(u2, route, nch, loc, glob, rows, tm):
    n, d = u2.shape
    width = d + LANES
    return pl.pallas_call(
        _dispatch_body,
        grid_spec=pltpu.PrefetchScalarGridSpec(
            num_scalar_prefetch=3,
            grid=(n // tm,),
            in_specs=[pl.BlockSpec((tm, d), lambda i, *_: (i, 0)),
                      pl.BlockSpec((8, tm), lambda i, *_: (0, i)),
                      pl.BlockSpec(memory_space=pl.ANY)],
            out_specs=pl.BlockSpec(memory_space=pl.ANY),
            scratch_shapes=[pltpu.VMEM((_local_rows(tm), width), BF16), pltpu.SemaphoreType.DMA],
        ),
        out_shape=jax.ShapeDtypeStruct((rows, width), BF16),
        input_output_aliases={5: 0},
        compiler_params=_params("arbitrary"),
        name="moe_dispatch",
    )(nch, loc, glob, u2, route, jnp.zeros((rows, width), BF16))


def _experts_body(te_ref, x_ref, wgu_ref, wd_ref, y_ref):
    used = te_ref[pl.program_id(0)] < N_EXPERTS

    @pl.when(used)
    def _():
        d = wgu_ref.shape[1]
        weight = x_ref[:, d:d + 1].astype(F32) + x_ref[:, d + 1:d + 2].astype(F32)
        hidden = jnp.dot(x_ref[:, 0:d], wgu_ref[0], preferred_element_type=F32)
        hg = hidden[:, :D_EXPERT]
        act = hg * _sigmoid(hg) * hidden[:, D_EXPERT:] * weight
        y_ref[...] = _mm(act, wd_ref[0]).astype(BF16)

    @pl.when(jnp.logical_not(used))
    def _():
        y_ref[...] = jnp.zeros_like(y_ref)


def _experts(xs, tile_expert, w_gu, w_down):
    rows, width = xs.shape
    d = w_down.shape[2]
    expert = lambda g, te: (jnp.minimum(te[g], N_EXPERTS - 1), 0, 0)
    return pl.pallas_call(
        _experts_body,
        grid_spec=pltpu.PrefetchScalarGridSpec(
            num_scalar_prefetch=1,
            grid=(rows // MOE_TM,),
            in_specs=[pl.BlockSpec((MOE_TM, width), lambda g, te: (g, 0)),
                      pl.BlockSpec((1, d, 2 * D_EXPERT), expert),
                      pl.BlockSpec((1, D_EXPERT, d), expert)],
            out_specs=pl.BlockSpec((MOE_TM, d), lambda g, te: (g, 0)),
        ),
        out_shape=jax.ShapeDtypeStruct((rows, d), BF16),
        compiler_params=_params("arbitrary"),
        name="moe_experts",
    )(tile_expert, xs, w_gu, w_down)


def _combine_body(nch_ref, loc_ref, glob_ref, route_ref, x1_ref, mod_ref, lng_ref, lnb_ref, ys_ref, o_ref,
                  back_ref, sem):
    i = pl.program_id(0)

    @pl.when(i == 0)
    def _():
        back_ref[...] = jnp.zeros_like(back_ref)

    copies = _segment_copies(i, nch_ref, loc_ref, glob_ref, back_ref, ys_ref, sem, to_global=False)
    copies("start")
    cols = _token_columns(route_ref[...]).astype(jnp.int32)
    local_row = _iota((x1_ref.shape[0], back_ref.shape[0]), 1)
    unperm = jnp.where((local_row == cols[:, 0:1]) | (local_row == cols[:, 1:2]), 1.0, 0.0).astype(BF16)
    copies("wait")
    ffn = jnp.dot(unperm, back_ref[...], preferred_element_type=F32)
    y = ALPHA * x1_ref[...] + (1.0 + mod_ref[0, 5:6, :]) * ffn
    o_ref[...] = _layer_norm(y, lng_ref[...], lnb_ref[...])


def _combine(ys, route, nch, loc, glob, x1, mod, ln_g, ln_b, seq, tm):
    n, d = x1.shape
    per_seq = seq // tm
    full = lambda i, *_: (0, 0)
    return pl.pallas_call(
        _combine_body,
        grid_spec=pltpu.PrefetchScalarGridSpec(
            num_scalar_prefetch=3,
            grid=(n // tm,),
            in_specs=[pl.BlockSpec((8, tm), lambda i, *_: (0, i)),
                      pl.BlockSpec((tm, d), lambda i, *_: (i, 0)),
                      pl.BlockSpec((1, 6, d), lambda i, *_: (i // per_seq, 0, 0)),
                      pl.BlockSpec((1, d), full), pl.BlockSpec((1, d), full),
                      pl.BlockSpec(memory_space=pl.ANY)],
            out_specs=pl.BlockSpec((tm, d), lambda i, *_: (i, 0)),
            scratch_shapes=[pltpu.VMEM((_local_rows(tm), d), BF16), pltpu.SemaphoreType.DMA],
        ),
        out_shape=jax.ShapeDtypeStruct((n, d), F32),
        compiler_params=_params("arbitrary"),
        name="moe_combine",
    )(nch, loc, glob, route, x1, mod, ln_g, ln_b, ys)


def _moe(u2, route, counts, w_gu, w_down, x1, mod, ln_g, ln_b, seq, tm):
    n = u2.shape[0]
    n_tiles = n // tm
    seg_rows = (counts + MOE_CHUNK - 1) // MOE_CHUNK * MOE_CHUNK
    loc = jnp.cumsum(seg_rows, axis=1) - seg_rows
    region = (jnp.sum(seg_rows, axis=0) + MOE_TM - 1) // MOE_TM * MOE_TM
    region_end = jnp.cumsum(region)
    glob = (region_end - region)[None, :] + jnp.cumsum(seg_rows, axis=0) - seg_rows
    rows = -(-(2 * n + n_tiles * N_EXPERTS * (MOE_CHUNK - 1) + N_EXPERTS * (MOE_TM - 1)) // MOE_TM) * MOE_TM
    tile_expert = jnp.sum(jnp.arange(rows // MOE_TM, dtype=jnp.int32)[:, None] * MOE_TM >= region_end[None, :],
                          axis=1).astype(jnp.int32)
    flat = lambda a: a.reshape(-1).astype(jnp.int32)
    nch, loc, glob = flat(seg_rows // MOE_CHUNK), flat(loc), flat(glob)
    xs = _dispatch(u2, route, nch, loc, glob, rows, tm)
    ys = _experts(xs, tile_expert, w_gu, w_down)
    return _combine(ys, route, nch, loc, glob, x1, mod, ln_g, ln_b, seq, tm)


def _tiles(seq):
    return min(512, seq), min(256, seq)


def kernel(x, c, w_ada, b_ada, w_in, hg_lb_logits, hg_norm_g, da_lambda, da_subln_g, rw_mu, rw_w0, rw_w_up,
           rw_a0, rw_a_up, rw_g_up, rw_k_k, rw_k_a, rw_r_k, rw_gn_g, rw_gn_b, w_merge, b_merge, w_branch, w_out,
           ln_g, ln_b, w_router, router_bias, w_exp_gate, w_exp_up, w_exp_down):
    bsz, seq, d = x.shape
    depth = w_in.shape[0]
    n = bsz * seq
    tm, blk = _tiles(seq)

    sm = jax.nn.softmax(hg_lb_logits.astype(F32), axis=0)
    hg_lb = jnp.cumsum(sm, axis=0) - sm[0:1]
    slopes = jnp.asarray([2.0 ** (-8.0 * (h + 1) / DA_HEADS) for h in range(DA_HEADS)], F32)

    mod_all = _ada(c, w_ada, b_ada).reshape(depth, bsz, 6, d)
    w_router_t = w_router.T
    router_bias = router_bias.reshape(N_EXPERTS, 1)

    xf = x.reshape(n, d)
    for l in range(depth):
        mod = mod_all[l]
        lq1, lk1, lq2, lk2 = da_lambda[l].astype(F32)
        lam_init = 0.8 - 0.6 * math.exp(-0.3 * l)
        lam = jnp.exp(jnp.sum(lq1 * lk1)) - jnp.exp(jnp.sum(lq2 * lk2)) + lam_init
        scal = jnp.concatenate([jnp.stack([lam, jnp.asarray(1.0 - lam_init, F32)]), slopes])

        w_gates = jnp.concatenate([w_merge[l, br] for br in range(3)], axis=1).astype(BF16)
        y, gates = _proj(xf, mod, w_in[l].astype(BF16), w_gates, b_merge[l].reshape(1, 3 * d), seq, tm)

        o_hg = _hgrn2(y, hg_lb[l].reshape(1, HG_W), hg_norm_g[l].reshape(1, HG_DV), bsz, seq)
        o_da = _diffattn(y, scal, da_subln_g[l].reshape(1, DA_DV), bsz, seq, blk)
        rw = dict(mu=rw_mu[l].reshape(1, -1), w0=rw_w0[l].reshape(1, -1), w_up=rw_w_up[l],
                  a0=rw_a0[l].reshape(1, -1), a_up=rw_a_up[l], g_up=rw_g_up[l],
                  k_k=rw_k_k[l].reshape(1, -1), k_a=rw_k_a[l].reshape(1, -1), r_k=rw_r_k[l].reshape(1, -1),
                  gn_g=rw_gn_g[l].reshape(1, -1), gn_b=rw_gn_b[l].reshape(1, -1))
        o_rw = _rwkv(y, rw, bsz, seq)

        x1, u2, route, counts = _merge(o_hg, o_da, o_rw, gates, xf, mod, w_branch[l].astype(BF16),
                                       w_out[l].astype(BF16), ln_g[l, 0].reshape(1, d), ln_b[l, 0].reshape(1, d),
                                       w_router_t, router_bias, seq, tm)
        w_gu = jnp.concatenate([w_exp_gate[l], w_exp_up[l]], axis=-1).astype(BF16)
        xf = _moe(u2, route, counts[:, :, 0].astype(jnp.int32), w_gu, w_exp_down[l].astype(BF16), x1, mod,
                  ln_g[l, 1].reshape(1, d), ln_b[l, 1].reshape(1, d), seq, tm)
    return xf.reshape(bsz, seq, d)
```

```python
import functools
import math

import jax
import jax.numpy as jnp
from jax import lax
from jax.experimental import pallas as pl
from jax.experimental.pallas import tpu as pltpu

D_MODEL = 1024
DEPTH = 4
HG_HEADS, HG_DK, HG_DV, HG_CHUNK, HG_SUB = 4, 128, 128, 64, 16
HG_W = HG_HEADS * HG_DV
HG_F_MIN = 1e-6
DA_HEADS, DA_DQK = 4, 64
DA_DV = 2 * DA_DQK
DA_W = DA_HEADS * DA_DV
MASK_VALUE = -1e30
LOG2E = math.log2(math.e)
PROJ_TN = 768
RW_HEADS, RW_DH, RW_CHUNK, RW_SUB = 8, 64, 64, 16
RW_STEP_CHUNKS = 2
RW_STEP_SEQS = 2
RW_W = RW_HEADS * RW_DH
RW_IN_W = 1792
RW_GN_EPS = 64e-5
IN_W = 5376
HG_COL, DA_COL, RW_COL = 0, 2048, 3584
N_EXPERTS, N_GROUPS, EXPERTS_PER_GROUP, D_EXPERT = 16, 4, 4, 512
MOE_CHUNK = 16
MOE_TM = 512
ALPHA = (2.0 * DEPTH) ** 0.25
LN_EPS = 1e-5
RMS_EPS = 1e-6
LANES = 128

F32 = jnp.float32
BF16 = jnp.bfloat16
HIGHEST = lax.Precision.HIGHEST
VMEM_LIMIT = 48 * 1024 * 1024

_NT = (((1,), (1,)), ((), ()))
_TN = (((0,), (0,)), ((), ()))


def _mm(a, b):
    return jnp.dot(a.astype(BF16), b.astype(BF16), preferred_element_type=F32)


def _mm_nt(a, b):
    return lax.dot_general(a.astype(BF16), b.astype(BF16), _NT, preferred_element_type=F32)


def _mm_tn(a, b):
    return lax.dot_general(a.astype(BF16), b.astype(BF16), _TN, preferred_element_type=F32)


def _mmh(a, b):
    return jnp.dot(a, b, precision=HIGHEST, preferred_element_type=F32)


def _mmh_nt(a, b):
    return lax.dot_general(a, b, _NT, precision=HIGHEST, preferred_element_type=F32)


def _seg_sum(x, seg):
    rows = x.shape[0]
    hi = x.astype(BF16)
    lo = (x - hi.astype(F32)).astype(BF16)
    halves = []
    for c0 in range(0, x.shape[1], seg.shape[0]):
        cols = slice(c0, c0 + seg.shape[0])
        both = jnp.dot(jnp.concatenate([hi[:, cols], lo[:, cols]], axis=0), seg, preferred_element_type=F32)
        halves.append(both[:rows] + both[rows:])
    return jnp.concatenate(halves, axis=1)


def _split_mm(x, w_twice, w_lo):
    hi = x.astype(BF16)
    lo = (x - hi.astype(F32)).astype(BF16)
    return (jnp.dot(jnp.concatenate([hi, lo], axis=1), w_twice, preferred_element_type=F32)
            + jnp.dot(hi, w_lo, preferred_element_type=F32))


def _chunk_cumsum(x):
    c = x.shape[0]
    hi = x.astype(BF16)
    rest = x - hi.astype(F32)
    mid = rest.astype(BF16)
    lo = (rest - mid.astype(F32)).astype(BF16)
    col = _iota((c, 4 * c), 1)
    tri = ((col & (c - 1)) <= _iota((c, 4 * c), 0)) & (col < 3 * c)
    return jnp.dot(jnp.where(tri, 1.0, 0.0).astype(BF16), jnp.concatenate([hi, mid, lo, lo], axis=0),
                   preferred_element_type=F32)


def _sigmoid(x):
    return 1.0 / (1.0 + jnp.exp(-x))


def _softplus(x):
    return jnp.maximum(x, 0.0) + jnp.log(1.0 + jnp.exp(-jnp.abs(x)))


def _iota(shape, dim):
    return lax.broadcasted_iota(jnp.int32, shape, dim)


def _params(*sem):
    return pltpu.CompilerParams(dimension_semantics=sem, vmem_limit_bytes=VMEM_LIMIT)


def _layer_norm(y, g, b):
    mu = jnp.mean(y, axis=-1, keepdims=True)
    d = y - mu
    var = jnp.mean(d * d, axis=-1, keepdims=True)
    return d * lax.rsqrt(var + LN_EPS) * g + b


def _ada_body(c_ref, w_ref, b_ref, o_ref):
    c = c_ref[...]
    o_ref[0] = _mmh(c * _sigmoid(c), w_ref[0]) + b_ref[0]


def _ada(c, w_ada, b_ada):
    depth, d, _ = w_ada.shape
    bsz = c.shape[0]
    return pl.pallas_call(
        _ada_body,
        grid=(depth, 6),
        in_specs=[pl.BlockSpec((bsz, d), lambda l, j: (0, 0)),
                  pl.BlockSpec((1, d, d), lambda l, j: (l, 0, j)),
                  pl.BlockSpec((1, 1, d), lambda l, j: (l, 0, j))],
        out_specs=pl.BlockSpec((1, bsz, d), lambda l, j: (l, 0, j)),
        out_shape=jax.ShapeDtypeStruct((depth, bsz, 6 * d), F32),
        compiler_params=_params("arbitrary", "arbitrary"),
        name="ada",
    )(c, w_ada, b_ada.reshape(depth, 1, 6 * d))


def _proj_body(x_ref, mod_ref, win_ref, wg_ref, bg_ref, y_ref, g_ref):
    u = (x_ref[...] * (1.0 + mod_ref[0, 1:2, :]) + mod_ref[0, 0:1, :]).astype(BF16)
    for c0 in range(0, IN_W, PROJ_TN):
        cols = slice(c0, c0 + PROJ_TN)
        y_ref[:, cols] = jnp.dot(u, win_ref[:, cols], preferred_element_type=F32).astype(BF16)
    for c0 in range(0, 3 * D_MODEL, PROJ_TN):
        cols = slice(c0, c0 + PROJ_TN)
        g = jnp.dot(u, wg_ref[:, cols], preferred_element_type=F32) + bg_ref[:, cols]
        g_ref[:, cols] = _sigmoid(g).astype(BF16)


def _proj(x, mod, w_in, w_gates, b_gates, seq, tm):
    n, d = x.shape
    per_seq = seq // tm
    tile = lambda i: (i, 0)
    resident = dict(index_map=lambda i: (0, 0), pipeline_mode=pl.Buffered(1))
    return pl.pallas_call(
        _proj_body,
        grid=(n // tm,),
        in_specs=[pl.BlockSpec((tm, d), tile),
                  pl.BlockSpec((1, 6, d), lambda i: (i // per_seq, 0, 0)),
                  pl.BlockSpec(w_in.shape, **resident),
                  pl.BlockSpec(w_gates.shape, **resident),
                  pl.BlockSpec(b_gates.shape, **resident)],
        out_specs=[pl.BlockSpec((tm, IN_W), tile), pl.BlockSpec((tm, 3 * d), tile)],
        out_shape=[jax.ShapeDtypeStruct((n, IN_W), BF16), jax.ShapeDtypeStruct((n, 3 * d), BF16)],
        compiler_params=_params("arbitrary"),
        name="proj",
    )(x, mod, w_in, w_gates, b_gates)


def _hgrn2_body(y_ref, lb_ref, ng_ref, o_ref, st_ref):
    c, sub = HG_CHUNK, HG_SUB

    @pl.when(pl.program_id(1) == 0)
    def _():
        st_ref[...] = jnp.zeros_like(st_ref)

    q = y_ref[:, 0:HG_W].astype(F32)
    z = y_ref[:, HG_W:2 * HG_W].astype(F32)
    v = y_ref[:, 2 * HG_W:3 * HG_W]
    og = y_ref[:, 3 * HG_W:4 * HG_W].astype(F32)
    lb = lb_ref[...]
    f = lb + (1.0 - lb) * _sigmoid(z)
    kin = (1.0 - lb) * _sigmoid(-z)
    b = _chunk_cumsum(jnp.log(jnp.maximum(f, HG_F_MIN)))
    log_k = jnp.log(kin)
    rel = log_k - b
    b_last = b[c - 1:c]
    q_dec = (q * jnp.exp(b)).astype(BF16)
    k_tail = (kin * jnp.exp(b_last - b)).astype(BF16)
    decay = jnp.exp(b_last)

    ones = jnp.ones((HG_DK, LANES), BF16)
    row_s = _iota((sub, c), 0)
    col_s = _iota((sub, c), 1)
    heads = range(HG_HEADS)
    blks = range(c // sub)
    hs = [slice(h * HG_DK, (h + 1) * HG_DK) for h in heads]

    diag = {}
    for h in heads:
        for blk in blks:
            rows = slice(blk * sub, (blk + 1) * sub)
            b_i, q_i, rel_i, lk_i = b[rows, hs[h]], q[rows, hs[h]], rel[rows, hs[h]], log_k[rows, hs[h]]
            terms = [q_i * jnp.exp(jnp.minimum(b_i + rel_i[s:s + 1], lk_i[s:s + 1])) for s in range(sub)]
            w = jnp.concatenate(terms, axis=0).astype(BF16)
            diag[h, blk] = jnp.dot(w, ones, preferred_element_type=F32)
    below = {}
    for h in heads:
        for blk in blks[1:]:
            r0 = blk * sub
            beta = b[r0 - 1:r0, hs[h]]
            q_t = q[r0:r0 + sub, hs[h]] * jnp.exp(b[r0:r0 + sub, hs[h]] - beta)
            k_h = kin[:, hs[h]] * jnp.exp(jnp.minimum(beta - b[:, hs[h]], 0.0))
            below[h, blk] = _mm_nt(q_t, k_h)
    scores = []
    for h in heads:
        a_rows = []
        for blk in blks:
            r0 = blk * sub
            a_blk = jnp.zeros((sub, c), F32)
            for s in range(sub):
                a_blk = jnp.where(col_s == r0 + s, diag[h, blk][s * sub:(s + 1) * sub, :c], a_blk)
            if blk > 0:
                a_blk = jnp.where(col_s < r0, below[h, blk], a_blk)
            a_rows.append(jnp.where(col_s <= row_s + r0, a_blk, 0.0))
        scores.append(jnp.concatenate(a_rows, axis=0).astype(BF16))

    st = [st_ref[h] for h in heads]
    intra = [jnp.dot(scores[h], v[:, hs[h]], preferred_element_type=F32) for h in heads]
    inter = [lax.dot_general(q_dec[:, hs[h]], st[h].astype(BF16), _NT, preferred_element_type=F32) for h in heads]
    for h in heads:
        st_ref[h] = st[h] * decay[:, hs[h]] + lax.dot_general(v[:, hs[h]], k_tail[:, hs[h]], _TN,
                                                             preferred_element_type=F32)
    for h in heads:
        o = intra[h] + inter[h]
        o = o * lax.rsqrt(jnp.mean(o * o, axis=-1, keepdims=True) + RMS_EPS) * ng_ref[...]
        o_ref[:, hs[h]] = (o * (og[:, hs[h]] * _sigmoid(og[:, hs[h]]))).astype(o_ref.dtype)


def _hgrn2(y, lb, norm_g, bsz, seq):
    n = y.shape[0]
    nc = seq // HG_CHUNK
    width = 4 * HG_W
    return pl.pallas_call(
        _hgrn2_body,
        grid=(bsz, nc),
        in_specs=[pl.BlockSpec((HG_CHUNK, width), lambda b, c: (b * nc + c, HG_COL // width)),
                  pl.BlockSpec((1, HG_W), lambda b, c: (0, 0)),
                  pl.BlockSpec((1, HG_DV), lambda b, c: (0, 0))],
        out_specs=pl.BlockSpec((HG_CHUNK, HG_W), lambda b, c: (b * nc + c, 0)),
        out_shape=jax.ShapeDtypeStruct((n, HG_W), BF16),
        scratch_shapes=[pltpu.VMEM((HG_HEADS, HG_DV, HG_DK), F32)],
        compiler_params=_params("arbitrary", "arbitrary"),
        name="hgrn2",
    )(y, lb, norm_g)


def _diffattn_body(scal_ref, q_ref, k_ref, vt_ref, g_ref, o_ref, qq_ref, sa_ref, sb_ref, p_ref, m_ref, l_ref,
                   sc_ref, acc_ref, *, blk):
    h = pl.program_id(1)
    i = pl.program_id(2)
    lam = scal_ref[0]
    out_scale = scal_ref[1]
    slope = scal_ref[2 + h] * LOG2E
    q0 = i * blk

    q = q_ref[...].astype(F32) * (DA_DQK ** -0.5 * LOG2E)
    lane = _iota(q.shape, 1)
    stacked = jnp.concatenate([jnp.where(lane < DA_DQK, q, 0.0), jnp.where(lane >= DA_DQK, q, 0.0)], axis=0)
    qq_ref[...] = stacked.T.astype(BF16)
    m_ref[...] = jnp.full(m_ref.shape, MASK_VALUE, F32)
    l_ref[...] = jnp.zeros(l_ref.shape, F32)
    sc_ref[...] = jnp.ones(sc_ref.shape, F32)
    acc_ref[...] = jnp.zeros(acc_ref.shape, F32)
    p_ref[...] = jnp.zeros(p_ref.shape, BF16)
    key_off = _iota((blk, LANES), 0)
    rel = slope * key_off.astype(F32)

    def scores(j):
        return jnp.dot(k_ref[pl.ds(j * blk, blk), :], qq_ref[...], preferred_element_type=F32)

    def softmax(j, src_ref, masked):
        bias = rel + slope * (j * blk - q0).astype(F32)
        for t in range(2 * blk // LANES):
            cols = slice(t * LANES, (t + 1) * LANES)
            s = src_ref[:, cols] + bias
            if masked:
                s = jnp.where(key_off <= ((_iota((blk, LANES), 1) + t * LANES) & (blk - 1)), s, MASK_VALUE)
            m_old = m_ref[:, cols]
            m_new = jnp.maximum(m_old, jnp.max(s, axis=0, keepdims=True))
            p = jnp.exp2(s - m_new)
            sc = jnp.exp2(m_old - m_new)
            l_ref[:, cols] = sc * l_ref[:, cols] + jnp.sum(p, axis=0, keepdims=True)
            p_ref[:, cols] = p.astype(BF16)
            sc_ref[:, cols] = sc
            m_ref[:, cols] = m_new

    def iteration(j, src_ref, dst_ref):
        sc_prev = sc_ref[...]
        pv = jnp.dot(vt_ref[jnp.maximum(j - 1, 0)], p_ref[...], preferred_element_type=F32)
        if dst_ref is not None:
            dst_ref[...] = scores(j + 1)
        softmax(j, src_ref, dst_ref is None)
        acc_ref[...] = acc_ref[...] * sc_prev + pv

    def body(pair, carry):
        iteration(2 * pair, sa_ref, sb_ref)
        iteration(2 * pair + 1, sb_ref, sa_ref)
        return carry

    sa_ref[...] = scores(0)
    lax.fori_loop(0, i // 2, body, 0)

    @pl.when(i % 2 == 0)
    def _():
        iteration(i, sa_ref, None)

    @pl.when(i % 2 == 1)
    def _():
        iteration(i - 1, sa_ref, sb_ref)
        iteration(i, sb_ref, None)

    acc = acc_ref[...] * sc_ref[...] + jnp.dot(vt_ref[i], p_ref[...], preferred_element_type=F32)
    o = acc / l_ref[...]
    d = o[:, :blk] - lam * o[:, blk:]
    g = jnp.concatenate([g_ref[...]] * (blk // LANES), axis=1)
    d = d * lax.rsqrt(jnp.mean(d * d, axis=0, keepdims=True) + RMS_EPS) * g * out_scale
    o_ref[...] = d.T.astype(o_ref.dtype)


def _diffattn(y, scal, subln_g, bsz, seq, blk):
    n = y.shape[0]
    nq = seq // blk
    qc, kc = DA_COL // DA_DV, (DA_COL + DA_W) // DA_DV
    v_t = y[:, DA_COL + 2 * DA_W:DA_COL + 3 * DA_W].reshape(bsz, nq, blk, DA_HEADS, DA_DV)
    v_t = v_t.transpose(0, 3, 1, 4, 2).reshape(bsz * DA_HEADS * nq, DA_DV, blk)
    g_col = jnp.broadcast_to(subln_g.reshape(DA_DV, 1), (DA_DV, LANES))
    row = (1, 2 * blk)
    return pl.pallas_call(
        functools.partial(_diffattn_body, blk=blk),
        scratch_shapes=[pltpu.VMEM((DA_DV, 2 * blk), BF16), pltpu.VMEM((blk, 2 * blk), F32),
                        pltpu.VMEM((blk, 2 * blk), F32), pltpu.VMEM((blk, 2 * blk), BF16), pltpu.VMEM(row, F32), pltpu.VMEM(row, F32),
                        pltpu.VMEM(row, F32), pltpu.VMEM((DA_DV, 2 * blk), F32)],
        grid=(bsz, DA_HEADS, nq),
        in_specs=[pl.BlockSpec(memory_space=pltpu.SMEM),
                  pl.BlockSpec((blk, DA_DV), lambda b, h, i: (b * nq + i, qc + h)),
                  pl.BlockSpec((seq, DA_DV), lambda b, h, i: (b, kc + h)),
                  pl.BlockSpec((nq, DA_DV, blk), lambda b, h, i: (b * DA_HEADS + h, 0, 0)),
                  pl.BlockSpec((DA_DV, LANES), lambda b, h, i: (0, 0))],
        out_specs=pl.BlockSpec((blk, DA_DV), lambda b, h, i: (b * nq + i, h)),
        out_shape=jax.ShapeDtypeStruct((n, DA_W), BF16),
        compiler_params=_params("arbitrary", "arbitrary", "arbitrary"),
        name="diffattn",
    )(scal, y, y, v_t, g_col)


def _attn_body(scal_ref, q_ref, k_ref, vt_ref, g_ref, o_ref, qq_ref, sa_ref, sb_ref, p_ref, m_ref, sc_ref,
               acc_ref, *, kb):
    h = pl.program_id(1)
    i = pl.program_id(2)
    lam = scal_ref[0]
    out_scale = scal_ref[1]
    slope = scal_ref[2 + h] * LOG2E
    qb = 2 * kb
    q0 = i * qb
    tiles_per_map = qb // LANES
    a_tiles = kb // LANES

    q = q_ref[...].astype(F32) * (DA_DQK ** -0.5 * LOG2E)
    lane = _iota(q.shape, 1)
    stacked = jnp.concatenate([jnp.where(lane < DA_DQK, q, 0.0), jnp.where(lane >= DA_DQK, q, 0.0)], axis=0)
    qq_ref[...] = stacked.T.astype(BF16)
    m_ref[...] = jnp.full(m_ref.shape, MASK_VALUE, F32)
    sc_ref[...] = jnp.ones(sc_ref.shape, F32)
    acc_ref[...] = jnp.zeros(acc_ref.shape, F32)
    p_ref[...] = jnp.zeros(p_ref.shape, BF16)
    key_off = _iota((kb, LANES), 0)
    rel = slope * key_off.astype(F32)

    def scores(j):
        return jnp.dot(k_ref[pl.ds(j * kb, kb), :], qq_ref[...], preferred_element_type=F32)

    def softmax(j, src_ref, diagonal_of):
        bias = rel + slope * (j * kb - q0).astype(F32)
        for t in range(2 * tiles_per_map):
            cols = slice(t * LANES, (t + 1) * LANES)
            in_map = t % tiles_per_map
            half = "A" if in_map < a_tiles else "B"
            if diagonal_of == "B" and half == "A":
                p_ref[:, cols] = jnp.zeros((kb, LANES), BF16)
                sc_ref[:, cols] = jnp.ones((1, LANES), F32)
                continue
            s = src_ref[:, cols] + bias
            if diagonal_of == half:
                q_off = _iota((kb, LANES), 1) + (in_map % a_tiles) * LANES
                s = jnp.where(key_off <= q_off, s, MASK_VALUE)
            m_old = m_ref[:, cols]
            m_new = jnp.maximum(m_old, jnp.max(s, axis=0, keepdims=True))
            p_ref[:, cols] = jnp.exp2(s - m_new).astype(BF16)
            sc_ref[:, cols] = jnp.exp2(m_old - m_new)
            m_ref[:, cols] = m_new

    def iteration(j, src_ref, dst_ref, diagonal_of=None):
        sc_prev = sc_ref[...]
        pv = jnp.dot(vt_ref[jnp.maximum(j - 1, 0)], p_ref[...], preferred_element_type=F32)
        if dst_ref is not None:
            dst_ref[...] = scores(j + 1)
        softmax(j, src_ref, diagonal_of)
        acc_ref[...] = acc_ref[...] * sc_prev + pv

    def pair(j0):
        iteration(j0, sa_ref, sb_ref)
        iteration(j0 + 1, sb_ref, sa_ref)

    def body(quad, carry):
        pair(4 * quad)
        pair(4 * quad + 2)
        return carry

    sa_ref[...] = scores(0)
    lax.fori_loop(0, i // 2, body, 0)

    @pl.when(i % 2 == 1)
    def _():
        pair(2 * i - 2)

    iteration(2 * i, sa_ref, sb_ref, "A")
    iteration(2 * i + 1, sb_ref, None, "B")
    acc = acc_ref[...] * sc_ref[...] + jnp.dot(vt_ref[2 * i + 1], p_ref[...], preferred_element_type=F32)
    o = acc[:DA_DV] / acc[DA_DV:DA_DV + 1]
    d = o[:, :qb] - lam * o[:, qb:]
    g = jnp.concatenate([g_ref[...]] * tiles_per_map, axis=1)
    d = d * lax.rsqrt(jnp.mean(d * d, axis=0, keepdims=True) + RMS_EPS) * g * out_scale
    o_ref[...] = d.T.astype(o_ref.dtype)


def _attn(y, scal, subln_g, bsz, seq, kb):
    n = y.shape[0]
    qb = 2 * kb
    nq, nk = seq // qb, seq // kb
    qc, kc = DA_COL // DA_DV, (DA_COL + DA_W) // DA_DV
    rows_v = DA_DV + 8
    v_t = y[:, DA_COL + 2 * DA_W:DA_COL + 3 * DA_W].reshape(bsz, nk, kb, DA_HEADS, DA_DV)
    v_t = v_t.transpose(0, 3, 1, 4, 2).reshape(bsz * DA_HEADS * nk, DA_DV, kb)
    extra = jnp.zeros((v_t.shape[0], 8, kb), BF16).at[:, 0, :].set(1.0)
    v_t = jnp.concatenate([v_t, extra], axis=1)
    g_col = jnp.broadcast_to(subln_g.reshape(DA_DV, 1), (DA_DV, LANES))
    row = (1, 2 * qb)
    return pl.pallas_call(
        functools.partial(_attn_body, kb=kb),
        scratch_shapes=[pltpu.VMEM((DA_DV, 2 * qb), BF16), pltpu.VMEM((kb, 2 * qb), F32),
                        pltpu.VMEM((kb, 2 * qb), F32), pltpu.VMEM((kb, 2 * qb), BF16), pltpu.VMEM(row, F32),
                        pltpu.VMEM(row, F32), pltpu.VMEM((rows_v, 2 * qb), F32)],
        grid=(bsz, DA_HEADS, nq),
        in_specs=[pl.BlockSpec(memory_space=pltpu.SMEM),
                  pl.BlockSpec((qb, DA_DV), lambda b, h, i: (b * nq + i, qc + h)),
                  pl.BlockSpec((seq, DA_DV), lambda b, h, i: (b, kc + h)),
                  pl.BlockSpec((nk, rows_v, kb), lambda b, h, i: (b * DA_HEADS + h, 0, 0)),
                  pl.BlockSpec((DA_DV, LANES), lambda b, h, i: (0, 0))],
        out_specs=pl.BlockSpec((qb, DA_DV), lambda b, h, i: (b * nq + i, h)),
        out_shape=jax.ShapeDtypeStruct((n, DA_W), BF16),
        compiler_params=_params("arbitrary", "arbitrary", "arbitrary"),
        name="diffattn",
    )(scal, y, y, v_t, g_col)


def _split_f32(x):
    hi = x.astype(BF16)
    hi_f = hi.astype(F32)
    return hi, hi_f, x - hi_f


def _dup_lhs(hi_f, lo_f, low_half):
    packed = jnp.where(low_half, hi_f, lo_f).astype(BF16)
    return jnp.concatenate([packed, packed], axis=1)


def _dup_rhs(hi, lo_f):
    lo = lo_f.astype(BF16)
    return jnp.concatenate([hi, hi, lo, lo], axis=0)


def _rwkv_body(y_ref, mu_ref, w0_ref, a0_ref, wa2_ref, walo_ref, gu2_ref, gulo_ref, kk_ref, ka_ref, rk_ref,
               gng_ref, gnb_ref, seg_ref, o_ref, st_ref, prev_ref, osc_ref):
    c, sub, dh = RW_CHUNK, RW_SUB, RW_DH

    @pl.when(pl.program_id(1) == 0)
    def _():
        st_ref[...] = jnp.zeros_like(st_ref)
        prev_ref[...] = jnp.zeros_like(prev_ref)

    n_seq, seq_rows = y_ref.shape[0], y_ref.shape[1]
    rows = n_seq * seq_rows
    shifted = []
    for si in range(n_seq):
        x = y_ref[si].astype(F32)
        x_prev = jnp.where(_iota(x.shape, 0) == 0, prev_ref[si:si + 1], pltpu.roll(x, 1, axis=0))
        prev_ref[si:si + 1] = x[seq_rows - 1:seq_rows]
        shifted.append(x + (x_prev - x) * mu_ref[...])
    xs = jnp.concatenate(shifted, axis=0)
    r = xs[:, 0:RW_W]
    k = xs[:, RW_W:2 * RW_W]
    v = xs[:, 2 * RW_W:3 * RW_W]
    wa = xs[:, 3 * RW_W:3 * RW_W + LANES]
    gd = xs[:, 3 * RW_W + LANES:RW_IN_W]

    lora = _split_mm(jnp.where(_iota(wa.shape, 1) < 64, jnp.tanh(wa), wa), wa2_ref[...], walo_ref[...])
    w_log = -_softplus(-(w0_ref[...] + lora[:, :RW_W])) - 0.5
    g = -jnp.exp(w_log)
    a = _sigmoid(a0_ref[...] + lora[:, RW_W:])
    gate = _split_mm(_sigmoid(gd), gu2_ref[...], gulo_ref[...])
    seg = seg_ref[...]
    kk = k * kk_ref[...]
    k2 = k * (1.0 + (a - 1.0) * ka_ref[...])
    sums = _seg_sum(jnp.concatenate([kk * kk, r * k2 * rk_ref[...]], axis=0), seg)
    kk = kk * lax.rsqrt(jnp.maximum(sums[:rows], 1e-12))
    bb = kk * a
    bonus = sums[rows:] * v

    chunks = range(rows // c)
    cr = [slice(ci * c, (ci + 1) * c) for ci in chunks]
    gc = jnp.concatenate([_chunk_cumsum(g[s]) for s in cr], axis=0)
    g_last = [gc[s][c - 1:c] for s in cr]
    e_inv = jnp.exp(-gc)
    e_tail = jnp.exp(jnp.concatenate([jnp.broadcast_to(gl, (c, RW_W)) for gl in g_last], axis=0) - gc)
    gam = [jnp.exp(gl) for gl in g_last]
    a_t = (-kk * jnp.exp(gc - g)).astype(BF16)
    r_t = (r * jnp.exp(gc)).astype(BF16)
    b_h = (bb * e_inv).astype(BF16)
    k_h = (k2 * e_inv).astype(BF16)
    k_bar = (k2 * e_tail).astype(BF16)
    b_bar = (bb * e_tail).astype(BF16)
    v_bf = v.astype(BF16)

    row2 = _iota((c, 2 * c), 0)
    lane2 = _iota((c, 2 * c), 1)
    col2 = lane2 & (c - 1)
    low_half = lane2 < c
    strict = row2 > col2
    incl = row2 >= col2
    same_blk = (row2 // sub) == (col2 // sub)
    eye = (row2 == col2).astype(F32)

    def dot(p, q):
        return jnp.dot(p, q, preferred_element_type=F32)

    heads = range(RW_HEADS)
    sls = [slice(h * dh, (h + 1) * dh) for h in heads]
    items = [(ci, h) for ci in chunks for h in heads]
    ar_h = [jnp.concatenate([a_t[cr[ci], sls[h]], r_t[cr[ci], sls[h]]], axis=0) for ci, h in items]
    quad = [lax.dot_general(x, jnp.concatenate([b_h[cr[ci], sls[h]], k_h[cr[ci], sls[h]]], axis=0), _NT,
                            preferred_element_type=F32)
            for x, (ci, h) in zip(ar_h, items)]
    top = [jnp.where(strict, q[:c], 0.0) for q in quad]
    a_ak = [t[:, c:].astype(BF16) for t in top]
    a_r = [jnp.where(incl, q[c:], 0.0).astype(BF16) for q in quad]
    a_ab = [jnp.where(low_half, t, pltpu.roll(t, c, axis=1)) for t in top]
    a_d = [jnp.where(same_blk, x, 0.0) for x in a_ab]
    a_o = [(x - y).astype(BF16) for x, y in zip(a_ab, a_d)]

    s1 = [_split_f32(x) for x in a_d]
    p2 = [dot(_dup_lhs(hf, lf, low_half), _dup_rhs(hi, lf)) for hi, hf, lf in s1]
    s2 = [_split_f32(x) for x in p2]
    rhs2 = [_dup_rhs(hi, lf) for hi, _, lf in s2]
    p4 = [dot(_dup_lhs(hf, lf, low_half), rhs) for (_, hf, lf), rhs in zip(s2, rhs2)]
    s4 = [_split_f32(x) for x in p4]
    rhs4 = [_dup_rhs(hi, lf) for hi, _, lf in s4]
    p8 = [dot(_dup_lhs(hf, lf, low_half), rhs) for (_, hf, lf), rhs in zip(s4, rhs4)]
    rhs8 = [_dup_rhs(hi, lf) for hi, _, lf in (_split_f32(x) for x in p8)]
    t_d = [eye + x for x in a_d]
    for rhs_all in (rhs2, rhs4, rhs8):
        st = [_split_f32(x) for x in t_d]
        t_d = [x + dot(_dup_lhs(hf, lf, low_half), rhs) for x, (_, hf, lf), rhs in zip(t_d, st, rhs_all)]
    t_d = [x.astype(BF16) for x in t_d]

    nn = [dot(t[:, :c], x) for t, x in zip(t_d, a_o)]
    nn_bf = [x.astype(BF16) for x in nn]
    n2 = [dot(x[:, :c], x) for x in nn_bf]
    n3 = [dot(x[:, :c], y.astype(BF16)) for x, y in zip(nn_bf, n2)]
    t_m = [dot((eye + x + y + z).astype(BF16)[:, :c], t).astype(BF16)[:, :c]
           for x, y, z, t in zip(nn, n2, n3, t_d)]

    v_h = [v_bf[cr[ci], sls[h]] for ci, h in items]
    akv = [dot(x, y).astype(BF16) for x, y in zip(a_ak, v_h)]
    at_m = [dot(t, x[:c]).astype(BF16) for t, x in zip(t_m, ar_h)]
    v_p = [dot(t, x) for t, x in zip(t_m, akv)]

    per_seq = seq_rows // c
    chains = [(si, h) for si in range(n_seq) for h in heads]
    state = [st_ref[si, h] for si, h in chains]
    for t in range(per_seq):
        it = [(si * per_seq + t) * RW_HEADS + h for si, h in chains]
        ck = [cr[si * per_seq + t] for si, _ in chains]
        proj = [lax.dot_general(jnp.concatenate([at_m[j], ar_h[j][c:]], axis=0), s.astype(BF16), _NT,
                                preferred_element_type=F32) for j, s in zip(it, state)]
        u = [(p[:c] + v_p[j]).astype(BF16) for p, j in zip(proj, it)]
        for p, uu, j, rws, (_, h) in zip(proj, u, it, ck, chains):
            osc_ref[rws, sls[h]] = p[c:] + dot(a_r[j], jnp.concatenate([uu, v_h[j]], axis=0))
        state = [s * gam[j // RW_HEADS][:, sls[h]] + lax.dot_general(
            jnp.concatenate([v_h[j], uu], axis=0),
            jnp.concatenate([k_bar[rws, sls[h]], b_bar[rws, sls[h]]], axis=0), _TN, preferred_element_type=F32)
            for s, uu, j, rws, (_, h) in zip(state, u, it, ck, chains)]
    for s, (si, h) in zip(state, chains):
        st_ref[si, h] = s

    o = osc_ref[...]
    mean = _seg_sum(o, seg) * (1.0 / dh)
    d = o - mean
    var = _seg_sum(d * d, seg) * (1.0 / dh)
    o = d * lax.rsqrt(var + RW_GN_EPS) * gng_ref[...] + gnb_ref[...]
    o = ((o + bonus) * gate).astype(o_ref.dtype)
    for si in range(n_seq):
        o_ref[si] = o[si * seq_rows:(si + 1) * seq_rows]


def _rwkv(y, p, bsz, seq):
    n = y.shape[0]
    half = RW_W // 2
    seg = (_iota((half, half), 0) // RW_DH == _iota((half, half), 1) // RW_DH).astype(BF16)

    def two_terms(w):
        hi = w.astype(BF16)
        return jnp.concatenate([hi, hi], axis=0), (w - hi.astype(F32)).astype(BF16)

    zeros = jnp.zeros_like(p["w_up"])
    wa2, wa_lo = two_terms(jnp.concatenate([jnp.concatenate([p["w_up"], zeros], axis=1),
                                            jnp.concatenate([zeros, p["a_up"]], axis=1)], axis=0))
    gu2, gu_lo = two_terms(p["g_up"])
    rows = [p["mu"], p["w0"], p["a0"], wa2, wa_lo, gu2, gu_lo, p["k_k"], p["k_a"], p["r_k"],
            p["gn_g"], p["gn_b"], seg]
    full = lambda b, c: (0, 0)
    step = RW_STEP_CHUNKS * RW_CHUNK
    n_seq = RW_STEP_SEQS if bsz % RW_STEP_SEQS == 0 else 1
    out = pl.pallas_call(
        _rwkv_body,
        grid=(bsz // n_seq, seq // step),
        in_specs=[pl.BlockSpec((n_seq, step, RW_IN_W), lambda b, c: (b, c, RW_COL // RW_IN_W))]
        + [pl.BlockSpec(a.shape, full) for a in rows],
        out_specs=pl.BlockSpec((n_seq, step, RW_W), lambda b, c: (b, c, 0)),
        out_shape=jax.ShapeDtypeStruct((bsz, seq, RW_W), BF16),
        scratch_shapes=[pltpu.VMEM((n_seq, RW_HEADS, RW_DH, RW_DH), F32),
                        pltpu.VMEM((n_seq, RW_IN_W), F32),
                        pltpu.VMEM((n_seq * step, RW_W), F32)],
        compiler_params=_params("arbitrary", "arbitrary"),
        name="rwkv7",
    )(y.reshape(bsz, seq, -1), *rows)
    return out.reshape(n, RW_W)


def _first_argmax(vals, row):
    top = jnp.max(vals, axis=0, keepdims=True)
    idx = jnp.min(jnp.where(vals == top, row, N_EXPERTS), axis=0, keepdims=True)
    return top, idx


def _merge_body(ohg_ref, oda_ref, orw_ref, gt_ref, x_ref, mod_ref, wb_ref, wo_ref, lng_ref, lnb_ref,
                wrt_ref, rb_ref, tri_ref, tri16_ref, x1_ref, u2_ref, route_ref, cnt_ref):
    d = D_MODEL
    merged = (gt_ref[:, 0:d].astype(F32) * jnp.dot(ohg_ref[...], wb_ref[0:HG_W, :], preferred_element_type=F32)
              + gt_ref[:, d:2 * d].astype(F32)
              * jnp.dot(oda_ref[...], wb_ref[HG_W:HG_W + DA_W, :], preferred_element_type=F32)
              + gt_ref[:, 2 * d:3 * d].astype(F32)
              * jnp.dot(orw_ref[...], wb_ref[HG_W + DA_W:, :], preferred_element_type=F32))
    mix = _mm(merged, wo_ref[...])
    x1 = _layer_norm(ALPHA * x_ref[...] + (1.0 + mod_ref[0, 2:3, :]) * mix, lng_ref[...], lnb_ref[...])
    x1_ref[...] = x1
    u2 = x1 * (1.0 + mod_ref[0, 4:5, :]) + mod_ref[0, 3:4, :]
    u2_ref[...] = u2.astype(BF16)

    logits = _mmh_nt(wrt_ref[...], u2)
    ex = jnp.exp(logits - jnp.max(logits, axis=0, keepdims=True))
    scores = ex / jnp.sum(ex, axis=0, keepdims=True)
    sel = scores + rb_ref[...]
    row = _iota(sel.shape, 0)
    best = None
    for grp in range(N_GROUPS):
        a, b, c2, d2 = (sel[grp * EXPERTS_PER_GROUP + i:grp * EXPERTS_PER_GROUP + i + 1] for i in range(4))
        hi1, lo1, hi2, lo2 = jnp.maximum(a, b), jnp.minimum(a, b), jnp.maximum(c2, d2), jnp.minimum(c2, d2)
        top2 = jnp.maximum(hi1, hi2) + jnp.maximum(jnp.minimum(hi1, hi2), jnp.maximum(lo1, lo2))
        if best is None:
            best, best_grp = top2, jnp.zeros_like(top2, dtype=jnp.int32)
        else:
            better = top2 > best
            best = jnp.where(better, top2, best)
            best_grp = jnp.where(better, grp, best_grp)
    masked = jnp.where(row // EXPERTS_PER_GROUP == best_grp, sel, MASK_VALUE)
    _, idx1 = _first_argmax(masked, row)
    _, idx2 = _first_argmax(jnp.where(row == idx1, -jnp.inf, masked), row)
    w1 = jnp.sum(jnp.where(row == idx1, scores, 0.0), axis=0, keepdims=True)
    w2 = jnp.sum(jnp.where(row == idx2, scores, 0.0), axis=0, keepdims=True)
    total = w1 + w2

    pick1 = row == idx1
    pick2 = row == idx2
    onehot = jnp.where(pick1 | pick2, 1.0, 0.0)
    cnt = jnp.sum(onehot, axis=1, keepdims=True)
    earlier = jnp.dot(onehot.astype(BF16), tri_ref[...], preferred_element_type=F32)
    chunks = jnp.floor((cnt + (MOE_CHUNK - 1)) * (1.0 / MOE_CHUNK))
    seg_start = MOE_CHUNK * jnp.dot(tri16_ref[...], jnp.broadcast_to(chunks, (N_EXPERTS, LANES)).astype(BF16),
                                    preferred_element_type=F32)[:, 0:1]
    pos = seg_start + earlier
    pos1 = jnp.sum(jnp.where(pick1, pos, 0.0), axis=0, keepdims=True)
    pos2 = jnp.sum(jnp.where(pick2, pos, 0.0), axis=0, keepdims=True)
    route_ref[...] = jnp.concatenate([pos1, pos2, w1 / total, w2 / total, jnp.zeros((4, pos1.shape[1]), F32)], axis=0)
    cnt_ref[0] = jnp.broadcast_to(cnt, (N_EXPERTS, LANES))


def _merge(o_hg, o_da, o_rw, gates, x, mod, w_branch, w_out, ln_g, ln_b, w_router_t, router_bias, seq, tm):
    n, d = x.shape
    per_seq = seq // tm
    tile = lambda i: (i, 0)
    full = lambda i: (0, 0)
    before = (_iota((tm, tm), 0) < _iota((tm, tm), 1)).astype(BF16)
    before16 = (_iota((N_EXPERTS, N_EXPERTS), 1) < _iota((N_EXPERTS, N_EXPERTS), 0)).astype(BF16)
    return pl.pallas_call(
        _merge_body,
        grid=(n // tm,),
        in_specs=[pl.BlockSpec((tm, HG_W), tile), pl.BlockSpec((tm, DA_W), tile), pl.BlockSpec((tm, RW_W), tile),
                  pl.BlockSpec((tm, 3 * d), tile), pl.BlockSpec((tm, d), tile),
                  pl.BlockSpec((1, 6, d), lambda i: (i // per_seq, 0, 0)),
                  pl.BlockSpec(w_branch.shape, full), pl.BlockSpec(w_out.shape, full),
                  pl.BlockSpec((1, d), full), pl.BlockSpec((1, d), full),
                  pl.BlockSpec((N_EXPERTS, d), full), pl.BlockSpec((N_EXPERTS, 1), full),
                  pl.BlockSpec((tm, tm), full), pl.BlockSpec((N_EXPERTS, N_EXPERTS), full)],
        out_specs=[pl.BlockSpec((tm, d), tile), pl.BlockSpec((tm, d), tile), pl.BlockSpec((8, tm), lambda i: (0, i)),
                   pl.BlockSpec((1, N_EXPERTS, LANES), lambda i: (i, 0, 0))],
        out_shape=[jax.ShapeDtypeStruct((n, d), F32), jax.ShapeDtypeStruct((n, d), BF16),
                   jax.ShapeDtypeStruct((8, n), F32), jax.ShapeDtypeStruct((n // tm, N_EXPERTS, LANES), F32)],
        compiler_params=_params("arbitrary"),
        name="merge",
    )(o_hg, o_da, o_rw, gates, x, mod, w_branch, w_out, ln_g, ln_b, w_router_t, router_bias, before, before16)


def _local_rows(tm):
    return -(-(2 * tm + N_EXPERTS * (MOE_CHUNK - 1)) // LANES) * LANES


def _token_columns(route):
    return jnp.concatenate([route, jnp.zeros((LANES - route.shape[0], route.shape[1]), F32)], axis=0).T


def _segment_copies(i, nch_ref, loc_ref, glob_ref, local_buf, global_buf, sem, to_global):
    def run(action):
        for e in range(N_EXPERTS):
            seg = i * N_EXPERTS + e
            loc0, glob0 = loc_ref[seg], glob_ref[seg]

            def one(c, carry):
                loc = local_buf.at[pl.ds(pl.multiple_of(loc0 + c * MOE_CHUNK, MOE_CHUNK), MOE_CHUNK), :]
                glob = global_buf.at[pl.ds(pl.multiple_of(glob0 + c * MOE_CHUNK, MOE_CHUNK), MOE_CHUNK), :]
                copy = pltpu.make_async_copy(loc, glob, sem) if to_global else pltpu.make_async_copy(glob, loc, sem)
                getattr(copy, action)()
                return carry

            lax.fori_loop(0, nch_ref[seg], one, 0)
    return run


def _dispatch_body(nch_ref, loc_ref, glob_ref, u_ref, route_ref, xs_in_ref, xs_ref, stage_ref, sem):
    del xs_in_ref
    i = pl.program_id(0)
    tm, d = u_ref.shape
    route = route_ref[...]
    local_row = _iota((stage_ref.shape[0], tm), 0)
    take1 = local_row == route[0:1].astype(jnp.int32)
    take2 = local_row == route[1:2].astype(jnp.int32)
    perm = jnp.where(take1 | take2, 1.0, 0.0).astype(BF16)
    stage_ref[:, 0:d] = jnp.dot(perm, u_ref[...], preferred_element_type=F32).astype(BF16)

    cols = _token_columns(route)
    lane = _iota((tm, LANES), 1)

    def weight_cols(w):
        hi = w.astype(BF16).astype(F32)
        return jnp.where(lane == 0, hi, jnp.where(lane == 1, w - hi, 0.0)).astype(BF16)

    stage_ref[:, d:d + LANES] = (
        jnp.dot(jnp.where(take1, 1.0, 0.0).astype(BF16), weight_cols(cols[:, 2:3]), preferred_element_type=F32)
        + jnp.dot(jnp.where(take2, 1.0, 0.0).astype(BF16), weight_cols(cols[:, 3:4]), preferred_element_type=F32)
    ).astype(BF16)

    copies = _segment_copies(i, nch_ref, loc_ref, glob_ref, stage_ref, xs_ref, sem, to_global=True)
    copies("start")
    copies("wait")


def _dispatch(u2, route, nch, loc, glob, rows, tm):
    n, d = u2.shape
    width = d + LANES
    return pl.pallas_call(
        _dispatch_body,
        grid_spec=pltpu.PrefetchScalarGridSpec(
            num_scalar_prefetch=3,
            grid=(n // tm,),
            in_specs=[pl.BlockSpec((tm, d), lambda i, *_: (i, 0)),
                      pl.BlockSpec((8, tm), lambda i, *_: (0, i)),
                      pl.BlockSpec(memory_space=pl.ANY)],
            out_specs=pl.BlockSpec(memory_space=pl.ANY),
            scratch_shapes=[pltpu.VMEM((_local_rows(tm), width), BF16), pltpu.SemaphoreType.DMA],
        ),
        out_shape=jax.ShapeDtypeStruct((rows, width), BF16),
        input_output_aliases={5: 0},
        compiler_params=_params("arbitrary"),
        name="moe_dispatch",
    )(nch, loc, glob, u2, route, jnp.zeros((rows, width), BF16))


def _experts_body(te_ref, x_ref, wgu_ref, wd_ref, y_ref):
    used = te_ref[pl.program_id(0)] < N_EXPERTS

    @pl.when(used)
    def _():
        d = wgu_ref.shape[1]
        weight = x_ref[:, d:d + 1].astype(F32) + x_ref[:, d + 1:d + 2].astype(F32)
        hidden = jnp.dot(x_ref[:, 0:d], wgu_ref[0], preferred_element_type=F32)
        hg = hidden[:, :D_EXPERT]
        act = hg * _sigmoid(hg) * hidden[:, D_EXPERT:] * weight
        y_ref[...] = _mm(act, wd_ref[0]).astype(BF16)

    @pl.when(jnp.logical_not(used))
    def _():
        y_ref[...] = jnp.zeros_like(y_ref)


def _experts(xs, tile_expert, w_gu, w_down):
    rows, width = xs.shape
    d = w_down.shape[2]
    expert = lambda g, te: (jnp.minimum(te[g], N_EXPERTS - 1), 0, 0)
    return pl.pallas_call(
        _experts_body,
        grid_spec=pltpu.PrefetchScalarGridSpec(
            num_scalar_prefetch=1,
            grid=(rows // MOE_TM,),
            in_specs=[pl.BlockSpec((MOE_TM, width), lambda g, te: (g, 0)),
                      pl.BlockSpec((1, d, 2 * D_EXPERT), expert),
                      pl.BlockSpec((1, D_EXPERT, d), expert)],
            out_specs=pl.BlockSpec((MOE_TM, d), lambda g, te: (g, 0)),
        ),
        out_shape=jax.ShapeDtypeStruct((rows, d), BF16),
        compiler_params=_params("arbitrary"),
        name="moe_experts",
    )(tile_expert, xs, w_gu, w_down)


def _combine_body(nch_ref, loc_ref, glob_ref, route_ref, x1_ref, mod_ref, lng_ref, lnb_ref, ys_ref, o_ref,
                  back_ref, sem):
    i = pl.program_id(0)

    @pl.when(i == 0)
    def _():
        back_ref[...] = jnp.zeros_like(back_ref)

    copies = _segment_copies(i, nch_ref, loc_ref, glob_ref, back_ref, ys_ref, sem, to_global=False)
    copies("start")
    cols = _token_columns(route_ref[...]).astype(jnp.int32)
    local_row = _iota((x1_ref.shape[0], back_ref.shape[0]), 1)
    unperm = jnp.where((local_row == cols[:, 0:1]) | (local_row == cols[:, 1:2]), 1.0, 0.0).astype(BF16)
    copies("wait")
    ffn = jnp.dot(unperm, back_ref[...], preferred_element_type=F32)
    y = ALPHA * x1_ref[...] + (1.0 + mod_ref[0, 5:6, :]) * ffn
    o_ref[...] = _layer_norm(y, lng_ref[...], lnb_ref[...])


def _combine(ys, route, nch, loc, glob, x1, mod, ln_g, ln_b, seq, tm):
    n, d = x1.shape
    per_seq = seq // tm
    full = lambda i, *_: (0, 0)
    return pl.pallas_call(
        _combine_body,
        grid_spec=pltpu.PrefetchScalarGridSpec(
            num_scalar_prefetch=3,
            grid=(n // tm,),
            in_specs=[pl.BlockSpec((8, tm), lambda i, *_: (0, i)),
                      pl.BlockSpec((tm, d), lambda i, *_: (i, 0)),
                      pl.BlockSpec((1, 6, d), lambda i, *_: (i // per_seq, 0, 0)),
                      pl.BlockSpec((1, d), full), pl.BlockSpec((1, d), full),
                      pl.BlockSpec(memory_space=pl.ANY)],
            out_specs=pl.BlockSpec((tm, d), lambda i, *_: (i, 0)),
            scratch_shapes=[pltpu.VMEM((_local_rows(tm), d), BF16), pltpu.SemaphoreType.DMA],
        ),
        out_shape=jax.ShapeDtypeStruct((n, d), F32),
        compiler_params=_params("arbitrary"),
        name="moe_combine",
    )(nch, loc, glob, route, x1, mod, ln_g, ln_b, ys)


def _moe(u2, route, counts, w_gu, w_down, x1, mod, ln_g, ln_b, seq, tm):
    n = u2.shape[0]
    n_tiles = n // tm
    seg_rows = (counts + MOE_CHUNK - 1) // MOE_CHUNK * MOE_CHUNK
    loc = jnp.cumsum(seg_rows, axis=1) - seg_rows
    region = (jnp.sum(seg_rows, axis=0) + MOE_TM - 1) // MOE_TM * MOE_TM
    region_end = jnp.cumsum(region)
    glob = (region_end - region)[None, :] + jnp.cumsum(seg_rows, axis=0) - seg_rows
    rows = -(-(2 * n + n_tiles * N_EXPERTS * (MOE_CHUNK - 1) + N_EXPERTS * (MOE_TM - 1)) // MOE_TM) * MOE_TM
    tile_expert = jnp.sum(jnp.arange(rows // MOE_TM, dtype=jnp.int32)[:, None] * MOE_TM >= region_end[None, :],
                          axis=1).astype(jnp.int32)
    flat = lambda a: a.reshape(-1).astype(jnp.int32)
    nch, loc, glob = flat(seg_rows // MOE_CHUNK), flat(loc), flat(glob)
    xs = _dispatch(u2, route, nch, loc, glob, rows, tm)
    ys = _experts(xs, tile_expert, w_gu, w_down)
    return _combine(ys, route, nch, loc, glob, x1, mod, ln_g, ln_b, seq, tm)


def _tiles(seq):
    return min(512, seq), min(256, seq // 2)


def kernel(x, c, w_ada, b_ada, w_in, hg_lb_logits, hg_norm_g, da_lambda, da_subln_g, rw_mu, rw_w0, rw_w_up,
           rw_a0, rw_a_up, rw_g_up, rw_k_k, rw_k_a, rw_r_k, rw_gn_g, rw_gn_b, w_merge, b_merge, w_branch, w_out,
           ln_g, ln_b, w_router, router_bias, w_exp_gate, w_exp_up, w_exp_down):
    bsz, seq, d = x.shape
    depth = w_in.shape[0]
    n = bsz * seq
    tm, blk = _tiles(seq)

    sm = jax.nn.softmax(hg_lb_logits.astype(F32), axis=0)
    hg_lb = jnp.cumsum(sm, axis=0) - sm[0:1]
    slopes = jnp.asarray([2.0 ** (-8.0 * (h + 1) / DA_HEADS) for h in range(DA_HEADS)], F32)

    mod_all = _ada(c, w_ada, b_ada).reshape(depth, bsz, 6, d)
    w_router_t = w_router.T
    router_bias = router_bias.reshape(N_EXPERTS, 1)

    xf = x.reshape(n, d)
    for l in range(depth):
        mod = mod_all[l]
        lq1, lk1, lq2, lk2 = da_lambda[l].astype(F32)
        lam_init = 0.8 - 0.6 * math.exp(-0.3 * l)
        lam = jnp.exp(jnp.sum(lq1 * lk1)) - jnp.exp(jnp.sum(lq2 * lk2)) + lam_init
        scal = jnp.concatenate([jnp.stack([lam, jnp.asarray(1.0 - lam_init, F32)]), slopes])

        w_gates = jnp.concatenate([w_merge[l, br] for br in range(3)], axis=1).astype(BF16)
        y, gates = _proj(xf, mod, w_in[l].astype(BF16), w_gates, b_merge[l].reshape(1, 3 * d), seq, tm)

        o_hg = _hgrn2(y, hg_lb[l].reshape(1, HG_W), hg_norm_g[l].reshape(1, HG_DV), bsz, seq)
        o_da = _attn(y, scal, da_subln_g[l].reshape(1, DA_DV), bsz, seq, blk)
        rw = dict(mu=rw_mu[l].reshape(1, -1), w0=rw_w0[l].reshape(1, -1), w_up=rw_w_up[l],
                  a0=rw_a0[l].reshape(1, -1), a_up=rw_a_up[l], g_up=rw_g_up[l],
                  k_k=rw_k_k[l].reshape(1, -1), k_a=rw_k_a[l].reshape(1, -1), r_k=rw_r_k[l].reshape(1, -1),
                  gn_g=rw_gn_g[l].reshape(1, -1), gn_b=rw_gn_b[l].reshape(1, -1))
        o_rw = _rwkv(y, rw, bsz, seq)

        x1, u2, route, counts = _merge(o_hg, o_da, o_rw, gates, xf, mod, w_branch[l].astype(BF16),
                                       w_out[l].astype(BF16), ln_g[l, 0].reshape(1, d), ln_b[l, 0].reshape(1, d),
                                       w_router_t, router_bias, seq, tm)
        w_gu = jnp.concatenate([w_exp_gate[l], w_exp_up[l]], axis=-1).astype(BF16)
        xf = _moe(u2, route, counts[:, :, 0].astype(jnp.int32), w_gu, w_exp_down[l].astype(BF16), x1, mod,
                  ln_g[l, 1].reshape(1, d), ln_b[l, 1].reshape(1, d), seq, tm)
    return xf.reshape(bsz, seq, d)
```

```python
import functools
import math

import jax
import jax.numpy as jnp
from jax import lax
from jax.experimental import pallas as pl
from jax.experimental.pallas import tpu as pltpu

D_MODEL = 1024
DEPTH = 4
HG_HEADS, HG_DK, HG_DV, HG_CHUNK, HG_SUB = 4, 128, 128, 64, 16
HG_W = HG_HEADS * HG_DV
HG_F_MIN = 1e-6
DA_HEADS, DA_DQK = 4, 64
DA_DV = 2 * DA_DQK
DA_W = DA_HEADS * DA_DV
MASK_VALUE = -1e30
LOG2E = math.log2(math.e)
PROJ_TN = 768
RW_HEADS, RW_DH, RW_CHUNK, RW_SUB = 8, 64, 64, 16
RW_STEP_CHUNKS = 2
RW_STEP_SEQS = 2
RW_W = RW_HEADS * RW_DH
RW_IN_W = 1792
RW_GN_EPS = 64e-5
IN_W = 5376
HG_COL, DA_COL, RW_COL = 0, 2048, 3584
N_EXPERTS, N_GROUPS, EXPERTS_PER_GROUP, D_EXPERT = 16, 4, 4, 512
MOE_CHUNK = 16
MOE_TM = 512
ALPHA = (2.0 * DEPTH) ** 0.25
LN_EPS = 1e-5
RMS_EPS = 1e-6
LANES = 128

F32 = jnp.float32
BF16 = jnp.bfloat16
HIGHEST = lax.Precision.HIGHEST
VMEM_LIMIT = 48 * 1024 * 1024

_NT = (((1,), (1,)), ((), ()))
_TN = (((0,), (0,)), ((), ()))


def _mm(a, b):
    return jnp.dot(a.astype(BF16), b.astype(BF16), preferred_element_type=F32)


def _mm_nt(a, b):
    return lax.dot_general(a.astype(BF16), b.astype(BF16), _NT, preferred_element_type=F32)


def _mm_tn(a, b):
    return lax.dot_general(a.astype(BF16), b.astype(BF16), _TN, preferred_element_type=F32)


def _mmh(a, b):
    return jnp.dot(a, b, precision=HIGHEST, preferred_element_type=F32)


def _mmh_nt(a, b):
    return lax.dot_general(a, b, _NT, precision=HIGHEST, preferred_element_type=F32)


def _seg_sum(x, seg):
    rows = x.shape[0]
    hi = x.astype(BF16)
    lo = (x - hi.astype(F32)).astype(BF16)
    halves = []
    for c0 in range(0, x.shape[1], seg.shape[0]):
        cols = slice(c0, c0 + seg.shape[0])
        both = jnp.dot(jnp.concatenate([hi[:, cols], lo[:, cols]], axis=0), seg, preferred_element_type=F32)
        halves.append(both[:rows] + both[rows:])
    return jnp.concatenate(halves, axis=1)


def _split_mm(x, w_twice, w_lo):
    hi = x.astype(BF16)
    lo = (x - hi.astype(F32)).astype(BF16)
    return (jnp.dot(jnp.concatenate([hi, lo], axis=1), w_twice, preferred_element_type=F32)
            + jnp.dot(hi, w_lo, preferred_element_type=F32))


def _chunk_cumsum(x):
    c = x.shape[0]
    hi = x.astype(BF16)
    rest = x - hi.astype(F32)
    mid = rest.astype(BF16)
    lo = (rest - mid.astype(F32)).astype(BF16)
    col = _iota((c, 4 * c), 1)
    tri = ((col & (c - 1)) <= _iota((c, 4 * c), 0)) & (col < 3 * c)
    return jnp.dot(jnp.where(tri, 1.0, 0.0).astype(BF16), jnp.concatenate([hi, mid, lo, lo], axis=0),
                   preferred_element_type=F32)


def _sigmoid(x):
    return 1.0 / (1.0 + jnp.exp(-x))


def _softplus(x):
    return jnp.maximum(x, 0.0) + jnp.log(1.0 + jnp.exp(-jnp.abs(x)))


def _iota(shape, dim):
    return lax.broadcasted_iota(jnp.int32, shape, dim)


def _params(*sem):
    return pltpu.CompilerParams(dimension_semantics=sem, vmem_limit_bytes=VMEM_LIMIT)


def _layer_norm(y, g, b):
    mu = jnp.mean(y, axis=-1, keepdims=True)
    d = y - mu
    var = jnp.mean(d * d, axis=-1, keepdims=True)
    return d * lax.rsqrt(var + LN_EPS) * g + b


def _ada_body(c_ref, w_ref, b_ref, o_ref):
    c = c_ref[...]
    o_ref[0] = _mmh(c * _sigmoid(c), w_ref[0]) + b_ref[0]


def _ada(c, w_ada, b_ada):
    depth, d, _ = w_ada.shape
    bsz = c.shape[0]
    return pl.pallas_call(
        _ada_body,
        grid=(depth, 6),
        in_specs=[pl.BlockSpec((bsz, d), lambda l, j: (0, 0)),
                  pl.BlockSpec((1, d, d), lambda l, j: (l, 0, j)),
                  pl.BlockSpec((1, 1, d), lambda l, j: (l, 0, j))],
        out_specs=pl.BlockSpec((1, bsz, d), lambda l, j: (l, 0, j)),
        out_shape=jax.ShapeDtypeStruct((depth, bsz, 6 * d), F32),
        compiler_params=_params("arbitrary", "arbitrary"),
        name="ada",
    )(c, w_ada, b_ada.reshape(depth, 1, 6 * d))


def _proj_body(x_ref, mod_ref, win_ref, wg_ref, bg_ref, y_ref, g_ref, vt_ref):
    u = (x_ref[...] * (1.0 + mod_ref[0, 1:2, :]) + mod_ref[0, 0:1, :]).astype(BF16)
    n_kb, kb = vt_ref.shape[1], vt_ref.shape[4]
    v_col = DA_COL + 2 * DA_W
    for c0 in range(0, IN_W, PROJ_TN):
        cols = slice(c0, c0 + PROJ_TN)
        res = jnp.dot(u, win_ref[:, cols], preferred_element_type=F32)
        y_ref[:, cols] = res.astype(BF16)
        if c0 == v_col:
            for h in range(DA_HEADS):
                v_t = res[:, h * DA_DV:(h + 1) * DA_DV].T.astype(BF16)
                for kbi in range(n_kb):
                    vt_ref[0, kbi, h, 0:DA_DV, :] = v_t[:, kbi * kb:(kbi + 1) * kb]
    ones_row = jnp.where(_iota((n_kb, DA_HEADS, 8, kb), 2) == 0, 1.0, 0.0).astype(BF16)
    vt_ref[0, :, :, DA_DV:DA_DV + 8, :] = ones_row
    for c0 in range(0, 3 * D_MODEL, PROJ_TN):
        cols = slice(c0, c0 + PROJ_TN)
        g = jnp.dot(u, wg_ref[:, cols], preferred_element_type=F32) + bg_ref[:, cols]
        g_ref[:, cols] = _sigmoid(g).astype(BF16)


def _proj(x, mod, w_in, w_gates, b_gates, seq, tm, kb):
    n, d = x.shape
    per_seq = seq // tm
    tile = lambda i: (i, 0)
    resident = dict(index_map=lambda i: (0, 0), pipeline_mode=pl.Buffered(1))
    vt_shape = (n // seq, seq // kb, DA_HEADS, DA_DV + 8, kb)
    return pl.pallas_call(
        _proj_body,
        grid=(n // tm,),
        in_specs=[pl.BlockSpec((tm, d), tile),
                  pl.BlockSpec((1, 6, d), lambda i: (i // per_seq, 0, 0)),
                  pl.BlockSpec(w_in.shape, **resident),
                  pl.BlockSpec(w_gates.shape, **resident),
                  pl.BlockSpec(b_gates.shape, **resident)],
        out_specs=[pl.BlockSpec((tm, IN_W), tile), pl.BlockSpec((tm, 3 * d), tile),
                   pl.BlockSpec((1, tm // kb) + vt_shape[2:], lambda i: (i // per_seq, i % per_seq, 0, 0, 0))],
        out_shape=[jax.ShapeDtypeStruct((n, IN_W), BF16), jax.ShapeDtypeStruct((n, 3 * d), BF16),
                   jax.ShapeDtypeStruct(vt_shape, BF16)],
        compiler_params=_params("arbitrary"),
        name="proj",
    )(x, mod, w_in, w_gates, b_gates)


def _hgrn2_body(y_ref, lb_ref, ng_ref, o_ref, st_ref):
    c, sub = HG_CHUNK, HG_SUB

    @pl.when(pl.program_id(1) == 0)
    def _():
        st_ref[...] = jnp.zeros_like(st_ref)

    q = y_ref[:, 0:HG_W].astype(F32)
    z = y_ref[:, HG_W:2 * HG_W].astype(F32)
    v = y_ref[:, 2 * HG_W:3 * HG_W]
    og = y_ref[:, 3 * HG_W:4 * HG_W].astype(F32)
    lb = lb_ref[...]
    f = lb + (1.0 - lb) * _sigmoid(z)
    kin = (1.0 - lb) * _sigmoid(-z)
    b = _chunk_cumsum(jnp.log(jnp.maximum(f, HG_F_MIN)))
    log_k = jnp.log(kin)
    rel = log_k - b
    b_last = b[c - 1:c]
    q_dec = (q * jnp.exp(b)).astype(BF16)
    k_tail = (kin * jnp.exp(b_last - b)).astype(BF16)
    decay = jnp.exp(b_last)

    ones = jnp.ones((HG_DK, LANES), BF16)
    row_s = _iota((sub, c), 0)
    col_s = _iota((sub, c), 1)
    heads = range(HG_HEADS)
    blks = range(c // sub)
    hs = [slice(h * HG_DK, (h + 1) * HG_DK) for h in heads]

    b2, rel2, lk2 = b * LOG2E, rel * LOG2E, log_k * LOG2E
    half = sub // 2
    diag = {}
    for h in heads:
        for blk in blks:
            rows = slice(blk * sub, (blk + 1) * sub)
            b_i, q_i, rel_i, lk_i = b2[rows, hs[h]], q[rows, hs[h]], rel2[rows, hs[h]], lk2[rows, hs[h]]
            terms = [q_i[t0:] * jnp.exp2(jnp.minimum(b_i[t0:] + rel_i[s:s + 1], lk_i[s:s + 1]))
                     for s in range(sub) for t0 in [0 if s < half else half]]
            w = jnp.concatenate(terms, axis=0).astype(BF16)
            diag[h, blk] = jnp.dot(w, ones, preferred_element_type=F32)
    below = {}
    for h in heads:
        for blk in blks[1:]:
            r0 = blk * sub
            beta = b[r0 - 1:r0, hs[h]]
            q_t = q[r0:r0 + sub, hs[h]] * jnp.exp(b[r0:r0 + sub, hs[h]] - beta)
            k_h = kin[:, hs[h]] * jnp.exp(jnp.minimum(beta - b[:, hs[h]], 0.0))
            below[h, blk] = _mm_nt(q_t, k_h)
    scores = []
    for h in heads:
        a_rows = []
        for blk in blks:
            r0 = blk * sub
            a_blk = jnp.zeros((sub, c), F32)
            for s in range(sub):
                if s < half:
                    sums = diag[h, blk][s * sub:(s + 1) * sub, :c]
                else:
                    start = half * sub + (s - half) * half
                    sums = jnp.concatenate([jnp.zeros((half, c), F32), diag[h, blk][start:start + half, :c]], axis=0)
                a_blk = jnp.where(col_s == r0 + s, sums, a_blk)
            if blk > 0:
                a_blk = jnp.where(col_s < r0, below[h, blk], a_blk)
            a_rows.append(jnp.where(col_s <= row_s + r0, a_blk, 0.0))
        scores.append(jnp.concatenate(a_rows, axis=0).astype(BF16))

    st = [st_ref[h] for h in heads]
    intra = [jnp.dot(scores[h], v[:, hs[h]], preferred_element_type=F32) for h in heads]
    inter = [lax.dot_general(q_dec[:, hs[h]], st[h].astype(BF16), _NT, preferred_element_type=F32) for h in heads]
    for h in heads:
        st_ref[h] = st[h] * decay[:, hs[h]] + lax.dot_general(v[:, hs[h]], k_tail[:, hs[h]], _TN,
                                                             preferred_element_type=F32)
    for h in heads:
        o = intra[h] + inter[h]
        o = o * lax.rsqrt(jnp.mean(o * o, axis=-1, keepdims=True) + RMS_EPS) * ng_ref[...]
        o_ref[:, hs[h]] = (o * (og[:, hs[h]] * _sigmoid(og[:, hs[h]]))).astype(o_ref.dtype)


def _hgrn2(y, lb, norm_g, bsz, seq):
    n = y.shape[0]
    nc = seq // HG_CHUNK
    width = 4 * HG_W
    return pl.pallas_call(
        _hgrn2_body,
        grid=(bsz, nc),
        in_specs=[pl.BlockSpec((HG_CHUNK, width), lambda b, c: (b * nc + c, HG_COL // width)),
                  pl.BlockSpec((1, HG_W), lambda b, c: (0, 0)),
                  pl.BlockSpec((1, HG_DV), lambda b, c: (0, 0))],
        out_specs=pl.BlockSpec((HG_CHUNK, HG_W), lambda b, c: (b * nc + c, 0)),
        out_shape=jax.ShapeDtypeStruct((n, HG_W), BF16),
        scratch_shapes=[pltpu.VMEM((HG_HEADS, HG_DV, HG_DK), F32)],
        compiler_params=_params("arbitrary", "arbitrary"),
        name="hgrn2",
    )(y, lb, norm_g)


def _attn_body(scal_ref, q_ref, k_ref, vt_ref, g_ref, o_ref, qq_ref, sa_ref, sb_ref, p_ref, m_ref, sc_ref,
               acc_ref, *, kb):
    h = pl.program_id(1)
    i = pl.program_id(2)
    lam = scal_ref[0]
    out_scale = scal_ref[1]
    slope = scal_ref[2 + h] * LOG2E
    qb = 2 * kb
    q0 = i * qb
    tiles_per_map = qb // LANES
    a_tiles = kb // LANES

    q = q_ref[...].astype(F32) * (DA_DQK ** -0.5 * LOG2E)
    lane = _iota(q.shape, 1)
    stacked = jnp.concatenate([jnp.where(lane < DA_DQK, q, 0.0), jnp.where(lane >= DA_DQK, q, 0.0)], axis=0)
    qq_ref[...] = stacked.T.astype(BF16)
    m_ref[...] = jnp.full(m_ref.shape, MASK_VALUE, F32)
    sc_ref[...] = jnp.ones(sc_ref.shape, F32)
    acc_ref[...] = jnp.zeros(acc_ref.shape, F32)
    p_ref[...] = jnp.zeros(p_ref.shape, BF16)
    key_off = _iota((kb, LANES), 0)
    rel = slope * key_off.astype(F32)

    def scores(j):
        return jnp.dot(k_ref[pl.ds(j * kb, kb), :], qq_ref[...], preferred_element_type=F32)

    def softmax(j, src_ref, diagonal_of):
        bias = rel + slope * (j * kb - q0).astype(F32)
        for t in range(2 * tiles_per_map):
            cols = slice(t * LANES, (t + 1) * LANES)
            in_map = t % tiles_per_map
            half = "A" if in_map < a_tiles else "B"
            if diagonal_of == "B" and half == "A":
                p_ref[:, cols] = jnp.zeros((kb, LANES), BF16)
                sc_ref[:, cols] = jnp.ones((1, LANES), F32)
                continue
            s = src_ref[:, cols] + bias
            if diagonal_of == half:
                q_off = _iota((kb, LANES), 1) + (in_map % a_tiles) * LANES
                s = jnp.where(key_off <= q_off, s, MASK_VALUE)
            m_old = m_ref[:, cols]
            m_new = jnp.maximum(m_old, jnp.max(s, axis=0, keepdims=True))
            p_ref[:, cols] = jnp.exp2(s - m_new).astype(BF16)
            sc_ref[:, cols] = jnp.exp2(m_old - m_new)
            m_ref[:, cols] = m_new

    def iteration(j, src_ref, dst_ref, diagonal_of=None):
        sc_prev = sc_ref[...]
        pv = jnp.dot(vt_ref[0, jnp.maximum(j - 1, 0), 0], p_ref[...], preferred_element_type=F32)
        if dst_ref is not None:
            dst_ref[...] = scores(j + 1)
        softmax(j, src_ref, diagonal_of)
        acc_ref[...] = acc_ref[...] * sc_prev + pv

    def pair(j0):
        iteration(j0, sa_ref, sb_ref)
        iteration(j0 + 1, sb_ref, sa_ref)

    def body(quad, carry):
        pair(4 * quad)
        pair(4 * quad + 2)
        return carry

    sa_ref[...] = scores(0)
    lax.fori_loop(0, i // 2, body, 0)

    @pl.when(i % 2 == 1)
    def _():
        pair(2 * i - 2)

    iteration(2 * i, sa_ref, sb_ref, "A")
    iteration(2 * i + 1, sb_ref, None, "B")
    acc = acc_ref[...] * sc_ref[...] + jnp.dot(vt_ref[0, 2 * i + 1, 0], p_ref[...], preferred_element_type=F32)
    o = acc[:DA_DV] / acc[DA_DV:DA_DV + 1]
    d = o[:, :qb] - lam * o[:, qb:]
    g = jnp.concatenate([g_ref[...]] * tiles_per_map, axis=1)
    d = d * lax.rsqrt(jnp.mean(d * d, axis=0, keepdims=True) + RMS_EPS) * g * out_scale
    o_ref[...] = d.T.astype(o_ref.dtype)


def _attn(y, v_t, scal, subln_g, bsz, seq, kb):
    n = y.shape[0]
    qb = 2 * kb
    nq, nk = seq // qb, seq // kb
    qc, kc = DA_COL // DA_DV, (DA_COL + DA_W) // DA_DV
    rows_v = v_t.shape[3]
    g_col = jnp.broadcast_to(subln_g.reshape(DA_DV, 1), (DA_DV, LANES))
    row = (1, 2 * qb)
    return pl.pallas_call(
        functools.partial(_attn_body, kb=kb),
        scratch_shapes=[pltpu.VMEM((DA_DV, 2 * qb), BF16), pltpu.VMEM((kb, 2 * qb), F32),
                        pltpu.VMEM((kb, 2 * qb), F32), pltpu.VMEM((kb, 2 * qb), BF16), pltpu.VMEM(row, F32),
                        pltpu.VMEM(row, F32), pltpu.VMEM((rows_v, 2 * qb), F32)],
        grid=(bsz, DA_HEADS, nq),
        in_specs=[pl.BlockSpec(memory_space=pltpu.SMEM),
                  pl.BlockSpec((qb, DA_DV), lambda b, h, i: (b * nq + i, qc + h)),
                  pl.BlockSpec((seq, DA_DV), lambda b, h, i: (b, kc + h)),
                  pl.BlockSpec((1, nk, 1, rows_v, kb), lambda b, h, i: (b, 0, h, 0, 0)),
                  pl.BlockSpec((DA_DV, LANES), lambda b, h, i: (0, 0))],
        out_specs=pl.BlockSpec((qb, DA_DV), lambda b, h, i: (b * nq + i, h)),
        out_shape=jax.ShapeDtypeStruct((n, DA_W), BF16),
        compiler_params=_params("arbitrary", "arbitrary", "arbitrary"),
        name="diffattn",
    )(scal, y, y, v_t, g_col)


def _split_f32(x):
    hi = x.astype(BF16)
    hi_f = hi.astype(F32)
    return hi, hi_f, x - hi_f


def _dup_lhs(hi_f, lo_f, low_half):
    packed = jnp.where(low_half, hi_f, lo_f).astype(BF16)
    return jnp.concatenate([packed, packed], axis=1)


def _dup_rhs(hi, lo_f):
    lo = lo_f.astype(BF16)
    return jnp.concatenate([hi, hi, lo, lo], axis=0)


def _rwkv_body(y_ref, mu_ref, w0_ref, a0_ref, wa2_ref, walo_ref, gu2_ref, gulo_ref, kk_ref, ka_ref, rk_ref,
               gng_ref, gnb_ref, seg_ref, o_ref, st_ref, prev_ref, osc_ref):
    c, sub, dh = RW_CHUNK, RW_SUB, RW_DH

    @pl.when(pl.program_id(1) == 0)
    def _():
        st_ref[...] = jnp.zeros_like(st_ref)
        prev_ref[...] = jnp.zeros_like(prev_ref)

    n_seq, seq_rows = y_ref.shape[0], y_ref.shape[1]
    rows = n_seq * seq_rows
    shifted = []
    for si in range(n_seq):
        x = y_ref[si].astype(F32)
        x_prev = jnp.where(_iota(x.shape, 0) == 0, prev_ref[si:si + 1], pltpu.roll(x, 1, axis=0))
        prev_ref[si:si + 1] = x[seq_rows - 1:seq_rows]
        shifted.append(x + (x_prev - x) * mu_ref[...])
    xs = jnp.concatenate(shifted, axis=0)
    r = xs[:, 0:RW_W]
    k = xs[:, RW_W:2 * RW_W]
    v = xs[:, 2 * RW_W:3 * RW_W]
    wa = xs[:, 3 * RW_W:3 * RW_W + LANES]
    gd = xs[:, 3 * RW_W + LANES:RW_IN_W]

    lora = _split_mm(jnp.where(_iota(wa.shape, 1) < 64, jnp.tanh(wa), wa), wa2_ref[...], walo_ref[...])
    w_log = -_softplus(-(w0_ref[...] + lora[:, :RW_W])) - 0.5
    g = -jnp.exp(w_log)
    a = _sigmoid(a0_ref[...] + lora[:, RW_W:])
    gate = _split_mm(_sigmoid(gd), gu2_ref[...], gulo_ref[...])
    seg = seg_ref[...]
    kk = k * kk_ref[...]
    k2 = k * (1.0 + (a - 1.0) * ka_ref[...])
    sums = _seg_sum(jnp.concatenate([kk * kk, r * k2 * rk_ref[...]], axis=0), seg)
    kk = kk * lax.rsqrt(jnp.maximum(sums[:rows], 1e-12))
    bb = kk * a
    bonus = sums[rows:] * v

    chunks = range(rows // c)
    cr = [slice(ci * c, (ci + 1) * c) for ci in chunks]
    gc = jnp.concatenate([_chunk_cumsum(g[s]) for s in cr], axis=0)
    g_last = [gc[s][c - 1:c] for s in cr]
    e_inv = jnp.exp(-gc)
    e_tail = jnp.exp(jnp.concatenate([jnp.broadcast_to(gl, (c, RW_W)) for gl in g_last], axis=0) - gc)
    gam = [jnp.exp(gl) for gl in g_last]
    a_t = (-kk * jnp.exp(gc - g)).astype(BF16)
    r_t = (r * jnp.exp(gc)).astype(BF16)
    b_h = (bb * e_inv).astype(BF16)
    k_h = (k2 * e_inv).astype(BF16)
    k_bar = (k2 * e_tail).astype(BF16)
    b_bar = (bb * e_tail).astype(BF16)
    v_bf = v.astype(BF16)

    row2 = _iota((c, 2 * c), 0)
    lane2 = _iota((c, 2 * c), 1)
    col2 = lane2 & (c - 1)
    low_half = lane2 < c
    strict = row2 > col2
    incl = row2 >= col2
    same_blk = (row2 // sub) == (col2 // sub)
    eye = (row2 == col2).astype(F32)

    def dot(p, q):
        return jnp.dot(p, q, preferred_element_type=F32)

    heads = range(RW_HEADS)
    sls = [slice(h * dh, (h + 1) * dh) for h in heads]
    items = [(ci, h) for ci in chunks for h in heads]
    ar_h = [jnp.concatenate([a_t[cr[ci], sls[h]], r_t[cr[ci], sls[h]]], axis=0) for ci, h in items]
    quad = [lax.dot_general(x, jnp.concatenate([b_h[cr[ci], sls[h]], k_h[cr[ci], sls[h]]], axis=0), _NT,
                            preferred_element_type=F32)
            for x, (ci, h) in zip(ar_h, items)]
    top = [jnp.where(strict, q[:c], 0.0) for q in quad]
    a_ak = [t[:, c:].astype(BF16) for t in top]
    a_r = [jnp.where(incl, q[c:], 0.0).astype(BF16) for q in quad]
    a_ab = [jnp.where(low_half, t, pltpu.roll(t, c, axis=1)) for t in top]
    a_d = [jnp.where(same_blk, x, 0.0) for x in a_ab]
    a_o = [(x - y).astype(BF16) for x, y in zip(a_ab, a_d)]

    s1 = [_split_f32(x) for x in a_d]
    p2 = [dot(_dup_lhs(hf, lf, low_half), _dup_rhs(hi, lf)) for hi, hf, lf in s1]
    s2 = [_split_f32(x) for x in p2]
    rhs2 = [_dup_rhs(hi, lf) for hi, _, lf in s2]
    p4 = [dot(_dup_lhs(hf, lf, low_half), rhs) for (_, hf, lf), rhs in zip(s2, rhs2)]
    s4 = [_split_f32(x) for x in p4]
    rhs4 = [_dup_rhs(hi, lf) for hi, _, lf in s4]
    p8 = [dot(_dup_lhs(hf, lf, low_half), rhs) for (_, hf, lf), rhs in zip(s4, rhs4)]
    rhs8 = [_dup_rhs(hi, lf) for hi, _, lf in (_split_f32(x) for x in p8)]
    t_d = [eye + x for x in a_d]
    for rhs_all in (rhs2, rhs4, rhs8):
        st = [_split_f32(x) for x in t_d]
        t_d = [x + dot(_dup_lhs(hf, lf, low_half), rhs) for x, (_, hf, lf), rhs in zip(t_d, st, rhs_all)]
    t_d = [x.astype(BF16) for x in t_d]

    nn = [dot(t[:, :c], x) for t, x in zip(t_d, a_o)]
    nn_bf = [x.astype(BF16) for x in nn]
    n2 = [dot(x[:, :c], x) for x in nn_bf]
    n3 = [dot(x[:, :c], y.astype(BF16)) for x, y in zip(nn_bf, n2)]
    t_m = [dot((eye + x + y + z).astype(BF16)[:, :c], t).astype(BF16)[:, :c]
           for x, y, z, t in zip(nn, n2, n3, t_d)]

    v_h = [v_bf[cr[ci], sls[h]] for ci, h in items]
    akv = [dot(x, y).astype(BF16) for x, y in zip(a_ak, v_h)]
    at_m = [dot(t, x[:c]).astype(BF16) for t, x in zip(t_m, ar_h)]
    v_p = [dot(t, x) for t, x in zip(t_m, akv)]

    per_seq = seq_rows // c
    chains = [(si, h) for si in range(n_seq) for h in heads]
    state = [st_ref[si, h] for si, h in chains]
    for t in range(per_seq):
        it = [(si * per_seq + t) * RW_HEADS + h for si, h in chains]
        ck = [cr[si * per_seq + t] for si, _ in chains]
        proj = [lax.dot_general(jnp.concatenate([at_m[j], ar_h[j][c:]], axis=0), s.astype(BF16), _NT,
                                preferred_element_type=F32) for j, s in zip(it, state)]
        u = [(p[:c] + v_p[j]).astype(BF16) for p, j in zip(proj, it)]
        for p, uu, j, rws, (_, h) in zip(proj, u, it, ck, chains):
            osc_ref[rws, sls[h]] = p[c:] + dot(a_r[j], jnp.concatenate([uu, v_h[j]], axis=0))
        state = [s * gam[j // RW_HEADS][:, sls[h]] + lax.dot_general(
            jnp.concatenate([v_h[j], uu], axis=0),
            jnp.concatenate([k_bar[rws, sls[h]], b_bar[rws, sls[h]]], axis=0), _TN, preferred_element_type=F32)
            for s, uu, j, rws, (_, h) in zip(state, u, it, ck, chains)]
    for s, (si, h) in zip(state, chains):
        st_ref[si, h] = s

    o = osc_ref[...]
    mean = _seg_sum(o, seg) * (1.0 / dh)
    d = o - mean
    var = _seg_sum(d * d, seg) * (1.0 / dh)
    o = d * lax.rsqrt(var + RW_GN_EPS) * gng_ref[...] + gnb_ref[...]
    o = ((o + bonus) * gate).astype(o_ref.dtype)
    for si in range(n_seq):
        o_ref[si] = o[si * seq_rows:(si + 1) * seq_rows]


def _rwkv(y, p, bsz, seq):
    n = y.shape[0]
    half = RW_W // 2
    seg = (_iota((half, half), 0) // RW_DH == _iota((half, half), 1) // RW_DH).astype(BF16)

    def two_terms(w):
        hi = w.astype(BF16)
        return jnp.concatenate([hi, hi], axis=0), (w - hi.astype(F32)).astype(BF16)

    zeros = jnp.zeros_like(p["w_up"])
    wa2, wa_lo = two_terms(jnp.concatenate([jnp.concatenate([p["w_up"], zeros], axis=1),
                                            jnp.concatenate([zeros, p["a_up"]], axis=1)], axis=0))
    gu2, gu_lo = two_terms(p["g_up"])
    rows = [p["mu"], p["w0"], p["a0"], wa2, wa_lo, gu2, gu_lo, p["k_k"], p["k_a"], p["r_k"],
            p["gn_g"], p["gn_b"], seg]
    full = lambda b, c: (0, 0)
    step = RW_STEP_CHUNKS * RW_CHUNK
    n_seq = RW_STEP_SEQS if bsz % RW_STEP_SEQS == 0 else 1
    out = pl.pallas_call(
        _rwkv_body,
        grid=(bsz // n_seq, seq // step),
        in_specs=[pl.BlockSpec((n_seq, step, RW_IN_W), lambda b, c: (b, c, RW_COL // RW_IN_W))]
        + [pl.BlockSpec(a.shape, full) for a in rows],
        out_specs=pl.BlockSpec((n_seq, step, RW_W), lambda b, c: (b, c, 0)),
        out_shape=jax.ShapeDtypeStruct((bsz, seq, RW_W), BF16),
        scratch_shapes=[pltpu.VMEM((n_seq, RW_HEADS, RW_DH, RW_DH), F32),
                        pltpu.VMEM((n_seq, RW_IN_W), F32),
                        pltpu.VMEM((n_seq * step, RW_W), F32)],
        compiler_params=_params("arbitrary", "arbitrary"),
        name="rwkv7",
    )(y.reshape(bsz, seq, -1), *rows)
    return out.reshape(n, RW_W)


def _first_argmax(vals, row):
    top = jnp.max(vals, axis=0, keepdims=True)
    idx = jnp.min(jnp.where(vals == top, row, N_EXPERTS), axis=0, keepdims=True)
    return top, idx


def _merge_body(ohg_ref, oda_ref, orw_ref, gt_ref, x_ref, mod_ref, wb_ref, wo_ref, lng_ref, lnb_ref,
                wrt_ref, rb_ref, tri_ref, tri16_ref, x1_ref, u2_ref, route_ref, cnt_ref):
    d = D_MODEL
    merged = (gt_ref[:, 0:d].astype(F32) * jnp.dot(ohg_ref[...], wb_ref[0:HG_W, :], preferred_element_type=F32)
              + gt_ref[:, d:2 * d].astype(F32)
              * jnp.dot(oda_ref[...], wb_ref[HG_W:HG_W + DA_W, :], preferred_element_type=F32)
              + gt_ref[:, 2 * d:3 * d].astype(F32)
              * jnp.dot(orw_ref[...], wb_ref[HG_W + DA_W:, :], preferred_element_type=F32))
    mix = _mm(merged, wo_ref[...])
    x1 = _layer_norm(ALPHA * x_ref[...] + (1.0 + mod_ref[0, 2:3, :]) * mix, lng_ref[...], lnb_ref[...])
    x1_ref[...] = x1
    u2 = x1 * (1.0 + mod_ref[0, 4:5, :]) + mod_ref[0, 3:4, :]
    u2_ref[...] = u2.astype(BF16)

    logits = _mmh_nt(wrt_ref[...], u2)
    ex = jnp.exp(logits - jnp.max(logits, axis=0, keepdims=True))
    scores = ex / jnp.sum(ex, axis=0, keepdims=True)
    sel = scores + rb_ref[...]
    row = _iota(sel.shape, 0)
    best = None
    for grp in range(N_GROUPS):
        a, b, c2, d2 = (sel[grp * EXPERTS_PER_GROUP + i:grp * EXPERTS_PER_GROUP + i + 1] for i in range(4))
        hi1, lo1, hi2, lo2 = jnp.maximum(a, b), jnp.minimum(a, b), jnp.maximum(c2, d2), jnp.minimum(c2, d2)
        top2 = jnp.maximum(hi1, hi2) + jnp.maximum(jnp.minimum(hi1, hi2), jnp.maximum(lo1, lo2))
        if best is None:
            best, best_grp = top2, jnp.zeros_like(top2, dtype=jnp.int32)
        else:
            better = top2 > best
            best = jnp.where(better, top2, best)
            best_grp = jnp.where(better, grp, best_grp)
    masked = jnp.where(row // EXPERTS_PER_GROUP == best_grp, sel, MASK_VALUE)
    _, idx1 = _first_argmax(masked, row)
    _, idx2 = _first_argmax(jnp.where(row == idx1, -jnp.inf, masked), row)
    w1 = jnp.sum(jnp.where(row == idx1, scores, 0.0), axis=0, keepdims=True)
    w2 = jnp.sum(jnp.where(row == idx2, scores, 0.0), axis=0, keepdims=True)
    total = w1 + w2

    pick1 = row == idx1
    pick2 = row == idx2
    onehot = jnp.where(pick1 | pick2, 1.0, 0.0)
    cnt = jnp.sum(onehot, axis=1, keepdims=True)
    earlier = jnp.dot(onehot.astype(BF16), tri_ref[...], preferred_element_type=F32)
    chunks = jnp.floor((cnt + (MOE_CHUNK - 1)) * (1.0 / MOE_CHUNK))
    seg_start = MOE_CHUNK * jnp.dot(tri16_ref[...], jnp.broadcast_to(chunks, (N_EXPERTS, LANES)).astype(BF16),
                                    preferred_element_type=F32)[:, 0:1]
    pos = seg_start + earlier
    pos1 = jnp.sum(jnp.where(pick1, pos, 0.0), axis=0, keepdims=True)
    pos2 = jnp.sum(jnp.where(pick2, pos, 0.0), axis=0, keepdims=True)
    route_ref[...] = jnp.concatenate([pos1, pos2, w1 / total, w2 / total, jnp.zeros((4, pos1.shape[1]), F32)], axis=0)
    cnt_ref[0] = jnp.broadcast_to(cnt, (N_EXPERTS, LANES))


def _merge(o_hg, o_da, o_rw, gates, x, mod, w_branch, w_out, ln_g, ln_b, w_router_t, router_bias, seq, tm):
    n, d = x.shape
    per_seq = seq // tm
    tile = lambda i: (i, 0)
    full = lambda i: (0, 0)
    before = (_iota((tm, tm), 0) < _iota((tm, tm), 1)).astype(BF16)
    before16 = (_iota((N_EXPERTS, N_EXPERTS), 1) < _iota((N_EXPERTS, N_EXPERTS), 0)).astype(BF16)
    return pl.pallas_call(
        _merge_body,
        grid=(n // tm,),
        in_specs=[pl.BlockSpec((tm, HG_W), tile), pl.BlockSpec((tm, DA_W), tile), pl.BlockSpec((tm, RW_W), tile),
                  pl.BlockSpec((tm, 3 * d), tile), pl.BlockSpec((tm, d), tile),
                  pl.BlockSpec((1, 6, d), lambda i: (i // per_seq, 0, 0)),
                  pl.BlockSpec(w_branch.shape, full), pl.BlockSpec(w_out.shape, full),
                  pl.BlockSpec((1, d), full), pl.BlockSpec((1, d), full),
                  pl.BlockSpec((N_EXPERTS, d), full), pl.BlockSpec((N_EXPERTS, 1), full),
                  pl.BlockSpec((tm, tm), full), pl.BlockSpec((N_EXPERTS, N_EXPERTS), full)],
        out_specs=[pl.BlockSpec((tm, d), tile), pl.BlockSpec((tm, d), tile), pl.BlockSpec((8, tm), lambda i: (0, i)),
                   pl.BlockSpec((1, N_EXPERTS, LANES), lambda i: (i, 0, 0))],
        out_shape=[jax.ShapeDtypeStruct((n, d), F32), jax.ShapeDtypeStruct((n, d), BF16),
                   jax.ShapeDtypeStruct((8, n), F32), jax.ShapeDtypeStruct((n // tm, N_EXPERTS, LANES), F32)],
        compiler_params=_params("arbitrary"),
        name="merge",
    )(o_hg, o_da, o_rw, gates, x, mod, w_branch, w_out, ln_g, ln_b, w_router_t, router_bias, before, before16)


def _local_rows(tm):
    return -(-(2 * tm + N_EXPERTS * (MOE_CHUNK - 1)) // LANES) * LANES


def _token_columns(route):
    return jnp.concatenate([route, jnp.zeros((LANES - route.shape[0], route.shape[1]), F32)], axis=0).T


def _segment_copies(i, nch_ref, loc_ref, glob_ref, local_buf, global_buf, sem, to_global):
    def run(action):
        for e in range(N_EXPERTS):
            seg = i * N_EXPERTS + e
            loc0, glob0 = loc_ref[seg], glob_ref[seg]

            def one(c, carry):
                loc = local_buf.at[pl.ds(pl.multiple_of(loc0 + c * MOE_CHUNK, MOE_CHUNK), MOE_CHUNK), :]
                glob = global_buf.at[pl.ds(pl.multiple_of(glob0 + c * MOE_CHUNK, MOE_CHUNK), MOE_CHUNK), :]
                copy = pltpu.make_async_copy(loc, glob, sem) if to_global else pltpu.make_async_copy(glob, loc, sem)
                getattr(copy, action)()
                return carry

            lax.fori_loop(0, nch_ref[seg], one, 0)
    return run


def _dispatch_body(nch_ref, loc_ref, glob_ref, u_ref, route_ref, xs_in_ref, xs_ref, stage2_ref, sems):
    del xs_in_ref
    i = pl.program_id(0)
    last = pl.num_programs(0) - 1
    tm, d = u_ref.shape
    slot = i % 2
    stage_ref = stage2_ref.at[slot]

    def tile_copies(tile):
        return _segment_copies(tile, nch_ref, loc_ref, glob_ref, stage2_ref.at[tile % 2], xs_ref,
                               sems.at[tile % 2], to_global=True)

    @pl.when(i >= 2)
    def _():
        tile_copies(i - 2)("wait")

    route = route_ref[...]
    local_row = _iota((stage_ref.shape[0], tm), 0)
    take1 = local_row == route[0:1].astype(jnp.int32)
    take2 = local_row == route[1:2].astype(jnp.int32)
    perm = jnp.where(take1 | take2, 1.0, 0.0).astype(BF16)
    stage_ref[:, 0:d] = jnp.dot(perm, u_ref[...], preferred_element_type=F32).astype(BF16)

    cols = _token_columns(route)
    lane = _iota((tm, LANES), 1)

    def weight_cols(w):
        hi = w.astype(BF16).astype(F32)
        return jnp.where(lane == 0, hi, jnp.where(lane == 1, w - hi, 0.0)).astype(BF16)

    stage_ref[:, d:d + LANES] = (
        jnp.dot(jnp.where(take1, 1.0, 0.0).astype(BF16), weight_cols(cols[:, 2:3]), preferred_element_type=F32)
        + jnp.dot(jnp.where(take2, 1.0, 0.0).astype(BF16), weight_cols(cols[:, 3:4]), preferred_element_type=F32)
    ).astype(BF16)

    tile_copies(i)("start")

    @pl.when(i == last)
    def _():

        @pl.when(i >= 1)
        def _():
            tile_copies(i - 1)("wait")

        tile_copies(i)("wait")


def _dispatch(u2, route, nch, loc, glob, rows, tm):
    n, d = u2.shape
    width = d + LANES
    return pl.pallas_call(
        _dispatch_body,
        grid_spec=pltpu.PrefetchScalarGridSpec(
            num_scalar_prefetch=3,
            grid=(n // tm,),
            in_specs=[pl.BlockSpec((tm, d), lambda i, *_: (i, 0)),
                      pl.BlockSpec((8, tm), lambda i, *_: (0, i)),
                      pl.BlockSpec(memory_space=pl.ANY)],
            out_specs=pl.BlockSpec(memory_space=pl.ANY),
            scratch_shapes=[pltpu.VMEM((2, _local_rows(tm), width), BF16), pltpu.SemaphoreType.DMA((2,))],
        ),
        out_shape=jax.ShapeDtypeStruct((rows, width), BF16),
        input_output_aliases={5: 0},
        compiler_params=_params("arbitrary"),
        name="moe_dispatch",
    )(nch, loc, glob, u2, route, jnp.zeros((rows, width), BF16))


def _experts_body(te_ref, x_ref, wgu_ref, wd_ref, y_ref):
    used = te_ref[pl.program_id(0)] < N_EXPERTS

    @pl.when(used)
    def _():
        d = wgu_ref.shape[1]
        weight = x_ref[:, d:d + 1].astype(F32) + x_ref[:, d + 1:d + 2].astype(F32)
        hidden = jnp.dot(x_ref[:, 0:d], wgu_ref[0], preferred_element_type=F32)
        hg = hidden[:, :D_EXPERT]
        act = hg * _sigmoid(hg) * hidden[:, D_EXPERT:] * weight
        y_ref[...] = _mm(act, wd_ref[0]).astype(BF16)

    @pl.when(jnp.logical_not(used))
    def _():
        y_ref[...] = jnp.zeros_like(y_ref)


def _experts(xs, tile_expert, w_gu, w_down):
    rows, width = xs.shape
    d = w_down.shape[2]
    expert = lambda g, te: (jnp.minimum(te[g], N_EXPERTS - 1), 0, 0)
    return pl.pallas_call(
        _experts_body,
        grid_spec=pltpu.PrefetchScalarGridSpec(
            num_scalar_prefetch=1,
            grid=(rows // MOE_TM,),
            in_specs=[pl.BlockSpec((MOE_TM, width), lambda g, te: (g, 0)),
                      pl.BlockSpec((1, d, 2 * D_EXPERT), expert),
                      pl.BlockSpec((1, D_EXPERT, d), expert)],
            out_specs=pl.BlockSpec((MOE_TM, d), lambda g, te: (g, 0)),
        ),
        out_shape=jax.ShapeDtypeStruct((rows, d), BF16),
        compiler_params=_params("arbitrary"),
        name="moe_experts",
    )(tile_expert, xs, w_gu, w_down)


def _combine_body(nch_ref, loc_ref, glob_ref, route_ref, x1_ref, mod_ref, lng_ref, lnb_ref, ys_ref, o_ref,
                  back2_ref, sems):
    i = pl.program_id(0)

    def tile_copies(tile):
        return _segment_copies(tile, nch_ref, loc_ref, glob_ref, back2_ref.at[tile % 2], ys_ref,
                               sems.at[tile % 2], to_global=False)

    @pl.when(i == 0)
    def _():
        back2_ref[...] = jnp.zeros_like(back2_ref)
        tile_copies(i)("start")

    @pl.when(i + 1 < pl.num_programs(0))
    def _():
        tile_copies(i + 1)("start")

    cols = _token_columns(route_ref[...]).astype(jnp.int32)
    local_row = _iota((x1_ref.shape[0], back2_ref.shape[1]), 1)
    unperm = jnp.where((local_row == cols[:, 0:1]) | (local_row == cols[:, 1:2]), 1.0, 0.0).astype(BF16)
    tile_copies(i)("wait")
    ffn = jnp.dot(unperm, back2_ref[i % 2], preferred_element_type=F32)
    y = ALPHA * x1_ref[...] + (1.0 + mod_ref[0, 5:6, :]) * ffn
    o_ref[...] = _layer_norm(y, lng_ref[...], lnb_ref[...])


def _combine(ys, route, nch, loc, glob, x1, mod, ln_g, ln_b, seq, tm):
    n, d = x1.shape
    per_seq = seq // tm
    full = lambda i, *_: (0, 0)
    return pl.pallas_call(
        _combine_body,
        grid_spec=pltpu.PrefetchScalarGridSpec(
            num_scalar_prefetch=3,
            grid=(n // tm,),
            in_specs=[pl.BlockSpec((8, tm), lambda i, *_: (0, i)),
                      pl.BlockSpec((tm, d), lambda i, *_: (i, 0)),
                      pl.BlockSpec((1, 6, d), lambda i, *_: (i // per_seq, 0, 0)),
                      pl.BlockSpec((1, d), full), pl.BlockSpec((1, d), full),
                      pl.BlockSpec(memory_space=pl.ANY)],
            out_specs=pl.BlockSpec((tm, d), lambda i, *_: (i, 0)),
            scratch_shapes=[pltpu.VMEM((2, _local_rows(tm), d), BF16), pltpu.SemaphoreType.DMA((2,))],
        ),
        out_shape=jax.ShapeDtypeStruct((n, d), F32),
        compiler_params=_params("arbitrary"),
        name="moe_combine",
    )(nch, loc, glob, route, x1, mod, ln_g, ln_b, ys)


def _moe(u2, route, counts, w_gu, w_down, x1, mod, ln_g, ln_b, seq, tm):
    n = u2.shape[0]
    n_tiles = n // tm
    seg_rows = (counts + MOE_CHUNK - 1) // MOE_CHUNK * MOE_CHUNK
    loc = jnp.cumsum(seg_rows, axis=1) - seg_rows
    region = (jnp.sum(seg_rows, axis=0) + MOE_TM - 1) // MOE_TM * MOE_TM
    region_end = jnp.cumsum(region)
    glob = (region_end - region)[None, :] + jnp.cumsum(seg_rows, axis=0) - seg_rows
    rows = -(-(2 * n + n_tiles * N_EXPERTS * (MOE_CHUNK - 1) + N_EXPERTS * (MOE_TM - 1)) // MOE_TM) * MOE_TM
    tile_expert = jnp.sum(jnp.arange(rows // MOE_TM, dtype=jnp.int32)[:, None] * MOE_TM >= region_end[None, :],
                          axis=1).astype(jnp.int32)
    flat = lambda a: a.reshape(-1).astype(jnp.int32)
    nch, loc, glob = flat(seg_rows // MOE_CHUNK), flat(loc), flat(glob)
    xs = _dispatch(u2, route, nch, loc, glob, rows, tm)
    ys = _experts(xs, tile_expert, w_gu, w_down)
    return _combine(ys, route, nch, loc, glob, x1, mod, ln_g, ln_b, seq, tm)


def _tiles(seq):
    return min(512, seq), min(256, seq // 2)


def kernel(x, c, w_ada, b_ada, w_in, hg_lb_logits, hg_norm_g, da_lambda, da_subln_g, rw_mu, rw_w0, rw_w_up,
           rw_a0, rw_a_up, rw_g_up, rw_k_k, rw_k_a, rw_r_k, rw_gn_g, rw_gn_b, w_merge, b_merge, w_branch, w_out,
           ln_g, ln_b, w_router, router_bias, w_exp_gate, w_exp_up, w_exp_down):
    bsz, seq, d = x.shape
    depth = w_in.shape[0]
    n = bsz * seq
    tm, blk = _tiles(seq)

    sm = jax.nn.softmax(hg_lb_logits.astype(F32), axis=0)
    hg_lb = jnp.cumsum(sm, axis=0) - sm[0:1]
    slopes = jnp.asarray([2.0 ** (-8.0 * (h + 1) / DA_HEADS) for h in range(DA_HEADS)], F32)

    mod_all = _ada(c, w_ada, b_ada).reshape(depth, bsz, 6, d)
    w_router_t = w_router.T
    router_bias = router_bias.reshape(N_EXPERTS, 1)

    xf = x.reshape(n, d)
    for l in range(depth):
        mod = mod_all[l]
        lq1, lk1, lq2, lk2 = da_lambda[l].astype(F32)
        lam_init = 0.8 - 0.6 * math.exp(-0.3 * l)
        lam = jnp.exp(jnp.sum(lq1 * lk1)) - jnp.exp(jnp.sum(lq2 * lk2)) + lam_init
        scal = jnp.concatenate([jnp.stack([lam, jnp.asarray(1.0 - lam_init, F32)]), slopes])

        w_gates = jnp.concatenate([w_merge[l, br] for br in range(3)], axis=1).astype(BF16)
        y, gates, v_t = _proj(xf, mod, w_in[l].astype(BF16), w_gates, b_merge[l].reshape(1, 3 * d), seq, tm, blk)

        o_hg = _hgrn2(y, hg_lb[l].reshape(1, HG_W), hg_norm_g[l].reshape(1, HG_DV), bsz, seq)
        o_da = _attn(y, v_t, scal, da_subln_g[l].reshape(1, DA_DV), bsz, seq, blk)
        rw = dict(mu=rw_mu[l].reshape(1, -1), w0=rw_w0[l].reshape(1, -1), w_up=rw_w_up[l],
                  a0=rw_a0[l].reshape(1, -1), a_up=rw_a_up[l], g_up=rw_g_up[l],
                  k_k=rw_k_k[l].reshape(1, -1), k_a=rw_k_a[l].reshape(1, -1), r_k=rw_r_k[l].reshape(1, -1),
                  gn_g=rw_gn_g[l].reshape(1, -1), gn_b=rw_gn_b[l].reshape(1, -1))
        o_rw = _rwkv(y, rw, bsz, seq)

        x1, u2, route, counts = _merge(o_hg, o_da, o_rw, gates, xf, mod, w_branch[l].astype(BF16),
                                       w_out[l].astype(BF16), ln_g[l, 0].reshape(1, d), ln_b[l, 0].reshape(1, d),
                                       w_router_t, router_bias, seq, tm)
        w_gu = jnp.concatenate([w_exp_gate[l], w_exp_up[l]], axis=-1).astype(BF16)
        xf = _moe(u2, route, counts[:, :, 0].astype(jnp.int32), w_gu, w_exp_down[l].astype(BF16), x1, mod,
                  ln_g[l, 1].reshape(1, d), ln_b[l, 1].reshape(1, d), seq, tm)
    return xf.reshape(bsz, seq, d)
```

```python
import functools
import math

import jax
import jax.numpy as jnp
from jax import lax
from jax.experimental import pallas as pl
from jax.experimental.pallas import tpu as pltpu

D_MODEL = 1024
DEPTH = 4
HG_HEADS, HG_DK, HG_DV, HG_CHUNK, HG_SUB = 4, 128, 128, 64, 16
HG_W = HG_HEADS * HG_DV
HG_F_MIN = 1e-6
DA_HEADS, DA_DQK = 4, 64
DA_DV = 2 * DA_DQK
DA_W = DA_HEADS * DA_DV
MASK_VALUE = -1e30
LOG2E = math.log2(math.e)
PROJ_TN = 768
RW_HEADS, RW_DH, RW_CHUNK, RW_SUB = 8, 64, 64, 16
RW_STEP_CHUNKS = 2
RW_STEP_SEQS = 2
RW_W = RW_HEADS * RW_DH
RW_IN_W = 1792
RW_GN_EPS = 64e-5
IN_W = 5376
HG_COL, DA_COL, RW_COL = 0, 2048, 3584
N_EXPERTS, N_GROUPS, EXPERTS_PER_GROUP, D_EXPERT = 16, 4, 4, 512
MOE_CHUNK = 16
MOE_TM = 512
MERGE_SUBTILES = 1
ALPHA = (2.0 * DEPTH) ** 0.25
LN_EPS = 1e-5
RMS_EPS = 1e-6
LANES = 128

F32 = jnp.float32
BF16 = jnp.bfloat16
HIGHEST = lax.Precision.HIGHEST
VMEM_LIMIT = 48 * 1024 * 1024

_NT = (((1,), (1,)), ((), ()))
_TN = (((0,), (0,)), ((), ()))


def _mm(a, b):
    return jnp.dot(a.astype(BF16), b.astype(BF16), preferred_element_type=F32)


def _mm_nt(a, b):
    return lax.dot_general(a.astype(BF16), b.astype(BF16), _NT, preferred_element_type=F32)


def _mm_tn(a, b):
    return lax.dot_general(a.astype(BF16), b.astype(BF16), _TN, preferred_element_type=F32)


def _mmh(a, b):
    return jnp.dot(a, b, precision=HIGHEST, preferred_element_type=F32)


def _mmh_nt(a, b):
    return lax.dot_general(a, b, _NT, precision=HIGHEST, preferred_element_type=F32)


def _seg_sum(x, seg):
    rows = x.shape[0]
    hi = x.astype(BF16)
    lo = (x - hi.astype(F32)).astype(BF16)
    halves = []
    for c0 in range(0, x.shape[1], seg.shape[0]):
        cols = slice(c0, c0 + seg.shape[0])
        both = jnp.dot(jnp.concatenate([hi[:, cols], lo[:, cols]], axis=0), seg, preferred_element_type=F32)
        halves.append(both[:rows] + both[rows:])
    return jnp.concatenate(halves, axis=1)


def _split_mm(x, w_twice, w_lo):
    hi = x.astype(BF16)
    lo = (x - hi.astype(F32)).astype(BF16)
    return (jnp.dot(jnp.concatenate([hi, lo], axis=1), w_twice, preferred_element_type=F32)
            + jnp.dot(hi, w_lo, preferred_element_type=F32))


def _chunk_cumsum(x):
    c = x.shape[0]
    hi = x.astype(BF16)
    rest = x - hi.astype(F32)
    mid = rest.astype(BF16)
    lo = (rest - mid.astype(F32)).astype(BF16)
    col = _iota((c, 4 * c), 1)
    tri = ((col & (c - 1)) <= _iota((c, 4 * c), 0)) & (col < 3 * c)
    return jnp.dot(jnp.where(tri, 1.0, 0.0).astype(BF16), jnp.concatenate([hi, mid, lo, lo], axis=0),
                   preferred_element_type=F32)


def _sigmoid(x):
    return 1.0 / (1.0 + jnp.exp(-x))


def _softplus(x):
    return jnp.maximum(x, 0.0) + jnp.log(1.0 + jnp.exp(-jnp.abs(x)))


def _iota(shape, dim):
    return lax.broadcasted_iota(jnp.int32, shape, dim)


def _params(*sem):
    return pltpu.CompilerParams(dimension_semantics=sem, vmem_limit_bytes=VMEM_LIMIT)


def _layer_norm(y, g, b):
    mu = jnp.mean(y, axis=-1, keepdims=True)
    d = y - mu
    var = jnp.mean(d * d, axis=-1, keepdims=True)
    return d * lax.rsqrt(var + LN_EPS) * g + b


def _ada_body(c_ref, w_ref, b_ref, o_ref):
    c = c_ref[...]
    o_ref[0] = _mmh(c * _sigmoid(c), w_ref[0]) + b_ref[0]


def _ada(c, w_ada, b_ada):
    depth, d, _ = w_ada.shape
    bsz = c.shape[0]
    return pl.pallas_call(
        _ada_body,
        grid=(depth, 6),
        in_specs=[pl.BlockSpec((bsz, d), lambda l, j: (0, 0)),
                  pl.BlockSpec((1, d, d), lambda l, j: (l, 0, j)),
                  pl.BlockSpec((1, 1, d), lambda l, j: (l, 0, j))],
        out_specs=pl.BlockSpec((1, bsz, d), lambda l, j: (l, 0, j)),
        out_shape=jax.ShapeDtypeStruct((depth, bsz, 6 * d), F32),
        compiler_params=_params("arbitrary", "arbitrary"),
        name="ada",
    )(c, w_ada, b_ada.reshape(depth, 1, 6 * d))


def _proj_body(x_ref, mod_ref, win_ref, wg_ref, bg_ref, y_ref, g_ref, vt_ref):
    u = (x_ref[...] * (1.0 + mod_ref[0, 1:2, :]) + mod_ref[0, 0:1, :]).astype(BF16)
    n_kb, kb = vt_ref.shape[1], vt_ref.shape[4]
    v_col = DA_COL + 2 * DA_W
    for c0 in range(0, IN_W, PROJ_TN):
        cols = slice(c0, c0 + PROJ_TN)
        res = jnp.dot(u, win_ref[:, cols], preferred_element_type=F32)
        y_ref[:, cols] = res.astype(BF16)
        if c0 == v_col:
            for h in range(DA_HEADS):
                slope2 = 2.0 ** (-8.0 * (h + 1) / DA_HEADS) * LOG2E
                key_w = jnp.exp2(slope2 * (_iota((1, res.shape[0]), 1) & (kb - 1)).astype(F32))
                v_t = (res[:, h * DA_DV:(h + 1) * DA_DV].T * key_w).astype(BF16)
                tail = jnp.where(_iota((8, res.shape[0]), 0) == 0, key_w, 0.0).astype(BF16)
                for kbi in range(n_kb):
                    keys = slice(kbi * kb, (kbi + 1) * kb)
                    vt_ref[0, kbi, h, 0:DA_DV, :] = v_t[:, keys]
                    vt_ref[0, kbi, h, DA_DV:DA_DV + 8, :] = tail[:, keys]
    for c0 in range(0, 3 * D_MODEL, PROJ_TN):
        cols = slice(c0, c0 + PROJ_TN)
        g = jnp.dot(u, wg_ref[:, cols], preferred_element_type=F32) + bg_ref[:, cols]
        g_ref[:, cols] = _sigmoid(g).astype(BF16)


def _proj(x, mod, w_in, w_gates, b_gates, seq, tm, kb):
    n, d = x.shape
    per_seq = seq // tm
    tile = lambda i: (i, 0)
    resident = dict(index_map=lambda i: (0, 0), pipeline_mode=pl.Buffered(1))
    vt_shape = (n // seq, seq // kb, DA_HEADS, DA_DV + 8, kb)
    return pl.pallas_call(
        _proj_body,
        grid=(n // tm,),
        in_specs=[pl.BlockSpec((tm, d), tile),
                  pl.BlockSpec((1, 6, d), lambda i: (i // per_seq, 0, 0)),
                  pl.BlockSpec(w_in.shape, **resident),
                  pl.BlockSpec(w_gates.shape, **resident),
                  pl.BlockSpec(b_gates.shape, **resident)],
        out_specs=[pl.BlockSpec((tm, IN_W), tile), pl.BlockSpec((tm, 3 * d), tile),
                   pl.BlockSpec((1, tm // kb) + vt_shape[2:], lambda i: (i // per_seq, i % per_seq, 0, 0, 0))],
        out_shape=[jax.ShapeDtypeStruct((n, IN_W), BF16), jax.ShapeDtypeStruct((n, 3 * d), BF16),
                   jax.ShapeDtypeStruct(vt_shape, BF16)],
        compiler_params=_params("arbitrary"),
        name="proj",
    )(x, mod, w_in, w_gates, b_gates)


def _hgrn2_body(y_ref, lb_ref, ng_ref, o_ref, st_ref):
    c, sub = HG_CHUNK, HG_SUB

    @pl.when(pl.program_id(1) == 0)
    def _():
        st_ref[...] = jnp.zeros_like(st_ref)

    q = y_ref[:, 0:HG_W].astype(F32)
    z = y_ref[:, HG_W:2 * HG_W].astype(F32)
    v = y_ref[:, 2 * HG_W:3 * HG_W]
    og = y_ref[:, 3 * HG_W:4 * HG_W].astype(F32)
    lb = lb_ref[...]
    f = lb + (1.0 - lb) * _sigmoid(z)
    kin = (1.0 - lb) * _sigmoid(-z)
    b = _chunk_cumsum(jnp.log(jnp.maximum(f, HG_F_MIN)))
    log_k = jnp.log(kin)
    rel = log_k - b
    b_last = b[c - 1:c]
    q_dec = (q * jnp.exp(b)).astype(BF16)
    k_tail = (kin * jnp.exp(b_last - b)).astype(BF16)
    decay = jnp.exp(b_last)

    ones = jnp.ones((HG_DK, LANES), BF16)
    row_s = _iota((sub, c), 0)
    col_s = _iota((sub, c), 1)
    heads = range(HG_HEADS)
    blks = range(c // sub)
    hs = [slice(h * HG_DK, (h + 1) * HG_DK) for h in heads]

    b2, rel2, lk2 = b * LOG2E, rel * LOG2E, log_k * LOG2E
    half = sub // 2
    diag = {}
    for h in heads:
        for blk in blks:
            rows = slice(blk * sub, (blk + 1) * sub)
            b_i, q_i, rel_i, lk_i = b2[rows, hs[h]], q[rows, hs[h]], rel2[rows, hs[h]], lk2[rows, hs[h]]
            terms = [q_i[t0:] * jnp.exp2(jnp.minimum(b_i[t0:] + rel_i[s:s + 1], lk_i[s:s + 1]))
                     for s in range(sub) for t0 in [0 if s < half else half]]
            w = jnp.concatenate(terms, axis=0).astype(BF16)
            diag[h, blk] = jnp.dot(w, ones, preferred_element_type=F32)
    below = {}
    for h in heads:
        for blk in blks[1:]:
            r0 = blk * sub
            beta = b[r0 - 1:r0, hs[h]]
            q_t = q[r0:r0 + sub, hs[h]] * jnp.exp(b[r0:r0 + sub, hs[h]] - beta)
            k_h = kin[:, hs[h]] * jnp.exp(jnp.minimum(beta - b[:, hs[h]], 0.0))
            below[h, blk] = _mm_nt(q_t, k_h)
    scores = []
    for h in heads:
        a_rows = []
        for blk in blks:
            r0 = blk * sub
            a_blk = jnp.zeros((sub, c), F32)
            for s in range(sub):
                if s < half:
                    sums = diag[h, blk][s * sub:(s + 1) * sub, :c]
                else:
                    start = half * sub + (s - half) * half
                    sums = jnp.concatenate([jnp.zeros((half, c), F32), diag[h, blk][start:start + half, :c]], axis=0)
                a_blk = jnp.where(col_s == r0 + s, sums, a_blk)
            if blk > 0:
                a_blk = jnp.where(col_s < r0, below[h, blk], a_blk)
            a_rows.append(jnp.where(col_s <= row_s + r0, a_blk, 0.0))
        scores.append(jnp.concatenate(a_rows, axis=0).astype(BF16))

    st = [st_ref[h] for h in heads]
    intra = [jnp.dot(scores[h], v[:, hs[h]], preferred_element_type=F32) for h in heads]
    inter = [lax.dot_general(q_dec[:, hs[h]], st[h].astype(BF16), _NT, preferred_element_type=F32) for h in heads]
    for h in heads:
        st_ref[h] = st[h] * decay[:, hs[h]] + lax.dot_general(v[:, hs[h]], k_tail[:, hs[h]], _TN,
                                                             preferred_element_type=F32)
    for h in heads:
        o = intra[h] + inter[h]
        o = o * lax.rsqrt(jnp.mean(o * o, axis=-1, keepdims=True) + RMS_EPS) * ng_ref[...]
        o_ref[:, hs[h]] = (o * (og[:, hs[h]] * _sigmoid(og[:, hs[h]]))).astype(o_ref.dtype)


def _hgrn2(y, lb, norm_g, bsz, seq):
    n = y.shape[0]
    nc = seq // HG_CHUNK
    width = 4 * HG_W
    return pl.pallas_call(
        _hgrn2_body,
        grid=(bsz, nc),
        in_specs=[pl.BlockSpec((HG_CHUNK, width), lambda b, c: (b * nc + c, HG_COL // width)),
                  pl.BlockSpec((1, HG_W), lambda b, c: (0, 0)),
                  pl.BlockSpec((1, HG_DV), lambda b, c: (0, 0))],
        out_specs=pl.BlockSpec((HG_CHUNK, HG_W), lambda b, c: (b * nc + c, 0)),
        out_shape=jax.ShapeDtypeStruct((n, HG_W), BF16),
        scratch_shapes=[pltpu.VMEM((HG_HEADS, HG_DV, HG_DK), F32)],
        compiler_params=_params("arbitrary", "arbitrary"),
        name="hgrn2",
    )(y, lb, norm_g)


def _attn_body(scal_ref, q_ref, k_ref, vt_ref, g_ref, o_ref, qq_ref, sa_ref, sb_ref, p_ref, m_ref, sc_ref,
               acc_ref, *, kb):
    h = pl.program_id(1)
    i = pl.program_id(2)
    lam = scal_ref[0]
    out_scale = scal_ref[1]
    slope = scal_ref[2 + h] * LOG2E
    qb = 2 * kb
    q0 = i * qb
    tiles_per_map = qb // LANES
    a_tiles = kb // LANES

    q = q_ref[...].astype(F32) * (DA_DQK ** -0.5 * LOG2E)
    lane = _iota(q.shape, 1)
    stacked = jnp.concatenate([jnp.where(lane < DA_DQK, q, 0.0), jnp.where(lane >= DA_DQK, q, 0.0)], axis=0)
    qq_ref[...] = stacked.T.astype(BF16)
    m_ref[...] = jnp.full(m_ref.shape, MASK_VALUE, F32)
    sc_ref[...] = jnp.ones(sc_ref.shape, F32)
    acc_ref[...] = jnp.zeros(acc_ref.shape, F32)
    p_ref[...] = jnp.zeros(p_ref.shape, BF16)
    key_off = _iota((kb, LANES), 0)

    def scores(j):
        return jnp.dot(k_ref[pl.ds(j * kb, kb), :], qq_ref[...], preferred_element_type=F32)

    def softmax(j, src_ref, diagonal_of):
        block_bias = slope * (j * kb - q0).astype(F32)
        for t in range(2 * tiles_per_map):
            cols = slice(t * LANES, (t + 1) * LANES)
            in_map = t % tiles_per_map
            half = "A" if in_map < a_tiles else "B"
            if diagonal_of == "B" and half == "A":
                p_ref[:, cols] = jnp.zeros((kb, LANES), BF16)
                sc_ref[:, cols] = jnp.ones((1, LANES), F32)
                continue
            s = src_ref[:, cols]
            if diagonal_of == half:
                q_off = _iota((kb, LANES), 1) + (in_map % a_tiles) * LANES
                s = jnp.where(key_off <= q_off, s, MASK_VALUE)
            m_old = m_ref[:, cols]
            m_new = jnp.maximum(m_old, jnp.max(s, axis=0, keepdims=True) + block_bias)
            p_ref[:, cols] = jnp.exp2(s - (m_new - block_bias)).astype(BF16)
            sc_ref[:, cols] = jnp.exp2(m_old - m_new)
            m_ref[:, cols] = m_new

    def iteration(j, src_ref, dst_ref, diagonal_of=None):
        sc_prev = sc_ref[...]
        pv = jnp.dot(vt_ref[0, jnp.maximum(j - 1, 0), 0], p_ref[...], preferred_element_type=F32)
        if dst_ref is not None:
            dst_ref[...] = scores(j + 1)
        acc_ref[...] = acc_ref[...] * sc_prev + pv
        softmax(j, src_ref, diagonal_of)

    def pair(j0):
        iteration(j0, sa_ref, sb_ref)
        iteration(j0 + 1, sb_ref, sa_ref)

    def body(quad, carry):
        pair(4 * quad)
        pair(4 * quad + 2)
        return carry

    sa_ref[...] = scores(0)
    lax.fori_loop(0, i // 2, body, 0)

    @pl.when(i % 2 == 1)
    def _():
        pair(2 * i - 2)

    iteration(2 * i, sa_ref, sb_ref, "A")
    iteration(2 * i + 1, sb_ref, None, "B")
    acc = acc_ref[...] * sc_ref[...] + jnp.dot(vt_ref[0, 2 * i + 1, 0], p_ref[...], preferred_element_type=F32)
    o = acc[:DA_DV] / acc[DA_DV:DA_DV + 1]
    d = o[:, :qb] - lam * o[:, qb:]
    g = jnp.concatenate([g_ref[...]] * tiles_per_map, axis=1)
    d = d * lax.rsqrt(jnp.mean(d * d, axis=0, keepdims=True) + RMS_EPS) * g * out_scale
    o_ref[...] = d.T.astype(o_ref.dtype)


def _attn(y, v_t, scal, subln_g, bsz, seq, kb):
    n = y.shape[0]
    qb = 2 * kb
    nq, nk = seq // qb, seq // kb
    qc, kc = DA_COL // DA_DV, (DA_COL + DA_W) // DA_DV
    rows_v = v_t.shape[3]
    g_col = jnp.broadcast_to(subln_g.reshape(DA_DV, 1), (DA_DV, LANES))
    row = (1, 2 * qb)
    return pl.pallas_call(
        functools.partial(_attn_body, kb=kb),
        scratch_shapes=[pltpu.VMEM((DA_DV, 2 * qb), BF16), pltpu.VMEM((kb, 2 * qb), F32),
                        pltpu.VMEM((kb, 2 * qb), F32), pltpu.VMEM((kb, 2 * qb), BF16), pltpu.VMEM(row, F32),
                        pltpu.VMEM(row, F32), pltpu.VMEM((rows_v, 2 * qb), F32)],
        grid=(bsz, DA_HEADS, nq),
        in_specs=[pl.BlockSpec(memory_space=pltpu.SMEM),
                  pl.BlockSpec((qb, DA_DV), lambda b, h, i: (b * nq + i, qc + h)),
                  pl.BlockSpec((seq, DA_DV), lambda b, h, i: (b, kc + h)),
                  pl.BlockSpec((1, nk, 1, rows_v, kb), lambda b, h, i: (b, 0, h, 0, 0)),
                  pl.BlockSpec((DA_DV, LANES), lambda b, h, i: (0, 0))],
        out_specs=pl.BlockSpec((qb, DA_DV), lambda b, h, i: (b * nq + i, h)),
        out_shape=jax.ShapeDtypeStruct((n, DA_W), BF16),
        compiler_params=_params("arbitrary", "arbitrary", "arbitrary"),
        name="diffattn",
    )(scal, y, y, v_t, g_col)


def _split_f32(x):
    hi = x.astype(BF16)
    hi_f = hi.astype(F32)
    return hi, hi_f, x - hi_f


def _dup_lhs(hi_f, lo_f, low_half):
    packed = jnp.where(low_half, hi_f, lo_f).astype(BF16)
    return jnp.concatenate([packed, packed], axis=1)


def _dup_rhs(hi, lo_f):
    lo = lo_f.astype(BF16)
    return jnp.concatenate([hi, hi, lo, lo], axis=0)


def _rwkv_body(y_ref, mu_ref, w0_ref, a0_ref, wa2_ref, walo_ref, gu2_ref, gulo_ref, kk_ref, ka_ref, rk_ref,
               gng_ref, gnb_ref, seg_ref, o_ref, st_ref, prev_ref, osc_ref):
    c, sub, dh = RW_CHUNK, RW_SUB, RW_DH

    @pl.when(pl.program_id(1) == 0)
    def _():
        st_ref[...] = jnp.zeros_like(st_ref)
        prev_ref[...] = jnp.zeros_like(prev_ref)

    n_seq, seq_rows = y_ref.shape[0], y_ref.shape[1]
    rows = n_seq * seq_rows
    shifted = []
    for si in range(n_seq):
        x = y_ref[si].astype(F32)
        x_prev = jnp.where(_iota(x.shape, 0) == 0, prev_ref[si:si + 1], pltpu.roll(x, 1, axis=0))
        prev_ref[si:si + 1] = x[seq_rows - 1:seq_rows]
        shifted.append(x + (x_prev - x) * mu_ref[...])
    xs = jnp.concatenate(shifted, axis=0)
    r = xs[:, 0:RW_W]
    k = xs[:, RW_W:2 * RW_W]
    v = xs[:, 2 * RW_W:3 * RW_W]
    wa = xs[:, 3 * RW_W:3 * RW_W + LANES]
    gd = xs[:, 3 * RW_W + LANES:RW_IN_W]

    lora = _split_mm(jnp.where(_iota(wa.shape, 1) < 64, jnp.tanh(wa), wa), wa2_ref[...], walo_ref[...])
    w_log = -_softplus(-(w0_ref[...] + lora[:, :RW_W])) - 0.5
    g = -jnp.exp(w_log)
    a = _sigmoid(a0_ref[...] + lora[:, RW_W:])
    gate = _split_mm(_sigmoid(gd), gu2_ref[...], gulo_ref[...])
    seg = seg_ref[...]
    kk = k * kk_ref[...]
    k2 = k * (1.0 + (a - 1.0) * ka_ref[...])
    sums = _seg_sum(jnp.concatenate([kk * kk, r * k2 * rk_ref[...]], axis=0), seg)
    kk = kk * lax.rsqrt(jnp.maximum(sums[:rows], 1e-12))
    bb = kk * a
    bonus = sums[rows:] * v

    chunks = range(rows // c)
    cr = [slice(ci * c, (ci + 1) * c) for ci in chunks]
    gc = jnp.concatenate([_chunk_cumsum(g[s]) for s in cr], axis=0)
    g_last = [gc[s][c - 1:c] for s in cr]
    e_inv = jnp.exp(-gc)
    e_tail = jnp.exp(jnp.concatenate([jnp.broadcast_to(gl, (c, RW_W)) for gl in g_last], axis=0) - gc)
    gam = [jnp.exp(gl) for gl in g_last]
    a_t = (-kk * jnp.exp(gc - g)).astype(BF16)
    r_t = (r * jnp.exp(gc)).astype(BF16)
    b_h = (bb * e_inv).astype(BF16)
    k_h = (k2 * e_inv).astype(BF16)
    k_bar = (k2 * e_tail).astype(BF16)
    b_bar = (bb * e_tail).astype(BF16)
    v_bf = v.astype(BF16)

    row2 = _iota((c, 2 * c), 0)
    lane2 = _iota((c, 2 * c), 1)
    col2 = lane2 & (c - 1)
    low_half = lane2 < c
    strict = row2 > col2
    incl = row2 >= col2
    same_blk = (row2 // sub) == (col2 // sub)
    eye = (row2 == col2).astype(F32)

    def dot(p, q):
        return jnp.dot(p, q, preferred_element_type=F32)

    heads = range(RW_HEADS)
    sls = [slice(h * dh, (h + 1) * dh) for h in heads]
    items = [(ci, h) for ci in chunks for h in heads]
    ar_h = [jnp.concatenate([a_t[cr[ci], sls[h]], r_t[cr[ci], sls[h]]], axis=0) for ci, h in items]
    quad = [lax.dot_general(x, jnp.concatenate([b_h[cr[ci], sls[h]], k_h[cr[ci], sls[h]]], axis=0), _NT,
                            preferred_element_type=F32)
            for x, (ci, h) in zip(ar_h, items)]
    top = [jnp.where(strict, q[:c], 0.0) for q in quad]
    a_ak = [t[:, c:].astype(BF16) for t in top]
    a_r = [jnp.where(incl, q[c:], 0.0).astype(BF16) for q in quad]
    a_ab = [jnp.where(low_half, t, pltpu.roll(t, c, axis=1)) for t in top]
    a_d = [jnp.where(same_blk, x, 0.0) for x in a_ab]
    a_o = [(x - y).astype(BF16) for x, y in zip(a_ab, a_d)]

    s1 = [_split_f32(x) for x in a_d]
    p2 = [dot(_dup_lhs(hf, lf, low_half), _dup_rhs(hi, lf)) for hi, hf, lf in s1]
    s2 = [_split_f32(x) for x in p2]
    rhs2 = [_dup_rhs(hi, lf) for hi, _, lf in s2]
    p4 = [dot(_dup_lhs(hf, lf, low_half), rhs) for (_, hf, lf), rhs in zip(s2, rhs2)]
    s4 = [_split_f32(x) for x in p4]
    rhs4 = [_dup_rhs(hi, lf) for hi, _, lf in s4]
    p8 = [dot(_dup_lhs(hf, lf, low_half), rhs) for (_, hf, lf), rhs in zip(s4, rhs4)]
    rhs8 = [_dup_rhs(hi, lf) for hi, _, lf in (_split_f32(x) for x in p8)]
    t_d = [eye + x for x in a_d]
    for rhs_all in (rhs2, rhs4, rhs8):
        st = [_split_f32(x) for x in t_d]
        t_d = [x + dot(_dup_lhs(hf, lf, low_half), rhs) for x, (_, hf, lf), rhs in zip(t_d, st, rhs_all)]
    t_d = [x.astype(BF16) for x in t_d]

    nn = [dot(t[:, :c], x) for t, x in zip(t_d, a_o)]
    nn_bf = [x.astype(BF16) for x in nn]
    n2 = [dot(x[:, :c], x) for x in nn_bf]
    n3 = [dot(x[:, :c], y.astype(BF16)) for x, y in zip(nn_bf, n2)]
    t_m = [dot((eye + x + y + z).astype(BF16)[:, :c], t).astype(BF16)[:, :c]
           for x, y, z, t in zip(nn, n2, n3, t_d)]

    v_h = [v_bf[cr[ci], sls[h]] for ci, h in items]
    akv = [dot(x, y).astype(BF16) for x, y in zip(a_ak, v_h)]
    at_m = [dot(t, x[:c]).astype(BF16) for t, x in zip(t_m, ar_h)]
    v_p = [dot(t, x) for t, x in zip(t_m, akv)]

    per_seq = seq_rows // c
    chains = [(si, h) for si in range(n_seq) for h in heads]
    state = [st_ref[si, h] for si, h in chains]
    for t in range(per_seq):
        it = [(si * per_seq + t) * RW_HEADS + h for si, h in chains]
        ck = [cr[si * per_seq + t] for si, _ in chains]
        proj = [lax.dot_general(jnp.concatenate([at_m[j], ar_h[j][c:]], axis=0), s.astype(BF16), _NT,
                                preferred_element_type=F32) for j, s in zip(it, state)]
        u = [(p[:c] + v_p[j]).astype(BF16) for p, j in zip(proj, it)]
        for p, uu, j, rws, (_, h) in zip(proj, u, it, ck, chains):
            osc_ref[rws, sls[h]] = p[c:] + dot(a_r[j], jnp.concatenate([uu, v_h[j]], axis=0))
        state = [s * gam[j // RW_HEADS][:, sls[h]] + lax.dot_general(
            jnp.concatenate([v_h[j], uu], axis=0),
            jnp.concatenate([k_bar[rws, sls[h]], b_bar[rws, sls[h]]], axis=0), _TN, preferred_element_type=F32)
            for s, uu, j, rws, (_, h) in zip(state, u, it, ck, chains)]
    for s, (si, h) in zip(state, chains):
        st_ref[si, h] = s

    o = osc_ref[...]
    mean = _seg_sum(o, seg) * (1.0 / dh)
    d = o - mean
    var = _seg_sum(d * d, seg) * (1.0 / dh)
    o = d * lax.rsqrt(var + RW_GN_EPS) * gng_ref[...] + gnb_ref[...]
    o = ((o + bonus) * gate).astype(o_ref.dtype)
    for si in range(n_seq):
        o_ref[si] = o[si * seq_rows:(si + 1) * seq_rows]


def _rwkv(y, p, bsz, seq):
    n = y.shape[0]
    half = RW_W // 2
    seg = (_iota((half, half), 0) // RW_DH == _iota((half, half), 1) // RW_DH).astype(BF16)

    def two_terms(w):
        hi = w.astype(BF16)
        return jnp.concatenate([hi, hi], axis=0), (w - hi.astype(F32)).astype(BF16)

    zeros = jnp.zeros_like(p["w_up"])
    wa2, wa_lo = two_terms(jnp.concatenate([jnp.concatenate([p["w_up"], zeros], axis=1),
                                            jnp.concatenate([zeros, p["a_up"]], axis=1)], axis=0))
    gu2, gu_lo = two_terms(p["g_up"])
    rows = [p["mu"], p["w0"], p["a0"], wa2, wa_lo, gu2, gu_lo, p["k_k"], p["k_a"], p["r_k"],
            p["gn_g"], p["gn_b"], seg]
    full = lambda b, c: (0, 0)
    step = RW_STEP_CHUNKS * RW_CHUNK
    n_seq = RW_STEP_SEQS if bsz % RW_STEP_SEQS == 0 else 1
    out = pl.pallas_call(
        _rwkv_body,
        grid=(bsz // n_seq, seq // step),
        in_specs=[pl.BlockSpec((n_seq, step, RW_IN_W), lambda b, c: (b, c, RW_COL // RW_IN_W))]
        + [pl.BlockSpec(a.shape, full) for a in rows],
        out_specs=pl.BlockSpec((n_seq, step, RW_W), lambda b, c: (b, c, 0)),
        out_shape=jax.ShapeDtypeStruct((bsz, seq, RW_W), BF16),
        scratch_shapes=[pltpu.VMEM((n_seq, RW_HEADS, RW_DH, RW_DH), F32),
                        pltpu.VMEM((n_seq, RW_IN_W), F32),
                        pltpu.VMEM((n_seq * step, RW_W), F32)],
        compiler_params=_params("arbitrary", "arbitrary"),
        name="rwkv7",
    )(y.reshape(bsz, seq, -1), *rows)
    return out.reshape(n, RW_W)


def _first_argmax(vals, row):
    top = jnp.max(vals, axis=0, keepdims=True)
    idx = jnp.min(jnp.where(vals == top, row, N_EXPERTS), axis=0, keepdims=True)
    return top, idx


def _merge_body(ohg_ref, oda_ref, orw_ref, gt_ref, x_ref, mod_ref, wb_ref, wo_ref, lng_ref, lnb_ref,
                wrt_ref, rb_ref, tri_ref, tri16_ref, x1_ref, u2_ref, route_ref, cnt_ref):
    d = D_MODEL
    sub_rows = tri_ref.shape[0]

    def route(rows):
        merged = (gt_ref[rows, 0:d].astype(F32)
                  * jnp.dot(ohg_ref[rows, :], wb_ref[0:HG_W, :], preferred_element_type=F32)
                  + gt_ref[rows, d:2 * d].astype(F32)
                  * jnp.dot(oda_ref[rows, :], wb_ref[HG_W:HG_W + DA_W, :], preferred_element_type=F32)
                  + gt_ref[rows, 2 * d:3 * d].astype(F32)
                  * jnp.dot(orw_ref[rows, :], wb_ref[HG_W + DA_W:, :], preferred_element_type=F32))
        mix = _mm(merged, wo_ref[...])
        x1 = _layer_norm(ALPHA * x_ref[rows, :] + (1.0 + mod_ref[0, 2:3, :]) * mix, lng_ref[...], lnb_ref[...])
        x1_ref[rows, :] = x1
        u2 = x1 * (1.0 + mod_ref[0, 4:5, :]) + mod_ref[0, 3:4, :]
        u2_ref[rows, :] = u2.astype(BF16)

        logits = _mmh_nt(wrt_ref[...], u2)
        ex = jnp.exp(logits - jnp.max(logits, axis=0, keepdims=True))
        scores = ex / jnp.sum(ex, axis=0, keepdims=True)
        sel = scores + rb_ref[...]
        row = _iota(sel.shape, 0)
        best = None
        for grp in range(N_GROUPS):
            a, b, c2, d2 = (sel[grp * EXPERTS_PER_GROUP + i:grp * EXPERTS_PER_GROUP + i + 1] for i in range(4))
            hi1, lo1, hi2, lo2 = jnp.maximum(a, b), jnp.minimum(a, b), jnp.maximum(c2, d2), jnp.minimum(c2, d2)
            top2 = jnp.maximum(hi1, hi2) + jnp.maximum(jnp.minimum(hi1, hi2), jnp.maximum(lo1, lo2))
            if best is None:
                best, best_grp = top2, jnp.zeros_like(top2, dtype=jnp.int32)
            else:
                better = top2 > best
                best = jnp.where(better, top2, best)
                best_grp = jnp.where(better, grp, best_grp)
        masked = jnp.where(row // EXPERTS_PER_GROUP == best_grp, sel, MASK_VALUE)
        _, idx1 = _first_argmax(masked, row)
        _, idx2 = _first_argmax(jnp.where(row == idx1, -jnp.inf, masked), row)
        pick1 = row == idx1
        pick2 = row == idx2
        w1 = jnp.sum(jnp.where(pick1, scores, 0.0), axis=0, keepdims=True)
        w2 = jnp.sum(jnp.where(pick2, scores, 0.0), axis=0, keepdims=True)
        onehot = jnp.where(pick1 | pick2, 1.0, 0.0)
        earlier = jnp.dot(onehot.astype(BF16), tri_ref[...], preferred_element_type=F32)
        return pick1, pick2, w1 / (w1 + w2), w2 / (w1 + w2), earlier, jnp.sum(onehot, axis=1, keepdims=True)

    parts = [route(slice(r0, r0 + sub_rows)) for r0 in range(0, x_ref.shape[0], sub_rows)]

    cnt = functools.reduce(jnp.add, [p[5] for p in parts])
    chunks = jnp.floor((cnt + (MOE_CHUNK - 1)) * (1.0 / MOE_CHUNK))
    seg_start = MOE_CHUNK * jnp.dot(tri16_ref[...], jnp.broadcast_to(chunks, (N_EXPERTS, LANES)).astype(BF16),
                                    preferred_element_type=F32)[:, 0:1]
    before = jnp.zeros_like(cnt)
    for k, (pick1, pick2, wn1, wn2, earlier, cnt_k) in enumerate(parts):
        pos = seg_start + before + earlier
        pos1 = jnp.sum(jnp.where(pick1, pos, 0.0), axis=0, keepdims=True)
        pos2 = jnp.sum(jnp.where(pick2, pos, 0.0), axis=0, keepdims=True)
        route_ref[:, k * sub_rows:(k + 1) * sub_rows] = jnp.concatenate(
            [pos1, pos2, wn1, wn2, jnp.zeros((4, sub_rows), F32)], axis=0)
        before = before + cnt_k
    cnt_ref[0] = jnp.broadcast_to(cnt, (N_EXPERTS, LANES))


def _merge(o_hg, o_da, o_rw, gates, x, mod, w_branch, w_out, ln_g, ln_b, w_router_t, router_bias, seq, tm):
    n, d = x.shape
    per_seq = seq // tm
    tile = lambda i: (i, 0)
    full = lambda i: (0, 0)
    sub = tm // MERGE_SUBTILES
    before = (_iota((sub, sub), 0) < _iota((sub, sub), 1)).astype(BF16)
    before16 = (_iota((N_EXPERTS, N_EXPERTS), 1) < _iota((N_EXPERTS, N_EXPERTS), 0)).astype(BF16)
    return pl.pallas_call(
        _merge_body,
        grid=(n // tm,),
        in_specs=[pl.BlockSpec((tm, HG_W), tile), pl.BlockSpec((tm, DA_W), tile), pl.BlockSpec((tm, RW_W), tile),
                  pl.BlockSpec((tm, 3 * d), tile), pl.BlockSpec((tm, d), tile),
                  pl.BlockSpec((1, 6, d), lambda i: (i // per_seq, 0, 0)),
                  pl.BlockSpec(w_branch.shape, full), pl.BlockSpec(w_out.shape, full),
                  pl.BlockSpec((1, d), full), pl.BlockSpec((1, d), full),
                  pl.BlockSpec((N_EXPERTS, d), full), pl.BlockSpec((N_EXPERTS, 1), full),
                  pl.BlockSpec((sub, sub), full), pl.BlockSpec((N_EXPERTS, N_EXPERTS), full)],
        out_specs=[pl.BlockSpec((tm, d), tile), pl.BlockSpec((tm, d), tile), pl.BlockSpec((8, tm), lambda i: (0, i)),
                   pl.BlockSpec((1, N_EXPERTS, LANES), lambda i: (i, 0, 0))],
        out_shape=[jax.ShapeDtypeStruct((n, d), F32), jax.ShapeDtypeStruct((n, d), BF16),
                   jax.ShapeDtypeStruct((8, n), F32), jax.ShapeDtypeStruct((n // tm, N_EXPERTS, LANES), F32)],
        compiler_params=_params("arbitrary"),
        name="merge",
    )(o_hg, o_da, o_rw, gates, x, mod, w_branch, w_out, ln_g, ln_b, w_router_t, router_bias, before, before16)


def _local_rows(tm):
    return -(-(2 * tm + N_EXPERTS * (MOE_CHUNK - 1)) // LANES) * LANES


def _token_columns(route):
    return jnp.concatenate([route, jnp.zeros((LANES - route.shape[0], route.shape[1]), F32)], axis=0).T


def _segment_copies(i, nch_ref, loc_ref, glob_ref, local_buf, global_buf, sem, to_global):
    def run(action):
        for e in range(N_EXPERTS):
            seg = i * N_EXPERTS + e
            loc0, glob0 = loc_ref[seg], glob_ref[seg]

            def one(c, carry):
                loc = local_buf.at[pl.ds(pl.multiple_of(loc0 + c * MOE_CHUNK, MOE_CHUNK), MOE_CHUNK), :]
                glob = global_buf.at[pl.ds(pl.multiple_of(glob0 + c * MOE_CHUNK, MOE_CHUNK), MOE_CHUNK), :]
                copy = pltpu.make_async_copy(loc, glob, sem) if to_global else pltpu.make_async_copy(glob, loc, sem)
                getattr(copy, action)()
                return carry

            lax.fori_loop(0, nch_ref[seg], one, 0)
    return run


def _dispatch_body(nch_ref, loc_ref, glob_ref, u_ref, route_ref, xs_in_ref, xs_ref, stage2_ref, sems):
    del xs_in_ref
    i = pl.program_id(0)
    last = pl.num_programs(0) - 1
    tm, d = u_ref.shape
    slot = i % 2
    stage_ref = stage2_ref.at[slot]

    def tile_copies(tile):
        return _segment_copies(tile, nch_ref, loc_ref, glob_ref, stage2_ref.at[tile % 2], xs_ref,
                               sems.at[tile % 2], to_global=True)

    @pl.when(i >= 2)
    def _():
        tile_copies(i - 2)("wait")

    route = route_ref[...]
    local_row = _iota((stage_ref.shape[0], tm), 0)
    take1 = local_row == route[0:1].astype(jnp.int32)
    take2 = local_row == route[1:2].astype(jnp.int32)
    perm = jnp.where(take1 | take2, 1.0, 0.0).astype(BF16)
    stage_ref[:, 0:d] = jnp.dot(perm, u_ref[...], preferred_element_type=F32).astype(BF16)

    cols = _token_columns(route)
    lane = _iota((tm, LANES), 1)

    def weight_cols(w):
        hi = w.astype(BF16).astype(F32)
        return jnp.where(lane == 0, hi, jnp.where(lane == 1, w - hi, 0.0)).astype(BF16)

    stage_ref[:, d:d + LANES] = (
        jnp.dot(jnp.where(take1, 1.0, 0.0).astype(BF16), weight_cols(cols[:, 2:3]), preferred_element_type=F32)
        + jnp.dot(jnp.where(take2, 1.0, 0.0).astype(BF16), weight_cols(cols[:, 3:4]), preferred_element_type=F32)
    ).astype(BF16)

    tile_copies(i)("start")

    @pl.when(i == last)
    def _():

        @pl.when(i >= 1)
        def _():
            tile_copies(i - 1)("wait")

        tile_copies(i)("wait")


def _dispatch(u2, route, nch, loc, glob, rows, tm):
    n, d = u2.shape
    width = d + LANES
    return pl.pallas_call(
        _dispatch_body,
        grid_spec=pltpu.PrefetchScalarGridSpec(
            num_scalar_prefetch=3,
            grid=(n // tm,),
            in_specs=[pl.BlockSpec((tm, d), lambda i, *_: (i, 0)),
                      pl.BlockSpec((8, tm), lambda i, *_: (0, i)),
                      pl.BlockSpec(memory_space=pl.ANY)],
            out_specs=pl.BlockSpec(memory_space=pl.ANY),
            scratch_shapes=[pltpu.VMEM((2, _local_rows(tm), width), BF16), pltpu.SemaphoreType.DMA((2,))],
        ),
        out_shape=jax.ShapeDtypeStruct((rows, width), BF16),
        input_output_aliases={5: 0},
        compiler_params=_params("arbitrary"),
        name="moe_dispatch",
    )(nch, loc, glob, u2, route, jnp.zeros((rows, width), BF16))


def _experts_body(te_ref, x_ref, wgu_ref, wd_ref, y_ref):
    used = te_ref[pl.program_id(0)] < N_EXPERTS

    @pl.when(used)
    def _():
        d = wgu_ref.shape[1]
        weight = x_ref[:, d:d + 1].astype(F32) + x_ref[:, d + 1:d + 2].astype(F32)
        hidden = jnp.dot(x_ref[:, 0:d], wgu_ref[0], preferred_element_type=F32)
        hg = hidden[:, :D_EXPERT]
        act = hg * _sigmoid(hg) * hidden[:, D_EXPERT:] * weight
        y_ref[...] = _mm(act, wd_ref[0]).astype(BF16)

    @pl.when(jnp.logical_not(used))
    def _():
        y_ref[...] = jnp.zeros_like(y_ref)


def _experts(xs, tile_expert, w_gu, w_down):
    rows, width = xs.shape
    d = w_down.shape[2]
    expert = lambda g, te: (jnp.minimum(te[g], N_EXPERTS - 1), 0, 0)
    return pl.pallas_call(
        _experts_body,
        grid_spec=pltpu.PrefetchScalarGridSpec(
            num_scalar_prefetch=1,
            grid=(rows // MOE_TM,),
            in_specs=[pl.BlockSpec((MOE_TM, width), lambda g, te: (g, 0)),
                      pl.BlockSpec((1, d, 2 * D_EXPERT), expert),
                      pl.BlockSpec((1, D_EXPERT, d), expert)],
            out_specs=pl.BlockSpec((MOE_TM, d), lambda g, te: (g, 0)),
        ),
        out_shape=jax.ShapeDtypeStruct((rows, d), BF16),
        compiler_params=_params("arbitrary"),
        name="moe_experts",
    )(tile_expert, xs, w_gu, w_down)


def _combine_body(nch_ref, loc_ref, glob_ref, route_ref, x1_ref, mod_ref, lng_ref, lnb_ref, ys_ref, o_ref,
                  back2_ref, sems):
    i = pl.program_id(0)

    def tile_copies(tile):
        return _segment_copies(tile, nch_ref, loc_ref, glob_ref, back2_ref.at[tile % 2], ys_ref,
                               sems.at[tile % 2], to_global=False)

    @pl.when(i == 0)
    def _():
        back2_ref[...] = jnp.zeros_like(back2_ref)
        tile_copies(i)("start")

    @pl.when(i + 1 < pl.num_programs(0))
    def _():
        tile_copies(i + 1)("start")

    cols = _token_columns(route_ref[...]).astype(jnp.int32)
    local_row = _iota((x1_ref.shape[0], back2_ref.shape[1]), 1)
    unperm = jnp.where((local_row == cols[:, 0:1]) | (local_row == cols[:, 1:2]), 1.0, 0.0).astype(BF16)
    tile_copies(i)("wait")
    ffn = jnp.dot(unperm, back2_ref[i % 2], preferred_element_type=F32)
    y = ALPHA * x1_ref[...] + (1.0 + mod_ref[0, 5:6, :]) * ffn
    o_ref[...] = _layer_norm(y, lng_ref[...], lnb_ref[...])


def _combine(ys, route, nch, loc, glob, x1, mod, ln_g, ln_b, seq, tm):
    n, d = x1.shape
    per_seq = seq // tm
    full = lambda i, *_: (0, 0)
    return pl.pallas_call(
        _combine_body,
        grid_spec=pltpu.PrefetchScalarGridSpec(
            num_scalar_prefetch=3,
            grid=(n // tm,),
            in_specs=[pl.BlockSpec((8, tm), lambda i, *_: (0, i)),
                      pl.BlockSpec((tm, d), lambda i, *_: (i, 0)),
                      pl.BlockSpec((1, 6, d), lambda i, *_: (i // per_seq, 0, 0)),
                      pl.BlockSpec((1, d), full), pl.BlockSpec((1, d), full),
                      pl.BlockSpec(memory_space=pl.ANY)],
            out_specs=pl.BlockSpec((tm, d), lambda i, *_: (i, 0)),
            scratch_shapes=[pltpu.VMEM((2, _local_rows(tm), d), BF16), pltpu.SemaphoreType.DMA((2,))],
        ),
        out_shape=jax.ShapeDtypeStruct((n, d), F32),
        compiler_params=_params("arbitrary"),
        name="moe_combine",
    )(nch, loc, glob, route, x1, mod, ln_g, ln_b, ys)


def _moe(u2, route, counts, w_gu, w_down, x1, mod, ln_g, ln_b, seq, tm):
    n = u2.shape[0]
    n_tiles = n // tm
    seg_rows = (counts + MOE_CHUNK - 1) // MOE_CHUNK * MOE_CHUNK
    loc = jnp.cumsum(seg_rows, axis=1) - seg_rows
    region = (jnp.sum(seg_rows, axis=0) + MOE_TM - 1) // MOE_TM * MOE_TM
    region_end = jnp.cumsum(region)
    glob = (region_end - region)[None, :] + jnp.cumsum(seg_rows, axis=0) - seg_rows
    rows = -(-(2 * n + n_tiles * N_EXPERTS * (MOE_CHUNK - 1) + N_EXPERTS * (MOE_TM - 1)) // MOE_TM) * MOE_TM
    tile_expert = jnp.sum(jnp.arange(rows // MOE_TM, dtype=jnp.int32)[:, None] * MOE_TM >= region_end[None, :],
                          axis=1).astype(jnp.int32)
    flat = lambda a: a.reshape(-1).astype(jnp.int32)
    nch, loc, glob = flat(seg_rows // MOE_CHUNK), flat(loc), flat(glob)
    xs = _dispatch(u2, route, nch, loc, glob, rows, tm)
    ys = _experts(xs, tile_expert, w_gu, w_down)
    return _combine(ys, route, nch, loc, glob, x1, mod, ln_g, ln_b, seq, tm)


def _tiles(seq):
    return min(512, seq), min(256, seq // 2)


def kernel(x, c, w_ada, b_ada, w_in, hg_lb_logits, hg_norm_g, da_lambda, da_subln_g, rw_mu, rw_w0, rw_w_up,
           rw_a0, rw_a_up, rw_g_up, rw_k_k, rw_k_a, rw_r_k, rw_gn_g, rw_gn_b, w_merge, b_merge, w_branch, w_out,
           ln_g, ln_b, w_router, router_bias, w_exp_gate, w_exp_up, w_exp_down):
    bsz, seq, d = x.shape
    depth = w_in.shape[0]
    n = bsz * seq
    tm, blk = _tiles(seq)

    sm = jax.nn.softmax(hg_lb_logits.astype(F32), axis=0)
    hg_lb = jnp.cumsum(sm, axis=0) - sm[0:1]
    slopes = jnp.asarray([2.0 ** (-8.0 * (h + 1) / DA_HEADS) for h in range(DA_HEADS)], F32)

    mod_all = _ada(c, w_ada, b_ada).reshape(depth, bsz, 6, d)
    w_router_t = w_router.T
    router_bias = router_bias.reshape(N_EXPERTS, 1)

    xf = x.reshape(n, d)
    for l in range(depth):
        mod = mod_all[l]
        lq1, lk1, lq2, lk2 = da_lambda[l].astype(F32)
        lam_init = 0.8 - 0.6 * math.exp(-0.3 * l)
        lam = jnp.exp(jnp.sum(lq1 * lk1)) - jnp.exp(jnp.sum(lq2 * lk2)) + lam_init
        scal = jnp.concatenate([jnp.stack([lam, jnp.asarray(1.0 - lam_init, F32)]), slopes])

        w_gates = jnp.concatenate([w_merge[l, br] for br in range(3)], axis=1).astype(BF16)
        y, gates, v_t = _proj(xf, mod, w_in[l].astype(BF16), w_gates, b_merge[l].reshape(1, 3 * d), seq, tm, blk)

        o_hg = _hgrn2(y, hg_lb[l].reshape(1, HG_W), hg_norm_g[l].reshape(1, HG_DV), bsz, seq)
        o_da = _attn(y, v_t, scal, da_subln_g[l].reshape(1, DA_DV), bsz, seq, blk)
        rw = dict(mu=rw_mu[l].reshape(1, -1), w0=rw_w0[l].reshape(1, -1), w_up=rw_w_up[l],
                  a0=rw_a0[l].reshape(1, -1), a_up=rw_a_up[l], g_up=rw_g_up[l],
                  k_k=rw_k_k[l].reshape(1, -1), k_a=rw_k_a[l].reshape(1, -1), r_k=rw_r_k[l].reshape(1, -1),
                  gn_g=rw_gn_g[l].reshape(1, -1), gn_b=rw_gn_b[l].reshape(1, -1))
        o_rw = _rwkv(y, rw, bsz, seq)

        x1, u2, route, counts = _merge(o_hg, o_da, o_rw, gates, xf, mod, w_branch[l].astype(BF16),
                                       w_out[l].astype(BF16), ln_g[l, 0].reshape(1, d), ln_b[l, 0].reshape(1, d),
                                       w_router_t, router_bias, seq, tm)
        w_gu = jnp.concatenate([w_exp_gate[l], w_exp_up[l]], axis=-1).astype(BF16)
        xf = _moe(u2, route, counts[:, :, 0].astype(jnp.int32), w_gu, w_exp_down[l].astype(BF16), x1, mod,
                  ln_g[l, 1].reshape(1, d), ln_b[l, 1].reshape(1, d), seq, tm)
    return xf.reshape(bsz, seq, d)
```

```python
import functools
import math

import jax
import jax.numpy as jnp
from jax import lax
from jax.experimental import pallas as pl
from jax.experimental.pallas import tpu as pltpu

D_MODEL = 1024
DEPTH = 4
HG_HEADS, HG_DK, HG_DV, HG_CHUNK, HG_SUB = 4, 128, 128, 64, 16
HG_W = HG_HEADS * HG_DV
HG_F_MIN = 1e-6
DA_HEADS, DA_DQK = 4, 64
DA_DV = 2 * DA_DQK
DA_W = DA_HEADS * DA_DV
MASK_VALUE = -1e30
LOG2E = math.log2(math.e)
PROJ_TN = 768
RW_HEADS, RW_DH, RW_CHUNK, RW_SUB = 8, 64, 64, 16
RW_STEP_CHUNKS = 2
RW_STEP_SEQS = 2
RW_W = RW_HEADS * RW_DH
RW_IN_W = 1792
RW_GN_EPS = 64e-5
IN_W = 5376
HG_COL, DA_COL, RW_COL = 0, 2048, 3584
N_EXPERTS, N_GROUPS, EXPERTS_PER_GROUP, D_EXPERT = 16, 4, 4, 512
MOE_CHUNK = 16
MOE_TM = 512
MERGE_SUBTILES = 1
ALPHA = (2.0 * DEPTH) ** 0.25
LN_EPS = 1e-5
RMS_EPS = 1e-6
LANES = 128

F32 = jnp.float32
BF16 = jnp.bfloat16
HIGHEST = lax.Precision.HIGHEST
VMEM_LIMIT = 48 * 1024 * 1024

_NT = (((1,), (1,)), ((), ()))
_TN = (((0,), (0,)), ((), ()))


def _mm(a, b):
    return jnp.dot(a.astype(BF16), b.astype(BF16), preferred_element_type=F32)


def _mm_nt(a, b):
    return lax.dot_general(a.astype(BF16), b.astype(BF16), _NT, preferred_element_type=F32)


def _mm_tn(a, b):
    return lax.dot_general(a.astype(BF16), b.astype(BF16), _TN, preferred_element_type=F32)


def _mmh(a, b):
    return jnp.dot(a, b, precision=HIGHEST, preferred_element_type=F32)


def _mmh_nt(a, b):
    return lax.dot_general(a, b, _NT, precision=HIGHEST, preferred_element_type=F32)


def _seg_sum(x, seg):
    rows = x.shape[0]
    hi = x.astype(BF16)
    lo = (x - hi.astype(F32)).astype(BF16)
    halves = []
    for c0 in range(0, x.shape[1], seg.shape[0]):
        cols = slice(c0, c0 + seg.shape[0])
        both = jnp.dot(jnp.concatenate([hi[:, cols], lo[:, cols]], axis=0), seg, preferred_element_type=F32)
        halves.append(both[:rows] + both[rows:])
    return jnp.concatenate(halves, axis=1)


def _split_mm(x, w_twice, w_lo):
    hi = x.astype(BF16)
    lo = (x - hi.astype(F32)).astype(BF16)
    return (jnp.dot(jnp.concatenate([hi, lo], axis=1), w_twice, preferred_element_type=F32)
            + jnp.dot(hi, w_lo, preferred_element_type=F32))


def _chunk_cumsum(x):
    c = x.shape[0]
    hi = x.astype(BF16)
    rest = x - hi.astype(F32)
    mid = rest.astype(BF16)
    lo = (rest - mid.astype(F32)).astype(BF16)
    col = _iota((c, 4 * c), 1)
    tri = ((col & (c - 1)) <= _iota((c, 4 * c), 0)) & (col < 3 * c)
    return jnp.dot(jnp.where(tri, 1.0, 0.0).astype(BF16), jnp.concatenate([hi, mid, lo, lo], axis=0),
                   preferred_element_type=F32)


def _sigmoid(x):
    return 1.0 / (1.0 + jnp.exp(-x))


def _softplus(x):
    return jnp.maximum(x, 0.0) + jnp.log(1.0 + jnp.exp(-jnp.abs(x)))


def _iota(shape, dim):
    return lax.broadcasted_iota(jnp.int32, shape, dim)


def _params(*sem):
    return pltpu.CompilerParams(dimension_semantics=sem, vmem_limit_bytes=VMEM_LIMIT)


def _layer_norm(y, g, b):
    mu = jnp.mean(y, axis=-1, keepdims=True)
    d = y - mu
    var = jnp.mean(d * d, axis=-1, keepdims=True)
    return d * lax.rsqrt(var + LN_EPS) * g + b


def _ada_body(c_ref, w_ref, b_ref, o_ref):
    c = c_ref[...]
    o_ref[0] = _mmh(c * _sigmoid(c), w_ref[0]) + b_ref[0]


def _ada(c, w_ada, b_ada):
    depth, d, _ = w_ada.shape
    bsz = c.shape[0]
    return pl.pallas_call(
        _ada_body,
        grid=(depth, 6),
        in_specs=[pl.BlockSpec((bsz, d), lambda l, j: (0, 0)),
                  pl.BlockSpec((1, d, d), lambda l, j: (l, 0, j)),
                  pl.BlockSpec((1, 1, d), lambda l, j: (l, 0, j))],
        out_specs=pl.BlockSpec((1, bsz, d), lambda l, j: (l, 0, j)),
        out_shape=jax.ShapeDtypeStruct((depth, bsz, 6 * d), F32),
        compiler_params=_params("arbitrary", "arbitrary"),
        name="ada",
    )(c, w_ada, b_ada.reshape(depth, 1, 6 * d))


def _proj_body(x_ref, mod_ref, win_ref, wg_ref, bg_ref, y_ref, g_ref, vt_ref):
    u = (x_ref[...] * (1.0 + mod_ref[0, 1:2, :]) + mod_ref[0, 0:1, :]).astype(BF16)
    n_kb, kb = vt_ref.shape[1], vt_ref.shape[4]
    v_col = DA_COL + 2 * DA_W
    for c0 in range(0, IN_W, PROJ_TN):
        cols = slice(c0, c0 + PROJ_TN)
        res = jnp.dot(u, win_ref[:, cols], preferred_element_type=F32)
        y_ref[:, cols] = res.astype(BF16)
        if c0 == v_col:
            for h in range(DA_HEADS):
                slope2 = 2.0 ** (-8.0 * (h + 1) / DA_HEADS) * LOG2E
                key_w = jnp.exp2(slope2 * (_iota((1, res.shape[0]), 1) & (kb - 1)).astype(F32))
                v_t = (res[:, h * DA_DV:(h + 1) * DA_DV].T * key_w).astype(BF16)
                tail = jnp.where(_iota((8, res.shape[0]), 0) == 0, key_w, 0.0).astype(BF16)
                for kbi in range(n_kb):
                    keys = slice(kbi * kb, (kbi + 1) * kb)
                    vt_ref[0, kbi, h, 0:DA_DV, :] = v_t[:, keys]
                    vt_ref[0, kbi, h, DA_DV:DA_DV + 8, :] = tail[:, keys]
    for c0 in range(0, 3 * D_MODEL, PROJ_TN):
        cols = slice(c0, c0 + PROJ_TN)
        g = jnp.dot(u, wg_ref[:, cols], preferred_element_type=F32) + bg_ref[:, cols]
        g_ref[:, cols] = _sigmoid(g).astype(BF16)


def _proj(x, mod, w_in, w_gates, b_gates, seq, tm, kb):
    n, d = x.shape
    per_seq = seq // tm
    tile = lambda i: (i, 0)
    resident = dict(index_map=lambda i: (0, 0), pipeline_mode=pl.Buffered(1))
    vt_shape = (n // seq, seq // kb, DA_HEADS, DA_DV + 8, kb)
    return pl.pallas_call(
        _proj_body,
        grid=(n // tm,),
        in_specs=[pl.BlockSpec((tm, d), tile),
                  pl.BlockSpec((1, 6, d), lambda i: (i // per_seq, 0, 0)),
                  pl.BlockSpec(w_in.shape, **resident),
                  pl.BlockSpec(w_gates.shape, **resident),
                  pl.BlockSpec(b_gates.shape, **resident)],
        out_specs=[pl.BlockSpec((tm, IN_W), tile), pl.BlockSpec((tm, 3 * d), tile),
                   pl.BlockSpec((1, tm // kb) + vt_shape[2:], lambda i: (i // per_seq, i % per_seq, 0, 0, 0))],
        out_shape=[jax.ShapeDtypeStruct((n, IN_W), BF16), jax.ShapeDtypeStruct((n, 3 * d), BF16),
                   jax.ShapeDtypeStruct(vt_shape, BF16)],
        compiler_params=_params("arbitrary"),
        name="proj",
    )(x, mod, w_in, w_gates, b_gates)


def _hgrn2_body(y_ref, lb_ref, ng_ref, o_ref, st_ref):
    c, sub = HG_CHUNK, HG_SUB

    @pl.when(pl.program_id(1) == 0)
    def _():
        st_ref[...] = jnp.zeros_like(st_ref)

    q = y_ref[:, 0:HG_W].astype(F32)
    z = y_ref[:, HG_W:2 * HG_W].astype(F32)
    v = y_ref[:, 2 * HG_W:3 * HG_W]
    og = y_ref[:, 3 * HG_W:4 * HG_W].astype(F32)
    lb = lb_ref[...]
    f = lb + (1.0 - lb) * _sigmoid(z)
    kin = (1.0 - lb) * _sigmoid(-z)
    b = _chunk_cumsum(jnp.log(jnp.maximum(f, HG_F_MIN)))
    log_k = jnp.log(kin)
    rel = log_k - b
    b_last = b[c - 1:c]
    q_dec = (q * jnp.exp(b)).astype(BF16)
    k_tail = (kin * jnp.exp(b_last - b)).astype(BF16)
    decay = jnp.exp(b_last)

    ones = jnp.ones((HG_DK, LANES), BF16)
    row_s = _iota((sub, c), 0)
    col_s = _iota((sub, c), 1)
    heads = range(HG_HEADS)
    blks = range(c // sub)
    hs = [slice(h * HG_DK, (h + 1) * HG_DK) for h in heads]

    b2, rel2, lk2 = b * LOG2E, rel * LOG2E, log_k * LOG2E
    half = sub // 2
    diag = {}
    for h in heads:
        for blk in blks:
            rows = slice(blk * sub, (blk + 1) * sub)
            b_i, q_i, rel_i, lk_i = b2[rows, hs[h]], q[rows, hs[h]], rel2[rows, hs[h]], lk2[rows, hs[h]]
            terms = [q_i[t0:] * jnp.exp2(jnp.minimum(b_i[t0:] + rel_i[s:s + 1], lk_i[s:s + 1]))
                     for s in range(sub) for t0 in [0 if s < half else half]]
            w = jnp.concatenate(terms, axis=0).astype(BF16)
            diag[h, blk] = jnp.dot(w, ones, preferred_element_type=F32)
    below = {}
    for h in heads:
        for blk in blks[1:]:
            r0 = blk * sub
            beta = b[r0 - 1:r0, hs[h]]
            q_t = q[r0:r0 + sub, hs[h]] * jnp.exp(b[r0:r0 + sub, hs[h]] - beta)
            k_h = kin[:, hs[h]] * jnp.exp(jnp.minimum(beta - b[:, hs[h]], 0.0))
            below[h, blk] = _mm_nt(q_t, k_h)
    scores = []
    for h in heads:
        a_rows = []
        for blk in blks:
            r0 = blk * sub
            a_blk = jnp.zeros((sub, c), F32)
            for s in range(sub):
                if s < half:
                    sums = diag[h, blk][s * sub:(s + 1) * sub, :c]
                else:
                    start = half * sub + (s - half) * half
                    sums = jnp.concatenate([jnp.zeros((half, c), F32), diag[h, blk][start:start + half, :c]], axis=0)
                a_blk = jnp.where(col_s == r0 + s, sums, a_blk)
            if blk > 0:
                a_blk = jnp.where(col_s < r0, below[h, blk], a_blk)
            a_rows.append(jnp.where(col_s <= row_s + r0, a_blk, 0.0))
        scores.append(jnp.concatenate(a_rows, axis=0).astype(BF16))

    st = [st_ref[h] for h in heads]
    intra = [jnp.dot(scores[h], v[:, hs[h]], preferred_element_type=F32) for h in heads]
    inter = [lax.dot_general(q_dec[:, hs[h]], st[h].astype(BF16), _NT, preferred_element_type=F32) for h in heads]
    for h in heads:
        st_ref[h] = st[h] * decay[:, hs[h]] + lax.dot_general(v[:, hs[h]], k_tail[:, hs[h]], _TN,
                                                             preferred_element_type=F32)
    for h in heads:
        o = intra[h] + inter[h]
        o = o * lax.rsqrt(jnp.mean(o * o, axis=-1, keepdims=True) + RMS_EPS) * ng_ref[...]
        o_ref[:, hs[h]] = (o * (og[:, hs[h]] * _sigmoid(og[:, hs[h]]))).astype(o_ref.dtype)


def _hgrn2(y, lb, norm_g, bsz, seq):
    n = y.shape[0]
    nc = seq // HG_CHUNK
    width = 4 * HG_W
    return pl.pallas_call(
        _hgrn2_body,
        grid=(bsz, nc),
        in_specs=[pl.BlockSpec((HG_CHUNK, width), lambda b, c: (b * nc + c, HG_COL // width)),
                  pl.BlockSpec((1, HG_W), lambda b, c: (0, 0)),
                  pl.BlockSpec((1, HG_DV), lambda b, c: (0, 0))],
        out_specs=pl.BlockSpec((HG_CHUNK, HG_W), lambda b, c: (b * nc + c, 0)),
        out_shape=jax.ShapeDtypeStruct((n, HG_W), BF16),
        scratch_shapes=[pltpu.VMEM((HG_HEADS, HG_DV, HG_DK), F32)],
        compiler_params=_params("arbitrary", "arbitrary"),
        name="hgrn2",
    )(y, lb, norm_g)


def _attn_body(scal_ref, q_ref, k_ref, vt_ref, g_ref, o_ref, qq_ref, sa_ref, sb_ref, p_ref, m_ref, sc_ref,
               acc_ref, *, kb):
    h = pl.program_id(1)
    i = pl.program_id(2)
    lam = scal_ref[0]
    out_scale = scal_ref[1]
    slope = scal_ref[2 + h] * LOG2E
    qb = 2 * kb
    q0 = i * qb
    tiles_per_map = qb // LANES
    a_tiles = kb // LANES

    q = q_ref[...].astype(F32) * (DA_DQK ** -0.5 * LOG2E)
    lane = _iota(q.shape, 1)
    stacked = jnp.concatenate([jnp.where(lane < DA_DQK, q, 0.0), jnp.where(lane >= DA_DQK, q, 0.0)], axis=0)
    qq_ref[...] = stacked.T.astype(BF16)
    m_ref[...] = jnp.full(m_ref.shape, MASK_VALUE, F32)
    sc_ref[...] = jnp.ones(sc_ref.shape, F32)
    acc_ref[...] = jnp.zeros(acc_ref.shape, F32)
    p_ref[...] = jnp.zeros(p_ref.shape, BF16)
    key_off = _iota((kb, LANES), 0)

    def scores(j):
        return jnp.dot(k_ref[pl.ds(j * kb, kb), :], qq_ref[...], preferred_element_type=F32)

    def softmax(j, src_ref, diagonal_of):
        block_bias = slope * (j * kb - q0).astype(F32)
        for t in range(2 * tiles_per_map):
            cols = slice(t * LANES, (t + 1) * LANES)
            in_map = t % tiles_per_map
            half = "A" if in_map < a_tiles else "B"
            if diagonal_of == "B" and half == "A":
                p_ref[t] = jnp.zeros((kb, LANES), BF16)
                sc_ref[:, cols] = jnp.ones((1, LANES), F32)
                continue
            s = src_ref[t]
            if diagonal_of == half:
                q_off = _iota((kb, LANES), 1) + (in_map % a_tiles) * LANES
                s = jnp.where(key_off <= q_off, s, MASK_VALUE)
            m_old = m_ref[:, cols]
            m_new = jnp.maximum(m_old, jnp.max(s, axis=0, keepdims=True) + block_bias)
            p_ref[t] = jnp.exp2(s - (m_new - block_bias)).astype(BF16)
            sc_ref[:, cols] = jnp.exp2(m_old - m_new)
            m_ref[:, cols] = m_new

    n_tiles = 2 * tiles_per_map

    def park(dst_ref, s):
        for t in range(n_tiles):
            dst_ref[t] = s[:, t * LANES:(t + 1) * LANES]

    def probabilities():
        return jnp.concatenate([p_ref[t] for t in range(n_tiles)], axis=1)

    def iteration(j, src_ref, dst_ref, diagonal_of=None):
        sc_prev = sc_ref[...]
        pv = jnp.dot(vt_ref[0, jnp.maximum(j - 1, 0), 0], probabilities(), preferred_element_type=F32)
        if dst_ref is not None:
            park(dst_ref, scores(j + 1))
        for t in range(n_tiles):
            cols = slice(t * LANES, (t + 1) * LANES)
            acc_ref[t] = acc_ref[t] * sc_prev[:, cols] + pv[:, cols]
        softmax(j, src_ref, diagonal_of)

    def pair(j0):
        iteration(j0, sa_ref, sb_ref)
        iteration(j0 + 1, sb_ref, sa_ref)

    def body(quad, carry):
        pair(4 * quad)
        pair(4 * quad + 2)
        return carry

    park(sa_ref, scores(0))
    lax.fori_loop(0, i // 2, body, 0)

    @pl.when(i % 2 == 1)
    def _():
        pair(2 * i - 2)

    iteration(2 * i, sa_ref, sb_ref, "A")
    iteration(2 * i + 1, sb_ref, None, "B")
    acc = (jnp.concatenate([acc_ref[t] for t in range(n_tiles)], axis=1) * sc_ref[...]
           + jnp.dot(vt_ref[0, 2 * i + 1, 0], probabilities(), preferred_element_type=F32))
    o = acc[:DA_DV] / acc[DA_DV:DA_DV + 1]
    d = o[:, :qb] - lam * o[:, qb:]
    g = jnp.concatenate([g_ref[...]] * tiles_per_map, axis=1)
    d = d * lax.rsqrt(jnp.mean(d * d, axis=0, keepdims=True) + RMS_EPS) * g * out_scale
    o_ref[...] = d.T.astype(o_ref.dtype)


def _attn(y, v_t, scal, subln_g, bsz, seq, kb):
    n = y.shape[0]
    qb = 2 * kb
    nq, nk = seq // qb, seq // kb
    qc, kc = DA_COL // DA_DV, (DA_COL + DA_W) // DA_DV
    rows_v = v_t.shape[3]
    g_col = jnp.broadcast_to(subln_g.reshape(DA_DV, 1), (DA_DV, LANES))
    row = (1, 2 * qb)
    return pl.pallas_call(
        functools.partial(_attn_body, kb=kb),
        scratch_shapes=[pltpu.VMEM((DA_DV, 2 * qb), BF16), pltpu.VMEM((2 * qb // LANES, kb, LANES), F32),
                        pltpu.VMEM((2 * qb // LANES, kb, LANES), F32),
                        pltpu.VMEM((2 * qb // LANES, kb, LANES), BF16), pltpu.VMEM(row, F32),
                        pltpu.VMEM(row, F32), pltpu.VMEM((2 * qb // LANES, rows_v, LANES), F32)],
        grid=(bsz, DA_HEADS, nq),
        in_specs=[pl.BlockSpec(memory_space=pltpu.SMEM),
                  pl.BlockSpec((qb, DA_DV), lambda b, h, i: (b * nq + i, qc + h)),
                  pl.BlockSpec((seq, DA_DV), lambda b, h, i: (b, kc + h)),
                  pl.BlockSpec((1, nk, 1, rows_v, kb), lambda b, h, i: (b, 0, h, 0, 0)),
                  pl.BlockSpec((DA_DV, LANES), lambda b, h, i: (0, 0))],
        out_specs=pl.BlockSpec((qb, DA_DV), lambda b, h, i: (b * nq + i, h)),
        out_shape=jax.ShapeDtypeStruct((n, DA_W), BF16),
        compiler_params=_params("arbitrary", "arbitrary", "arbitrary"),
        name="diffattn",
    )(scal, y, y, v_t, g_col)


def _split_f32(x):
    hi = x.astype(BF16)
    hi_f = hi.astype(F32)
    return hi, hi_f, x - hi_f


def _dup_lhs(hi_f, lo_f, low_half):
    packed = jnp.where(low_half, hi_f, lo_f).astype(BF16)
    return jnp.concatenate([packed, packed], axis=1)


def _dup_rhs(hi, lo_f):
    lo = lo_f.astype(BF16)
    return jnp.concatenate([hi, hi, lo, lo], axis=0)


def _rwkv_body(y_ref, mu_ref, w0_ref, a0_ref, wa2_ref, walo_ref, gu2_ref, gulo_ref, kk_ref, ka_ref, rk_ref,
               gng_ref, gnb_ref, seg_ref, o_ref, st_ref, prev_ref, osc_ref):
    c, sub, dh = RW_CHUNK, RW_SUB, RW_DH

    @pl.when(pl.program_id(1) == 0)
    def _():
        st_ref[...] = jnp.zeros_like(st_ref)
        prev_ref[...] = jnp.zeros_like(prev_ref)

    n_seq, seq_rows = y_ref.shape[0], y_ref.shape[1]
    rows = n_seq * seq_rows
    shifted = []
    for si in range(n_seq):
        x = y_ref[si].astype(F32)
        x_prev = jnp.where(_iota(x.shape, 0) == 0, prev_ref[si:si + 1], pltpu.roll(x, 1, axis=0))
        prev_ref[si:si + 1] = x[seq_rows - 1:seq_rows]
        shifted.append(x + (x_prev - x) * mu_ref[...])
    xs = jnp.concatenate(shifted, axis=0)
    r = xs[:, 0:RW_W]
    k = xs[:, RW_W:2 * RW_W]
    v = xs[:, 2 * RW_W:3 * RW_W]
    wa = xs[:, 3 * RW_W:3 * RW_W + LANES]
    gd = xs[:, 3 * RW_W + LANES:RW_IN_W]

    lora = _split_mm(jnp.where(_iota(wa.shape, 1) < 64, jnp.tanh(wa), wa), wa2_ref[...], walo_ref[...])
    w_log = -_softplus(-(w0_ref[...] + lora[:, :RW_W])) - 0.5
    g = -jnp.exp(w_log)
    a = _sigmoid(a0_ref[...] + lora[:, RW_W:])
    gate = _split_mm(_sigmoid(gd), gu2_ref[...], gulo_ref[...])
    seg = seg_ref[...]
    kk = k * kk_ref[...]
    k2 = k * (1.0 + (a - 1.0) * ka_ref[...])
    sums = _seg_sum(jnp.concatenate([kk * kk, r * k2 * rk_ref[...]], axis=0), seg)
    kk = kk * lax.rsqrt(jnp.maximum(sums[:rows], 1e-12))
    bb = kk * a
    bonus = sums[rows:] * v

    chunks = range(rows // c)
    cr = [slice(ci * c, (ci + 1) * c) for ci in chunks]
    gc = jnp.concatenate([_chunk_cumsum(g[s]) for s in cr], axis=0)
    g_last = [gc[s][c - 1:c] for s in cr]
    e_inv = jnp.exp(-gc)
    e_tail = jnp.exp(jnp.concatenate([jnp.broadcast_to(gl, (c, RW_W)) for gl in g_last], axis=0) - gc)
    gam = [jnp.exp(gl) for gl in g_last]
    a_t = (-kk * jnp.exp(gc - g)).astype(BF16)
    r_t = (r * jnp.exp(gc)).astype(BF16)
    b_h = (bb * e_inv).astype(BF16)
    k_h = (k2 * e_inv).astype(BF16)
    k_bar = (k2 * e_tail).astype(BF16)
    b_bar = (bb * e_tail).astype(BF16)
    v_bf = v.astype(BF16)

    row2 = _iota((c, 2 * c), 0)
    lane2 = _iota((c, 2 * c), 1)
    col2 = lane2 & (c - 1)
    low_half = lane2 < c
    strict = row2 > col2
    incl = row2 >= col2
    same_blk = (row2 // sub) == (col2 // sub)
    eye = (row2 == col2).astype(F32)

    def dot(p, q):
        return jnp.dot(p, q, preferred_element_type=F32)

    heads = range(RW_HEADS)
    sls = [slice(h * dh, (h + 1) * dh) for h in heads]
    items = [(ci, h) for ci in chunks for h in heads]
    ar_h = [jnp.concatenate([a_t[cr[ci], sls[h]], r_t[cr[ci], sls[h]]], axis=0) for ci, h in items]
    quad = [lax.dot_general(x, jnp.concatenate([b_h[cr[ci], sls[h]], k_h[cr[ci], sls[h]]], axis=0), _NT,
                            preferred_element_type=F32)
            for x, (ci, h) in zip(ar_h, items)]
    top = [jnp.where(strict, q[:c], 0.0) for q in quad]
    a_ak = [t[:, c:].astype(BF16) for t in top]
    a_r = [jnp.where(incl, q[c:], 0.0).astype(BF16) for q in quad]
    a_ab = [jnp.where(low_half, t, pltpu.roll(t, c, axis=1)) for t in top]
    a_d = [jnp.where(same_blk, x, 0.0) for x in a_ab]
    a_o = [(x - y).astype(BF16) for x, y in zip(a_ab, a_d)]

    s1 = [_split_f32(x) for x in a_d]
    p2 = [dot(_dup_lhs(hf, lf, low_half), _dup_rhs(hi, lf)) for hi, hf, lf in s1]
    s2 = [_split_f32(x) for x in p2]
    rhs2 = [_dup_rhs(hi, lf) for hi, _, lf in s2]
    p4 = [dot(_dup_lhs(hf, lf, low_half), rhs) for (_, hf, lf), rhs in zip(s2, rhs2)]
    s4 = [_split_f32(x) for x in p4]
    rhs4 = [_dup_rhs(hi, lf) for hi, _, lf in s4]
    p8 = [dot(_dup_lhs(hf, lf, low_half), rhs) for (_, hf, lf), rhs in zip(s4, rhs4)]
    rhs8 = [_dup_rhs(hi, lf) for hi, _, lf in (_split_f32(x) for x in p8)]
    t_d = [eye + x for x in a_d]
    for rhs_all in (rhs2, rhs4, rhs8):
        st = [_split_f32(x) for x in t_d]
        t_d = [x + dot(_dup_lhs(hf, lf, low_half), rhs) for x, (_, hf, lf), rhs in zip(t_d, st, rhs_all)]
    t_d = [x.astype(BF16) for x in t_d]

    nn = [dot(t[:, :c], x) for t, x in zip(t_d, a_o)]
    nn_bf = [x.astype(BF16) for x in nn]
    n2 = [dot(x[:, :c], x) for x in nn_bf]
    n3 = [dot(x[:, :c], y.astype(BF16)) for x, y in zip(nn_bf, n2)]
    t_m = [dot((eye + x + y + z).astype(BF16)[:, :c], t).astype(BF16)[:, :c]
           for x, y, z, t in zip(nn, n2, n3, t_d)]

    v_h = [v_bf[cr[ci], sls[h]] for ci, h in items]
    akv = [dot(x, y).astype(BF16) for x, y in zip(a_ak, v_h)]
    at_m = [dot(t, x[:c]).astype(BF16) for t, x in zip(t_m, ar_h)]
    v_p = [dot(t, x) for t, x in zip(t_m, akv)]

    per_seq = seq_rows // c
    chains = [(si, h) for si in range(n_seq) for h in heads]
    state = [st_ref[si, h] for si, h in chains]
    for t in range(per_seq):
        it = [(si * per_seq + t) * RW_HEADS + h for si, h in chains]
        ck = [cr[si * per_seq + t] for si, _ in chains]
        proj = [lax.dot_general(jnp.concatenate([at_m[j], ar_h[j][c:]], axis=0), s.astype(BF16), _NT,
                                preferred_element_type=F32) for j, s in zip(it, state)]
        u = [(p[:c] + v_p[j]).astype(BF16) for p, j in zip(proj, it)]
        for p, uu, j, rws, (_, h) in zip(proj, u, it, ck, chains):
            osc_ref[rws, sls[h]] = p[c:] + dot(a_r[j], jnp.concatenate([uu, v_h[j]], axis=0))
        state = [s * gam[j // RW_HEADS][:, sls[h]] + lax.dot_general(
            jnp.concatenate([v_h[j], uu], axis=0),
            jnp.concatenate([k_bar[rws, sls[h]], b_bar[rws, sls[h]]], axis=0), _TN, preferred_element_type=F32)
            for s, uu, j, rws, (_, h) in zip(state, u, it, ck, chains)]
    for s, (si, h) in zip(state, chains):
        st_ref[si, h] = s

    o = osc_ref[...]
    mean = _seg_sum(o, seg) * (1.0 / dh)
    d = o - mean
    var = _seg_sum(d * d, seg) * (1.0 / dh)
    o = d * lax.rsqrt(var + RW_GN_EPS) * gng_ref[...] + gnb_ref[...]
    o = ((o + bonus) * gate).astype(o_ref.dtype)
    for si in range(n_seq):
        o_ref[si] = o[si * seq_rows:(si + 1) * seq_rows]


def _rwkv(y, p, bsz, seq):
    n = y.shape[0]
    half = RW_W // 2
    seg = (_iota((half, half), 0) // RW_DH == _iota((half, half), 1) // RW_DH).astype(BF16)

    def two_terms(w):
        hi = w.astype(BF16)
        return jnp.concatenate([hi, hi], axis=0), (w - hi.astype(F32)).astype(BF16)

    zeros = jnp.zeros_like(p["w_up"])
    wa2, wa_lo = two_terms(jnp.concatenate([jnp.concatenate([p["w_up"], zeros], axis=1),
                                            jnp.concatenate([zeros, p["a_up"]], axis=1)], axis=0))
    gu2, gu_lo = two_terms(p["g_up"])
    rows = [p["mu"], p["w0"], p["a0"], wa2, wa_lo, gu2, gu_lo, p["k_k"], p["k_a"], p["r_k"],
            p["gn_g"], p["gn_b"], seg]
    full = lambda b, c: (0, 0)
    step = RW_STEP_CHUNKS * RW_CHUNK
    n_seq = RW_STEP_SEQS if bsz % RW_STEP_SEQS == 0 else 1
    out = pl.pallas_call(
        _rwkv_body,
        grid=(bsz // n_seq, seq // step),
        in_specs=[pl.BlockSpec((n_seq, step, RW_IN_W), lambda b, c: (b, c, RW_COL // RW_IN_W))]
        + [pl.BlockSpec(a.shape, full) for a in rows],
        out_specs=pl.BlockSpec((n_seq, step, RW_W), lambda b, c: (b, c, 0)),
        out_shape=jax.ShapeDtypeStruct((bsz, seq, RW_W), BF16),
        scratch_shapes=[pltpu.VMEM((n_seq, RW_HEADS, RW_DH, RW_DH), F32),
                        pltpu.VMEM((n_seq, RW_IN_W), F32),
                        pltpu.VMEM((n_seq * step, RW_W), F32)],
        compiler_params=_params("arbitrary", "arbitrary"),
        name="rwkv7",
    )(y.reshape(bsz, seq, -1), *rows)
    return out.reshape(n, RW_W)


def _first_argmax(vals, row):
    top = jnp.max(vals, axis=0, keepdims=True)
    idx = jnp.min(jnp.where(vals == top, row, N_EXPERTS), axis=0, keepdims=True)
    return top, idx


def _merge_body(ohg_ref, oda_ref, orw_ref, gt_ref, x_ref, mod_ref, wb_ref, wo_ref, lng_ref, lnb_ref,
                wrt_ref, rb_ref, tri_ref, tri16_ref, x1_ref, u2_ref, route_ref, cnt_ref):
    d = D_MODEL
    sub_rows = tri_ref.shape[0]

    def route(rows):
        merged = (gt_ref[rows, 0:d].astype(F32)
                  * jnp.dot(ohg_ref[rows, :], wb_ref[0:HG_W, :], preferred_element_type=F32)
                  + gt_ref[rows, d:2 * d].astype(F32)
                  * jnp.dot(oda_ref[rows, :], wb_ref[HG_W:HG_W + DA_W, :], preferred_element_type=F32)
                  + gt_ref[rows, 2 * d:3 * d].astype(F32)
                  * jnp.dot(orw_ref[rows, :], wb_ref[HG_W + DA_W:, :], preferred_element_type=F32))
        mix = _mm(merged, wo_ref[...])
        x1 = _layer_norm(ALPHA * x_ref[rows, :] + (1.0 + mod_ref[0, 2:3, :]) * mix, lng_ref[...], lnb_ref[...])
        x1_ref[rows, :] = x1
        u2 = x1 * (1.0 + mod_ref[0, 4:5, :]) + mod_ref[0, 3:4, :]
        u2_ref[rows, :] = u2.astype(BF16)

        logits = _mmh_nt(wrt_ref[...], u2)
        ex = jnp.exp(logits - jnp.max(logits, axis=0, keepdims=True))
        scores = ex / jnp.sum(ex, axis=0, keepdims=True)
        sel = scores + rb_ref[...]
        row = _iota(sel.shape, 0)
        best = None
        for grp in range(N_GROUPS):
            a, b, c2, d2 = (sel[grp * EXPERTS_PER_GROUP + i:grp * EXPERTS_PER_GROUP + i + 1] for i in range(4))
            hi1, lo1, hi2, lo2 = jnp.maximum(a, b), jnp.minimum(a, b), jnp.maximum(c2, d2), jnp.minimum(c2, d2)
            top2 = jnp.maximum(hi1, hi2) + jnp.maximum(jnp.minimum(hi1, hi2), jnp.maximum(lo1, lo2))
            if best is None:
                best, best_grp = top2, jnp.zeros_like(top2, dtype=jnp.int32)
            else:
                better = top2 > best
                best = jnp.where(better, top2, best)
                best_grp = jnp.where(better, grp, best_grp)
        masked = jnp.where(row // EXPERTS_PER_GROUP == best_grp, sel, MASK_VALUE)
        _, idx1 = _first_argmax(masked, row)
        _, idx2 = _first_argmax(jnp.where(row == idx1, -jnp.inf, masked), row)
        pick1 = row == idx1
        pick2 = row == idx2
        w1 = jnp.sum(jnp.where(pick1, scores, 0.0), axis=0, keepdims=True)
        w2 = jnp.sum(jnp.where(pick2, scores, 0.0), axis=0, keepdims=True)
        onehot = jnp.where(pick1 | pick2, 1.0, 0.0)
        earlier = jnp.dot(onehot.astype(BF16), tri_ref[...], preferred_element_type=F32)
        return pick1, pick2, w1 / (w1 + w2), w2 / (w1 + w2), earlier, jnp.sum(onehot, axis=1, keepdims=True)

    parts = [route(slice(r0, r0 + sub_rows)) for r0 in range(0, x_ref.shape[0], sub_rows)]

    cnt = functools.reduce(jnp.add, [p[5] for p in parts])
    chunks = jnp.floor((cnt + (MOE_CHUNK - 1)) * (1.0 / MOE_CHUNK))
    seg_start = MOE_CHUNK * jnp.dot(tri16_ref[...], jnp.broadcast_to(chunks, (N_EXPERTS, LANES)).astype(BF16),
                                    preferred_element_type=F32)[:, 0:1]
    before = jnp.zeros_like(cnt)
    for k, (pick1, pick2, wn1, wn2, earlier, cnt_k) in enumerate(parts):
        pos = seg_start + before + earlier
        pos1 = jnp.sum(jnp.where(pick1, pos, 0.0), axis=0, keepdims=True)
        pos2 = jnp.sum(jnp.where(pick2, pos, 0.0), axis=0, keepdims=True)
        route_ref[:, k * sub_rows:(k + 1) * sub_rows] = jnp.concatenate(
            [pos1, pos2, wn1, wn2, jnp.zeros((4, sub_rows), F32)], axis=0)
        before = before + cnt_k
    cnt_ref[0] = jnp.broadcast_to(cnt, (N_EXPERTS, LANES))


def _merge(o_hg, o_da, o_rw, gates, x, mod, w_branch, w_out, ln_g, ln_b, w_router_t, router_bias, seq, tm):
    n, d = x.shape
    per_seq = seq // tm
    tile = lambda i: (i, 0)
    full = lambda i: (0, 0)
    sub = tm // MERGE_SUBTILES
    before = (_iota((sub, sub), 0) < _iota((sub, sub), 1)).astype(BF16)
    before16 = (_iota((N_EXPERTS, N_EXPERTS), 1) < _iota((N_EXPERTS, N_EXPERTS), 0)).astype(BF16)
    return pl.pallas_call(
        _merge_body,
        grid=(n // tm,),
        in_specs=[pl.BlockSpec((tm, HG_W), tile), pl.BlockSpec((tm, DA_W), tile), pl.BlockSpec((tm, RW_W), tile),
                  pl.BlockSpec((tm, 3 * d), tile), pl.BlockSpec((tm, d), tile),
                  pl.BlockSpec((1, 6, d), lambda i: (i // per_seq, 0, 0)),
                  pl.BlockSpec(w_branch.shape, full), pl.BlockSpec(w_out.shape, full),
                  pl.BlockSpec((1, d), full), pl.BlockSpec((1, d), full),
                  pl.BlockSpec((N_EXPERTS, d), full), pl.BlockSpec((N_EXPERTS, 1), full),
                  pl.BlockSpec((sub, sub), full), pl.BlockSpec((N_EXPERTS, N_EXPERTS), full)],
        out_specs=[pl.BlockSpec((tm, d), tile), pl.BlockSpec((tm, d), tile), pl.BlockSpec((8, tm), lambda i: (0, i)),
                   pl.BlockSpec((1, N_EXPERTS, LANES), lambda i: (i, 0, 0))],
        out_shape=[jax.ShapeDtypeStruct((n, d), F32), jax.ShapeDtypeStruct((n, d), BF16),
                   jax.ShapeDtypeStruct((8, n), F32), jax.ShapeDtypeStruct((n // tm, N_EXPERTS, LANES), F32)],
        compiler_params=_params("arbitrary"),
        name="merge",
    )(o_hg, o_da, o_rw, gates, x, mod, w_branch, w_out, ln_g, ln_b, w_router_t, router_bias, before, before16)


def _local_rows(tm):
    return -(-(2 * tm + N_EXPERTS * (MOE_CHUNK - 1)) // LANES) * LANES


def _token_columns(route):
    return jnp.concatenate([route, jnp.zeros((LANES - route.shape[0], route.shape[1]), F32)], axis=0).T


def _segment_copies(i, nch_ref, loc_ref, glob_ref, local_buf, global_buf, sem, to_global):
    def run(action):
        for e in range(N_EXPERTS):
            seg = i * N_EXPERTS + e
            loc0, glob0 = loc_ref[seg], glob_ref[seg]

            def one(c, carry):
                loc = local_buf.at[pl.ds(pl.multiple_of(loc0 + c * MOE_CHUNK, MOE_CHUNK), MOE_CHUNK), :]
                glob = global_buf.at[pl.ds(pl.multiple_of(glob0 + c * MOE_CHUNK, MOE_CHUNK), MOE_CHUNK), :]
                copy = pltpu.make_async_copy(loc, glob, sem) if to_global else pltpu.make_async_copy(glob, loc, sem)
                getattr(copy, action)()
                return carry

            lax.fori_loop(0, nch_ref[seg], one, 0)
    return run


def _dispatch_body(nch_ref, loc_ref, glob_ref, u_ref, route_ref, xs_in_ref, xs_ref, stage2_ref, sems):
    del xs_in_ref
    i = pl.program_id(0)
    last = pl.num_programs(0) - 1
    tm, d = u_ref.shape
    slot = i % 2
    stage_ref = stage2_ref.at[slot]

    def tile_copies(tile):
        return _segment_copies(tile, nch_ref, loc_ref, glob_ref, stage2_ref.at[tile % 2], xs_ref,
                               sems.at[tile % 2], to_global=True)

    @pl.when(i >= 2)
    def _():
        tile_copies(i - 2)("wait")

    route = route_ref[...]
    local_row = _iota((stage_ref.shape[0], tm), 0)
    take1 = local_row == route[0:1].astype(jnp.int32)
    take2 = local_row == route[1:2].astype(jnp.int32)
    perm = jnp.where(take1 | take2, 1.0, 0.0).astype(BF16)
    stage_ref[:, 0:d] = jnp.dot(perm, u_ref[...], preferred_element_type=F32).astype(BF16)

    cols = _token_columns(route)
    lane = _iota((tm, LANES), 1)

    def weight_cols(w):
        hi = w.astype(BF16).astype(F32)
        return jnp.where(lane == 0, hi, jnp.where(lane == 1, w - hi, 0.0)).astype(BF16)

    stage_ref[:, d:d + LANES] = (
        jnp.dot(jnp.where(take1, 1.0, 0.0).astype(BF16), weight_cols(cols[:, 2:3]), preferred_element_type=F32)
        + jnp.dot(jnp.where(take2, 1.0, 0.0).astype(BF16), weight_cols(cols[:, 3:4]), preferred_element_type=F32)
    ).astype(BF16)

    tile_copies(i)("start")

    @pl.when(i == last)
    def _():

        @pl.when(i >= 1)
        def _():
            tile_copies(i - 1)("wait")

        tile_copies(i)("wait")


def _dispatch(u2, route, nch, loc, glob, rows, tm):
    n, d = u2.shape
    width = d + LANES
    return pl.pallas_call(
        _dispatch_body,
        grid_spec=pltpu.PrefetchScalarGridSpec(
            num_scalar_prefetch=3,
            grid=(n // tm,),
            in_specs=[pl.BlockSpec((tm, d), lambda i, *_: (i, 0)),
                      pl.BlockSpec((8, tm), lambda i, *_: (0, i)),
                      pl.BlockSpec(memory_space=pl.ANY)],
            out_specs=pl.BlockSpec(memory_space=pl.ANY),
            scratch_shapes=[pltpu.VMEM((2, _local_rows(tm), width), BF16), pltpu.SemaphoreType.DMA((2,))],
        ),
        out_shape=jax.ShapeDtypeStruct((rows, width), BF16),
        input_output_aliases={5: 0},
        compiler_params=_params("arbitrary"),
        name="moe_dispatch",
    )(nch, loc, glob, u2, route, jnp.zeros((rows, width), BF16))


def _experts_body(te_ref, x_ref, wgu_ref, wd_ref, y_ref):
    used = te_ref[pl.program_id(0)] < N_EXPERTS

    @pl.when(used)
    def _():
        d = wgu_ref.shape[1]
        weight = x_ref[:, d:d + 1].astype(F32) + x_ref[:, d + 1:d + 2].astype(F32)
        hidden = jnp.dot(x_ref[:, 0:d], wgu_ref[0], preferred_element_type=F32)
        hg = hidden[:, :D_EXPERT]
        act = hg * _sigmoid(hg) * hidden[:, D_EXPERT:] * weight
        y_ref[...] = _mm(act, wd_ref[0]).astype(BF16)

    @pl.when(jnp.logical_not(used))
    def _():
        y_ref[...] = jnp.zeros_like(y_ref)


def _experts(xs, tile_expert, w_gu, w_down):
    rows, width = xs.shape
    d = w_down.shape[2]
    expert = lambda g, te: (jnp.minimum(te[g], N_EXPERTS - 1), 0, 0)
    return pl.pallas_call(
        _experts_body,
        grid_spec=pltpu.PrefetchScalarGridSpec(
            num_scalar_prefetch=1,
            grid=(rows // MOE_TM,),
            in_specs=[pl.BlockSpec((MOE_TM, width), lambda g, te: (g, 0)),
                      pl.BlockSpec((1, d, 2 * D_EXPERT), expert),
                      pl.BlockSpec((1, D_EXPERT, d), expert)],
            out_specs=pl.BlockSpec((MOE_TM, d), lambda g, te: (g, 0)),
        ),
        out_shape=jax.ShapeDtypeStruct((rows, d), BF16),
        compiler_params=_params("arbitrary"),
        name="moe_experts",
    )(tile_expert, xs, w_gu, w_down)


def _combine_body(nch_ref, loc_ref, glob_ref, route_ref, x1_ref, mod_ref, lng_ref, lnb_ref, ys_ref, o_ref,
                  back2_ref, sems):
    i = pl.program_id(0)

    def tile_copies(tile):
        return _segment_copies(tile, nch_ref, loc_ref, glob_ref, back2_ref.at[tile % 2], ys_ref,
                               sems.at[tile % 2], to_global=False)

    @pl.when(i == 0)
    def _():
        back2_ref[...] = jnp.zeros_like(back2_ref)
        tile_copies(i)("start")

    @pl.when(i + 1 < pl.num_programs(0))
    def _():
        tile_copies(i + 1)("start")

    cols = _token_columns(route_ref[...]).astype(jnp.int32)
    local_row = _iota((x1_ref.shape[0], back2_ref.shape[1]), 1)
    unperm = jnp.where((local_row == cols[:, 0:1]) | (local_row == cols[:, 1:2]), 1.0, 0.0).astype(BF16)
    tile_copies(i)("wait")
    ffn = jnp.dot(unperm, back2_ref[i % 2], preferred_element_type=F32)
    y = ALPHA * x1_ref[...] + (1.0 + mod_ref[0, 5:6, :]) * ffn
    o_ref[...] = _layer_norm(y, lng_ref[...], lnb_ref[...])


def _combine(ys, route, nch, loc, glob, x1, mod, ln_g, ln_b, seq, tm):
    n, d = x1.shape
    per_seq = seq // tm
    full = lambda i, *_: (0, 0)
    return pl.pallas_call(
        _combine_body,
        grid_spec=pltpu.PrefetchScalarGridSpec(
            num_scalar_prefetch=3,
            grid=(n // tm,),
            in_specs=[pl.BlockSpec((8, tm), lambda i, *_: (0, i)),
                      pl.BlockSpec((tm, d), lambda i, *_: (i, 0)),
                      pl.BlockSpec((1, 6, d), lambda i, *_: (i // per_seq, 0, 0)),
                      pl.BlockSpec((1, d), full), pl.BlockSpec((1, d), full),
                      pl.BlockSpec(memory_space=pl.ANY)],
            out_specs=pl.BlockSpec((tm, d), lambda i, *_: (i, 0)),
            scratch_shapes=[pltpu.VMEM((2, _local_rows(tm), d), BF16), pltpu.SemaphoreType.DMA((2,))],
        ),
        out_shape=jax.ShapeDtypeStruct((n, d), F32),
        compiler_params=_params("arbitrary"),
        name="moe_combine",
    )(nch, loc, glob, route, x1, mod, ln_g, ln_b, ys)


def _moe(u2, route, counts, w_gu, w_down, x1, mod, ln_g, ln_b, seq, tm):
    n = u2.shape[0]
    n_tiles = n // tm
    seg_rows = (counts + MOE_CHUNK - 1) // MOE_CHUNK * MOE_CHUNK
    loc = jnp.cumsum(seg_rows, axis=1) - seg_rows
    region = (jnp.sum(seg_rows, axis=0) + MOE_TM - 1) // MOE_TM * MOE_TM
    region_end = jnp.cumsum(region)
    glob = (region_end - region)[None, :] + jnp.cumsum(seg_rows, axis=0) - seg_rows
    rows = -(-(2 * n + n_tiles * N_EXPERTS * (MOE_CHUNK - 1) + N_EXPERTS * (MOE_TM - 1)) // MOE_TM) * MOE_TM
    tile_expert = jnp.sum(jnp.arange(rows // MOE_TM, dtype=jnp.int32)[:, None] * MOE_TM >= region_end[None, :],
                          axis=1).astype(jnp.int32)
    flat = lambda a: a.reshape(-1).astype(jnp.int32)
    nch, loc, glob = flat(seg_rows // MOE_CHUNK), flat(loc), flat(glob)
    xs = _dispatch(u2, route, nch, loc, glob, rows, tm)
    ys = _experts(xs, tile_expert, w_gu, w_down)
    return _combine(ys, route, nch, loc, glob, x1, mod, ln_g, ln_b, seq, tm)


def _tiles(seq):
    return min(512, seq), min(256, seq // 2)


def kernel(x, c, w_ada, b_ada, w_in, hg_lb_logits, hg_norm_g, da_lambda, da_subln_g, rw_mu, rw_w0, rw_w_up,
           rw_a0, rw_a_up, rw_g_up, rw_k_k, rw_k_a, rw_r_k, rw_gn_g, rw_gn_b, w_merge, b_merge, w_branch, w_out,
           ln_g, ln_b, w_router, router_bias, w_exp_gate, w_exp_up, w_exp_down):
    bsz, seq, d = x.shape
    depth = w_in.shape[0]
    n = bsz * seq
    tm, blk = _tiles(seq)

    sm = jax.nn.softmax(hg_lb_logits.astype(F32), axis=0)
    hg_lb = jnp.cumsum(sm, axis=0) - sm[0:1]
    slopes = jnp.asarray([2.0 ** (-8.0 * (h + 1) / DA_HEADS) for h in range(DA_HEADS)], F32)

    mod_all = _ada(c, w_ada, b_ada).reshape(depth, bsz, 6, d)
    w_router_t = w_router.T
    router_bias = router_bias.reshape(N_EXPERTS, 1)

    xf = x.reshape(n, d)
    for l in range(depth):
        mod = mod_all[l]
        lq1, lk1, lq2, lk2 = da_lambda[l].astype(F32)
        lam_init = 0.8 - 0.6 * math.exp(-0.3 * l)
        lam = jnp.exp(jnp.sum(lq1 * lk1)) - jnp.exp(jnp.sum(lq2 * lk2)) + lam_init
        scal = jnp.concatenate([jnp.stack([lam, jnp.asarray(1.0 - lam_init, F32)]), slopes])

        w_gates = jnp.concatenate([w_merge[l, br] for br in range(3)], axis=1).astype(BF16)
        y, gates, v_t = _proj(xf, mod, w_in[l].astype(BF16), w_gates, b_merge[l].reshape(1, 3 * d), seq, tm, blk)

        o_hg = _hgrn2(y, hg_lb[l].reshape(1, HG_W), hg_norm_g[l].reshape(1, HG_DV), bsz, seq)
        o_da = _attn(y, v_t, scal, da_subln_g[l].reshape(1, DA_DV), bsz, seq, blk)
        rw = dict(mu=rw_mu[l].reshape(1, -1), w0=rw_w0[l].reshape(1, -1), w_up=rw_w_up[l],
                  a0=rw_a0[l].reshape(1, -1), a_up=rw_a_up[l], g_up=rw_g_up[l],
                  k_k=rw_k_k[l].reshape(1, -1), k_a=rw_k_a[l].reshape(1, -1), r_k=rw_r_k[l].reshape(1, -1),
                  gn_g=rw_gn_g[l].reshape(1, -1), gn_b=rw_gn_b[l].reshape(1, -1))
        o_rw = _rwkv(y, rw, bsz, seq)

        x1, u2, route, counts = _merge(o_hg, o_da, o_rw, gates, xf, mod, w_branch[l].astype(BF16),
                                       w_out[l].astype(BF16), ln_g[l, 0].reshape(1, d), ln_b[l, 0].reshape(1, d),
                                       w_router_t, router_bias, seq, tm)
        w_gu = jnp.concatenate([w_exp_gate[l], w_exp_up[l]], axis=-1).astype(BF16)
        xf = _moe(u2, route, counts[:, :, 0].astype(jnp.int32), w_gu, w_exp_down[l].astype(BF16), x1, mod,
                  ln_g[l, 1].reshape(1, d), ln_b[l, 1].reshape(1, d), seq, tm)
    return xf.reshape(bsz, seq, d)
```

```python
import functools
import math

import jax
import jax.numpy as jnp
from jax import lax
from jax.experimental import pallas as pl
from jax.experimental.pallas import tpu as pltpu

D_MODEL = 1024
DEPTH = 4
HG_HEADS, HG_DK, HG_DV, HG_CHUNK, HG_SUB = 4, 128, 128, 64, 16
HG_W = HG_HEADS * HG_DV
HG_F_MIN = 1e-6
HG_STEP_SEQS = 2
HG_SAFE_SPAN = 60.0
DA_HEADS, DA_DQK = 4, 64
DA_DV = 2 * DA_DQK
DA_W = DA_HEADS * DA_DV
MASK_VALUE = -1e30
LOG2E = math.log2(math.e)
PROJ_TN = 768
RW_HEADS, RW_DH, RW_CHUNK, RW_SUB = 8, 64, 64, 16
RW_STEP_CHUNKS = 2
RW_STEP_SEQS = 2
RW_W = RW_HEADS * RW_DH
RW_IN_W = 1792
RW_GN_EPS = 64e-5
IN_W = 5376
HG_COL, DA_COL, RW_COL = 0, 2048, 3584
N_EXPERTS, N_GROUPS, EXPERTS_PER_GROUP, D_EXPERT = 16, 4, 4, 512
MOE_CHUNK = 16
MOE_TM = 512
MERGE_SUBTILES = 1
ALPHA = (2.0 * DEPTH) ** 0.25
LN_EPS = 1e-5
RMS_EPS = 1e-6
LANES = 128

F32 = jnp.float32
BF16 = jnp.bfloat16
HIGHEST = lax.Precision.HIGHEST
VMEM_LIMIT = 48 * 1024 * 1024

_NT = (((1,), (1,)), ((), ()))
_TN = (((0,), (0,)), ((), ()))


def _mm(a, b):
    return jnp.dot(a.astype(BF16), b.astype(BF16), preferred_element_type=F32)


def _mm_nt(a, b):
    return lax.dot_general(a.astype(BF16), b.astype(BF16), _NT, preferred_element_type=F32)


def _mm_tn(a, b):
    return lax.dot_general(a.astype(BF16), b.astype(BF16), _TN, preferred_element_type=F32)


def _mmh(a, b):
    return jnp.dot(a, b, precision=HIGHEST, preferred_element_type=F32)


def _mmh_nt(a, b):
    return lax.dot_general(a, b, _NT, precision=HIGHEST, preferred_element_type=F32)


def _seg_sum(x, seg):
    rows = x.shape[0]
    hi = x.astype(BF16)
    lo = (x - hi.astype(F32)).astype(BF16)
    halves = []
    for c0 in range(0, x.shape[1], seg.shape[0]):
        cols = slice(c0, c0 + seg.shape[0])
        both = jnp.dot(jnp.concatenate([hi[:, cols], lo[:, cols]], axis=0), seg, preferred_element_type=F32)
        halves.append(both[:rows] + both[rows:])
    return jnp.concatenate(halves, axis=1)


def _split_mm(x, w_twice, w_lo):
    hi = x.astype(BF16)
    lo = (x - hi.astype(F32)).astype(BF16)
    return (jnp.dot(jnp.concatenate([hi, lo], axis=1), w_twice, preferred_element_type=F32)
            + jnp.dot(hi, w_lo, preferred_element_type=F32))


def _chunk_cumsum(x):
    c = x.shape[0]
    hi = x.astype(BF16)
    rest = x - hi.astype(F32)
    mid = rest.astype(BF16)
    lo = (rest - mid.astype(F32)).astype(BF16)
    col = _iota((c, 4 * c), 1)
    tri = ((col & (c - 1)) <= _iota((c, 4 * c), 0)) & (col < 3 * c)
    return jnp.dot(jnp.where(tri, 1.0, 0.0).astype(BF16), jnp.concatenate([hi, mid, lo, lo], axis=0),
                   preferred_element_type=F32)


def _sigmoid(x):
    return 1.0 / (1.0 + jnp.exp(-x))


def _softplus(x):
    return jnp.maximum(x, 0.0) + jnp.log(1.0 + jnp.exp(-jnp.abs(x)))


def _iota(shape, dim):
    return lax.broadcasted_iota(jnp.int32, shape, dim)


def _params(*sem):
    return pltpu.CompilerParams(dimension_semantics=sem, vmem_limit_bytes=VMEM_LIMIT)


def _layer_norm(y, g, b):
    mu = jnp.mean(y, axis=-1, keepdims=True)
    d = y - mu
    var = jnp.mean(d * d, axis=-1, keepdims=True)
    return d * lax.rsqrt(var + LN_EPS) * g + b


def _ada_body(c_ref, w_ref, b_ref, o_ref):
    c = c_ref[...]
    o_ref[0] = _mmh(c * _sigmoid(c), w_ref[0]) + b_ref[0]


def _ada(c, w_ada, b_ada):
    depth, d, _ = w_ada.shape
    bsz = c.shape[0]
    return pl.pallas_call(
        _ada_body,
        grid=(depth, 6),
        in_specs=[pl.BlockSpec((bsz, d), lambda l, j: (0, 0)),
                  pl.BlockSpec((1, d, d), lambda l, j: (l, 0, j)),
                  pl.BlockSpec((1, 1, d), lambda l, j: (l, 0, j))],
        out_specs=pl.BlockSpec((1, bsz, d), lambda l, j: (l, 0, j)),
        out_shape=jax.ShapeDtypeStruct((depth, bsz, 6 * d), F32),
        compiler_params=_params("arbitrary", "arbitrary"),
        name="ada",
    )(c, w_ada, b_ada.reshape(depth, 1, 6 * d))


def _proj_body(x_ref, mod_ref, win_ref, wg_ref, bg_ref, y_ref, g_ref, vt_ref):
    u = (x_ref[...] * (1.0 + mod_ref[0, 1:2, :]) + mod_ref[0, 0:1, :]).astype(BF16)
    n_kb, kb = vt_ref.shape[1], vt_ref.shape[4]
    v_col = DA_COL + 2 * DA_W
    for c0 in range(0, IN_W, PROJ_TN):
        cols = slice(c0, c0 + PROJ_TN)
        res = jnp.dot(u, win_ref[:, cols], preferred_element_type=F32)
        y_ref[:, cols] = res.astype(BF16)
        if c0 == v_col:
            for h in range(DA_HEADS):
                slope2 = 2.0 ** (-8.0 * (h + 1) / DA_HEADS) * LOG2E
                key_w = jnp.exp2(slope2 * (_iota((1, res.shape[0]), 1) & (kb - 1)).astype(F32))
                v_t = (res[:, h * DA_DV:(h + 1) * DA_DV].T * key_w).astype(BF16)
                tail = jnp.where(_iota((8, res.shape[0]), 0) == 0, key_w, 0.0).astype(BF16)
                for kbi in range(n_kb):
                    keys = slice(kbi * kb, (kbi + 1) * kb)
                    vt_ref[0, kbi, h, 0:DA_DV, :] = v_t[:, keys]
                    vt_ref[0, kbi, h, DA_DV:DA_DV + 8, :] = tail[:, keys]
    for c0 in range(0, 3 * D_MODEL, PROJ_TN):
        cols = slice(c0, c0 + PROJ_TN)
        g = jnp.dot(u, wg_ref[:, cols], preferred_element_type=F32) + bg_ref[:, cols]
        g_ref[:, cols] = _sigmoid(g).astype(BF16)


def _proj(x, mod, w_in, w_gates, b_gates, seq, tm, kb):
    n, d = x.shape
    per_seq = seq // tm
    tile = lambda i: (i, 0)
    resident = dict(index_map=lambda i: (0, 0), pipeline_mode=pl.Buffered(1))
    vt_shape = (n // seq, seq // kb, DA_HEADS, DA_DV + 8, kb)
    return pl.pallas_call(
        _proj_body,
        grid=(n // tm,),
        in_specs=[pl.BlockSpec((tm, d), tile),
                  pl.BlockSpec((1, 6, d), lambda i: (i // per_seq, 0, 0)),
                  pl.BlockSpec(w_in.shape, **resident),
                  pl.BlockSpec(w_gates.shape, **resident),
                  pl.BlockSpec(b_gates.shape, **resident)],
        out_specs=[pl.BlockSpec((tm, IN_W), tile), pl.BlockSpec((tm, 3 * d), tile),
                   pl.BlockSpec((1, tm // kb) + vt_shape[2:], lambda i: (i // per_seq, i % per_seq, 0, 0, 0))],
        out_shape=[jax.ShapeDtypeStruct((n, IN_W), BF16), jax.ShapeDtypeStruct((n, 3 * d), BF16),
                   jax.ShapeDtypeStruct(vt_shape, BF16)],
        compiler_params=_params("arbitrary"),
        name="proj",
    )(x, mod, w_in, w_gates, b_gates)


def _hgrn2_body(y_ref, lb_ref, ng_ref, o_ref, st_ref):
    c, sub = HG_CHUNK, HG_SUB
    n_seq = y_ref.shape[0]

    @pl.when(pl.program_id(1) == 0)
    def _():
        st_ref[...] = jnp.zeros_like(st_ref)

    def part(k):
        return jnp.concatenate([y_ref[si, :, k * HG_W:(k + 1) * HG_W] for si in range(n_seq)], axis=0)

    q = part(0).astype(F32)
    z = part(1).astype(F32)
    lb = lb_ref[...]
    f = lb + (1.0 - lb) * _sigmoid(z)
    kin = (1.0 - lb) * _sigmoid(-z)
    logf = jnp.log(jnp.maximum(f, HG_F_MIN))
    b = jnp.concatenate([_chunk_cumsum(logf[si * c:(si + 1) * c]) for si in range(n_seq)], axis=0)

    ones = jnp.ones((HG_DK, LANES), BF16)
    row_s = _iota((sub, c), 0)
    col_s = _iota((sub, c), 1)
    blks = range(c // sub)
    hs = [slice(h * HG_DK, (h + 1) * HG_DK) for h in range(HG_HEADS)]
    units = [(si, h) for si in range(n_seq) for h in range(HG_HEADS)]

    betas = {(si, blk): (b[si * c + blk * sub - 1:si * c + blk * sub] if blk else jnp.zeros((1, HG_W), F32))
             for si in range(n_seq) for blk in blks}
    span = functools.reduce(jnp.maximum, [beta - b[si * c + (blk + 1) * sub - 1:si * c + (blk + 1) * sub]
                                          for (si, blk), beta in betas.items()])
    safe = jnp.max(span) <= HG_SAFE_SPAN

    def block_products(cap):
        prods = {}
        for si, h in units:
            keys = slice(si * c, (si + 1) * c)
            for blk in blks:
                rows = slice(si * c + blk * sub, si * c + (blk + 1) * sub)
                beta = betas[si, blk][:, hs[h]]
                q_t = q[rows, hs[h]] * jnp.exp(b[rows, hs[h]] - beta)
                k_h = kin[keys, hs[h]] * jnp.exp(jnp.minimum(beta - b[keys, hs[h]], cap))
                prods[si, h, blk] = _mm_nt(q_t, k_h)
        return prods

    def finish(prods, diag_sums):
        scores = []
        for si, h in units:
            a_rows = []
            for blk in blks:
                r0 = blk * sub
                a_blk = prods[si, h, blk]
                if diag_sums is not None:
                    own = jnp.zeros((sub, c), F32)
                    for s in range(sub):
                        own = jnp.where(col_s == r0 + s, diag_sums(si, h, blk, s), own)
                    a_blk = jnp.where(col_s < r0, a_blk, own)
                a_rows.append(jnp.where(col_s <= row_s + r0, a_blk, 0.0))
            scores.append(jnp.concatenate(a_rows, axis=0).astype(BF16))
        v = part(2)
        og = part(3).astype(F32)
        b_last = [b[(si + 1) * c - 1:(si + 1) * c] for si in range(n_seq)]
        q_dec = (q * jnp.exp(b)).astype(BF16)
        k_tail = (kin * jnp.exp(jnp.concatenate([jnp.broadcast_to(x, (c, HG_W)) for x in b_last], axis=0) - b)
                  ).astype(BF16)
        st = [st_ref[si, h] for si, h in units]
        rows = [slice(si * c, (si + 1) * c) for si, _ in units]
        intra = [jnp.dot(sc, v[r, hs[h]], preferred_element_type=F32) for sc, r, (_, h) in zip(scores, rows, units)]
        inter = [lax.dot_general(q_dec[r, hs[h]], s.astype(BF16), _NT, preferred_element_type=F32)
                 for s, r, (_, h) in zip(st, rows, units)]
        for s, r, (si, h) in zip(st, rows, units):
            st_ref[si, h] = s * jnp.exp(b_last[si][:, hs[h]]) + lax.dot_general(
                v[r, hs[h]], k_tail[r, hs[h]], _TN, preferred_element_type=F32)
        for x, y, r, (si, h) in zip(intra, inter, rows, units):
            o = x + y
            o = o * lax.rsqrt(jnp.mean(o * o, axis=-1, keepdims=True) + RMS_EPS) * ng_ref[...]
            o_ref[si, :, hs[h]] = (o * (og[r, hs[h]] * _sigmoid(og[r, hs[h]]))).astype(o_ref.dtype)

    @pl.when(safe)
    def _():
        finish(block_products(HG_SAFE_SPAN), None)

    @pl.when(jnp.logical_not(safe))
    def _():
        log_k = jnp.log(kin)
        b2, rel2, lk2 = b * LOG2E, (log_k - b) * LOG2E, log_k * LOG2E
        half = sub // 2
        diag = {}
        for si, h in units:
            for blk in blks:
                rows = slice(si * c + blk * sub, si * c + (blk + 1) * sub)
                b_i, q_i, rel_i, lk_i = b2[rows, hs[h]], q[rows, hs[h]], rel2[rows, hs[h]], lk2[rows, hs[h]]
                terms = [q_i[t0:] * jnp.exp2(jnp.minimum(b_i[t0:] + rel_i[s:s + 1], lk_i[s:s + 1]))
                         for s in range(sub) for t0 in [0 if s < half else half]]
                w = jnp.concatenate(terms, axis=0).astype(BF16)
                diag[si, h, blk] = jnp.dot(w, ones, preferred_element_type=F32)

        def diag_sums(si, h, blk, s):
            sums = diag[si, h, blk]
            if s < half:
                return sums[s * sub:(s + 1) * sub, :c]
            start = half * sub + (s - half) * half
            return jnp.concatenate([jnp.zeros((half, c), F32), sums[start:start + half, :c]], axis=0)

        finish(block_products(0.0), diag_sums)


def _hgrn2(y, lb, norm_g, bsz, seq):
    n = y.shape[0]
    width = 4 * HG_W
    n_seq = HG_STEP_SEQS if bsz % HG_STEP_SEQS == 0 else 1
    out = pl.pallas_call(
        _hgrn2_body,
        grid=(bsz // n_seq, seq // HG_CHUNK),
        in_specs=[pl.BlockSpec((n_seq, HG_CHUNK, width), lambda b, c: (b, c, HG_COL // width)),
                  pl.BlockSpec((1, HG_W), lambda b, c: (0, 0)),
                  pl.BlockSpec((1, HG_DV), lambda b, c: (0, 0))],
        out_specs=pl.BlockSpec((n_seq, HG_CHUNK, HG_W), lambda b, c: (b, c, 0)),
        out_shape=jax.ShapeDtypeStruct((bsz, seq, HG_W), BF16),
        scratch_shapes=[pltpu.VMEM((n_seq, HG_HEADS, HG_DV, HG_DK), F32)],
        compiler_params=_params("arbitrary", "arbitrary"),
        name="hgrn2",
    )(y.reshape(bsz, seq, -1), lb, norm_g)
    return out.reshape(n, HG_W)


def _attn_body(scal_ref, q_ref, k_ref, vt_ref, g_ref, o_ref, qq_ref, sa_ref, sb_ref, p_ref, m_ref, sc_ref,
               acc_ref, *, kb):
    h = pl.program_id(1)
    i = pl.program_id(2)
    lam = scal_ref[0]
    out_scale = scal_ref[1]
    slope = scal_ref[2 + h] * LOG2E
    qb = 2 * kb
    q0 = i * qb
    tiles_per_map = qb // LANES
    a_tiles = kb // LANES

    q = q_ref[...].astype(F32) * (DA_DQK ** -0.5 * LOG2E)
    lane = _iota(q.shape, 1)
    stacked = jnp.concatenate([jnp.where(lane < DA_DQK, q, 0.0), jnp.where(lane >= DA_DQK, q, 0.0)], axis=0)
    qq_ref[...] = stacked.T.astype(BF16)
    m_ref[...] = jnp.full(m_ref.shape, MASK_VALUE, F32)
    sc_ref[...] = jnp.ones(sc_ref.shape, F32)
    acc_ref[...] = jnp.zeros(acc_ref.shape, F32)
    p_ref[...] = jnp.zeros(p_ref.shape, BF16)
    key_off = _iota((kb, LANES), 0)

    def scores(j):
        return jnp.dot(k_ref[pl.ds(j * kb, kb), :], qq_ref[...], preferred_element_type=F32)

    def softmax(j, src_ref, diagonal_of):
        block_bias = slope * (j * kb - q0).astype(F32)
        for t in range(2 * tiles_per_map):
            cols = slice(t * LANES, (t + 1) * LANES)
            in_map = t % tiles_per_map
            half = "A" if in_map < a_tiles else "B"
            if diagonal_of == "B" and half == "A":
                p_ref[t] = jnp.zeros((kb, LANES), BF16)
                sc_ref[:, cols] = jnp.ones((1, LANES), F32)
                continue
            s = src_ref[t]
            if diagonal_of == half:
                q_off = _iota((kb, LANES), 1) + (in_map % a_tiles) * LANES
                s = jnp.where(key_off <= q_off, s, MASK_VALUE)
            m_old = m_ref[:, cols]
            m_new = jnp.maximum(m_old, jnp.max(s, axis=0, keepdims=True) + block_bias)
            p_ref[t] = jnp.exp2(s - (m_new - block_bias)).astype(BF16)
            sc_ref[:, cols] = jnp.exp2(m_old - m_new)
            m_ref[:, cols] = m_new

    n_tiles = 2 * tiles_per_map

    def park(dst_ref, s):
        for t in range(n_tiles):
            dst_ref[t] = s[:, t * LANES:(t + 1) * LANES]

    def probabilities():
        return jnp.concatenate([p_ref[t] for t in range(n_tiles)], axis=1)

    def iteration(j, src_ref, dst_ref, diagonal_of=None):
        sc_prev = sc_ref[...]
        pv = jnp.dot(vt_ref[0, jnp.maximum(j - 1, 0), 0], probabilities(), preferred_element_type=F32)
        if dst_ref is not None:
            park(dst_ref, scores(j + 1))
        for t in range(n_tiles):
            cols = slice(t * LANES, (t + 1) * LANES)
            acc_ref[t] = acc_ref[t] * sc_prev[:, cols] + pv[:, cols]
        softmax(j, src_ref, diagonal_of)

    def pair(j0):
        iteration(j0, sa_ref, sb_ref)
        iteration(j0 + 1, sb_ref, sa_ref)

    def body(quad, carry):
        pair(4 * quad)
        pair(4 * quad + 2)
        return carry

    park(sa_ref, scores(0))
    lax.fori_loop(0, i // 2, body, 0)

    @pl.when(i % 2 == 1)
    def _():
        pair(2 * i - 2)

    iteration(2 * i, sa_ref, sb_ref, "A")
    iteration(2 * i + 1, sb_ref, None, "B")
    acc = (jnp.concatenate([acc_ref[t] for t in range(n_tiles)], axis=1) * sc_ref[...]
           + jnp.dot(vt_ref[0, 2 * i + 1, 0], probabilities(), preferred_element_type=F32))
    o = acc[:DA_DV] / acc[DA_DV:DA_DV + 1]
    d = o[:, :qb] - lam * o[:, qb:]
    g = jnp.concatenate([g_ref[...]] * tiles_per_map, axis=1)
    d = d * lax.rsqrt(jnp.mean(d * d, axis=0, keepdims=True) + RMS_EPS) * g * out_scale
    o_ref[...] = d.T.astype(o_ref.dtype)


def _attn(y, v_t, scal, subln_g, bsz, seq, kb):
    n = y.shape[0]
    qb = 2 * kb
    nq, nk = seq // qb, seq // kb
    qc, kc = DA_COL // DA_DV, (DA_COL + DA_W) // DA_DV
    rows_v = v_t.shape[3]
    g_col = jnp.broadcast_to(subln_g.reshape(DA_DV, 1), (DA_DV, LANES))
    row = (1, 2 * qb)
    return pl.pallas_call(
        functools.partial(_attn_body, kb=kb),
        scratch_shapes=[pltpu.VMEM((DA_DV, 2 * qb), BF16), pltpu.VMEM((2 * qb // LANES, kb, LANES), F32),
                        pltpu.VMEM((2 * qb // LANES, kb, LANES), F32),
                        pltpu.VMEM((2 * qb // LANES, kb, LANES), BF16), pltpu.VMEM(row, F32),
                        pltpu.VMEM(row, F32), pltpu.VMEM((2 * qb // LANES, rows_v, LANES), F32)],
        grid=(bsz, DA_HEADS, nq),
        in_specs=[pl.BlockSpec(memory_space=pltpu.SMEM),
                  pl.BlockSpec((qb, DA_DV), lambda b, h, i: (b * nq + i, qc + h)),
                  pl.BlockSpec((seq, DA_DV), lambda b, h, i: (b, kc + h)),
                  pl.BlockSpec((1, nk, 1, rows_v, kb), lambda b, h, i: (b, 0, h, 0, 0)),
                  pl.BlockSpec((DA_DV, LANES), lambda b, h, i: (0, 0))],
        out_specs=pl.BlockSpec((qb, DA_DV), lambda b, h, i: (b * nq + i, h)),
        out_shape=jax.ShapeDtypeStruct((n, DA_W), BF16),
        compiler_params=_params("arbitrary", "arbitrary", "arbitrary"),
        name="diffattn",
    )(scal, y, y, v_t, g_col)


def _split_f32(x):
    hi = x.astype(BF16)
    hi_f = hi.astype(F32)
    return hi, hi_f, x - hi_f


def _dup_lhs(hi_f, lo_f, low_half):
    packed = jnp.where(low_half, hi_f, lo_f).astype(BF16)
    return jnp.concatenate([packed, packed], axis=1)


def _dup_rhs(hi, lo_f):
    lo = lo_f.astype(BF16)
    return jnp.concatenate([hi, hi, lo, lo], axis=0)


def _rwkv_body(y_ref, mu_ref, w0_ref, a0_ref, wa2_ref, walo_ref, gu2_ref, gulo_ref, kk_ref, ka_ref, rk_ref,
               gng_ref, gnb_ref, seg_ref, o_ref, st_ref, prev_ref, osc_ref):
    c, sub, dh = RW_CHUNK, RW_SUB, RW_DH

    @pl.when(pl.program_id(1) == 0)
    def _():
        st_ref[...] = jnp.zeros_like(st_ref)
        prev_ref[...] = jnp.zeros_like(prev_ref)

    n_seq, seq_rows = y_ref.shape[0], y_ref.shape[1]
    rows = n_seq * seq_rows
    shifted = []
    for si in range(n_seq):
        x = y_ref[si].astype(F32)
        x_prev = jnp.where(_iota(x.shape, 0) == 0, prev_ref[si:si + 1], pltpu.roll(x, 1, axis=0))
        prev_ref[si:si + 1] = x[seq_rows - 1:seq_rows]
        shifted.append(x + (x_prev - x) * mu_ref[...])
    xs = jnp.concatenate(shifted, axis=0)
    r = xs[:, 0:RW_W]
    k = xs[:, RW_W:2 * RW_W]
    v = xs[:, 2 * RW_W:3 * RW_W]
    wa = xs[:, 3 * RW_W:3 * RW_W + LANES]
    gd = xs[:, 3 * RW_W + LANES:RW_IN_W]

    lora = _split_mm(jnp.where(_iota(wa.shape, 1) < 64, jnp.tanh(wa), wa), wa2_ref[...], walo_ref[...])
    w_log = -_softplus(-(w0_ref[...] + lora[:, :RW_W])) - 0.5
    g = -jnp.exp(w_log)
    a = _sigmoid(a0_ref[...] + lora[:, RW_W:])
    gate = _split_mm(_sigmoid(gd), gu2_ref[...], gulo_ref[...])
    seg = seg_ref[...]
    kk = k * kk_ref[...]
    k2 = k * (1.0 + (a - 1.0) * ka_ref[...])
    sums = _seg_sum(jnp.concatenate([kk * kk, r * k2 * rk_ref[...]], axis=0), seg)
    kk = kk * lax.rsqrt(jnp.maximum(sums[:rows], 1e-12))
    bb = kk * a
    bonus = sums[rows:] * v

    chunks = range(rows // c)
    cr = [slice(ci * c, (ci + 1) * c) for ci in chunks]
    gc = jnp.concatenate([_chunk_cumsum(g[s]) for s in cr], axis=0)
    g_last = [gc[s][c - 1:c] for s in cr]
    e_inv = jnp.exp(-gc)
    e_tail = jnp.exp(jnp.concatenate([jnp.broadcast_to(gl, (c, RW_W)) for gl in g_last], axis=0) - gc)
    gam = [jnp.exp(gl) for gl in g_last]
    a_t = (-kk * jnp.exp(gc - g)).astype(BF16)
    r_t = (r * jnp.exp(gc)).astype(BF16)
    b_h = (bb * e_inv).astype(BF16)
    k_h = (k2 * e_inv).astype(BF16)
    k_bar = (k2 * e_tail).astype(BF16)
    b_bar = (bb * e_tail).astype(BF16)
    v_bf = v.astype(BF16)

    row2 = _iota((c, 2 * c), 0)
    lane2 = _iota((c, 2 * c), 1)
    col2 = lane2 & (c - 1)
    low_half = lane2 < c
    strict = row2 > col2
    incl = row2 >= col2
    same_blk = (row2 // sub) == (col2 // sub)
    eye = (row2 == col2).astype(F32)

    def dot(p, q):
        return jnp.dot(p, q, preferred_element_type=F32)

    heads = range(RW_HEADS)
    sls = [slice(h * dh, (h + 1) * dh) for h in heads]
    items = [(ci, h) for ci in chunks for h in heads]
    ar_h = [jnp.concatenate([a_t[cr[ci], sls[h]], r_t[cr[ci], sls[h]]], axis=0) for ci, h in items]
    quad = [lax.dot_general(x, jnp.concatenate([b_h[cr[ci], sls[h]], k_h[cr[ci], sls[h]]], axis=0), _NT,
                            preferred_element_type=F32)
            for x, (ci, h) in zip(ar_h, items)]
    top = [jnp.where(strict, q[:c], 0.0) for q in quad]
    a_ak = [t[:, c:].astype(BF16) for t in top]
    a_r = [jnp.where(incl, q[c:], 0.0).astype(BF16) for q in quad]
    a_ab = [jnp.where(low_half, t, pltpu.roll(t, c, axis=1)) for t in top]
    a_d = [jnp.where(same_blk, x, 0.0) for x in a_ab]
    a_o = [(x - y).astype(BF16) for x, y in zip(a_ab, a_d)]

    s1 = [_split_f32(x) for x in a_d]
    p2 = [dot(_dup_lhs(hf, lf, low_half), _dup_rhs(hi, lf)) for hi, hf, lf in s1]
    s2 = [_split_f32(x) for x in p2]
    rhs2 = [_dup_rhs(hi, lf) for hi, _, lf in s2]
    p4 = [dot(_dup_lhs(hf, lf, low_half), rhs) for (_, hf, lf), rhs in zip(s2, rhs2)]
    s4 = [_split_f32(x) for x in p4]
    rhs4 = [_dup_rhs(hi, lf) for hi, _, lf in s4]
    p8 = [dot(_dup_lhs(hf, lf, low_half), rhs) for (_, hf, lf), rhs in zip(s4, rhs4)]
    rhs8 = [_dup_rhs(hi, lf) for hi, _, lf in (_split_f32(x) for x in p8)]
    t_d = [eye + x for x in a_d]
    for rhs_all in (rhs2, rhs4, rhs8):
        st = [_split_f32(x) for x in t_d]
        t_d = [x + dot(_dup_lhs(hf, lf, low_half), rhs) for x, (_, hf, lf), rhs in zip(t_d, st, rhs_all)]
    t_d = [x.astype(BF16) for x in t_d]

    nn = [dot(t[:, :c], x) for t, x in zip(t_d, a_o)]
    nn_bf = [x.astype(BF16) for x in nn]
    n2 = [dot(x[:, :c], x) for x in nn_bf]
    n3 = [dot(x[:, :c], y.astype(BF16)) for x, y in zip(nn_bf, n2)]
    t_m = [dot((eye + x + y + z).astype(BF16)[:, :c], t).astype(BF16)[:, :c]
           for x, y, z, t in zip(nn, n2, n3, t_d)]

    v_h = [v_bf[cr[ci], sls[h]] for ci, h in items]
    akv = [dot(x, y).astype(BF16) for x, y in zip(a_ak, v_h)]
    at_m = [dot(t, x[:c]).astype(BF16) for t, x in zip(t_m, ar_h)]
    v_p = [dot(t, x) for t, x in zip(t_m, akv)]

    per_seq = seq_rows // c
    chains = [(si, h) for si in range(n_seq) for h in heads]
    state = [st_ref[si, h] for si, h in chains]
    for t in range(per_seq):
        it = [(si * per_seq + t) * RW_HEADS + h for si, h in chains]
        ck = [cr[si * per_seq + t] for si, _ in chains]
        proj = [lax.dot_general(jnp.concatenate([at_m[j], ar_h[j][c:]], axis=0), s.astype(BF16), _NT,
                                preferred_element_type=F32) for j, s in zip(it, state)]
        u = [(p[:c] + v_p[j]).astype(BF16) for p, j in zip(proj, it)]
        for p, uu, j, rws, (_, h) in zip(proj, u, it, ck, chains):
            osc_ref[rws, sls[h]] = p[c:] + dot(a_r[j], jnp.concatenate([uu, v_h[j]], axis=0))
        state = [s * gam[j // RW_HEADS][:, sls[h]] + lax.dot_general(
            jnp.concatenate([v_h[j], uu], axis=0),
            jnp.concatenate([k_bar[rws, sls[h]], b_bar[rws, sls[h]]], axis=0), _TN, preferred_element_type=F32)
            for s, uu, j, rws, (_, h) in zip(state, u, it, ck, chains)]
    for s, (si, h) in zip(state, chains):
        st_ref[si, h] = s

    o = osc_ref[...]
    mean = _seg_sum(o, seg) * (1.0 / dh)
    d = o - mean
    var = _seg_sum(d * d, seg) * (1.0 / dh)
    o = d * lax.rsqrt(var + RW_GN_EPS) * gng_ref[...] + gnb_ref[...]
    o = ((o + bonus) * gate).astype(o_ref.dtype)
    for si in range(n_seq):
        o_ref[si] = o[si * seq_rows:(si + 1) * seq_rows]


def _rwkv(y, p, bsz, seq):
    n = y.shape[0]
    half = RW_W // 2
    seg = (_iota((half, half), 0) // RW_DH == _iota((half, half), 1) // RW_DH).astype(BF16)

    def two_terms(w):
        hi = w.astype(BF16)
        return jnp.concatenate([hi, hi], axis=0), (w - hi.astype(F32)).astype(BF16)

    zeros = jnp.zeros_like(p["w_up"])
    wa2, wa_lo = two_terms(jnp.concatenate([jnp.concatenate([p["w_up"], zeros], axis=1),
                                            jnp.concatenate([zeros, p["a_up"]], axis=1)], axis=0))
    gu2, gu_lo = two_terms(p["g_up"])
    rows = [p["mu"], p["w0"], p["a0"], wa2, wa_lo, gu2, gu_lo, p["k_k"], p["k_a"], p["r_k"],
            p["gn_g"], p["gn_b"], seg]
    full = lambda b, c: (0, 0)
    step = RW_STEP_CHUNKS * RW_CHUNK
    n_seq = RW_STEP_SEQS if bsz % RW_STEP_SEQS == 0 else 1
    out = pl.pallas_call(
        _rwkv_body,
        grid=(bsz // n_seq, seq // step),
        in_specs=[pl.BlockSpec((n_seq, step, RW_IN_W), lambda b, c: (b, c, RW_COL // RW_IN_W))]
        + [pl.BlockSpec(a.shape, full) for a in rows],
        out_specs=pl.BlockSpec((n_seq, step, RW_W), lambda b, c: (b, c, 0)),
        out_shape=jax.ShapeDtypeStruct((bsz, seq, RW_W), BF16),
        scratch_shapes=[pltpu.VMEM((n_seq, RW_HEADS, RW_DH, RW_DH), F32),
                        pltpu.VMEM((n_seq, RW_IN_W), F32),
                        pltpu.VMEM((n_seq * step, RW_W), F32)],
        compiler_params=_params("arbitrary", "arbitrary"),
        name="rwkv7",
    )(y.reshape(bsz, seq, -1), *rows)
    return out.reshape(n, RW_W)


def _first_argmax(vals, row):
    top = jnp.max(vals, axis=0, keepdims=True)
    idx = jnp.min(jnp.where(vals == top, row, N_EXPERTS), axis=0, keepdims=True)
    return top, idx


def _merge_body(ohg_ref, oda_ref, orw_ref, gt_ref, x_ref, mod_ref, wb_ref, wo_ref, lng_ref, lnb_ref,
                wrt_ref, rb_ref, tri_ref, tri16_ref, x1_ref, u2_ref, route_ref, cnt_ref):
    d = D_MODEL
    sub_rows = tri_ref.shape[0]

    def route(rows):
        merged = (gt_ref[rows, 0:d].astype(F32)
                  * jnp.dot(ohg_ref[rows, :], wb_ref[0:HG_W, :], preferred_element_type=F32)
                  + gt_ref[rows, d:2 * d].astype(F32)
                  * jnp.dot(oda_ref[rows, :], wb_ref[HG_W:HG_W + DA_W, :], preferred_element_type=F32)
                  + gt_ref[rows, 2 * d:3 * d].astype(F32)
                  * jnp.dot(orw_ref[rows, :], wb_ref[HG_W + DA_W:, :], preferred_element_type=F32))
        mix = _mm(merged, wo_ref[...])
        x1 = _layer_norm(ALPHA * x_ref[rows, :] + (1.0 + mod_ref[0, 2:3, :]) * mix, lng_ref[...], lnb_ref[...])
        x1_ref[rows, :] = x1
        u2 = x1 * (1.0 + mod_ref[0, 4:5, :]) + mod_ref[0, 3:4, :]
        u2_ref[rows, :] = u2.astype(BF16)

        logits = _mmh_nt(wrt_ref[...], u2)
        ex = jnp.exp(logits - jnp.max(logits, axis=0, keepdims=True))
        scores = ex / jnp.sum(ex, axis=0, keepdims=True)
        sel = scores + rb_ref[...]
        row = _iota(sel.shape, 0)
        best = None
        for grp in range(N_GROUPS):
            a, b, c2, d2 = (sel[grp * EXPERTS_PER_GROUP + i:grp * EXPERTS_PER_GROUP + i + 1] for i in range(4))
            hi1, lo1, hi2, lo2 = jnp.maximum(a, b), jnp.minimum(a, b), jnp.maximum(c2, d2), jnp.minimum(c2, d2)
            top2 = jnp.maximum(hi1, hi2) + jnp.maximum(jnp.minimum(hi1, hi2), jnp.maximum(lo1, lo2))
            if best is None:
                best, best_grp = top2, jnp.zeros_like(top2, dtype=jnp.int32)
            else:
                better = top2 > best
                best = jnp.where(better, top2, best)
                best_grp = jnp.where(better, grp, best_grp)
        masked = jnp.where(row // EXPERTS_PER_GROUP == best_grp, sel, MASK_VALUE)
        _, idx1 = _first_argmax(masked, row)
        _, idx2 = _first_argmax(jnp.where(row == idx1, -jnp.inf, masked), row)
        pick1 = row == idx1
        pick2 = row == idx2
        w1 = jnp.sum(jnp.where(pick1, scores, 0.0), axis=0, keepdims=True)
        w2 = jnp.sum(jnp.where(pick2, scores, 0.0), axis=0, keepdims=True)
        onehot = jnp.where(pick1 | pick2, 1.0, 0.0)
        earlier = jnp.dot(onehot.astype(BF16), tri_ref[...], preferred_element_type=F32)
        return pick1, pick2, w1 / (w1 + w2), w2 / (w1 + w2), earlier, jnp.sum(onehot, axis=1, keepdims=True)

    parts = [route(slice(r0, r0 + sub_rows)) for r0 in range(0, x_ref.shape[0], sub_rows)]

    cnt = functools.reduce(jnp.add, [p[5] for p in parts])
    chunks = jnp.floor((cnt + (MOE_CHUNK - 1)) * (1.0 / MOE_CHUNK))
    seg_start = MOE_CHUNK * jnp.dot(tri16_ref[...], jnp.broadcast_to(chunks, (N_EXPERTS, LANES)).astype(BF16),
                                    preferred_element_type=F32)[:, 0:1]
    before = jnp.zeros_like(cnt)
    for k, (pick1, pick2, wn1, wn2, earlier, cnt_k) in enumerate(parts):
        pos = seg_start + before + earlier
        pos1 = jnp.sum(jnp.where(pick1, pos, 0.0), axis=0, keepdims=True)
        pos2 = jnp.sum(jnp.where(pick2, pos, 0.0), axis=0, keepdims=True)
        route_ref[:, k * sub_rows:(k + 1) * sub_rows] = jnp.concatenate(
            [pos1, pos2, wn1, wn2, jnp.zeros((4, sub_rows), F32)], axis=0)
        before = before + cnt_k
    cnt_ref[0] = jnp.broadcast_to(cnt, (N_EXPERTS, LANES))


def _merge(o_hg, o_da, o_rw, gates, x, mod, w_branch, w_out, ln_g, ln_b, w_router_t, router_bias, seq, tm):
    n, d = x.shape
    per_seq = seq // tm
    tile = lambda i: (i, 0)
    full = lambda i: (0, 0)
    sub = tm // MERGE_SUBTILES
    before = (_iota((sub, sub), 0) < _iota((sub, sub), 1)).astype(BF16)
    before16 = (_iota((N_EXPERTS, N_EXPERTS), 1) < _iota((N_EXPERTS, N_EXPERTS), 0)).astype(BF16)
    return pl.pallas_call(
        _merge_body,
        grid=(n // tm,),
        in_specs=[pl.BlockSpec((tm, HG_W), tile), pl.BlockSpec((tm, DA_W), tile), pl.BlockSpec((tm, RW_W), tile),
                  pl.BlockSpec((tm, 3 * d), tile), pl.BlockSpec((tm, d), tile),
                  pl.BlockSpec((1, 6, d), lambda i: (i // per_seq, 0, 0)),
                  pl.BlockSpec(w_branch.shape, full), pl.BlockSpec(w_out.shape, full),
                  pl.BlockSpec((1, d), full), pl.BlockSpec((1, d), full),
                  pl.BlockSpec((N_EXPERTS, d), full), pl.BlockSpec((N_EXPERTS, 1), full),
                  pl.BlockSpec((sub, sub), full), pl.BlockSpec((N_EXPERTS, N_EXPERTS), full)],
        out_specs=[pl.BlockSpec((tm, d), tile), pl.BlockSpec((tm, d), tile), pl.BlockSpec((8, tm), lambda i: (0, i)),
                   pl.BlockSpec((1, N_EXPERTS, LANES), lambda i: (i, 0, 0))],
        out_shape=[jax.ShapeDtypeStruct((n, d), F32), jax.ShapeDtypeStruct((n, d), BF16),
                   jax.ShapeDtypeStruct((8, n), F32), jax.ShapeDtypeStruct((n // tm, N_EXPERTS, LANES), F32)],
        compiler_params=_params("arbitrary"),
        name="merge",
    )(o_hg, o_da, o_rw, gates, x, mod, w_branch, w_out, ln_g, ln_b, w_router_t, router_bias, before, before16)


def _local_rows(tm):
    return -(-(2 * tm + N_EXPERTS * (MOE_CHUNK - 1)) // LANES) * LANES


def _token_columns(route):
    return jnp.concatenate([route, jnp.zeros((LANES - route.shape[0], route.shape[1]), F32)], axis=0).T


def _segment_copies(i, nch_ref, loc_ref, glob_ref, local_buf, global_buf, sem, to_global):
    def run(action):
        for e in range(N_EXPERTS):
            seg = i * N_EXPERTS + e
            loc0, glob0 = loc_ref[seg], glob_ref[seg]

            def one(c, carry):
                loc = local_buf.at[pl.ds(pl.multiple_of(loc0 + c * MOE_CHUNK, MOE_CHUNK), MOE_CHUNK), :]
                glob = global_buf.at[pl.ds(pl.multiple_of(glob0 + c * MOE_CHUNK, MOE_CHUNK), MOE_CHUNK), :]
                copy = pltpu.make_async_copy(loc, glob, sem) if to_global else pltpu.make_async_copy(glob, loc, sem)
                getattr(copy, action)()
                return carry

            lax.fori_loop(0, nch_ref[seg], one, 0)
    return run


def _dispatch_body(nch_ref, loc_ref, glob_ref, u_ref, route_ref, xs_in_ref, xs_ref, stage2_ref, sems):
    del xs_in_ref
    i = pl.program_id(0)
    last = pl.num_programs(0) - 1
    tm, d = u_ref.shape
    slot = i % 2
    stage_ref = stage2_ref.at[slot]

    def tile_copies(tile):
        return _segment_copies(tile, nch_ref, loc_ref, glob_ref, stage2_ref.at[tile % 2], xs_ref,
                               sems.at[tile % 2], to_global=True)

    @pl.when(i >= 2)
    def _():
        tile_copies(i - 2)("wait")

    route = route_ref[...]
    local_row = _iota((stage_ref.shape[0], tm), 0)
    take1 = local_row == route[0:1].astype(jnp.int32)
    take2 = local_row == route[1:2].astype(jnp.int32)
    perm = jnp.where(take1 | take2, 1.0, 0.0).astype(BF16)
    stage_ref[:, 0:d] = jnp.dot(perm, u_ref[...], preferred_element_type=F32).astype(BF16)

    cols = _token_columns(route)
    lane = _iota((tm, LANES), 1)

    def weight_cols(w):
        hi = w.astype(BF16).astype(F32)
        return jnp.where(lane == 0, hi, jnp.where(lane == 1, w - hi, 0.0)).astype(BF16)

    stage_ref[:, d:d + LANES] = (
        jnp.dot(jnp.where(take1, 1.0, 0.0).astype(BF16), weight_cols(cols[:, 2:3]), preferred_element_type=F32)
        + jnp.dot(jnp.where(take2, 1.0, 0.0).astype(BF16), weight_cols(cols[:, 3:4]), preferred_element_type=F32)
    ).astype(BF16)

    tile_copies(i)("start")

    @pl.when(i == last)
    def _():

        @pl.when(i >= 1)
        def _():
            tile_copies(i - 1)("wait")

        tile_copies(i)("wait")


def _dispatch(u2, route, nch, loc, glob, rows, tm):
    n, d = u2.shape
    width = d + LANES
    return pl.pallas_call(
        _dispatch_body,
        grid_spec=pltpu.PrefetchScalarGridSpec(
            num_scalar_prefetch=3,
            grid=(n // tm,),
            in_specs=[pl.BlockSpec((tm, d), lambda i, *_: (i, 0)),
                      pl.BlockSpec((8, tm), lambda i, *_: (0, i)),
                      pl.BlockSpec(memory_space=pl.ANY)],
            out_specs=pl.BlockSpec(memory_space=pl.ANY),
            scratch_shapes=[pltpu.VMEM((2, _local_rows(tm), width), BF16), pltpu.SemaphoreType.DMA((2,))],
        ),
        out_shape=jax.ShapeDtypeStruct((rows, width), BF16),
        input_output_aliases={5: 0},
        compiler_params=_params("arbitrary"),
        name="moe_dispatch",
    )(nch, loc, glob, u2, route, jnp.zeros((rows, width), BF16))


def _experts_body(te_ref, x_ref, wgu_ref, wd_ref, y_ref):
    used = te_ref[pl.program_id(0)] < N_EXPERTS

    @pl.when(used)
    def _():
        d = wgu_ref.shape[1]
        weight = x_ref[:, d:d + 1].astype(F32) + x_ref[:, d + 1:d + 2].astype(F32)
        hidden = jnp.dot(x_ref[:, 0:d], wgu_ref[0], preferred_element_type=F32)
        hg = hidden[:, :D_EXPERT]
        act = hg * _sigmoid(hg) * hidden[:, D_EXPERT:] * weight
        y_ref[...] = _mm(act, wd_ref[0]).astype(BF16)

    @pl.when(jnp.logical_not(used))
    def _():
        y_ref[...] = jnp.zeros_like(y_ref)


def _experts(xs, tile_expert, w_gu, w_down):
    rows, width = xs.shape
    d = w_down.shape[2]
    expert = lambda g, te: (jnp.minimum(te[g], N_EXPERTS - 1), 0, 0)
    return pl.pallas_call(
        _experts_body,
        grid_spec=pltpu.PrefetchScalarGridSpec(
            num_scalar_prefetch=1,
            grid=(rows // MOE_TM,),
            in_specs=[pl.BlockSpec((MOE_TM, width), lambda g, te: (g, 0)),
                      pl.BlockSpec((1, d, 2 * D_EXPERT), expert),
                      pl.BlockSpec((1, D_EXPERT, d), expert)],
            out_specs=pl.BlockSpec((MOE_TM, d), lambda g, te: (g, 0)),
        ),
        out_shape=jax.ShapeDtypeStruct((rows, d), BF16),
        compiler_params=_params("arbitrary"),
        name="moe_experts",
    )(tile_expert, xs, w_gu, w_down)


def _combine_body(nch_ref, loc_ref, glob_ref, route_ref, x1_ref, mod_ref, lng_ref, lnb_ref, ys_ref, o_ref,
                  back2_ref, sems):
    i = pl.program_id(0)

    def tile_copies(tile):
        return _segment_copies(tile, nch_ref, loc_ref, glob_ref, back2_ref.at[tile % 2], ys_ref,
                               sems.at[tile % 2], to_global=False)

    @pl.when(i == 0)
    def _():
        back2_ref[...] = jnp.zeros_like(back2_ref)
        tile_copies(i)("start")

    @pl.when(i + 1 < pl.num_programs(0))
    def _():
        tile_copies(i + 1)("start")

    cols = _token_columns(route_ref[...]).astype(jnp.int32)
    local_row = _iota((x1_ref.shape[0], back2_ref.shape[1]), 1)
    unperm = jnp.where((local_row == cols[:, 0:1]) | (local_row == cols[:, 1:2]), 1.0, 0.0).astype(BF16)
    tile_copies(i)("wait")
    ffn = jnp.dot(unperm, back2_ref[i % 2], preferred_element_type=F32)
    y = ALPHA * x1_ref[...] + (1.0 + mod_ref[0, 5:6, :]) * ffn
    o_ref[...] = _layer_norm(y, lng_ref[...], lnb_ref[...])


def _combine(ys, route, nch, loc, glob, x1, mod, ln_g, ln_b, seq, tm):
    n, d = x1.shape
    per_seq = seq // tm
    full = lambda i, *_: (0, 0)
    return pl.pallas_call(
        _combine_body,
        grid_spec=pltpu.PrefetchScalarGridSpec(
            num_scalar_prefetch=3,
            grid=(n // tm,),
            in_specs=[pl.BlockSpec((8, tm), lambda i, *_: (0, i)),
                      pl.BlockSpec((tm, d), lambda i, *_: (i, 0)),
                      pl.BlockSpec((1, 6, d), lambda i, *_: (i // per_seq, 0, 0)),
                      pl.BlockSpec((1, d), full), pl.BlockSpec((1, d), full),
                      pl.BlockSpec(memory_space=pl.ANY)],
            out_specs=pl.BlockSpec((tm, d), lambda i, *_: (i, 0)),
            scratch_shapes=[pltpu.VMEM((2, _local_rows(tm), d), BF16), pltpu.SemaphoreType.DMA((2,))],
        ),
        out_shape=jax.ShapeDtypeStruct((n, d), F32),
        compiler_params=_params("arbitrary"),
        name="moe_combine",
    )(nch, loc, glob, route, x1, mod, ln_g, ln_b, ys)


def _moe(u2, route, counts, w_gu, w_down, x1, mod, ln_g, ln_b, seq, tm):
    n = u2.shape[0]
    n_tiles = n // tm
    seg_rows = (counts + MOE_CHUNK - 1) // MOE_CHUNK * MOE_CHUNK
    loc = jnp.cumsum(seg_rows, axis=1) - seg_rows
    region = (jnp.sum(seg_rows, axis=0) + MOE_TM - 1) // MOE_TM * MOE_TM
    region_end = jnp.cumsum(region)
    glob = (region_end - region)[None, :] + jnp.cumsum(seg_rows, axis=0) - seg_rows
    rows = -(-(2 * n + n_tiles * N_EXPERTS * (MOE_CHUNK - 1) + N_EXPERTS * (MOE_TM - 1)) // MOE_TM) * MOE_TM
    tile_expert = jnp.sum(jnp.arange(rows // MOE_TM, dtype=jnp.int32)[:, None] * MOE_TM >= region_end[None, :],
                          axis=1).astype(jnp.int32)
    flat = lambda a: a.reshape(-1).astype(jnp.int32)
    nch, loc, glob = flat(seg_rows // MOE_CHUNK), flat(loc), flat(glob)
    xs = _dispatch(u2, route, nch, loc, glob, rows, tm)
    ys = _experts(xs, tile_expert, w_gu, w_down)
    return _combine(ys, route, nch, loc, glob, x1, mod, ln_g, ln_b, seq, tm)


def _tiles(seq):
    return min(512, seq), min(256, seq // 2)


def kernel(x, c, w_ada, b_ada, w_in, hg_lb_logits, hg_norm_g, da_lambda, da_subln_g, rw_mu, rw_w0, rw_w_up,
           rw_a0, rw_a_up, rw_g_up, rw_k_k, rw_k_a, rw_r_k, rw_gn_g, rw_gn_b, w_merge, b_merge, w_branch, w_out,
           ln_g, ln_b, w_router, router_bias, w_exp_gate, w_exp_up, w_exp_down):
    bsz, seq, d = x.shape
    depth = w_in.shape[0]
    n = bsz * seq
    tm, blk = _tiles(seq)

    sm = jax.nn.softmax(hg_lb_logits.astype(F32), axis=0)
    hg_lb = jnp.cumsum(sm, axis=0) - sm[0:1]
    slopes = jnp.asarray([2.0 ** (-8.0 * (h + 1) / DA_HEADS) for h in range(DA_HEADS)], F32)

    mod_all = _ada(c, w_ada, b_ada).reshape(depth, bsz, 6, d)
    w_router_t = w_router.T
    router_bias = router_bias.reshape(N_EXPERTS, 1)

    xf = x.reshape(n, d)
    for l in range(depth):
        mod = mod_all[l]
        lq1, lk1, lq2, lk2 = da_lambda[l].astype(F32)
        lam_init = 0.8 - 0.6 * math.exp(-0.3 * l)
        lam = jnp.exp(jnp.sum(lq1 * lk1)) - jnp.exp(jnp.sum(lq2 * lk2)) + lam_init
        scal = jnp.concatenate([jnp.stack([lam, jnp.asarray(1.0 - lam_init, F32)]), slopes])

        w_gates = jnp.concatenate([w_merge[l, br] for br in range(3)], axis=1).astype(BF16)
        y, gates, v_t = _proj(xf, mod, w_in[l].astype(BF16), w_gates, b_merge[l].reshape(1, 3 * d), seq, tm, blk)

        o_hg = _hgrn2(y, hg_lb[l].reshape(1, HG_W), hg_norm_g[l].reshape(1, HG_DV), bsz, seq)
        o_da = _attn(y, v_t, scal, da_subln_g[l].reshape(1, DA_DV), bsz, seq, blk)
        rw = dict(mu=rw_mu[l].reshape(1, -1), w0=rw_w0[l].reshape(1, -1), w_up=rw_w_up[l],
                  a0=rw_a0[l].reshape(1, -1), a_up=rw_a_up[l], g_up=rw_g_up[l],
                  k_k=rw_k_k[l].reshape(1, -1), k_a=rw_k_a[l].reshape(1, -1), r_k=rw_r_k[l].reshape(1, -1),
                  gn_g=rw_gn_g[l].reshape(1, -1), gn_b=rw_gn_b[l].reshape(1, -1))
        o_rw = _rwkv(y, rw, bsz, seq)

        x1, u2, route, counts = _merge(o_hg, o_da, o_rw, gates, xf, mod, w_branch[l].astype(BF16),
                                       w_out[l].astype(BF16), ln_g[l, 0].reshape(1, d), ln_b[l, 0].reshape(1, d),
                                       w_router_t, router_bias, seq, tm)
        w_gu = jnp.concatenate([w_exp_gate[l], w_exp_up[l]], axis=-1).astype(BF16)
        xf = _moe(u2, route, counts[:, :, 0].astype(jnp.int32), w_gu, w_exp_down[l].astype(BF16), x1, mod,
                  ln_g[l, 1].reshape(1, d), ln_b[l, 1].reshape(1, d), seq, tm)
    return xf.reshape(bsz, seq, d)
```

```python
import functools
import math

import jax
import jax.numpy as jnp
from jax import lax
from jax.experimental import pallas as pl
from jax.experimental.pallas import tpu as pltpu

D_MODEL = 1024
DEPTH = 4
HG_HEADS, HG_DK, HG_DV, HG_CHUNK, HG_SUB = 4, 128, 128, 64, 16
HG_W = HG_HEADS * HG_DV
HG_F_MIN = 1e-6
HG_STEP_SEQS = 2
HG_SAFE_SPAN = 60.0
DA_HEADS, DA_DQK = 4, 64
DA_DV = 2 * DA_DQK
DA_W = DA_HEADS * DA_DV
MASK_VALUE = -1e30
LOG2E = math.log2(math.e)
PROJ_TN = 768
RW_HEADS, RW_DH, RW_CHUNK, RW_SUB = 8, 64, 64, 16
RW_STEP_CHUNKS = 2
RW_STEP_SEQS = 2
RW_W = RW_HEADS * RW_DH
RW_IN_W = 1792
RW_GN_EPS = 64e-5
IN_W = 5376
HG_COL, DA_COL, RW_COL = 0, 2048, 3584
N_EXPERTS, N_GROUPS, EXPERTS_PER_GROUP, D_EXPERT = 16, 4, 4, 512
MOE_CHUNK = 16
MOE_TM = 512
MERGE_SUBTILES = 1
ALPHA = (2.0 * DEPTH) ** 0.25
LN_EPS = 1e-5
RMS_EPS = 1e-6
LANES = 128

F32 = jnp.float32
BF16 = jnp.bfloat16
HIGHEST = lax.Precision.HIGHEST
VMEM_LIMIT = 48 * 1024 * 1024

_NT = (((1,), (1,)), ((), ()))
_TN = (((0,), (0,)), ((), ()))


def _mm(a, b):
    return jnp.dot(a.astype(BF16), b.astype(BF16), preferred_element_type=F32)


def _mm_nt(a, b):
    return lax.dot_general(a.astype(BF16), b.astype(BF16), _NT, preferred_element_type=F32)


def _mmh(a, b):
    return jnp.dot(a, b, precision=HIGHEST, preferred_element_type=F32)


def _seg_sum(x, seg):
    rows = x.shape[0]
    hi = x.astype(BF16)
    lo = (x - hi.astype(F32)).astype(BF16)
    halves = []
    for c0 in range(0, x.shape[1], seg.shape[0]):
        cols = slice(c0, c0 + seg.shape[0])
        both = jnp.dot(jnp.concatenate([hi[:, cols], lo[:, cols]], axis=0), seg, preferred_element_type=F32)
        halves.append(both[:rows] + both[rows:])
    return jnp.concatenate(halves, axis=1)


def _split_mm(x, w_twice, w_lo):
    hi = x.astype(BF16)
    lo = (x - hi.astype(F32)).astype(BF16)
    return (jnp.dot(jnp.concatenate([hi, lo], axis=1), w_twice, preferred_element_type=F32)
            + jnp.dot(hi, w_lo, preferred_element_type=F32))


def _chunk_cumsum(x):
    c = x.shape[0]
    hi = x.astype(BF16)
    rest = x - hi.astype(F32)
    mid = rest.astype(BF16)
    lo = (rest - mid.astype(F32)).astype(BF16)
    col = _iota((c, 4 * c), 1)
    tri = ((col & (c - 1)) <= _iota((c, 4 * c), 0)) & (col < 3 * c)
    return jnp.dot(jnp.where(tri, 1.0, 0.0).astype(BF16), jnp.concatenate([hi, mid, lo, lo], axis=0),
                   preferred_element_type=F32)


def _sigmoid(x):
    return 1.0 / (1.0 + jnp.exp(-x))


def _softplus(x):
    return jnp.maximum(x, 0.0) + jnp.log(1.0 + jnp.exp(-jnp.abs(x)))


def _iota(shape, dim):
    return lax.broadcasted_iota(jnp.int32, shape, dim)


def _params(*sem):
    return pltpu.CompilerParams(dimension_semantics=sem, vmem_limit_bytes=VMEM_LIMIT)


def _layer_norm(y, g, b):
    mu = jnp.mean(y, axis=-1, keepdims=True)
    d = y - mu
    var = jnp.mean(d * d, axis=-1, keepdims=True)
    return d * lax.rsqrt(var + LN_EPS) * g + b


def _ada_body(c_ref, w_ref, b_ref, o_ref):
    c = c_ref[...]
    o_ref[0] = _mmh(c * _sigmoid(c), w_ref[0]) + b_ref[0]


def _ada(c, w_ada, b_ada):
    depth, d, _ = w_ada.shape
    bsz = c.shape[0]
    return pl.pallas_call(
        _ada_body,
        grid=(depth, 6),
        in_specs=[pl.BlockSpec((bsz, d), lambda l, j: (0, 0)),
                  pl.BlockSpec((1, d, d), lambda l, j: (l, 0, j)),
                  pl.BlockSpec((1, 1, d), lambda l, j: (l, 0, j))],
        out_specs=pl.BlockSpec((1, bsz, d), lambda l, j: (l, 0, j)),
        out_shape=jax.ShapeDtypeStruct((depth, bsz, 6 * d), F32),
        compiler_params=_params("arbitrary", "arbitrary"),
        name="ada",
    )(c, w_ada, b_ada.reshape(depth, 1, 6 * d))


def _proj_body(x_ref, mod_ref, win_ref, wg_ref, bg_ref, y_ref, g_ref, vt_ref):
    u = (x_ref[...] * (1.0 + mod_ref[0, 1:2, :]) + mod_ref[0, 0:1, :]).astype(BF16)
    n_kb, kb = vt_ref.shape[1], vt_ref.shape[4]
    v_col = DA_COL + 2 * DA_W
    for c0 in range(0, IN_W, PROJ_TN):
        cols = slice(c0, c0 + PROJ_TN)
        res = jnp.dot(u, win_ref[:, cols], preferred_element_type=F32)
        y_ref[:, cols] = res.astype(BF16)
        if c0 == v_col:
            for h in range(DA_HEADS):
                slope2 = 2.0 ** (-8.0 * (h + 1) / DA_HEADS) * LOG2E
                key_w = jnp.exp2(slope2 * (_iota((1, res.shape[0]), 1) & (kb - 1)).astype(F32))
                v_t = (res[:, h * DA_DV:(h + 1) * DA_DV].T * key_w).astype(BF16)
                tail = jnp.where(_iota((8, res.shape[0]), 0) == 0, key_w, 0.0).astype(BF16)
                for kbi in range(n_kb):
                    keys = slice(kbi * kb, (kbi + 1) * kb)
                    vt_ref[0, kbi, h, 0:DA_DV, :] = v_t[:, keys]
                    vt_ref[0, kbi, h, DA_DV:DA_DV + 8, :] = tail[:, keys]
    for c0 in range(0, 3 * D_MODEL, PROJ_TN):
        cols = slice(c0, c0 + PROJ_TN)
        g = jnp.dot(u, wg_ref[:, cols], preferred_element_type=F32) + bg_ref[:, cols]
        g_ref[:, cols] = _sigmoid(g).astype(BF16)


def _proj(x, mod, w_in, w_gates, b_gates, seq, tm, kb):
    n, d = x.shape
    per_seq = seq // tm
    tile = lambda i: (i, 0)
    resident = dict(index_map=lambda i: (0, 0), pipeline_mode=pl.Buffered(1))
    vt_shape = (n // seq, seq // kb, DA_HEADS, DA_DV + 8, kb)
    return pl.pallas_call(
        _proj_body,
        grid=(n // tm,),
        in_specs=[pl.BlockSpec((tm, d), tile),
                  pl.BlockSpec((1, 6, d), lambda i: (i // per_seq, 0, 0)),
                  pl.BlockSpec(w_in.shape, **resident),
                  pl.BlockSpec(w_gates.shape, **resident),
                  pl.BlockSpec(b_gates.shape, **resident)],
        out_specs=[pl.BlockSpec((tm, IN_W), tile), pl.BlockSpec((tm, 3 * d), tile),
                   pl.BlockSpec((1, tm // kb) + vt_shape[2:], lambda i: (i // per_seq, i % per_seq, 0, 0, 0))],
        out_shape=[jax.ShapeDtypeStruct((n, IN_W), BF16), jax.ShapeDtypeStruct((n, 3 * d), BF16),
                   jax.ShapeDtypeStruct(vt_shape, BF16)],
        compiler_params=_params("arbitrary"),
        name="proj",
    )(x, mod, w_in, w_gates, b_gates)


def _hgrn2_body(y_ref, lb_ref, ng_ref, o_ref, st_ref):
    c, sub = HG_CHUNK, HG_SUB
    n_seq = y_ref.shape[0]

    @pl.when(pl.program_id(1) == 0)
    def _():
        st_ref[...] = jnp.zeros_like(st_ref)

    def part(k):
        return jnp.concatenate([y_ref[si, :, k * HG_W:(k + 1) * HG_W] for si in range(n_seq)], axis=0)

    q = part(0).astype(F32)
    z = part(1).astype(F32)
    lb = lb_ref[...]
    f = lb + (1.0 - lb) * _sigmoid(z)
    kin = (1.0 - lb) * _sigmoid(-z)
    logf = jnp.log(jnp.maximum(f, HG_F_MIN))
    b = jnp.concatenate([_chunk_cumsum(logf[si * c:(si + 1) * c]) for si in range(n_seq)], axis=0)

    ones = jnp.ones((HG_DK, LANES), BF16)
    row_s = _iota((sub, c), 0)
    col_s = _iota((sub, c), 1)
    blks = range(c // sub)
    hs = [slice(h * HG_DK, (h + 1) * HG_DK) for h in range(HG_HEADS)]
    units = [(si, h) for si in range(n_seq) for h in range(HG_HEADS)]

    betas = {(si, blk): (b[si * c + blk * sub - 1:si * c + blk * sub] if blk else jnp.zeros((1, HG_W), F32))
             for si in range(n_seq) for blk in blks}
    span = functools.reduce(jnp.maximum, [beta - b[si * c + (blk + 1) * sub - 1:si * c + (blk + 1) * sub]
                                          for (si, blk), beta in betas.items()])
    safe = jnp.max(span) <= HG_SAFE_SPAN

    def block_products(cap):
        prods = {}
        for si, h in units:
            keys = slice(si * c, (si + 1) * c)
            for blk in blks:
                rows = slice(si * c + blk * sub, si * c + (blk + 1) * sub)
                beta = betas[si, blk][:, hs[h]]
                q_t = q[rows, hs[h]] * jnp.exp(b[rows, hs[h]] - beta)
                k_h = kin[keys, hs[h]] * jnp.exp(jnp.minimum(beta - b[keys, hs[h]], cap))
                prods[si, h, blk] = _mm_nt(q_t, k_h)
        return prods

    def finish(prods, diag_sums):
        scores = []
        for si, h in units:
            a_rows = []
            for blk in blks:
                r0 = blk * sub
                a_blk = prods[si, h, blk]
                if diag_sums is not None:
                    own = jnp.zeros((sub, c), F32)
                    for s in range(sub):
                        own = jnp.where(col_s == r0 + s, diag_sums(si, h, blk, s), own)
                    a_blk = jnp.where(col_s < r0, a_blk, own)
                a_rows.append(jnp.where(col_s <= row_s + r0, a_blk, 0.0))
            scores.append(jnp.concatenate(a_rows, axis=0).astype(BF16))
        v = part(2)
        og = part(3).astype(F32)
        b_last = [b[(si + 1) * c - 1:(si + 1) * c] for si in range(n_seq)]
        q_dec = (q * jnp.exp(b)).astype(BF16)
        k_tail = (kin * jnp.exp(jnp.concatenate([jnp.broadcast_to(x, (c, HG_W)) for x in b_last], axis=0) - b)
                  ).astype(BF16)
        st = [st_ref[si, h] for si, h in units]
        rows = [slice(si * c, (si + 1) * c) for si, _ in units]
        intra = [jnp.dot(sc, v[r, hs[h]], preferred_element_type=F32) for sc, r, (_, h) in zip(scores, rows, units)]
        inter = [lax.dot_general(q_dec[r, hs[h]], s.astype(BF16), _NT, preferred_element_type=F32)
                 for s, r, (_, h) in zip(st, rows, units)]
        for s, r, (si, h) in zip(st, rows, units):
            st_ref[si, h] = s * jnp.exp(b_last[si][:, hs[h]]) + lax.dot_general(
                v[r, hs[h]], k_tail[r, hs[h]], _TN, preferred_element_type=F32)
        for x, y, r, (si, h) in zip(intra, inter, rows, units):
            o = x + y
            o = o * lax.rsqrt(jnp.mean(o * o, axis=-1, keepdims=True) + RMS_EPS) * ng_ref[...]
            o_ref[si, :, hs[h]] = (o * (og[r, hs[h]] * _sigmoid(og[r, hs[h]]))).astype(o_ref.dtype)

    @pl.when(safe)
    def _():
        finish(block_products(HG_SAFE_SPAN), None)

    @pl.when(jnp.logical_not(safe))
    def _():
        log_k = jnp.log(kin)
        b2, rel2, lk2 = b * LOG2E, (log_k - b) * LOG2E, log_k * LOG2E
        half = sub // 2
        diag = {}
        for si, h in units:
            for blk in blks:
                rows = slice(si * c + blk * sub, si * c + (blk + 1) * sub)
                b_i, q_i, rel_i, lk_i = b2[rows, hs[h]], q[rows, hs[h]], rel2[rows, hs[h]], lk2[rows, hs[h]]
                terms = [q_i[t0:] * jnp.exp2(jnp.minimum(b_i[t0:] + rel_i[s:s + 1], lk_i[s:s + 1]))
                         for s in range(sub) for t0 in [0 if s < half else half]]
                w = jnp.concatenate(terms, axis=0).astype(BF16)
                diag[si, h, blk] = jnp.dot(w, ones, preferred_element_type=F32)

        def diag_sums(si, h, blk, s):
            sums = diag[si, h, blk]
            if s < half:
                return sums[s * sub:(s + 1) * sub, :c]
            start = half * sub + (s - half) * half
            return jnp.concatenate([jnp.zeros((half, c), F32), sums[start:start + half, :c]], axis=0)

        finish(block_products(0.0), diag_sums)


def _hgrn2(y, lb, norm_g, bsz, seq):
    n = y.shape[0]
    width = 4 * HG_W
    n_seq = HG_STEP_SEQS if bsz % HG_STEP_SEQS == 0 else 1
    out = pl.pallas_call(
        _hgrn2_body,
        grid=(bsz // n_seq, seq // HG_CHUNK),
        in_specs=[pl.BlockSpec((n_seq, HG_CHUNK, width), lambda b, c: (b, c, HG_COL // width)),
                  pl.BlockSpec((1, HG_W), lambda b, c: (0, 0)),
                  pl.BlockSpec((1, HG_DV), lambda b, c: (0, 0))],
        out_specs=pl.BlockSpec((n_seq, HG_CHUNK, HG_W), lambda b, c: (b, c, 0)),
        out_shape=jax.ShapeDtypeStruct((bsz, seq, HG_W), BF16),
        scratch_shapes=[pltpu.VMEM((n_seq, HG_HEADS, HG_DV, HG_DK), F32)],
        compiler_params=_params("arbitrary", "arbitrary"),
        name="hgrn2",
    )(y.reshape(bsz, seq, -1), lb, norm_g)
    return out.reshape(n, HG_W)


def _attn_body(scal_ref, q_ref, k_ref, vt_ref, g_ref, o_ref, qq_ref, sa_ref, sb_ref, p_ref, m_ref, sc_ref,
               acc_ref, *, kb):
    h = pl.program_id(1)
    i = pl.program_id(2)
    lam = scal_ref[0]
    out_scale = scal_ref[1]
    slope = scal_ref[2 + h] * LOG2E
    qb = 2 * kb
    q0 = i * qb
    tiles_per_map = qb // LANES
    a_tiles = kb // LANES

    q = q_ref[...].astype(F32) * (DA_DQK ** -0.5 * LOG2E)
    lane = _iota(q.shape, 1)
    stacked = jnp.concatenate([jnp.where(lane < DA_DQK, q, 0.0), jnp.where(lane >= DA_DQK, q, 0.0)], axis=0)
    qq_ref[...] = stacked.T.astype(BF16)
    m_ref[...] = jnp.full(m_ref.shape, MASK_VALUE, F32)
    sc_ref[...] = jnp.ones(sc_ref.shape, F32)
    acc_ref[...] = jnp.zeros(acc_ref.shape, F32)
    p_ref[...] = jnp.zeros(p_ref.shape, BF16)
    key_off = _iota((kb, LANES), 0)

    def scores(j):
        return jnp.dot(k_ref[pl.ds(j * kb, kb), :], qq_ref[...], preferred_element_type=F32)

    def softmax(j, src_ref, diagonal_of):
        block_bias = slope * (j * kb - q0).astype(F32)
        for t in range(2 * tiles_per_map):
            cols = slice(t * LANES, (t + 1) * LANES)
            in_map = t % tiles_per_map
            half = "A" if in_map < a_tiles else "B"
            if diagonal_of == "B" and half == "A":
                p_ref[t] = jnp.zeros((kb, LANES), BF16)
                sc_ref[:, cols] = jnp.ones((1, LANES), F32)
                continue
            s = src_ref[t]
            if diagonal_of == half:
                q_off = _iota((kb, LANES), 1) + (in_map % a_tiles) * LANES
                s = jnp.where(key_off <= q_off, s, MASK_VALUE)
            m_old = m_ref[:, cols]
            m_new = jnp.maximum(m_old, jnp.max(s, axis=0, keepdims=True) + block_bias)
            p_ref[t] = jnp.exp2(s - (m_new - block_bias)).astype(BF16)
            sc_ref[:, cols] = jnp.exp2(m_old - m_new)
            m_ref[:, cols] = m_new

    n_tiles = 2 * tiles_per_map

    def park(dst_ref, s):
        for t in range(n_tiles):
            dst_ref[t] = s[:, t * LANES:(t + 1) * LANES]

    def probabilities():
        return jnp.concatenate([p_ref[t] for t in range(n_tiles)], axis=1)

    def iteration(j, src_ref, dst_ref, diagonal_of=None):
        sc_prev = sc_ref[...]
        pv = jnp.dot(vt_ref[0, jnp.maximum(j - 1, 0), 0], probabilities(), preferred_element_type=F32)
        if dst_ref is not None:
            park(dst_ref, scores(j + 1))
        for t in range(n_tiles):
            cols = slice(t * LANES, (t + 1) * LANES)
            acc_ref[t] = acc_ref[t] * sc_prev[:, cols] + pv[:, cols]
        softmax(j, src_ref, diagonal_of)

    def pair(j0):
        iteration(j0, sa_ref, sb_ref)
        iteration(j0 + 1, sb_ref, sa_ref)

    def body(quad, carry):
        pair(4 * quad)
        pair(4 * quad + 2)
        return carry

    park(sa_ref, scores(0))
    lax.fori_loop(0, i // 2, body, 0)

    @pl.when(i % 2 == 1)
    def _():
        pair(2 * i - 2)

    iteration(2 * i, sa_ref, sb_ref, "A")
    iteration(2 * i + 1, sb_ref, None, "B")
    acc = (jnp.concatenate([acc_ref[t] for t in range(n_tiles)], axis=1) * sc_ref[...]
           + jnp.dot(vt_ref[0, 2 * i + 1, 0], probabilities(), preferred_element_type=F32))
    o = acc[:DA_DV] / acc[DA_DV:DA_DV + 1]
    d = o[:, :qb] - lam * o[:, qb:]
    g = jnp.concatenate([g_ref[...]] * tiles_per_map, axis=1)
    d = d * lax.rsqrt(jnp.mean(d * d, axis=0, keepdims=True) + RMS_EPS) * g * out_scale
    o_ref[...] = d.T.astype(o_ref.dtype)


def _attn(y, v_t, scal, subln_g, bsz, seq, kb):
    n = y.shape[0]
    qb = 2 * kb
    nq, nk = seq // qb, seq // kb
    qc, kc = DA_COL // DA_DV, (DA_COL + DA_W) // DA_DV
    rows_v = v_t.shape[3]
    g_col = jnp.broadcast_to(subln_g.reshape(DA_DV, 1), (DA_DV, LANES))
    row = (1, 2 * qb)
    return pl.pallas_call(
        functools.partial(_attn_body, kb=kb),
        scratch_shapes=[pltpu.VMEM((DA_DV, 2 * qb), BF16), pltpu.VMEM((2 * qb // LANES, kb, LANES), F32),
                        pltpu.VMEM((2 * qb // LANES, kb, LANES), F32),
                        pltpu.VMEM((2 * qb // LANES, kb, LANES), BF16), pltpu.VMEM(row, F32),
                        pltpu.VMEM(row, F32), pltpu.VMEM((2 * qb // LANES, rows_v, LANES), F32)],
        grid=(bsz, DA_HEADS, nq),
        in_specs=[pl.BlockSpec(memory_space=pltpu.SMEM),
                  pl.BlockSpec((qb, DA_DV), lambda b, h, i: (b * nq + i, qc + h)),
                  pl.BlockSpec((seq, DA_DV), lambda b, h, i: (b, kc + h)),
                  pl.BlockSpec((1, nk, 1, rows_v, kb), lambda b, h, i: (b, 0, h, 0, 0)),
                  pl.BlockSpec((DA_DV, LANES), lambda b, h, i: (0, 0))],
        out_specs=pl.BlockSpec((qb, DA_DV), lambda b, h, i: (b * nq + i, h)),
        out_shape=jax.ShapeDtypeStruct((n, DA_W), BF16),
        compiler_params=_params("arbitrary", "arbitrary", "arbitrary"),
        name="diffattn",
    )(scal, y, y, v_t, g_col)


def _split_f32(x):
    hi = x.astype(BF16)
    hi_f = hi.astype(F32)
    return hi, hi_f, x - hi_f


def _dup_lhs(hi_f, lo_f, low_half):
    packed = jnp.where(low_half, hi_f, lo_f).astype(BF16)
    return jnp.concatenate([packed, packed], axis=1)


def _dup_rhs(hi, lo_f):
    lo = lo_f.astype(BF16)
    return jnp.concatenate([hi, hi, lo, lo], axis=0)


def _rwkv_body(y_ref, mu_ref, w0_ref, a0_ref, wa2_ref, walo_ref, gu2_ref, gulo_ref, kk_ref, ka_ref, rk_ref,
               gng_ref, gnb_ref, seg_ref, o_ref, st_ref, prev_ref, osc_ref):
    c, sub, dh = RW_CHUNK, RW_SUB, RW_DH

    @pl.when(pl.program_id(1) == 0)
    def _():
        st_ref[...] = jnp.zeros_like(st_ref)
        prev_ref[...] = jnp.zeros_like(prev_ref)

    n_seq, seq_rows = y_ref.shape[0], y_ref.shape[1]
    rows = n_seq * seq_rows
    shifted = []
    for si in range(n_seq):
        x = y_ref[si].astype(F32)
        x_prev = jnp.where(_iota(x.shape, 0) == 0, prev_ref[si:si + 1], pltpu.roll(x, 1, axis=0))
        prev_ref[si:si + 1] = x[seq_rows - 1:seq_rows]
        shifted.append(x + (x_prev - x) * mu_ref[...])
    xs = jnp.concatenate(shifted, axis=0)
    r = xs[:, 0:RW_W]
    k = xs[:, RW_W:2 * RW_W]
    v = xs[:, 2 * RW_W:3 * RW_W]
    wa = xs[:, 3 * RW_W:3 * RW_W + LANES]
    gd = xs[:, 3 * RW_W + LANES:RW_IN_W]

    lora = _split_mm(jnp.where(_iota(wa.shape, 1) < 64, jnp.tanh(wa), wa), wa2_ref[...], walo_ref[...])
    w_log = -_softplus(-(w0_ref[...] + lora[:, :RW_W])) - 0.5
    g = -jnp.exp(w_log)
    a = _sigmoid(a0_ref[...] + lora[:, RW_W:])
    gate = _split_mm(_sigmoid(gd), gu2_ref[...], gulo_ref[...])
    seg = seg_ref[...]
    kk = k * kk_ref[...]
    k2 = k * (1.0 + (a - 1.0) * ka_ref[...])
    sums = _seg_sum(jnp.concatenate([kk * kk, r * k2 * rk_ref[...]], axis=0), seg)
    kk = kk * lax.rsqrt(jnp.maximum(sums[:rows], 1e-12))
    bb = kk * a
    bonus = sums[rows:] * v

    chunks = range(rows // c)
    cr = [slice(ci * c, (ci + 1) * c) for ci in chunks]
    gc = jnp.concatenate([_chunk_cumsum(g[s]) for s in cr], axis=0)
    g_last = [gc[s][c - 1:c] for s in cr]
    e_inv = jnp.exp(-gc)
    e_tail = jnp.exp(jnp.concatenate([jnp.broadcast_to(gl, (c, RW_W)) for gl in g_last], axis=0) - gc)
    gam = [jnp.exp(gl) for gl in g_last]
    a_t = (-kk * jnp.exp(gc - g)).astype(BF16)
    r_t = (r * jnp.exp(gc)).astype(BF16)
    b_h = (bb * e_inv).astype(BF16)
    k_h = (k2 * e_inv).astype(BF16)
    k_bar = (k2 * e_tail).astype(BF16)
    b_bar = (bb * e_tail).astype(BF16)
    v_bf = v.astype(BF16)

    row2 = _iota((c, 2 * c), 0)
    lane2 = _iota((c, 2 * c), 1)
    col2 = lane2 & (c - 1)
    low_half = lane2 < c
    strict = row2 > col2
    incl = row2 >= col2
    same_blk = (row2 // sub) == (col2 // sub)
    eye = (row2 == col2).astype(F32)

    def dot(p, q):
        return jnp.dot(p, q, preferred_element_type=F32)

    heads = range(RW_HEADS)
    sls = [slice(h * dh, (h + 1) * dh) for h in heads]
    items = [(ci, h) for ci in chunks for h in heads]
    ar_h = [jnp.concatenate([a_t[cr[ci], sls[h]], r_t[cr[ci], sls[h]]], axis=0) for ci, h in items]
    quad = [lax.dot_general(x, jnp.concatenate([b_h[cr[ci], sls[h]], k_h[cr[ci], sls[h]]], axis=0), _NT,
                            preferred_element_type=F32)
            for x, (ci, h) in zip(ar_h, items)]
    top = [jnp.where(strict, q[:c], 0.0) for q in quad]
    a_ak = [t[:, c:].astype(BF16) for t in top]
    a_r = [jnp.where(incl, q[c:], 0.0).astype(BF16) for q in quad]
    a_ab = [jnp.where(low_half, t, pltpu.roll(t, c, axis=1)) for t in top]
    a_d = [jnp.where(same_blk, x, 0.0) for x in a_ab]
    a_o = [(x - y).astype(BF16) for x, y in zip(a_ab, a_d)]

    s1 = [_split_f32(x) for x in a_d]
    p2 = [dot(_dup_lhs(hf, lf, low_half), _dup_rhs(hi, lf)) for hi, hf, lf in s1]
    s2 = [_split_f32(x) for x in p2]
    rhs2 = [_dup_rhs(hi, lf) for hi, _, lf in s2]
    p4 = [dot(_dup_lhs(hf, lf, low_half), rhs) for (_, hf, lf), rhs in zip(s2, rhs2)]
    s4 = [_split_f32(x) for x in p4]
    rhs4 = [_dup_rhs(hi, lf) for hi, _, lf in s4]
    p8 = [dot(_dup_lhs(hf, lf, low_half), rhs) for (_, hf, lf), rhs in zip(s4, rhs4)]
    rhs8 = [_dup_rhs(hi, lf) for hi, _, lf in (_split_f32(x) for x in p8)]
    t_d = [eye + x for x in a_d]
    for rhs_all in (rhs2, rhs4, rhs8):
        st = [_split_f32(x) for x in t_d]
        t_d = [x + dot(_dup_lhs(hf, lf, low_half), rhs) for x, (_, hf, lf), rhs in zip(t_d, st, rhs_all)]
    t_d = [x.astype(BF16) for x in t_d]

    nn = [dot(t[:, :c], x) for t, x in zip(t_d, a_o)]
    nn_bf = [x.astype(BF16) for x in nn]
    n2 = [dot(x[:, :c], x) for x in nn_bf]
    n3 = [dot(x[:, :c], y.astype(BF16)) for x, y in zip(nn_bf, n2)]
    t_m = [dot((eye + x + y + z).astype(BF16)[:, :c], t).astype(BF16)[:, :c]
           for x, y, z, t in zip(nn, n2, n3, t_d)]

    v_h = [v_bf[cr[ci], sls[h]] for ci, h in items]
    akv = [dot(x, y).astype(BF16) for x, y in zip(a_ak, v_h)]
    at_m = [dot(t, x[:c]).astype(BF16) for t, x in zip(t_m, ar_h)]
    v_p = [dot(t, x) for t, x in zip(t_m, akv)]

    per_seq = seq_rows // c
    chains = [(si, h) for si in range(n_seq) for h in heads]
    state = [st_ref[si, h] for si, h in chains]
    for t in range(per_seq):
        it = [(si * per_seq + t) * RW_HEADS + h for si, h in chains]
        ck = [cr[si * per_seq + t] for si, _ in chains]
        proj = [lax.dot_general(jnp.concatenate([at_m[j], ar_h[j][c:]], axis=0), s.astype(BF16), _NT,
                                preferred_element_type=F32) for j, s in zip(it, state)]
        u = [(p[:c] + v_p[j]).astype(BF16) for p, j in zip(proj, it)]
        for p, uu, j, rws, (_, h) in zip(proj, u, it, ck, chains):
            osc_ref[rws, sls[h]] = p[c:] + dot(a_r[j], jnp.concatenate([uu, v_h[j]], axis=0))
        state = [s * gam[j // RW_HEADS][:, sls[h]] + lax.dot_general(
            jnp.concatenate([v_h[j], uu], axis=0),
            jnp.concatenate([k_bar[rws, sls[h]], b_bar[rws, sls[h]]], axis=0), _TN, preferred_element_type=F32)
            for s, uu, j, rws, (_, h) in zip(state, u, it, ck, chains)]
    for s, (si, h) in zip(state, chains):
        st_ref[si, h] = s

    o = osc_ref[...]
    mean = _seg_sum(o, seg) * (1.0 / dh)
    d = o - mean
    var = _seg_sum(d * d, seg) * (1.0 / dh)
    o = d * lax.rsqrt(var + RW_GN_EPS) * gng_ref[...] + gnb_ref[...]
    o = ((o + bonus) * gate).astype(o_ref.dtype)
    for si in range(n_seq):
        o_ref[si] = o[si * seq_rows:(si + 1) * seq_rows]


def _rwkv(y, p, bsz, seq):
    n = y.shape[0]
    half = RW_W // 2
    seg = (_iota((half, half), 0) // RW_DH == _iota((half, half), 1) // RW_DH).astype(BF16)

    def two_terms(w):
        hi = w.astype(BF16)
        return jnp.concatenate([hi, hi], axis=0), (w - hi.astype(F32)).astype(BF16)

    zeros = jnp.zeros_like(p["w_up"])
    wa2, wa_lo = two_terms(jnp.concatenate([jnp.concatenate([p["w_up"], zeros], axis=1),
                                            jnp.concatenate([zeros, p["a_up"]], axis=1)], axis=0))
    gu2, gu_lo = two_terms(p["g_up"])
    rows = [p["mu"], p["w0"], p["a0"], wa2, wa_lo, gu2, gu_lo, p["k_k"], p["k_a"], p["r_k"],
            p["gn_g"], p["gn_b"], seg]
    full = lambda b, c: (0, 0)
    step = RW_STEP_CHUNKS * RW_CHUNK
    n_seq = RW_STEP_SEQS if bsz % RW_STEP_SEQS == 0 else 1
    out = pl.pallas_call(
        _rwkv_body,
        grid=(bsz // n_seq, seq // step),
        in_specs=[pl.BlockSpec((n_seq, step, RW_IN_W), lambda b, c: (b, c, RW_COL // RW_IN_W))]
        + [pl.BlockSpec(a.shape, full) for a in rows],
        out_specs=pl.BlockSpec((n_seq, step, RW_W), lambda b, c: (b, c, 0)),
        out_shape=jax.ShapeDtypeStruct((bsz, seq, RW_W), BF16),
        scratch_shapes=[pltpu.VMEM((n_seq, RW_HEADS, RW_DH, RW_DH), F32),
                        pltpu.VMEM((n_seq, RW_IN_W), F32),
                        pltpu.VMEM((n_seq * step, RW_W), F32)],
        compiler_params=_params("arbitrary", "arbitrary"),
        name="rwkv7",
    )(y.reshape(bsz, seq, -1), *rows)
    return out.reshape(n, RW_W)


def _first_argmax(vals, row):
    top = jnp.max(vals, axis=0, keepdims=True)
    idx = jnp.min(jnp.where(vals == top, row, N_EXPERTS), axis=0, keepdims=True)
    return top, idx


def _merge_body(ohg_ref, oda_ref, orw_ref, gt_ref, x_ref, mod_ref, wb_ref, wo_ref, lng_ref, lnb_ref,
                wrt_ref, rb_ref, tri_ref, tri16_ref, x1_ref, u2_ref, route_ref, cnt_ref):
    d = D_MODEL
    sub_rows = tri_ref.shape[0]

    def route(rows):
        merged = (gt_ref[rows, 0:d].astype(F32)
                  * jnp.dot(ohg_ref[rows, :], wb_ref[0:HG_W, :], preferred_element_type=F32)
                  + gt_ref[rows, d:2 * d].astype(F32)
                  * jnp.dot(oda_ref[rows, :], wb_ref[HG_W:HG_W + DA_W, :], preferred_element_type=F32)
                  + gt_ref[rows, 2 * d:3 * d].astype(F32)
                  * jnp.dot(orw_ref[rows, :], wb_ref[HG_W + DA_W:, :], preferred_element_type=F32))
        mix = _mm(merged, wo_ref[...])
        x1 = _layer_norm(ALPHA * x_ref[rows, :] + (1.0 + mod_ref[0, 2:3, :]) * mix, lng_ref[...], lnb_ref[...])
        x1_ref[rows, :] = x1
        u2 = x1 * (1.0 + mod_ref[0, 4:5, :]) + mod_ref[0, 3:4, :]
        u2_hi = u2.astype(BF16)
        u2_ref[rows, :] = u2_hi

        u2_lo = (u2 - u2_hi.astype(F32)).astype(BF16)
        two = lax.dot_general(wrt_ref[...], u2_hi, _NT, preferred_element_type=F32)
        logits = (two[:N_EXPERTS] + two[N_EXPERTS:]
                  + lax.dot_general(wrt_ref[0:N_EXPERTS, :], u2_lo, _NT, preferred_element_type=F32))
        ex = jnp.exp(logits - jnp.max(logits, axis=0, keepdims=True))
        scores = ex / jnp.sum(ex, axis=0, keepdims=True)
        sel = scores + rb_ref[...]
        row = _iota(sel.shape, 0)
        best = None
        for grp in range(N_GROUPS):
            a, b, c2, d2 = (sel[grp * EXPERTS_PER_GROUP + i:grp * EXPERTS_PER_GROUP + i + 1] for i in range(4))
            hi1, lo1, hi2, lo2 = jnp.maximum(a, b), jnp.minimum(a, b), jnp.maximum(c2, d2), jnp.minimum(c2, d2)
            top2 = jnp.maximum(hi1, hi2) + jnp.maximum(jnp.minimum(hi1, hi2), jnp.maximum(lo1, lo2))
            if best is None:
                best, best_grp = top2, jnp.zeros_like(top2, dtype=jnp.int32)
            else:
                better = top2 > best
                best = jnp.where(better, top2, best)
                best_grp = jnp.where(better, grp, best_grp)
        masked = jnp.where(row // EXPERTS_PER_GROUP == best_grp, sel, MASK_VALUE)
        _, idx1 = _first_argmax(masked, row)
        _, idx2 = _first_argmax(jnp.where(row == idx1, -jnp.inf, masked), row)
        pick1 = row == idx1
        pick2 = row == idx2
        w1 = jnp.sum(jnp.where(pick1, scores, 0.0), axis=0, keepdims=True)
        w2 = jnp.sum(jnp.where(pick2, scores, 0.0), axis=0, keepdims=True)
        onehot = jnp.where(pick1 | pick2, 1.0, 0.0)
        earlier = jnp.dot(onehot.astype(BF16), tri_ref[...], preferred_element_type=F32)
        return pick1, pick2, w1 / (w1 + w2), w2 / (w1 + w2), earlier, jnp.sum(onehot, axis=1, keepdims=True)

    parts = [route(slice(r0, r0 + sub_rows)) for r0 in range(0, x_ref.shape[0], sub_rows)]

    cnt = functools.reduce(jnp.add, [p[5] for p in parts])
    chunks = jnp.floor((cnt + (MOE_CHUNK - 1)) * (1.0 / MOE_CHUNK))
    seg_start = MOE_CHUNK * jnp.dot(tri16_ref[...], jnp.broadcast_to(chunks, (N_EXPERTS, LANES)).astype(BF16),
                                    preferred_element_type=F32)[:, 0:1]
    before = jnp.zeros_like(cnt)
    for k, (pick1, pick2, wn1, wn2, earlier, cnt_k) in enumerate(parts):
        pos = seg_start + before + earlier
        pos1 = jnp.sum(jnp.where(pick1, pos, 0.0), axis=0, keepdims=True)
        pos2 = jnp.sum(jnp.where(pick2, pos, 0.0), axis=0, keepdims=True)
        route_ref[:, k * sub_rows:(k + 1) * sub_rows] = jnp.concatenate(
            [pos1, pos2, wn1, wn2, jnp.zeros((4, sub_rows), F32)], axis=0)
        before = before + cnt_k
    cnt_ref[0] = jnp.broadcast_to(cnt, (N_EXPERTS, LANES))


def _merge(o_hg, o_da, o_rw, gates, x, mod, w_branch, w_out, ln_g, ln_b, w_router_t, router_bias, seq, tm):
    n, d = x.shape
    per_seq = seq // tm
    tile = lambda i: (i, 0)
    full = lambda i: (0, 0)
    sub = tm // MERGE_SUBTILES
    before = (_iota((sub, sub), 0) < _iota((sub, sub), 1)).astype(BF16)
    before16 = (_iota((N_EXPERTS, N_EXPERTS), 1) < _iota((N_EXPERTS, N_EXPERTS), 0)).astype(BF16)
    w_hi = w_router_t.astype(BF16)
    w_two = jnp.concatenate([w_hi, (w_router_t - w_hi.astype(F32)).astype(BF16)], axis=0)
    return pl.pallas_call(
        _merge_body,
        grid=(n // tm,),
        in_specs=[pl.BlockSpec((tm, HG_W), tile), pl.BlockSpec((tm, DA_W), tile), pl.BlockSpec((tm, RW_W), tile),
                  pl.BlockSpec((tm, 3 * d), tile), pl.BlockSpec((tm, d), tile),
                  pl.BlockSpec((1, 6, d), lambda i: (i // per_seq, 0, 0)),
                  pl.BlockSpec(w_branch.shape, full), pl.BlockSpec(w_out.shape, full),
                  pl.BlockSpec((1, d), full), pl.BlockSpec((1, d), full),
                  pl.BlockSpec((2 * N_EXPERTS, d), full), pl.BlockSpec((N_EXPERTS, 1), full),
                  pl.BlockSpec((sub, sub), full), pl.BlockSpec((N_EXPERTS, N_EXPERTS), full)],
        out_specs=[pl.BlockSpec((tm, d), tile), pl.BlockSpec((tm, d), tile), pl.BlockSpec((8, tm), lambda i: (0, i)),
                   pl.BlockSpec((1, N_EXPERTS, LANES), lambda i: (i, 0, 0))],
        out_shape=[jax.ShapeDtypeStruct((n, d), F32), jax.ShapeDtypeStruct((n, d), BF16),
                   jax.ShapeDtypeStruct((8, n), F32), jax.ShapeDtypeStruct((n // tm, N_EXPERTS, LANES), F32)],
        compiler_params=_params("arbitrary"),
        name="merge",
    )(o_hg, o_da, o_rw, gates, x, mod, w_branch, w_out, ln_g, ln_b, w_two, router_bias, before, before16)


def _local_rows(tm):
    return -(-(2 * tm + N_EXPERTS * (MOE_CHUNK - 1)) // LANES) * LANES


def _token_columns(route):
    return jnp.concatenate([route, jnp.zeros((LANES - route.shape[0], route.shape[1]), F32)], axis=0).T


def _segment_copies(i, nch_ref, loc_ref, glob_ref, local_buf, global_buf, sem, to_global):
    def run(action):
        for e in range(N_EXPERTS):
            seg = i * N_EXPERTS + e
            loc0, glob0 = loc_ref[seg], glob_ref[seg]

            def one(c, carry):
                loc = local_buf.at[pl.ds(pl.multiple_of(loc0 + c * MOE_CHUNK, MOE_CHUNK), MOE_CHUNK), :]
                glob = global_buf.at[pl.ds(pl.multiple_of(glob0 + c * MOE_CHUNK, MOE_CHUNK), MOE_CHUNK), :]
                copy = pltpu.make_async_copy(loc, glob, sem) if to_global else pltpu.make_async_copy(glob, loc, sem)
                getattr(copy, action)()
                return carry

            lax.fori_loop(0, nch_ref[seg], one, 0)
    return run


def _dispatch_body(nch_ref, loc_ref, glob_ref, u_ref, route_ref, xs_in_ref, xs_ref, stage2_ref, sems):
    del xs_in_ref
    i = pl.program_id(0)
    last = pl.num_programs(0) - 1
    tm, d = u_ref.shape
    slot = i % 2
    stage_ref = stage2_ref.at[slot]

    def tile_copies(tile):
        return _segment_copies(tile, nch_ref, loc_ref, glob_ref, stage2_ref.at[tile % 2], xs_ref,
                               sems.at[tile % 2], to_global=True)

    @pl.when(i >= 2)
    def _():
        tile_copies(i - 2)("wait")

    route = route_ref[...]
    local_row = _iota((stage_ref.shape[0], tm), 0)
    take1 = local_row == route[0:1].astype(jnp.int32)
    take2 = local_row == route[1:2].astype(jnp.int32)
    perm = jnp.where(take1 | take2, 1.0, 0.0).astype(BF16)
    stage_ref[:, 0:d] = jnp.dot(perm, u_ref[...], preferred_element_type=F32).astype(BF16)

    cols = _token_columns(route)
    lane = _iota((tm, LANES), 1)

    def weight_cols(w):
        hi = w.astype(BF16).astype(F32)
        return jnp.where(lane == 0, hi, jnp.where(lane == 1, w - hi, 0.0)).astype(BF16)

    stage_ref[:, d:d + LANES] = (
        jnp.dot(jnp.where(take1, 1.0, 0.0).astype(BF16), weight_cols(cols[:, 2:3]), preferred_element_type=F32)
        + jnp.dot(jnp.where(take2, 1.0, 0.0).astype(BF16), weight_cols(cols[:, 3:4]), preferred_element_type=F32)
    ).astype(BF16)

    tile_copies(i)("start")

    @pl.when(i == last)
    def _():

        @pl.when(i >= 1)
        def _():
            tile_copies(i - 1)("wait")

        tile_copies(i)("wait")


def _dispatch(u2, route, nch, loc, glob, rows, tm):
    n, d = u2.shape
    width = d + LANES
    return pl.pallas_call(
        _dispatch_body,
        grid_spec=pltpu.PrefetchScalarGridSpec(
            num_scalar_prefetch=3,
            grid=(n // tm,),
            in_specs=[pl.BlockSpec((tm, d), lambda i, *_: (i, 0)),
                      pl.BlockSpec((8, tm), lambda i, *_: (0, i)),
                      pl.BlockSpec(memory_space=pl.ANY)],
            out_specs=pl.BlockSpec(memory_space=pl.ANY),
            scratch_shapes=[pltpu.VMEM((2, _local_rows(tm), width), BF16), pltpu.SemaphoreType.DMA((2,))],
        ),
        out_shape=jax.ShapeDtypeStruct((rows, width), BF16),
        input_output_aliases={5: 0},
        compiler_params=_params("arbitrary"),
        name="moe_dispatch",
    )(nch, loc, glob, u2, route, jnp.zeros((rows, width), BF16))


def _experts_body(te_ref, x_ref, wg_ref, wu_ref, wd_ref, y_ref, wgu_bf, wd_bf):
    g = pl.program_id(0)
    expert = te_ref[g]
    used = expert < N_EXPERTS

    @pl.when(jnp.logical_and(used, jnp.logical_or(g == 0, expert != te_ref[jnp.maximum(g - 1, 0)])))
    def _():
        wgu_bf[:, :D_EXPERT] = wg_ref[0, 0].astype(BF16)
        wgu_bf[:, D_EXPERT:] = wu_ref[0, 0].astype(BF16)
        wd_bf[...] = wd_ref[0, 0].astype(BF16)

    @pl.when(used)
    def _():
        d = wgu_bf.shape[0]
        weight = x_ref[:, d:d + 1].astype(F32) + x_ref[:, d + 1:d + 2].astype(F32)
        hidden = jnp.dot(x_ref[:, 0:d], wgu_bf[...], preferred_element_type=F32)
        hg = hidden[:, :D_EXPERT]
        act = hg * _sigmoid(hg) * hidden[:, D_EXPERT:] * weight
        y_ref[...] = _mm(act, wd_bf[...]).astype(BF16)

    @pl.when(jnp.logical_not(used))
    def _():
        y_ref[...] = jnp.zeros_like(y_ref)


def _experts(xs, tile_expert, w_gate, w_up, w_down, layer):
    rows, width = xs.shape
    d = w_down.shape[3]
    expert = lambda g, te: (layer, jnp.minimum(te[g], N_EXPERTS - 1), 0, 0)
    return pl.pallas_call(
        _experts_body,
        grid_spec=pltpu.PrefetchScalarGridSpec(
            num_scalar_prefetch=1,
            grid=(rows // MOE_TM,),
            in_specs=[pl.BlockSpec((MOE_TM, width), lambda g, te: (g, 0)),
                      pl.BlockSpec((1, 1, d, D_EXPERT), expert),
                      pl.BlockSpec((1, 1, d, D_EXPERT), expert),
                      pl.BlockSpec((1, 1, D_EXPERT, d), expert)],
            out_specs=pl.BlockSpec((MOE_TM, d), lambda g, te: (g, 0)),
            scratch_shapes=[pltpu.VMEM((d, 2 * D_EXPERT), BF16), pltpu.VMEM((D_EXPERT, d), BF16)],
        ),
        out_shape=jax.ShapeDtypeStruct((rows, d), BF16),
        compiler_params=_params("arbitrary"),
        name="moe_experts",
    )(tile_expert, xs, w_gate, w_up, w_down)


def _combine_body(nch_ref, loc_ref, glob_ref, route_ref, x1_ref, mod_ref, lng_ref, lnb_ref, ys_ref, o_ref,
                  back2_ref, sems):
    i = pl.program_id(0)

    def tile_copies(tile):
        return _segment_copies(tile, nch_ref, loc_ref, glob_ref, back2_ref.at[tile % 2], ys_ref,
                               sems.at[tile % 2], to_global=False)

    @pl.when(i == 0)
    def _():
        back2_ref[...] = jnp.zeros_like(back2_ref)
        tile_copies(i)("start")

    @pl.when(i + 1 < pl.num_programs(0))
    def _():
        tile_copies(i + 1)("start")

    cols = _token_columns(route_ref[...]).astype(jnp.int32)
    local_row = _iota((x1_ref.shape[0], back2_ref.shape[1]), 1)
    unperm = jnp.where((local_row == cols[:, 0:1]) | (local_row == cols[:, 1:2]), 1.0, 0.0).astype(BF16)
    tile_copies(i)("wait")
    ffn = jnp.dot(unperm, back2_ref[i % 2], preferred_element_type=F32)
    y = ALPHA * x1_ref[...] + (1.0 + mod_ref[0, 5:6, :]) * ffn
    o_ref[...] = _layer_norm(y, lng_ref[...], lnb_ref[...])


def _combine(ys, route, nch, loc, glob, x1, mod, ln_g, ln_b, seq, tm):
    n, d = x1.shape
    per_seq = seq // tm
    full = lambda i, *_: (0, 0)
    return pl.pallas_call(
        _combine_body,
        grid_spec=pltpu.PrefetchScalarGridSpec(
            num_scalar_prefetch=3,
            grid=(n // tm,),
            in_specs=[pl.BlockSpec((8, tm), lambda i, *_: (0, i)),
                      pl.BlockSpec((tm, d), lambda i, *_: (i, 0)),
                      pl.BlockSpec((1, 6, d), lambda i, *_: (i // per_seq, 0, 0)),
                      pl.BlockSpec((1, d), full), pl.BlockSpec((1, d), full),
                      pl.BlockSpec(memory_space=pl.ANY)],
            out_specs=pl.BlockSpec((tm, d), lambda i, *_: (i, 0)),
            scratch_shapes=[pltpu.VMEM((2, _local_rows(tm), d), BF16), pltpu.SemaphoreType.DMA((2,))],
        ),
        out_shape=jax.ShapeDtypeStruct((n, d), F32),
        compiler_params=_params("arbitrary"),
        name="moe_combine",
    )(nch, loc, glob, route, x1, mod, ln_g, ln_b, ys)


def _moe(u2, route, counts, expert_weights, layer, x1, mod, ln_g, ln_b, seq, tm):
    n = u2.shape[0]
    n_tiles = n // tm
    seg_rows = (counts + MOE_CHUNK - 1) // MOE_CHUNK * MOE_CHUNK
    loc = jnp.cumsum(seg_rows, axis=1) - seg_rows
    region = (jnp.sum(seg_rows, axis=0) + MOE_TM - 1) // MOE_TM * MOE_TM
    region_end = jnp.cumsum(region)
    glob = (region_end - region)[None, :] + jnp.cumsum(seg_rows, axis=0) - seg_rows
    rows = -(-(2 * n + n_tiles * N_EXPERTS * (MOE_CHUNK - 1) + N_EXPERTS * (MOE_TM - 1)) // MOE_TM) * MOE_TM
    tile_expert = jnp.sum(jnp.arange(rows // MOE_TM, dtype=jnp.int32)[:, None] * MOE_TM >= region_end[None, :],
                          axis=1).astype(jnp.int32)
    flat = lambda a: a.reshape(-1).astype(jnp.int32)
    nch, loc, glob = flat(seg_rows // MOE_CHUNK), flat(loc), flat(glob)
    xs = _dispatch(u2, route, nch, loc, glob, rows, tm)
    ys = _experts(xs, tile_expert, *expert_weights, layer)
    return _combine(ys, route, nch, loc, glob, x1, mod, ln_g, ln_b, seq, tm)


def _tiles(seq):
    return min(512, seq), min(256, seq // 2)


def kernel(x, c, w_ada, b_ada, w_in, hg_lb_logits, hg_norm_g, da_lambda, da_subln_g, rw_mu, rw_w0, rw_w_up,
           rw_a0, rw_a_up, rw_g_up, rw_k_k, rw_k_a, rw_r_k, rw_gn_g, rw_gn_b, w_merge, b_merge, w_branch, w_out,
           ln_g, ln_b, w_router, router_bias, w_exp_gate, w_exp_up, w_exp_down):
    bsz, seq, d = x.shape
    depth = w_in.shape[0]
    n = bsz * seq
    tm, blk = _tiles(seq)

    sm = jax.nn.softmax(hg_lb_logits.astype(F32), axis=0)
    hg_lb = jnp.cumsum(sm, axis=0) - sm[0:1]
    slopes = jnp.asarray([2.0 ** (-8.0 * (h + 1) / DA_HEADS) for h in range(DA_HEADS)], F32)

    mod_all = _ada(c, w_ada, b_ada).reshape(depth, bsz, 6, d)
    w_router_t = w_router.T
    router_bias = router_bias.reshape(N_EXPERTS, 1)

    xf = x.reshape(n, d)
    for l in range(depth):
        mod = mod_all[l]
        lq1, lk1, lq2, lk2 = da_lambda[l].astype(F32)
        lam_init = 0.8 - 0.6 * math.exp(-0.3 * l)
        lam = jnp.exp(jnp.sum(lq1 * lk1)) - jnp.exp(jnp.sum(lq2 * lk2)) + lam_init
        scal = jnp.concatenate([jnp.stack([lam, jnp.asarray(1.0 - lam_init, F32)]), slopes])

        w_gates = jnp.concatenate([w_merge[l, br] for br in range(3)], axis=1).astype(BF16)
        y, gates, v_t = _proj(xf, mod, w_in[l].astype(BF16), w_gates, b_merge[l].reshape(1, 3 * d), seq, tm, blk)

        o_hg = _hgrn2(y, hg_lb[l].reshape(1, HG_W), hg_norm_g[l].reshape(1, HG_DV), bsz, seq)
        o_da = _attn(y, v_t, scal, da_subln_g[l].reshape(1, DA_DV), bsz, seq, blk)
        rw = dict(mu=rw_mu[l].reshape(1, -1), w0=rw_w0[l].reshape(1, -1), w_up=rw_w_up[l],
                  a0=rw_a0[l].reshape(1, -1), a_up=rw_a_up[l], g_up=rw_g_up[l],
                  k_k=rw_k_k[l].reshape(1, -1), k_a=rw_k_a[l].reshape(1, -1), r_k=rw_r_k[l].reshape(1, -1),
                  gn_g=rw_gn_g[l].reshape(1, -1), gn_b=rw_gn_b[l].reshape(1, -1))
        o_rw = _rwkv(y, rw, bsz, seq)

        x1, u2, route, counts = _merge(o_hg, o_da, o_rw, gates, xf, mod, w_branch[l].astype(BF16),
                                       w_out[l].astype(BF16), ln_g[l, 0].reshape(1, d), ln_b[l, 0].reshape(1, d),
                                       w_router_t, router_bias, seq, tm)
        xf = _moe(u2, route, counts[:, :, 0].astype(jnp.int32), (w_exp_gate, w_exp_up, w_exp_down), l, x1, mod,
                  ln_g[l, 1].reshape(1, d), ln_b[l, 1].reshape(1, d), seq, tm)
    return xf.reshape(bsz, seq, d)
```

```python
import functools
import math

import jax
import jax.numpy as jnp
from jax import lax
from jax.experimental import pallas as pl
from jax.experimental.pallas import tpu as pltpu

D_MODEL = 1024
DEPTH = 4
HG_HEADS, HG_DK, HG_DV, HG_CHUNK, HG_SUB = 4, 128, 128, 64, 16
HG_W = HG_HEADS * HG_DV
HG_F_MIN = 1e-6
HG_STEP_SEQS = 2
HG_SAFE_SPAN = 60.0
DA_HEADS, DA_DQK = 4, 64
DA_DV = 2 * DA_DQK
DA_W = DA_HEADS * DA_DV
MASK_VALUE = -1e30
LOG2E = math.log2(math.e)
PROJ_TN = 768
RW_HEADS, RW_DH, RW_CHUNK, RW_SUB = 8, 64, 64, 16
RW_STEP_CHUNKS = 2
RW_STEP_SEQS = 2
RW_W = RW_HEADS * RW_DH
RW_IN_W = 1792
RW_GN_EPS = 64e-5
IN_W = 5376
HG_COL, DA_COL, RW_COL = 0, 2048, 3584
N_EXPERTS, N_GROUPS, EXPERTS_PER_GROUP, D_EXPERT = 16, 4, 4, 512
MOE_CHUNK = 16
MOE_TM = 512
ALPHA = (2.0 * DEPTH) ** 0.25
LN_EPS = 1e-5
RMS_EPS = 1e-6
LANES = 128

F32 = jnp.float32
BF16 = jnp.bfloat16
HIGHEST = lax.Precision.HIGHEST
VMEM_LIMIT = 48 * 1024 * 1024

_NT = (((1,), (1,)), ((), ()))
_TN = (((0,), (0,)), ((), ()))


def _mm(a, b):
    return jnp.dot(a.astype(BF16), b.astype(BF16), preferred_element_type=F32)


def _mm_nt(a, b):
    return lax.dot_general(a.astype(BF16), b.astype(BF16), _NT, preferred_element_type=F32)


def _mmh(a, b):
    return jnp.dot(a, b, precision=HIGHEST, preferred_element_type=F32)


def _seg_sum(x, seg):
    rows = x.shape[0]
    hi = x.astype(BF16)
    lo = (x - hi.astype(F32)).astype(BF16)
    halves = []
    for c0 in range(0, x.shape[1], seg.shape[0]):
        cols = slice(c0, c0 + seg.shape[0])
        both = jnp.dot(jnp.concatenate([hi[:, cols], lo[:, cols]], axis=0), seg, preferred_element_type=F32)
        halves.append(both[:rows] + both[rows:])
    return jnp.concatenate(halves, axis=1)


def _split_mm(x, w_twice, w_lo):
    hi = x.astype(BF16)
    lo = (x - hi.astype(F32)).astype(BF16)
    return (jnp.dot(jnp.concatenate([hi, lo], axis=1), w_twice, preferred_element_type=F32)
            + jnp.dot(hi, w_lo, preferred_element_type=F32))


def _chunk_cumsum(x):
    c = x.shape[0]
    hi = x.astype(BF16)
    rest = x - hi.astype(F32)
    mid = rest.astype(BF16)
    lo = (rest - mid.astype(F32)).astype(BF16)
    col = _iota((c, 4 * c), 1)
    tri = ((col & (c - 1)) <= _iota((c, 4 * c), 0)) & (col < 3 * c)
    return jnp.dot(jnp.where(tri, 1.0, 0.0).astype(BF16), jnp.concatenate([hi, mid, lo, lo], axis=0),
                   preferred_element_type=F32)


def _sigmoid(x):
    return 1.0 / (1.0 + jnp.exp(-x))


def _softplus(x):
    return jnp.maximum(x, 0.0) + jnp.log(1.0 + jnp.exp(-jnp.abs(x)))


def _iota(shape, dim):
    return lax.broadcasted_iota(jnp.int32, shape, dim)


def _params(*sem):
    return pltpu.CompilerParams(dimension_semantics=sem, vmem_limit_bytes=VMEM_LIMIT)


def _layer_norm(y, g, b):
    mu = jnp.mean(y, axis=-1, keepdims=True)
    d = y - mu
    var = jnp.mean(d * d, axis=-1, keepdims=True)
    return d * lax.rsqrt(var + LN_EPS) * g + b


def _ada_body(c_ref, w_ref, b_ref, o_ref):
    c = c_ref[...]
    o_ref[0] = _mmh(c * _sigmoid(c), w_ref[0]) + b_ref[0]


def _ada(c, w_ada, b_ada):
    depth, d, _ = w_ada.shape
    bsz = c.shape[0]
    return pl.pallas_call(
        _ada_body,
        grid=(depth, 6),
        in_specs=[pl.BlockSpec((bsz, d), lambda l, j: (0, 0)),
                  pl.BlockSpec((1, d, d), lambda l, j: (l, 0, j)),
                  pl.BlockSpec((1, 1, d), lambda l, j: (l, 0, j))],
        out_specs=pl.BlockSpec((1, bsz, d), lambda l, j: (l, 0, j)),
        out_shape=jax.ShapeDtypeStruct((depth, bsz, 6 * d), F32),
        compiler_params=_params("arbitrary", "arbitrary"),
        name="ada",
    )(c, w_ada, b_ada.reshape(depth, 1, 6 * d))


def _proj_body(x_ref, mod_ref, win_ref, wg_ref, bg_ref, y_ref, g_ref, vt_ref):
    u = (x_ref[...] * (1.0 + mod_ref[0, 1:2, :]) + mod_ref[0, 0:1, :]).astype(BF16)
    n_kb, kb = vt_ref.shape[1], vt_ref.shape[4]
    v_col = DA_COL + 2 * DA_W
    for c0 in range(0, IN_W, PROJ_TN):
        cols = slice(c0, c0 + PROJ_TN)
        res = jnp.dot(u, win_ref[:, cols], preferred_element_type=F32)
        y_ref[:, cols] = res.astype(BF16)
        if c0 == v_col:
            for h in range(DA_HEADS):
                slope2 = 2.0 ** (-8.0 * (h + 1) / DA_HEADS) * LOG2E
                key_w = jnp.exp2(slope2 * (_iota((1, res.shape[0]), 1) & (kb - 1)).astype(F32))
                v_t = (res[:, h * DA_DV:(h + 1) * DA_DV].T * key_w).astype(BF16)
                tail = jnp.where(_iota((8, res.shape[0]), 0) == 0, key_w, 0.0).astype(BF16)
                for kbi in range(n_kb):
                    keys = slice(kbi * kb, (kbi + 1) * kb)
                    vt_ref[0, kbi, h, 0:DA_DV, :] = v_t[:, keys]
                    vt_ref[0, kbi, h, DA_DV:DA_DV + 8, :] = tail[:, keys]
    for c0 in range(0, 3 * D_MODEL, PROJ_TN):
        cols = slice(c0, c0 + PROJ_TN)
        g = jnp.dot(u, wg_ref[:, cols], preferred_element_type=F32) + bg_ref[:, cols]
        g_ref[:, cols] = _sigmoid(g).astype(BF16)


def _proj(x, mod, w_in, w_gates, b_gates, seq, tm, kb):
    n, d = x.shape
    per_seq = seq // tm
    tile = lambda i: (i, 0)
    resident = dict(index_map=lambda i: (0, 0), pipeline_mode=pl.Buffered(1))
    vt_shape = (n // seq, seq // kb, DA_HEADS, DA_DV + 8, kb)
    return pl.pallas_call(
        _proj_body,
        grid=(n // tm,),
        in_specs=[pl.BlockSpec((tm, d), tile),
                  pl.BlockSpec((1, 6, d), lambda i: (i // per_seq, 0, 0)),
                  pl.BlockSpec(w_in.shape, **resident),
                  pl.BlockSpec(w_gates.shape, **resident),
                  pl.BlockSpec(b_gates.shape, **resident)],
        out_specs=[pl.BlockSpec((tm, IN_W), tile), pl.BlockSpec((tm, 3 * d), tile),
                   pl.BlockSpec((1, tm // kb) + vt_shape[2:], lambda i: (i // per_seq, i % per_seq, 0, 0, 0))],
        out_shape=[jax.ShapeDtypeStruct((n, IN_W), BF16), jax.ShapeDtypeStruct((n, 3 * d), BF16),
                   jax.ShapeDtypeStruct(vt_shape, BF16)],
        compiler_params=_params("arbitrary"),
        name="proj",
    )(x, mod, w_in, w_gates, b_gates)


def _hgrn2_body(y_ref, lb_ref, ng_ref, o_ref, st_ref):
    c, sub = HG_CHUNK, HG_SUB
    n_seq = y_ref.shape[0]

    @pl.when(pl.program_id(1) == 0)
    def _():
        st_ref[...] = jnp.zeros_like(st_ref)

    def part(k):
        return jnp.concatenate([y_ref[si, :, k * HG_W:(k + 1) * HG_W] for si in range(n_seq)], axis=0)

    q = part(0).astype(F32)
    z = part(1).astype(F32)
    lb = lb_ref[...]
    f = lb + (1.0 - lb) * _sigmoid(z)
    kin = (1.0 - lb) * _sigmoid(-z)
    logf = jnp.log(jnp.maximum(f, HG_F_MIN))
    b = jnp.concatenate([_chunk_cumsum(logf[si * c:(si + 1) * c]) for si in range(n_seq)], axis=0)

    ones = jnp.ones((HG_DK, LANES), BF16)
    row_s = _iota((sub, c), 0)
    col_s = _iota((sub, c), 1)
    blks = range(c // sub)
    hs = [slice(h * HG_DK, (h + 1) * HG_DK) for h in range(HG_HEADS)]
    units = [(si, h) for si in range(n_seq) for h in range(HG_HEADS)]

    betas = {(si, blk): (b[si * c + blk * sub - 1:si * c + blk * sub] if blk else jnp.zeros((1, HG_W), F32))
             for si in range(n_seq) for blk in blks}
    span = functools.reduce(jnp.maximum, [beta - b[si * c + (blk + 1) * sub - 1:si * c + (blk + 1) * sub]
                                          for (si, blk), beta in betas.items()])
    safe = jnp.max(span) <= HG_SAFE_SPAN

    def block_products(cap):
        prods = {}
        for si, h in units:
            keys = slice(si * c, (si + 1) * c)
            for blk in blks:
                rows = slice(si * c + blk * sub, si * c + (blk + 1) * sub)
                beta = betas[si, blk][:, hs[h]]
                q_t = q[rows, hs[h]] * jnp.exp(b[rows, hs[h]] - beta)
                k_h = kin[keys, hs[h]] * jnp.exp(jnp.minimum(beta - b[keys, hs[h]], cap))
                prods[si, h, blk] = _mm_nt(q_t, k_h)
        return prods

    def finish(prods, diag_sums):
        scores = []
        for si, h in units:
            a_rows = []
            for blk in blks:
                r0 = blk * sub
                a_blk = prods[si, h, blk]
                if diag_sums is not None:
                    own = jnp.zeros((sub, c), F32)
                    for s in range(sub):
                        own = jnp.where(col_s == r0 + s, diag_sums(si, h, blk, s), own)
                    a_blk = jnp.where(col_s < r0, a_blk, own)
                a_rows.append(jnp.where(col_s <= row_s + r0, a_blk, 0.0))
            scores.append(jnp.concatenate(a_rows, axis=0).astype(BF16))
        v = part(2)
        og = part(3).astype(F32)
        b_last = [b[(si + 1) * c - 1:(si + 1) * c] for si in range(n_seq)]
        q_dec = (q * jnp.exp(b)).astype(BF16)
        k_tail = (kin * jnp.exp(jnp.concatenate([jnp.broadcast_to(x, (c, HG_W)) for x in b_last], axis=0) - b)
                  ).astype(BF16)
        st = [st_ref[si, h] for si, h in units]
        rows = [slice(si * c, (si + 1) * c) for si, _ in units]
        intra = [jnp.dot(sc, v[r, hs[h]], preferred_element_type=F32) for sc, r, (_, h) in zip(scores, rows, units)]
        inter = [lax.dot_general(q_dec[r, hs[h]], s.astype(BF16), _NT, preferred_element_type=F32)
                 for s, r, (_, h) in zip(st, rows, units)]
        for s, r, (si, h) in zip(st, rows, units):
            st_ref[si, h] = s * jnp.exp(b_last[si][:, hs[h]]) + lax.dot_general(
                v[r, hs[h]], k_tail[r, hs[h]], _TN, preferred_element_type=F32)
        for x, y, r, (si, h) in zip(intra, inter, rows, units):
            o = x + y
            o = o * lax.rsqrt(jnp.mean(o * o, axis=-1, keepdims=True) + RMS_EPS) * ng_ref[...]
            o_ref[si, :, hs[h]] = (o * (og[r, hs[h]] * _sigmoid(og[r, hs[h]]))).astype(o_ref.dtype)

    @pl.when(safe)
    def _():
        finish(block_products(HG_SAFE_SPAN), None)

    @pl.when(jnp.logical_not(safe))
    def _():
        log_k = jnp.log(kin)
        b2, rel2, lk2 = b * LOG2E, (log_k - b) * LOG2E, log_k * LOG2E
        half = sub // 2
        diag = {}
        for si, h in units:
            for blk in blks:
                rows = slice(si * c + blk * sub, si * c + (blk + 1) * sub)
                b_i, q_i, rel_i, lk_i = b2[rows, hs[h]], q[rows, hs[h]], rel2[rows, hs[h]], lk2[rows, hs[h]]
                terms = [q_i[t0:] * jnp.exp2(jnp.minimum(b_i[t0:] + rel_i[s:s + 1], lk_i[s:s + 1]))
                         for s in range(sub) for t0 in [0 if s < half else half]]
                w = jnp.concatenate(terms, axis=0).astype(BF16)
                diag[si, h, blk] = jnp.dot(w, ones, preferred_element_type=F32)

        def diag_sums(si, h, blk, s):
            sums = diag[si, h, blk]
            if s < half:
                return sums[s * sub:(s + 1) * sub, :c]
            start = half * sub + (s - half) * half
            return jnp.concatenate([jnp.zeros((half, c), F32), sums[start:start + half, :c]], axis=0)

        finish(block_products(0.0), diag_sums)


def _hgrn2(y, lb, norm_g, bsz, seq):
    n = y.shape[0]
    width = 4 * HG_W
    n_seq = HG_STEP_SEQS if bsz % HG_STEP_SEQS == 0 else 1
    out = pl.pallas_call(
        _hgrn2_body,
        grid=(bsz // n_seq, seq // HG_CHUNK),
        in_specs=[pl.BlockSpec((n_seq, HG_CHUNK, width), lambda b, c: (b, c, HG_COL // width)),
                  pl.BlockSpec((1, HG_W), lambda b, c: (0, 0)),
                  pl.BlockSpec((1, HG_DV), lambda b, c: (0, 0))],
        out_specs=pl.BlockSpec((n_seq, HG_CHUNK, HG_W), lambda b, c: (b, c, 0)),
        out_shape=jax.ShapeDtypeStruct((bsz, seq, HG_W), BF16),
        scratch_shapes=[pltpu.VMEM((n_seq, HG_HEADS, HG_DV, HG_DK), F32)],
        compiler_params=_params("arbitrary", "arbitrary"),
        name="hgrn2",
    )(y.reshape(bsz, seq, -1), lb, norm_g)
    return out.reshape(n, HG_W)


def _attn_body(scal_ref, q_ref, k_ref, vt_ref, g_ref, o_ref, qq_ref, sa_ref, sb_ref, p_ref, m_ref, sc_ref,
               acc_ref, *, kb):
    h = pl.program_id(1)
    i = pl.program_id(2)
    lam = scal_ref[0]
    out_scale = scal_ref[1]
    slope = scal_ref[2 + h] * LOG2E
    qb = 2 * kb
    q0 = i * qb
    tiles_per_map = qb // LANES
    a_tiles = kb // LANES

    q = q_ref[...].astype(F32) * (DA_DQK ** -0.5 * LOG2E)
    lane = _iota(q.shape, 1)
    stacked = jnp.concatenate([jnp.where(lane < DA_DQK, q, 0.0), jnp.where(lane >= DA_DQK, q, 0.0)], axis=0)
    qq_ref[...] = stacked.T.astype(BF16)
    m_ref[...] = jnp.full(m_ref.shape, MASK_VALUE, F32)
    sc_ref[...] = jnp.ones(sc_ref.shape, F32)
    acc_ref[...] = jnp.zeros(acc_ref.shape, F32)
    p_ref[...] = jnp.zeros(p_ref.shape, BF16)
    key_off = _iota((kb, LANES), 0)

    def scores(j):
        return jnp.dot(k_ref[pl.ds(j * kb, kb), :], qq_ref[...], preferred_element_type=F32)

    def softmax(j, src_ref, diagonal_of):
        block_bias = slope * (j * kb - q0).astype(F32)
        for t in range(2 * tiles_per_map):
            cols = slice(t * LANES, (t + 1) * LANES)
            in_map = t % tiles_per_map
            half = "A" if in_map < a_tiles else "B"
            if diagonal_of == "B" and half == "A":
                p_ref[t] = jnp.zeros((kb, LANES), BF16)
                sc_ref[:, cols] = jnp.ones((1, LANES), F32)
                continue
            s = src_ref[t]
            if diagonal_of == half:
                q_off = _iota((kb, LANES), 1) + (in_map % a_tiles) * LANES
                s = jnp.where(key_off <= q_off, s, MASK_VALUE)
            m_old = m_ref[:, cols]
            m_new = jnp.maximum(m_old, jnp.max(s, axis=0, keepdims=True) + block_bias)
            p_ref[t] = jnp.exp2(s - (m_new - block_bias)).astype(BF16)
            sc_ref[:, cols] = jnp.exp2(m_old - m_new)
            m_ref[:, cols] = m_new

    n_tiles = 2 * tiles_per_map

    def park(dst_ref, s):
        for t in range(n_tiles):
            dst_ref[t] = s[:, t * LANES:(t + 1) * LANES]

    def probabilities():
        return jnp.concatenate([p_ref[t] for t in range(n_tiles)], axis=1)

    def iteration(j, src_ref, dst_ref, diagonal_of=None):
        sc_prev = sc_ref[...]
        pv = jnp.dot(vt_ref[0, jnp.maximum(j - 1, 0), 0], probabilities(), preferred_element_type=F32)
        if dst_ref is not None:
            park(dst_ref, scores(j + 1))
        for t in range(n_tiles):
            cols = slice(t * LANES, (t + 1) * LANES)
            acc_ref[t] = acc_ref[t] * sc_prev[:, cols] + pv[:, cols]
        softmax(j, src_ref, diagonal_of)

    def pair(j0):
        iteration(j0, sa_ref, sb_ref)
        iteration(j0 + 1, sb_ref, sa_ref)

    def body(quad, carry):
        pair(4 * quad)
        pair(4 * quad + 2)
        return carry

    park(sa_ref, scores(0))
    lax.fori_loop(0, i // 2, body, 0)

    @pl.when(i % 2 == 1)
    def _():
        pair(2 * i - 2)

    iteration(2 * i, sa_ref, sb_ref, "A")
    iteration(2 * i + 1, sb_ref, None, "B")
    acc = (jnp.concatenate([acc_ref[t] for t in range(n_tiles)], axis=1) * sc_ref[...]
           + jnp.dot(vt_ref[0, 2 * i + 1, 0], probabilities(), preferred_element_type=F32))
    o = acc[:DA_DV] / acc[DA_DV:DA_DV + 1]
    d = o[:, :qb] - lam * o[:, qb:]
    g = jnp.concatenate([g_ref[...]] * tiles_per_map, axis=1)
    d = d * lax.rsqrt(jnp.mean(d * d, axis=0, keepdims=True) + RMS_EPS) * g * out_scale
    o_ref[...] = d.T.astype(o_ref.dtype)


def _attn(y, v_t, scal, subln_g, bsz, seq, kb):
    n = y.shape[0]
    qb = 2 * kb
    nq, nk = seq // qb, seq // kb
    qc, kc = DA_COL // DA_DV, (DA_COL + DA_W) // DA_DV
    rows_v = v_t.shape[3]
    g_col = jnp.broadcast_to(subln_g.reshape(DA_DV, 1), (DA_DV, LANES))
    row = (1, 2 * qb)
    return pl.pallas_call(
        functools.partial(_attn_body, kb=kb),
        scratch_shapes=[pltpu.VMEM((DA_DV, 2 * qb), BF16), pltpu.VMEM((2 * qb // LANES, kb, LANES), F32),
                        pltpu.VMEM((2 * qb // LANES, kb, LANES), F32),
                        pltpu.VMEM((2 * qb // LANES, kb, LANES), BF16), pltpu.VMEM(row, F32),
                        pltpu.VMEM(row, F32), pltpu.VMEM((2 * qb // LANES, rows_v, LANES), F32)],
        grid=(bsz, DA_HEADS, nq),
        in_specs=[pl.BlockSpec(memory_space=pltpu.SMEM),
                  pl.BlockSpec((qb, DA_DV), lambda b, h, i: (b * nq + i, qc + h)),
                  pl.BlockSpec((seq, DA_DV), lambda b, h, i: (b, kc + h)),
                  pl.BlockSpec((1, nk, 1, rows_v, kb), lambda b, h, i: (b, 0, h, 0, 0)),
                  pl.BlockSpec((DA_DV, LANES), lambda b, h, i: (0, 0))],
        out_specs=pl.BlockSpec((qb, DA_DV), lambda b, h, i: (b * nq + i, h)),
        out_shape=jax.ShapeDtypeStruct((n, DA_W), BF16),
        compiler_params=_params("arbitrary", "arbitrary", "arbitrary"),
        name="diffattn",
    )(scal, y, y, v_t, g_col)


def _split_f32(x):
    hi = x.astype(BF16)
    hi_f = hi.astype(F32)
    return hi, hi_f, x - hi_f


def _dup_lhs(hi_f, lo_f, low_half):
    packed = jnp.where(low_half, hi_f, lo_f).astype(BF16)
    return jnp.concatenate([packed, packed], axis=1)


def _dup_rhs(hi, lo_f):
    lo = lo_f.astype(BF16)
    return jnp.concatenate([hi, hi, lo, lo], axis=0)


def _rwkv_body(y_ref, mu_ref, w0_ref, a0_ref, wa2_ref, walo_ref, gu2_ref, gulo_ref, kk_ref, ka_ref, rk_ref,
               gng_ref, gnb_ref, seg_ref, o_ref, st_ref, prev_ref, osc_ref):
    c, sub, dh = RW_CHUNK, RW_SUB, RW_DH

    @pl.when(pl.program_id(1) == 0)
    def _():
        st_ref[...] = jnp.zeros_like(st_ref)
        prev_ref[...] = jnp.zeros_like(prev_ref)

    n_seq, seq_rows = y_ref.shape[0], y_ref.shape[1]
    rows = n_seq * seq_rows
    shifted = []
    for si in range(n_seq):
        x = y_ref[si].astype(F32)
        x_prev = jnp.where(_iota(x.shape, 0) == 0, prev_ref[si:si + 1], pltpu.roll(x, 1, axis=0))
        prev_ref[si:si + 1] = x[seq_rows - 1:seq_rows]
        shifted.append(x + (x_prev - x) * mu_ref[...])
    xs = jnp.concatenate(shifted, axis=0)
    r = xs[:, 0:RW_W]
    k = xs[:, RW_W:2 * RW_W]
    v = xs[:, 2 * RW_W:3 * RW_W]
    wa = xs[:, 3 * RW_W:3 * RW_W + LANES]
    gd = xs[:, 3 * RW_W + LANES:RW_IN_W]

    lora = _split_mm(jnp.where(_iota(wa.shape, 1) < 64, jnp.tanh(wa), wa), wa2_ref[...], walo_ref[...])
    w_log = -_softplus(-(w0_ref[...] + lora[:, :RW_W])) - 0.5
    g = -jnp.exp(w_log)
    a = _sigmoid(a0_ref[...] + lora[:, RW_W:])
    gate = _split_mm(_sigmoid(gd), gu2_ref[...], gulo_ref[...])
    seg = seg_ref[...]
    kk = k * kk_ref[...]
    k2 = k * (1.0 + (a - 1.0) * ka_ref[...])
    sums = _seg_sum(jnp.concatenate([kk * kk, r * k2 * rk_ref[...]], axis=0), seg)
    kk = kk * lax.rsqrt(jnp.maximum(sums[:rows], 1e-12))
    bb = kk * a
    bonus = sums[rows:] * v

    chunks = range(rows // c)
    cr = [slice(ci * c, (ci + 1) * c) for ci in chunks]
    gc = jnp.concatenate([_chunk_cumsum(g[s]) for s in cr], axis=0)
    g_last = [gc[s][c - 1:c] for s in cr]
    e_inv = jnp.exp(-gc)
    e_tail = jnp.exp(jnp.concatenate([jnp.broadcast_to(gl, (c, RW_W)) for gl in g_last], axis=0) - gc)
    gam = [jnp.exp(gl) for gl in g_last]
    a_t = (-kk * jnp.exp(gc - g)).astype(BF16)
    r_t = (r * jnp.exp(gc)).astype(BF16)
    b_h = (bb * e_inv).astype(BF16)
    k_h = (k2 * e_inv).astype(BF16)
    k_bar = (k2 * e_tail).astype(BF16)
    b_bar = (bb * e_tail).astype(BF16)
    v_bf = v.astype(BF16)

    row2 = _iota((c, 2 * c), 0)
    lane2 = _iota((c, 2 * c), 1)
    col2 = lane2 & (c - 1)
    low_half = lane2 < c
    strict = row2 > col2
    incl = row2 >= col2
    same_blk = (row2 // sub) == (col2 // sub)
    eye = (row2 == col2).astype(F32)

    def dot(p, q):
        return jnp.dot(p, q, preferred_element_type=F32)

    heads = range(RW_HEADS)
    sls = [slice(h * dh, (h + 1) * dh) for h in heads]
    items = [(ci, h) for ci in chunks for h in heads]
    ar_h = [jnp.concatenate([a_t[cr[ci], sls[h]], r_t[cr[ci], sls[h]]], axis=0) for ci, h in items]
    quad = [lax.dot_general(x, jnp.concatenate([b_h[cr[ci], sls[h]], k_h[cr[ci], sls[h]]], axis=0), _NT,
                            preferred_element_type=F32)
            for x, (ci, h) in zip(ar_h, items)]
    top = [jnp.where(strict, q[:c], 0.0) for q in quad]
    a_ak = [t[:, c:].astype(BF16) for t in top]
    a_r = [jnp.where(incl, q[c:], 0.0).astype(BF16) for q in quad]
    a_ab = [jnp.where(low_half, t, pltpu.roll(t, c, axis=1)) for t in top]
    a_d = [jnp.where(same_blk, x, 0.0) for x in a_ab]
    a_o = [(x - y).astype(BF16) for x, y in zip(a_ab, a_d)]

    s1 = [_split_f32(x) for x in a_d]
    p2 = [dot(_dup_lhs(hf, lf, low_half), _dup_rhs(hi, lf)) for hi, hf, lf in s1]
    s2 = [_split_f32(x) for x in p2]
    rhs2 = [_dup_rhs(hi, lf) for hi, _, lf in s2]
    p4 = [dot(_dup_lhs(hf, lf, low_half), rhs) for (_, hf, lf), rhs in zip(s2, rhs2)]
    s4 = [_split_f32(x) for x in p4]
    rhs4 = [_dup_rhs(hi, lf) for hi, _, lf in s4]
    p8 = [dot(_dup_lhs(hf, lf, low_half), rhs) for (_, hf, lf), rhs in zip(s4, rhs4)]
    rhs8 = [_dup_rhs(hi, lf) for hi, _, lf in (_split_f32(x) for x in p8)]
    t_d = [eye + x for x in a_d]
    for rhs_all in (rhs2, rhs4, rhs8):
        st = [_split_f32(x) for x in t_d]
        t_d = [x + dot(_dup_lhs(hf, lf, low_half), rhs) for x, (_, hf, lf), rhs in zip(t_d, st, rhs_all)]
    t_d = [x.astype(BF16) for x in t_d]

    nn = [dot(t[:, :c], x) for t, x in zip(t_d, a_o)]
    nn_bf = [x.astype(BF16) for x in nn]
    n2 = [dot(x[:, :c], x) for x in nn_bf]
    n3 = [dot(x[:, :c], y.astype(BF16)) for x, y in zip(nn_bf, n2)]
    t_m = [dot((eye + x + y + z).astype(BF16)[:, :c], t).astype(BF16)[:, :c]
           for x, y, z, t in zip(nn, n2, n3, t_d)]

    v_h = [v_bf[cr[ci], sls[h]] for ci, h in items]
    akv = [dot(x, y).astype(BF16) for x, y in zip(a_ak, v_h)]
    at_m = [dot(t, x[:c]).astype(BF16) for t, x in zip(t_m, ar_h)]
    v_p = [dot(t, x) for t, x in zip(t_m, akv)]

    per_seq = seq_rows // c
    chains = [(si, h) for si in range(n_seq) for h in heads]
    state = [st_ref[si, h] for si, h in chains]
    for t in range(per_seq):
        it = [(si * per_seq + t) * RW_HEADS + h for si, h in chains]
        ck = [cr[si * per_seq + t] for si, _ in chains]
        proj = [lax.dot_general(jnp.concatenate([at_m[j], ar_h[j][c:]], axis=0), s.astype(BF16), _NT,
                                preferred_element_type=F32) for j, s in zip(it, state)]
        u = [(p[:c] + v_p[j]).astype(BF16) for p, j in zip(proj, it)]
        for p, uu, j, rws, (_, h) in zip(proj, u, it, ck, chains):
            osc_ref[rws, sls[h]] = p[c:] + dot(a_r[j], jnp.concatenate([uu, v_h[j]], axis=0))
        state = [s * gam[j // RW_HEADS][:, sls[h]] + lax.dot_general(
            jnp.concatenate([v_h[j], uu], axis=0),
            jnp.concatenate([k_bar[rws, sls[h]], b_bar[rws, sls[h]]], axis=0), _TN, preferred_element_type=F32)
            for s, uu, j, rws, (_, h) in zip(state, u, it, ck, chains)]
    for s, (si, h) in zip(state, chains):
        st_ref[si, h] = s

    o = osc_ref[...]
    mean = _seg_sum(o, seg) * (1.0 / dh)
    d = o - mean
    var = _seg_sum(d * d, seg) * (1.0 / dh)
    o = d * lax.rsqrt(var + RW_GN_EPS) * gng_ref[...] + gnb_ref[...]
    o = ((o + bonus) * gate).astype(o_ref.dtype)
    for si in range(n_seq):
        o_ref[si] = o[si * seq_rows:(si + 1) * seq_rows]


def _rwkv(y, p, bsz, seq):
    n = y.shape[0]
    half = RW_W // 2
    seg = (_iota((half, half), 0) // RW_DH == _iota((half, half), 1) // RW_DH).astype(BF16)

    def two_terms(w):
        hi = w.astype(BF16)
        return jnp.concatenate([hi, hi], axis=0), (w - hi.astype(F32)).astype(BF16)

    zeros = jnp.zeros_like(p["w_up"])
    wa2, wa_lo = two_terms(jnp.concatenate([jnp.concatenate([p["w_up"], zeros], axis=1),
                                            jnp.concatenate([zeros, p["a_up"]], axis=1)], axis=0))
    gu2, gu_lo = two_terms(p["g_up"])
    rows = [p["mu"], p["w0"], p["a0"], wa2, wa_lo, gu2, gu_lo, p["k_k"], p["k_a"], p["r_k"],
            p["gn_g"], p["gn_b"], seg]
    full = lambda b, c: (0, 0)
    step = RW_STEP_CHUNKS * RW_CHUNK
    n_seq = RW_STEP_SEQS if bsz % RW_STEP_SEQS == 0 else 1
    out = pl.pallas_call(
        _rwkv_body,
        grid=(bsz // n_seq, seq // step),
        in_specs=[pl.BlockSpec((n_seq, step, RW_IN_W), lambda b, c: (b, c, RW_COL // RW_IN_W))]
        + [pl.BlockSpec(a.shape, full) for a in rows],
        out_specs=pl.BlockSpec((n_seq, step, RW_W), lambda b, c: (b, c, 0)),
        out_shape=jax.ShapeDtypeStruct((bsz, seq, RW_W), BF16),
        scratch_shapes=[pltpu.VMEM((n_seq, RW_HEADS, RW_DH, RW_DH), F32),
                        pltpu.VMEM((n_seq, RW_IN_W), F32),
                        pltpu.VMEM((n_seq * step, RW_W), F32)],
        compiler_params=_params("arbitrary", "arbitrary"),
        name="rwkv7",
    )(y.reshape(bsz, seq, -1), *rows)
    return out.reshape(n, RW_W)


def _first_argmax(vals, row):
    top = jnp.max(vals, axis=0, keepdims=True)
    idx = jnp.min(jnp.where(vals == top, row, N_EXPERTS), axis=0, keepdims=True)
    return top, idx


def _merge_body(ohg_ref, oda_ref, orw_ref, gt_ref, x_ref, mod_ref, wb_ref, wo_ref, lng_ref, lnb_ref,
                wrt_ref, rb_ref, tri_ref, tri16_ref, x1_ref, u2_ref, route_ref, cnt_ref):
    d = D_MODEL
    sub_rows = tri_ref.shape[0]

    def route(rows):
        merged = (gt_ref[rows, 0:d].astype(F32)
                  * jnp.dot(ohg_ref[rows, :], wb_ref[0:HG_W, :], preferred_element_type=F32)
                  + gt_ref[rows, d:2 * d].astype(F32)
                  * jnp.dot(oda_ref[rows, :], wb_ref[HG_W:HG_W + DA_W, :], preferred_element_type=F32)
                  + gt_ref[rows, 2 * d:3 * d].astype(F32)
                  * jnp.dot(orw_ref[rows, :], wb_ref[HG_W + DA_W:, :], preferred_element_type=F32))
        mix = _mm(merged, wo_ref[...])
        x1 = _layer_norm(ALPHA * x_ref[rows, :] + (1.0 + mod_ref[0, 2:3, :]) * mix, lng_ref[...], lnb_ref[...])
        x1_ref[rows, :] = x1
        u2 = x1 * (1.0 + mod_ref[0, 4:5, :]) + mod_ref[0, 3:4, :]
        u2_hi = u2.astype(BF16)
        u2_ref[rows, :] = u2_hi

        u2_lo = (u2 - u2_hi.astype(F32)).astype(BF16)
        two = lax.dot_general(wrt_ref[...], u2_hi, _NT, preferred_element_type=F32)
        logits = (two[:N_EXPERTS] + two[N_EXPERTS:]
                  + lax.dot_general(wrt_ref[0:N_EXPERTS, :], u2_lo, _NT, preferred_element_type=F32))
        ex = jnp.exp(logits - jnp.max(logits, axis=0, keepdims=True))
        scores = ex / jnp.sum(ex, axis=0, keepdims=True)
        sel = scores + rb_ref[...]
        row = _iota(sel.shape, 0)
        best = None
        for grp in range(N_GROUPS):
            a, b, c2, d2 = (sel[grp * EXPERTS_PER_GROUP + i:grp * EXPERTS_PER_GROUP + i + 1] for i in range(4))
            hi1, lo1, hi2, lo2 = jnp.maximum(a, b), jnp.minimum(a, b), jnp.maximum(c2, d2), jnp.minimum(c2, d2)
            top2 = jnp.maximum(hi1, hi2) + jnp.maximum(jnp.minimum(hi1, hi2), jnp.maximum(lo1, lo2))
            if best is None:
                best, best_grp = top2, jnp.zeros_like(top2, dtype=jnp.int32)
            else:
                better = top2 > best
                best = jnp.where(better, top2, best)
                best_grp = jnp.where(better, grp, best_grp)
        masked = jnp.where(row // EXPERTS_PER_GROUP == best_grp, sel, MASK_VALUE)
        _, idx1 = _first_argmax(masked, row)
        _, idx2 = _first_argmax(jnp.where(row == idx1, -jnp.inf, masked), row)
        pick1 = row == idx1
        pick2 = row == idx2
        w1 = jnp.sum(jnp.where(pick1, scores, 0.0), axis=0, keepdims=True)
        w2 = jnp.sum(jnp.where(pick2, scores, 0.0), axis=0, keepdims=True)
        onehot = jnp.where(pick1 | pick2, 1.0, 0.0)
        earlier = jnp.dot(onehot.astype(BF16), tri_ref[...], preferred_element_type=F32)
        return pick1, pick2, w1 / (w1 + w2), w2 / (w1 + w2), earlier, jnp.sum(onehot, axis=1, keepdims=True)

    parts = [route(slice(r0, r0 + sub_rows)) for r0 in range(0, x_ref.shape[0], sub_rows)]

    cnt = functools.reduce(jnp.add, [p[5] for p in parts])
    chunks = jnp.floor((cnt + (MOE_CHUNK - 1)) * (1.0 / MOE_CHUNK))
    seg_start = MOE_CHUNK * jnp.dot(tri16_ref[...], jnp.broadcast_to(chunks, (N_EXPERTS, LANES)).astype(BF16),
                                    preferred_element_type=F32)[:, 0:1]
    before = jnp.zeros_like(cnt)
    for k, (pick1, pick2, wn1, wn2, earlier, cnt_k) in enumerate(parts):
        pos = seg_start + before + earlier
        pos1 = jnp.sum(jnp.where(pick1, pos, 0.0), axis=0, keepdims=True)
        pos2 = jnp.sum(jnp.where(pick2, pos, 0.0), axis=0, keepdims=True)
        route_ref[:, k * sub_rows:(k + 1) * sub_rows] = jnp.concatenate(
            [pos1, pos2, wn1, wn2, jnp.zeros((4, sub_rows), F32)], axis=0)
        before = before + cnt_k
    cnt_ref[0] = jnp.broadcast_to(cnt, (N_EXPERTS, LANES))


def _merge(o_hg, o_da, o_rw, gates, x, mod, w_branch, w_out, ln_g, ln_b, w_router_t, router_bias, seq, tm):
    n, d = x.shape
    per_seq = seq // tm
    tile = lambda i: (i, 0)
    full = lambda i: (0, 0)
    sub = tm
    before =(_iota((sub, sub), 0) < _iota((sub, sub), 1)).astype(BF16)
    before16 = (_iota((N_EXPERTS, N_EXPERTS), 1) < _iota((N_EXPERTS, N_EXPERTS), 0)).astype(BF16)
    w_hi = w_router_t.astype(BF16)
    w_two = jnp.concatenate([w_hi, (w_router_t - w_hi.astype(F32)).astype(BF16)], axis=0)
    return pl.pallas_call(
        _merge_body,
        grid=(n // tm,),
        in_specs=[pl.BlockSpec((tm, HG_W), tile), pl.BlockSpec((tm, DA_W), tile), pl.BlockSpec((tm, RW_W), tile),
                  pl.BlockSpec((tm, 3 * d), tile), pl.BlockSpec((tm, d), tile),
                  pl.BlockSpec((1, 6, d), lambda i: (i // per_seq, 0, 0)),
                  pl.BlockSpec(w_branch.shape, full), pl.BlockSpec(w_out.shape, full),
                  pl.BlockSpec((1, d), full), pl.BlockSpec((1, d), full),
                  pl.BlockSpec((2 * N_EXPERTS, d), full), pl.BlockSpec((N_EXPERTS, 1), full),
                  pl.BlockSpec((sub, sub), full), pl.BlockSpec((N_EXPERTS, N_EXPERTS), full)],
        out_specs=[pl.BlockSpec((tm, d), tile), pl.BlockSpec((tm, d), tile), pl.BlockSpec((8, tm), lambda i: (0, i)),
                   pl.BlockSpec((1, N_EXPERTS, LANES), lambda i: (i, 0, 0))],
        out_shape=[jax.ShapeDtypeStruct((n, d), F32), jax.ShapeDtypeStruct((n, d), BF16),
                   jax.ShapeDtypeStruct((8, n), F32), jax.ShapeDtypeStruct((n // tm, N_EXPERTS, LANES), F32)],
        compiler_params=_params("arbitrary"),
        name="merge",
    )(o_hg, o_da, o_rw, gates, x, mod, w_branch, w_out, ln_g, ln_b, w_two, router_bias, before, before16)


def _local_rows(tm):
    return -(-(2 * tm + N_EXPERTS * (MOE_CHUNK - 1)) // LANES) * LANES


def _token_columns(route):
    return jnp.concatenate([route, jnp.zeros((LANES - route.shape[0], route.shape[1]), F32)], axis=0).T


def _segment_copies(i, nch_ref, loc_ref, glob_ref, local_buf, global_buf, sem, to_global):
    def run(action):
        for e in range(N_EXPERTS):
            seg = i * N_EXPERTS + e
            loc0, glob0 = loc_ref[seg], glob_ref[seg]

            def one(c, carry):
                loc = local_buf.at[pl.ds(pl.multiple_of(loc0 + c * MOE_CHUNK, MOE_CHUNK), MOE_CHUNK), :]
                glob = global_buf.at[pl.ds(pl.multiple_of(glob0 + c * MOE_CHUNK, MOE_CHUNK), MOE_CHUNK), :]
                copy = pltpu.make_async_copy(loc, glob, sem) if to_global else pltpu.make_async_copy(glob, loc, sem)
                getattr(copy, action)()
                return carry

            lax.fori_loop(0, nch_ref[seg], one, 0)
    return run


def _dispatch_body(nch_ref, loc_ref, glob_ref, u_ref, route_ref, xs_in_ref, xs_ref, stage2_ref, sems):
    del xs_in_ref
    i = pl.program_id(0)
    last = pl.num_programs(0) - 1
    tm, d = u_ref.shape
    slot = i % 2
    stage_ref = stage2_ref.at[slot]

    def tile_copies(tile):
        return _segment_copies(tile, nch_ref, loc_ref, glob_ref, stage2_ref.at[tile % 2], xs_ref,
                               sems.at[tile % 2], to_global=True)

    @pl.when(i >= 2)
    def _():
        tile_copies(i - 2)("wait")

    route = route_ref[...]
    local_row = _iota((stage_ref.shape[0], tm), 0)
    take1 = local_row == route[0:1].astype(jnp.int32)
    take2 = local_row == route[1:2].astype(jnp.int32)
    perm = jnp.where(take1 | take2, 1.0, 0.0).astype(BF16)
    stage_ref[:, 0:d] = jnp.dot(perm, u_ref[...], preferred_element_type=F32).astype(BF16)

    cols = _token_columns(route)
    lane = _iota((tm, LANES), 1)

    def weight_cols(w):
        hi = w.astype(BF16).astype(F32)
        return jnp.where(lane == 0, hi, jnp.where(lane == 1, w - hi, 0.0)).astype(BF16)

    stage_ref[:, d:d + LANES] = (
        jnp.dot(jnp.where(take1, 1.0, 0.0).astype(BF16), weight_cols(cols[:, 2:3]), preferred_element_type=F32)
        + jnp.dot(jnp.where(take2, 1.0, 0.0).astype(BF16), weight_cols(cols[:, 3:4]), preferred_element_type=F32)
    ).astype(BF16)

    tile_copies(i)("start")

    @pl.when(i == last)
    def _():

        @pl.when(i >= 1)
        def _():
            tile_copies(i - 1)("wait")

        tile_copies(i)("wait")


def _dispatch(u2, route, nch, loc, glob, rows, tm):
    n, d = u2.shape
    width = d + LANES
    return pl.pallas_call(
        _dispatch_body,
        grid_spec=pltpu.PrefetchScalarGridSpec(
            num_scalar_prefetch=3,
            grid=(n // tm,),
            in_specs=[pl.BlockSpec((tm, d), lambda i, *_: (i, 0)),
                      pl.BlockSpec((8, tm), lambda i, *_: (0, i)),
                      pl.BlockSpec(memory_space=pl.ANY)],
            out_specs=pl.BlockSpec(memory_space=pl.ANY),
            scratch_shapes=[pltpu.VMEM((2, _local_rows(tm), width), BF16), pltpu.SemaphoreType.DMA((2,))],
        ),
        out_shape=jax.ShapeDtypeStruct((rows, width), BF16),
        input_output_aliases={5: 0},
        compiler_params=_params("arbitrary"),
        name="moe_dispatch",
    )(nch, loc, glob, u2, route, jnp.zeros((rows, width), BF16))


def _experts_body(te_ref, x_ref, wg_ref, wu_ref, wd_ref, y_ref, wgu_bf, wd_bf):
    g = pl.program_id(0)
    expert = te_ref[g]
    used = expert < N_EXPERTS

    @pl.when(jnp.logical_and(used, jnp.logical_or(g == 0, expert != te_ref[jnp.maximum(g - 1, 0)])))
    def _():
        wgu_bf[:, :D_EXPERT] = wg_ref[0, 0].astype(BF16)
        wgu_bf[:, D_EXPERT:] = wu_ref[0, 0].astype(BF16)
        wd_bf[...] = wd_ref[0, 0].astype(BF16)

    @pl.when(used)
    def _():
        d = wgu_bf.shape[0]
        weight = x_ref[:, d:d + 1].astype(F32) + x_ref[:, d + 1:d + 2].astype(F32)
        hidden = jnp.dot(x_ref[:, 0:d], wgu_bf[...], preferred_element_type=F32)
        hg = hidden[:, :D_EXPERT]
        act = hg * _sigmoid(hg) * hidden[:, D_EXPERT:] * weight
        y_ref[...] = _mm(act, wd_bf[...]).astype(BF16)

    @pl.when(jnp.logical_not(used))
    def _():
        y_ref[...] = jnp.zeros_like(y_ref)


def _experts(xs, tile_expert, w_gate, w_up, w_down, layer):
    rows, width = xs.shape
    d = w_down.shape[3]
    expert = lambda g, te: (layer, jnp.minimum(te[g], N_EXPERTS - 1), 0, 0)
    return pl.pallas_call(
        _experts_body,
        grid_spec=pltpu.PrefetchScalarGridSpec(
            num_scalar_prefetch=1,
            grid=(rows // MOE_TM,),
            in_specs=[pl.BlockSpec((MOE_TM, width), lambda g, te: (g, 0)),
                      pl.BlockSpec((1, 1, d, D_EXPERT), expert),
                      pl.BlockSpec((1, 1, d, D_EXPERT), expert),
                      pl.BlockSpec((1, 1, D_EXPERT, d), expert)],
            out_specs=pl.BlockSpec((MOE_TM, d), lambda g, te: (g, 0)),
            scratch_shapes=[pltpu.VMEM((d, 2 * D_EXPERT), BF16), pltpu.VMEM((D_EXPERT, d), BF16)],
        ),
        out_shape=jax.ShapeDtypeStruct((rows, d), BF16),
        compiler_params=_params("arbitrary"),
        name="moe_experts",
    )(tile_expert, xs, w_gate, w_up, w_down)


def _combine_body(nch_ref, loc_ref, glob_ref, route_ref, x1_ref, mod_ref, lng_ref, lnb_ref, ys_ref, o_ref,
                  back2_ref, sems):
    i = pl.program_id(0)

    def tile_copies(tile):
        return _segment_copies(tile, nch_ref, loc_ref, glob_ref, back2_ref.at[tile % 2], ys_ref,
                               sems.at[tile % 2], to_global=False)

    @pl.when(i == 0)
    def _():
        back2_ref[...] = jnp.zeros_like(back2_ref)
        tile_copies(i)("start")

    @pl.when(i + 1 < pl.num_programs(0))
    def _():
        tile_copies(i + 1)("start")

    cols = _token_columns(route_ref[...]).astype(jnp.int32)
    local_row = _iota((x1_ref.shape[0], back2_ref.shape[1]), 1)
    unperm = jnp.where((local_row == cols[:, 0:1]) | (local_row == cols[:, 1:2]), 1.0, 0.0).astype(BF16)
    tile_copies(i)("wait")
    ffn = jnp.dot(unperm, back2_ref[i % 2], preferred_element_type=F32)
    y = ALPHA * x1_ref[...] + (1.0 + mod_ref[0, 5:6, :]) * ffn
    o_ref[...] = _layer_norm(y, lng_ref[...], lnb_ref[...])


def _combine(ys, route, nch, loc, glob, x1, mod, ln_g, ln_b, seq, tm):
    n, d = x1.shape
    per_seq = seq // tm
    full = lambda i, *_: (0, 0)
    return pl.pallas_call(
        _combine_body,
        grid_spec=pltpu.PrefetchScalarGridSpec(
            num_scalar_prefetch=3,
            grid=(n // tm,),
            in_specs=[pl.BlockSpec((8, tm), lambda i, *_: (0, i)),
                      pl.BlockSpec((tm, d), lambda i, *_: (i, 0)),
                      pl.BlockSpec((1, 6, d), lambda i, *_: (i // per_seq, 0, 0)),
                      pl.BlockSpec((1, d), full), pl.BlockSpec((1, d), full),
                      pl.BlockSpec(memory_space=pl.ANY)],
            out_specs=pl.BlockSpec((tm, d), lambda i, *_: (i, 0)),
            scratch_shapes=[pltpu.VMEM((2, _local_rows(tm), d), BF16), pltpu.SemaphoreType.DMA((2,))],
        ),
        out_shape=jax.ShapeDtypeStruct((n, d), F32),
        compiler_params=_params("arbitrary"),
        name="moe_combine",
    )(nch, loc, glob, route, x1, mod, ln_g, ln_b, ys)


def _moe(u2, route, counts, expert_weights, layer, x1, mod, ln_g, ln_b, seq, tm):
    n = u2.shape[0]
    n_tiles = n // tm
    seg_rows = (counts + MOE_CHUNK - 1) // MOE_CHUNK * MOE_CHUNK
    loc = jnp.cumsum(seg_rows, axis=1) - seg_rows
    region = (jnp.sum(seg_rows, axis=0) + MOE_TM - 1) // MOE_TM * MOE_TM
    region_end = jnp.cumsum(region)
    glob = (region_end - region)[None, :] + jnp.cumsum(seg_rows, axis=0) - seg_rows
    rows = -(-(2 * n + n_tiles * N_EXPERTS * (MOE_CHUNK - 1) + N_EXPERTS * (MOE_TM - 1)) // MOE_TM) * MOE_TM
    tile_expert = jnp.sum(jnp.arange(rows // MOE_TM, dtype=jnp.int32)[:, None] * MOE_TM >= region_end[None, :],
                          axis=1).astype(jnp.int32)
    flat = lambda a: a.reshape(-1).astype(jnp.int32)
    nch, loc, glob = flat(seg_rows // MOE_CHUNK), flat(loc), flat(glob)
    xs = _dispatch(u2, route, nch, loc, glob, rows, tm)
    ys = _experts(xs, tile_expert, *expert_weights, layer)
    return _combine(ys, route, nch, loc, glob, x1, mod, ln_g, ln_b, seq, tm)


def _tiles(seq):
    return min(512, seq), min(256, seq // 2)


def kernel(x, c, w_ada, b_ada, w_in, hg_lb_logits, hg_norm_g, da_lambda, da_subln_g, rw_mu, rw_w0, rw_w_up,
           rw_a0, rw_a_up, rw_g_up, rw_k_k, rw_k_a, rw_r_k, rw_gn_g, rw_gn_b, w_merge, b_merge, w_branch, w_out,
           ln_g, ln_b, w_router, router_bias, w_exp_gate, w_exp_up, w_exp_down):
    bsz, seq, d = x.shape
    depth = w_in.shape[0]
    n = bsz * seq
    tm, blk = _tiles(seq)

    sm = jax.nn.softmax(hg_lb_logits.astype(F32), axis=0)
    hg_lb = jnp.cumsum(sm, axis=0) - sm[0:1]
    slopes = jnp.asarray([2.0 ** (-8.0 * (h + 1) / DA_HEADS) for h in range(DA_HEADS)], F32)

    mod_all = _ada(c, w_ada, b_ada).reshape(depth, bsz, 6, d)
    w_router_t = w_router.T
    router_bias = router_bias.reshape(N_EXPERTS, 1)

    xf = x.reshape(n, d)
    for l in range(depth):
        mod = mod_all[l]
        lq1, lk1, lq2, lk2 = da_lambda[l].astype(F32)
        lam_init = 0.8 - 0.6 * math.exp(-0.3 * l)
        lam = jnp.exp(jnp.sum(lq1 * lk1)) - jnp.exp(jnp.sum(lq2 * lk2)) + lam_init
        scal = jnp.concatenate([jnp.stack([lam, jnp.asarray(1.0 - lam_init, F32)]), slopes])

        w_gates = jnp.concatenate([w_merge[l, br] for br in range(3)], axis=1).astype(BF16)
        y, gates, v_t = _proj(xf, mod, w_in[l].astype(BF16), w_gates, b_merge[l].reshape(1, 3 * d), seq, tm, blk)

        o_hg = _hgrn2(y, hg_lb[l].reshape(1, HG_W), hg_norm_g[l].reshape(1, HG_DV), bsz, seq)
        o_da = _attn(y, v_t, scal, da_subln_g[l].reshape(1, DA_DV), bsz, seq, blk)
        rw = dict(mu=rw_mu[l].reshape(1, -1), w0=rw_w0[l].reshape(1, -1), w_up=rw_w_up[l],
                  a0=rw_a0[l].reshape(1, -1), a_up=rw_a_up[l], g_up=rw_g_up[l],
                  k_k=rw_k_k[l].reshape(1, -1), k_a=rw_k_a[l].reshape(1, -1), r_k=rw_r_k[l].reshape(1, -1),
                  gn_g=rw_gn_g[l].reshape(1, -1), gn_b=rw_gn_b[l].reshape(1, -1))
        o_rw = _rwkv(y, rw, bsz, seq)

        x1, u2, route, counts = _merge(o_hg, o_da, o_rw, gates, xf, mod, w_branch[l].astype(BF16),
                                       w_out[l].astype(BF16), ln_g[l, 0].reshape(1, d), ln_b[l, 0].reshape(1, d),
                                       w_router_t, router_bias, seq, tm)
        xf = _moe(u2, route, counts[:, :, 0].astype(jnp.int32), (w_exp_gate, w_exp_up, w_exp_down), l, x1, mod,
                  ln_g[l, 1].reshape(1, d), ln_b[l, 1].reshape(1, d), seq, tm)
    return xf.reshape(bsz, seq, d)
```

```python
import functools
import math

import jax
import jax.numpy as jnp
from jax import lax
from jax.experimental import pallas as pl
from jax.experimental.pallas import tpu as pltpu

D_MODEL = 1024
DEPTH = 4
HG_HEADS, HG_DK, HG_DV, HG_CHUNK, HG_SUB = 4, 128, 128, 64, 16
HG_W = HG_HEADS * HG_DV
HG_F_MIN = 1e-6
HG_STEP_SEQS = 4
HG_SAFE_SPAN = 60.0
DA_HEADS, DA_DQK = 4, 64
DA_DV = 2 * DA_DQK
DA_W = DA_HEADS * DA_DV
MASK_VALUE = -1e30
LOG2E = math.log2(math.e)
PROJ_TN = 768
RW_HEADS, RW_DH, RW_CHUNK, RW_SUB = 8, 64, 64, 16
RW_STEP_CHUNKS = 2
RW_STEP_SEQS = 2
RW_W = RW_HEADS * RW_DH
RW_IN_W = 1792
RW_GN_EPS = 64e-5
IN_W = 5376
HG_COL, DA_COL, RW_COL = 0, 2048, 3584
N_EXPERTS, N_GROUPS, EXPERTS_PER_GROUP, D_EXPERT = 16, 4, 4, 512
MOE_CHUNK = 16
MOE_TM = 512
ALPHA = (2.0 * DEPTH) ** 0.25
LN_EPS = 1e-5
RMS_EPS = 1e-6
LANES = 128

F32 = jnp.float32
BF16 = jnp.bfloat16
HIGHEST = lax.Precision.HIGHEST
VMEM_LIMIT = 48 * 1024 * 1024

_NT = (((1,), (1,)), ((), ()))
_TN = (((0,), (0,)), ((), ()))


def _mm(a, b):
    return jnp.dot(a.astype(BF16), b.astype(BF16), preferred_element_type=F32)


def _mm_nt(a, b):
    return lax.dot_general(a.astype(BF16), b.astype(BF16), _NT, preferred_element_type=F32)


def _mmh(a, b):
    return jnp.dot(a, b, precision=HIGHEST, preferred_element_type=F32)


def _seg_sum(x, seg):
    rows = x.shape[0]
    hi = x.astype(BF16)
    lo = (x - hi.astype(F32)).astype(BF16)
    halves = []
    for c0 in range(0, x.shape[1], seg.shape[0]):
        cols = slice(c0, c0 + seg.shape[0])
        both = jnp.dot(jnp.concatenate([hi[:, cols], lo[:, cols]], axis=0), seg, preferred_element_type=F32)
        halves.append(both[:rows] + both[rows:])
    return jnp.concatenate(halves, axis=1)


def _split_mm(x, w_twice, w_lo):
    hi = x.astype(BF16)
    lo = (x - hi.astype(F32)).astype(BF16)
    return (jnp.dot(jnp.concatenate([hi, lo], axis=1), w_twice, preferred_element_type=F32)
            + jnp.dot(hi, w_lo, preferred_element_type=F32))


def _chunk_cumsum(x):
    c = x.shape[0]
    hi = x.astype(BF16)
    rest = x - hi.astype(F32)
    mid = rest.astype(BF16)
    lo = (rest - mid.astype(F32)).astype(BF16)
    col = _iota((c, 4 * c), 1)
    tri = ((col & (c - 1)) <= _iota((c, 4 * c), 0)) & (col < 3 * c)
    return jnp.dot(jnp.where(tri, 1.0, 0.0).astype(BF16), jnp.concatenate([hi, mid, lo, lo], axis=0),
                   preferred_element_type=F32)


def _sigmoid(x):
    return 1.0 / (1.0 + jnp.exp(-x))


def _softplus(x):
    return jnp.maximum(x, 0.0) + jnp.log(1.0 + jnp.exp(-jnp.abs(x)))


def _iota(shape, dim):
    return lax.broadcasted_iota(jnp.int32, shape, dim)


def _params(*sem):
    return pltpu.CompilerParams(dimension_semantics=sem, vmem_limit_bytes=VMEM_LIMIT)


def _layer_norm(y, g, b):
    mu = jnp.mean(y, axis=-1, keepdims=True)
    d = y - mu
    var = jnp.mean(d * d, axis=-1, keepdims=True)
    return d * lax.rsqrt(var + LN_EPS) * g + b


def _ada_body(c_ref, w_ref, b_ref, o_ref):
    c = c_ref[...]
    o_ref[0] = _mmh(c * _sigmoid(c), w_ref[0]) + b_ref[0]


def _ada(c, w_ada, b_ada):
    depth, d, _ = w_ada.shape
    bsz = c.shape[0]
    return pl.pallas_call(
        _ada_body,
        grid=(depth, 6),
        in_specs=[pl.BlockSpec((bsz, d), lambda l, j: (0, 0)),
                  pl.BlockSpec((1, d, d), lambda l, j: (l, 0, j)),
                  pl.BlockSpec((1, 1, d), lambda l, j: (l, 0, j))],
        out_specs=pl.BlockSpec((1, bsz, d), lambda l, j: (l, 0, j)),
        out_shape=jax.ShapeDtypeStruct((depth, bsz, 6 * d), F32),
        compiler_params=_params("arbitrary", "arbitrary"),
        name="ada",
    )(c, w_ada, b_ada.reshape(depth, 1, 6 * d))


def _proj_body(x_ref, mod_ref, win_ref, wg_ref, bg_ref, y_ref, g_ref, vt_ref):
    u = (x_ref[...] * (1.0 + mod_ref[0, 1:2, :]) + mod_ref[0, 0:1, :]).astype(BF16)
    n_kb, kb = vt_ref.shape[1], vt_ref.shape[4]
    v_col = DA_COL + 2 * DA_W
    for c0 in range(0, IN_W, PROJ_TN):
        cols = slice(c0, c0 + PROJ_TN)
        res = jnp.dot(u, win_ref[:, cols], preferred_element_type=F32)
        y_ref[:, cols] = res.astype(BF16)
        if c0 == v_col:
            for h in range(DA_HEADS):
                slope2 = 2.0 ** (-8.0 * (h + 1) / DA_HEADS) * LOG2E
                key_w = jnp.exp2(slope2 * (_iota((1, res.shape[0]), 1) & (kb - 1)).astype(F32))
                v_t = (res[:, h * DA_DV:(h + 1) * DA_DV].T * key_w).astype(BF16)
                tail = jnp.where(_iota((8, res.shape[0]), 0) == 0, key_w, 0.0).astype(BF16)
                for kbi in range(n_kb):
                    keys = slice(kbi * kb, (kbi + 1) * kb)
                    vt_ref[0, kbi, h, 0:DA_DV, :] = v_t[:, keys]
                    vt_ref[0, kbi, h, DA_DV:DA_DV + 8, :] = tail[:, keys]
    for c0 in range(0, 3 * D_MODEL, PROJ_TN):
        cols = slice(c0, c0 + PROJ_TN)
        g = jnp.dot(u, wg_ref[:, cols], preferred_element_type=F32) + bg_ref[:, cols]
        g_ref[:, cols] = _sigmoid(g).astype(BF16)


def _proj(x, mod, w_in, w_gates, b_gates, seq, tm, kb):
    n, d = x.shape
    per_seq = seq // tm
    tile = lambda i: (i, 0)
    resident = dict(index_map=lambda i: (0, 0), pipeline_mode=pl.Buffered(1))
    vt_shape = (n // seq, seq // kb, DA_HEADS, DA_DV + 8, kb)
    return pl.pallas_call(
        _proj_body,
        grid=(n // tm,),
        in_specs=[pl.BlockSpec((tm, d), tile),
                  pl.BlockSpec((1, 6, d), lambda i: (i // per_seq, 0, 0)),
                  pl.BlockSpec(w_in.shape, **resident),
                  pl.BlockSpec(w_gates.shape, **resident),
                  pl.BlockSpec(b_gates.shape, **resident)],
        out_specs=[pl.BlockSpec((tm, IN_W), tile), pl.BlockSpec((tm, 3 * d), tile),
                   pl.BlockSpec((1, tm // kb) + vt_shape[2:], lambda i: (i // per_seq, i % per_seq, 0, 0, 0))],
        out_shape=[jax.ShapeDtypeStruct((n, IN_W), BF16), jax.ShapeDtypeStruct((n, 3 * d), BF16),
                   jax.ShapeDtypeStruct(vt_shape, BF16)],
        compiler_params=_params("arbitrary"),
        name="proj",
    )(x, mod, w_in, w_gates, b_gates)


def _hgrn2_body(y_ref, lb_ref, ng_ref, o_ref, st_ref):
    c, sub = HG_CHUNK, HG_SUB
    n_seq = y_ref.shape[0]

    @pl.when(pl.program_id(1) == 0)
    def _():
        st_ref[...] = jnp.zeros_like(st_ref)

    def part(k):
        return jnp.concatenate([y_ref[si, :, k * HG_W:(k + 1) * HG_W] for si in range(n_seq)], axis=0)

    q = part(0).astype(F32)
    z = part(1).astype(F32)
    lb = lb_ref[...]
    f = lb + (1.0 - lb) * _sigmoid(z)
    kin = (1.0 - lb) * _sigmoid(-z)
    logf = jnp.log(jnp.maximum(f, HG_F_MIN))
    b = jnp.concatenate([_chunk_cumsum(logf[si * c:(si + 1) * c]) for si in range(n_seq)], axis=0)

    ones = jnp.ones((HG_DK, LANES), BF16)
    row_s = _iota((sub, c), 0)
    col_s = _iota((sub, c), 1)
    blks = range(c // sub)
    hs = [slice(h * HG_DK, (h + 1) * HG_DK) for h in range(HG_HEADS)]
    units = [(si, h) for si in range(n_seq) for h in range(HG_HEADS)]

    betas = {(si, blk): (b[si * c + blk * sub - 1:si * c + blk * sub] if blk else jnp.zeros((1, HG_W), F32))
             for si in range(n_seq) for blk in blks}
    span = functools.reduce(jnp.maximum, [beta - b[si * c + (blk + 1) * sub - 1:si * c + (blk + 1) * sub]
                                          for (si, blk), beta in betas.items()])
    safe = jnp.max(span) <= HG_SAFE_SPAN

    def block_products(cap):
        prods = {}
        for si, h in units:
            keys = slice(si * c, (si + 1) * c)
            for blk in blks:
                rows = slice(si * c + blk * sub, si * c + (blk + 1) * sub)
                beta = betas[si, blk][:, hs[h]]
                q_t = q[rows, hs[h]] * jnp.exp(b[rows, hs[h]] - beta)
                k_h = kin[keys, hs[h]] * jnp.exp(jnp.minimum(beta - b[keys, hs[h]], cap))
                prods[si, h, blk] = _mm_nt(q_t, k_h)
        return prods

    def finish(prods, diag_sums):
        scores = []
        for si, h in units:
            a_rows = []
            for blk in blks:
                r0 = blk * sub
                a_blk = prods[si, h, blk]
                if diag_sums is not None:
                    own = jnp.zeros((sub, c), F32)
                    for s in range(sub):
                        own = jnp.where(col_s == r0 + s, diag_sums(si, h, blk, s), own)
                    a_blk = jnp.where(col_s < r0, a_blk, own)
                a_rows.append(jnp.where(col_s <= row_s + r0, a_blk, 0.0))
            scores.append(jnp.concatenate(a_rows, axis=0).astype(BF16))
        v = part(2)
        og = part(3).astype(F32)
        b_last = [b[(si + 1) * c - 1:(si + 1) * c] for si in range(n_seq)]
        q_dec = (q * jnp.exp(b)).astype(BF16)
        k_tail = (kin * jnp.exp(jnp.concatenate([jnp.broadcast_to(x, (c, HG_W)) for x in b_last], axis=0) - b)
                  ).astype(BF16)
        st = [st_ref[si, h] for si, h in units]
        rows = [slice(si * c, (si + 1) * c) for si, _ in units]
        intra = [jnp.dot(sc, v[r, hs[h]], preferred_element_type=F32) for sc, r, (_, h) in zip(scores, rows, units)]
        inter = [lax.dot_general(q_dec[r, hs[h]], s.astype(BF16), _NT, preferred_element_type=F32)
                 for s, r, (_, h) in zip(st, rows, units)]
        for s, r, (si, h) in zip(st, rows, units):
            st_ref[si, h] = s * jnp.exp(b_last[si][:, hs[h]]) + lax.dot_general(
                v[r, hs[h]], k_tail[r, hs[h]], _TN, preferred_element_type=F32)
        for x, y, r, (si, h) in zip(intra, inter, rows, units):
            o = x + y
            o = o * lax.rsqrt(jnp.mean(o * o, axis=-1, keepdims=True) + RMS_EPS) * ng_ref[...]
            o_ref[si, :, hs[h]] = (o * (og[r, hs[h]] * _sigmoid(og[r, hs[h]]))).astype(o_ref.dtype)

    @pl.when(safe)
    def _():
        finish(block_products(HG_SAFE_SPAN), None)

    @pl.when(jnp.logical_not(safe))
    def _():
        log_k = jnp.log(kin)
        b2, rel2, lk2 = b * LOG2E, (log_k - b) * LOG2E, log_k * LOG2E
        half = sub // 2
        diag = {}
        for si, h in units:
            for blk in blks:
                rows = slice(si * c + blk * sub, si * c + (blk + 1) * sub)
                b_i, q_i, rel_i, lk_i = b2[rows, hs[h]], q[rows, hs[h]], rel2[rows, hs[h]], lk2[rows, hs[h]]
                terms = [q_i[t0:] * jnp.exp2(jnp.minimum(b_i[t0:] + rel_i[s:s + 1], lk_i[s:s + 1]))
                         for s in range(sub) for t0 in [0 if s < half else half]]
                w = jnp.concatenate(terms, axis=0).astype(BF16)
                diag[si, h, blk] = jnp.dot(w, ones, preferred_element_type=F32)

        def diag_sums(si, h, blk, s):
            sums = diag[si, h, blk]
            if s < half:
                return sums[s * sub:(s + 1) * sub, :c]
            start = half * sub + (s - half) * half
            return jnp.concatenate([jnp.zeros((half, c), F32), sums[start:start + half, :c]], axis=0)

        finish(block_products(0.0), diag_sums)


def _hgrn2(y, lb, norm_g, bsz, seq):
    n = y.shape[0]
    width = 4 * HG_W
    n_seq = HG_STEP_SEQS if bsz % HG_STEP_SEQS == 0 else 1
    out = pl.pallas_call(
        _hgrn2_body,
        grid=(bsz // n_seq, seq // HG_CHUNK),
        in_specs=[pl.BlockSpec((n_seq, HG_CHUNK, width), lambda b, c: (b, c, HG_COL // width)),
                  pl.BlockSpec((1, HG_W), lambda b, c: (0, 0)),
                  pl.BlockSpec((1, HG_DV), lambda b, c: (0, 0))],
        out_specs=pl.BlockSpec((n_seq, HG_CHUNK, HG_W), lambda b, c: (b, c, 0)),
        out_shape=jax.ShapeDtypeStruct((bsz, seq, HG_W), BF16),
        scratch_shapes=[pltpu.VMEM((n_seq, HG_HEADS, HG_DV, HG_DK), F32)],
        compiler_params=_params("arbitrary", "arbitrary"),
        name="hgrn2",
    )(y.reshape(bsz, seq, -1), lb, norm_g)
    return out.reshape(n, HG_W)


def _attn_body(scal_ref, q_ref, k_ref, vt_ref, g_ref, o_ref, qq_ref, sa_ref, sb_ref, p_ref, m_ref, sc_ref,
               acc_ref, *, kb):
    h = pl.program_id(1)
    i = pl.program_id(2)
    lam = scal_ref[0]
    out_scale = scal_ref[1]
    slope = scal_ref[2 + h] * LOG2E
    qb = 2 * kb
    q0 = i * qb
    tiles_per_map = qb // LANES
    a_tiles = kb // LANES

    q = q_ref[...].astype(F32) * (DA_DQK ** -0.5 * LOG2E)
    lane = _iota(q.shape, 1)
    stacked = jnp.concatenate([jnp.where(lane < DA_DQK, q, 0.0), jnp.where(lane >= DA_DQK, q, 0.0)], axis=0)
    qq_ref[...] = stacked.T.astype(BF16)
    m_ref[...] = jnp.full(m_ref.shape, MASK_VALUE, F32)
    sc_ref[...] = jnp.ones(sc_ref.shape, F32)
    acc_ref[...] = jnp.zeros(acc_ref.shape, F32)
    p_ref[...] = jnp.zeros(p_ref.shape, BF16)
    key_off = _iota((kb, LANES), 0)

    def scores(j):
        return jnp.dot(k_ref[pl.ds(j * kb, kb), :], qq_ref[...], preferred_element_type=F32)

    def softmax(j, src_ref, diagonal_of):
        block_bias = slope * (j * kb - q0).astype(F32)
        for t in range(2 * tiles_per_map):
            cols = slice(t * LANES, (t + 1) * LANES)
            in_map = t % tiles_per_map
            half = "A" if in_map < a_tiles else "B"
            if diagonal_of == "B" and half == "A":
                p_ref[t] = jnp.zeros((kb, LANES), BF16)
                sc_ref[:, cols] = jnp.ones((1, LANES), F32)
                continue
            s = src_ref[t]
            if diagonal_of == half:
                q_off = _iota((kb, LANES), 1) + (in_map % a_tiles) * LANES
                s = jnp.where(key_off <= q_off, s, MASK_VALUE)
            m_old = m_ref[:, cols]
            m_new = jnp.maximum(m_old, jnp.max(s, axis=0, keepdims=True) + block_bias)
            p_ref[t] = jnp.exp2(s - (m_new - block_bias)).astype(BF16)
            sc_ref[:, cols] = jnp.exp2(m_old - m_new)
            m_ref[:, cols] = m_new

    n_tiles = 2 * tiles_per_map

    def park(dst_ref, s):
        for t in range(n_tiles):
            dst_ref[t] = s[:, t * LANES:(t + 1) * LANES]

    def probabilities():
        return jnp.concatenate([p_ref[t] for t in range(n_tiles)], axis=1)

    def iteration(j, src_ref, dst_ref, diagonal_of=None):
        sc_prev = sc_ref[...]
        pv = jnp.dot(vt_ref[0, jnp.maximum(j - 1, 0), 0], probabilities(), preferred_element_type=F32)
        if dst_ref is not None:
            park(dst_ref, scores(j + 1))
        for t in range(n_tiles):
            cols = slice(t * LANES, (t + 1) * LANES)
            acc_ref[t] = acc_ref[t] * sc_prev[:, cols] + pv[:, cols]
        softmax(j, src_ref, diagonal_of)

    def pair(j0):
        iteration(j0, sa_ref, sb_ref)
        iteration(j0 + 1, sb_ref, sa_ref)

    def body(quad, carry):
        pair(4 * quad)
        pair(4 * quad + 2)
        return carry

    park(sa_ref, scores(0))
    lax.fori_loop(0, i // 2, body, 0)

    @pl.when(i % 2 == 1)
    def _():
        pair(2 * i - 2)

    iteration(2 * i, sa_ref, sb_ref, "A")
    iteration(2 * i + 1, sb_ref, None, "B")
    acc = (jnp.concatenate([acc_ref[t] for t in range(n_tiles)], axis=1) * sc_ref[...]
           + jnp.dot(vt_ref[0, 2 * i + 1, 0], probabilities(), preferred_element_type=F32))
    o = acc[:DA_DV] / acc[DA_DV:DA_DV + 1]
    d = o[:, :qb] - lam * o[:, qb:]
    g = jnp.concatenate([g_ref[...]] * tiles_per_map, axis=1)
    d = d * lax.rsqrt(jnp.mean(d * d, axis=0, keepdims=True) + RMS_EPS) * g * out_scale
    o_ref[...] = d.T.astype(o_ref.dtype)


def _attn(y, v_t, scal, subln_g, bsz, seq, kb):
    n = y.shape[0]
    qb = 2 * kb
    nq, nk = seq // qb, seq // kb
    qc, kc = DA_COL // DA_DV, (DA_COL + DA_W) // DA_DV
    rows_v = v_t.shape[3]
    g_col = jnp.broadcast_to(subln_g.reshape(DA_DV, 1), (DA_DV, LANES))
    row = (1, 2 * qb)
    return pl.pallas_call(
        functools.partial(_attn_body, kb=kb),
        scratch_shapes=[pltpu.VMEM((DA_DV, 2 * qb), BF16), pltpu.VMEM((2 * qb // LANES, kb, LANES), F32),
                        pltpu.VMEM((2 * qb // LANES, kb, LANES), F32),
                        pltpu.VMEM((2 * qb // LANES, kb, LANES), BF16), pltpu.VMEM(row, F32),
                        pltpu.VMEM(row, F32), pltpu.VMEM((2 * qb // LANES, rows_v, LANES), F32)],
        grid=(bsz, DA_HEADS, nq),
        in_specs=[pl.BlockSpec(memory_space=pltpu.SMEM),
                  pl.BlockSpec((qb, DA_DV), lambda b, h, i: (b * nq + i, qc + h)),
                  pl.BlockSpec((seq, DA_DV), lambda b, h, i: (b, kc + h)),
                  pl.BlockSpec((1, nk, 1, rows_v, kb), lambda b, h, i: (b, 0, h, 0, 0)),
                  pl.BlockSpec((DA_DV, LANES), lambda b, h, i: (0, 0))],
        out_specs=pl.BlockSpec((qb, DA_DV), lambda b, h, i: (b * nq + i, h)),
        out_shape=jax.ShapeDtypeStruct((n, DA_W), BF16),
        compiler_params=_params("arbitrary", "arbitrary", "arbitrary"),
        name="diffattn",
    )(scal, y, y, v_t, g_col)


def _split_f32(x):
    hi = x.astype(BF16)
    hi_f = hi.astype(F32)
    return hi, hi_f, x - hi_f


def _dup_lhs(hi_f, lo_f, low_half):
    packed = jnp.where(low_half, hi_f, lo_f).astype(BF16)
    return jnp.concatenate([packed, packed], axis=1)


def _dup_rhs(hi, lo_f):
    lo = lo_f.astype(BF16)
    return jnp.concatenate([hi, hi, lo, lo], axis=0)


def _rwkv_body(y_ref, mu_ref, w0_ref, a0_ref, wa2_ref, walo_ref, gu2_ref, gulo_ref, kk_ref, ka_ref, rk_ref,
               gng_ref, gnb_ref, seg_ref, o_ref, st_ref, prev_ref, osc_ref):
    c, sub, dh = RW_CHUNK, RW_SUB, RW_DH

    @pl.when(pl.program_id(1) == 0)
    def _():
        st_ref[...] = jnp.zeros_like(st_ref)
        prev_ref[...] = jnp.zeros_like(prev_ref)

    n_seq, seq_rows = y_ref.shape[0], y_ref.shape[1]
    rows = n_seq * seq_rows
    shifted = []
    for si in range(n_seq):
        x = y_ref[si].astype(F32)
        x_prev = jnp.where(_iota(x.shape, 0) == 0, prev_ref[si:si + 1], pltpu.roll(x, 1, axis=0))
        prev_ref[si:si + 1] = x[seq_rows - 1:seq_rows]
        shifted.append(x + (x_prev - x) * mu_ref[...])
    xs = jnp.concatenate(shifted, axis=0)
    r = xs[:, 0:RW_W]
    k = xs[:, RW_W:2 * RW_W]
    v = xs[:, 2 * RW_W:3 * RW_W]
    wa = xs[:, 3 * RW_W:3 * RW_W + LANES]
    gd = xs[:, 3 * RW_W + LANES:RW_IN_W]

    lora = _split_mm(jnp.where(_iota(wa.shape, 1) < 64, jnp.tanh(wa), wa), wa2_ref[...], walo_ref[...])
    w_log = -_softplus(-(w0_ref[...] + lora[:, :RW_W])) - 0.5
    g = -jnp.exp(w_log)
    a = _sigmoid(a0_ref[...] + lora[:, RW_W:])
    gate = _split_mm(_sigmoid(gd), gu2_ref[...], gulo_ref[...])
    seg = seg_ref[...]
    kk = k * kk_ref[...]
    k2 = k * (1.0 + (a - 1.0) * ka_ref[...])
    sums = _seg_sum(jnp.concatenate([kk * kk, r * k2 * rk_ref[...]], axis=0), seg)
    kk = kk * lax.rsqrt(jnp.maximum(sums[:rows], 1e-12))
    bb = kk * a
    bonus = sums[rows:] * v

    chunks = range(rows // c)
    cr = [slice(ci * c, (ci + 1) * c) for ci in chunks]
    gc = jnp.concatenate([_chunk_cumsum(g[s]) for s in cr], axis=0)
    g_last = [gc[s][c - 1:c] for s in cr]
    e_inv = jnp.exp(-gc)
    e_tail = jnp.exp(jnp.concatenate([jnp.broadcast_to(gl, (c, RW_W)) for gl in g_last], axis=0) - gc)
    gam = [jnp.exp(gl) for gl in g_last]
    a_t = (-kk * jnp.exp(gc - g)).astype(BF16)
    r_t = (r * jnp.exp(gc)).astype(BF16)
    b_h = (bb * e_inv).astype(BF16)
    k_h = (k2 * e_inv).astype(BF16)
    k_bar = (k2 * e_tail).astype(BF16)
    b_bar = (bb * e_tail).astype(BF16)
    v_bf = v.astype(BF16)

    row2 = _iota((c, 2 * c), 0)
    lane2 = _iota((c, 2 * c), 1)
    col2 = lane2 & (c - 1)
    low_half = lane2 < c
    strict = row2 > col2
    incl = row2 >= col2
    same_blk = (row2 // sub) == (col2 // sub)
    eye = (row2 == col2).astype(F32)

    def dot(p, q):
        return jnp.dot(p, q, preferred_element_type=F32)

    heads = range(RW_HEADS)
    sls = [slice(h * dh, (h + 1) * dh) for h in heads]
    items = [(ci, h) for ci in chunks for h in heads]
    ar_h = [jnp.concatenate([a_t[cr[ci], sls[h]], r_t[cr[ci], sls[h]]], axis=0) for ci, h in items]
    quad = [lax.dot_general(x, jnp.concatenate([b_h[cr[ci], sls[h]], k_h[cr[ci], sls[h]]], axis=0), _NT,
                            preferred_element_type=F32)
            for x, (ci, h) in zip(ar_h, items)]
    top = [jnp.where(strict, q[:c], 0.0) for q in quad]
    a_ak = [t[:, c:].astype(BF16) for t in top]
    a_r = [jnp.where(incl, q[c:], 0.0).astype(BF16) for q in quad]
    a_ab = [jnp.where(low_half, t, pltpu.roll(t, c, axis=1)) for t in top]
    a_d = [jnp.where(same_blk, x, 0.0) for x in a_ab]
    a_o = [(x - y).astype(BF16) for x, y in zip(a_ab, a_d)]

    s1 = [_split_f32(x) for x in a_d]
    p2 = [dot(_dup_lhs(hf, lf, low_half), _dup_rhs(hi, lf)) for hi, hf, lf in s1]
    s2 = [_split_f32(x) for x in p2]
    rhs2 = [_dup_rhs(hi, lf) for hi, _, lf in s2]
    p4 = [dot(_dup_lhs(hf, lf, low_half), rhs) for (_, hf, lf), rhs in zip(s2, rhs2)]
    s4 = [_split_f32(x) for x in p4]
    rhs4 = [_dup_rhs(hi, lf) for hi, _, lf in s4]
    p8 = [dot(_dup_lhs(hf, lf, low_half), rhs) for (_, hf, lf), rhs in zip(s4, rhs4)]
    rhs8 = [_dup_rhs(hi, lf) for hi, _, lf in (_split_f32(x) for x in p8)]
    t_d = [eye + x for x in a_d]
    for rhs_all in (rhs2, rhs4, rhs8):
        st = [_split_f32(x) for x in t_d]
        t_d = [x + dot(_dup_lhs(hf, lf, low_half), rhs) for x, (_, hf, lf), rhs in zip(t_d, st, rhs_all)]
    t_d = [x.astype(BF16) for x in t_d]

    nn = [dot(t[:, :c], x) for t, x in zip(t_d, a_o)]
    nn_bf = [x.astype(BF16) for x in nn]
    n2 = [dot(x[:, :c], x) for x in nn_bf]
    n3 = [dot(x[:, :c], y.astype(BF16)) for x, y in zip(nn_bf, n2)]
    t_m = [dot((eye + x + y + z).astype(BF16)[:, :c], t).astype(BF16)[:, :c]
           for x, y, z, t in zip(nn, n2, n3, t_d)]

    v_h = [v_bf[cr[ci], sls[h]] for ci, h in items]
    akv = [dot(x, y).astype(BF16) for x, y in zip(a_ak, v_h)]
    at_m = [dot(t, x[:c]).astype(BF16) for t, x in zip(t_m, ar_h)]
    v_p = [dot(t, x) for t, x in zip(t_m, akv)]

    per_seq = seq_rows // c
    chains = [(si, h) for si in range(n_seq) for h in heads]
    state = [st_ref[si, h] for si, h in chains]
    for t in range(per_seq):
        it = [(si * per_seq + t) * RW_HEADS + h for si, h in chains]
        ck = [cr[si * per_seq + t] for si, _ in chains]
        proj = [lax.dot_general(jnp.concatenate([at_m[j], ar_h[j][c:]], axis=0), s.astype(BF16), _NT,
                                preferred_element_type=F32) for j, s in zip(it, state)]
        u = [(p[:c] + v_p[j]).astype(BF16) for p, j in zip(proj, it)]
        for p, uu, j, rws, (_, h) in zip(proj, u, it, ck, chains):
            osc_ref[rws, sls[h]] = p[c:] + dot(a_r[j], jnp.concatenate([uu, v_h[j]], axis=0))
        state = [s * gam[j // RW_HEADS][:, sls[h]] + lax.dot_general(
            jnp.concatenate([v_h[j], uu], axis=0),
            jnp.concatenate([k_bar[rws, sls[h]], b_bar[rws, sls[h]]], axis=0), _TN, preferred_element_type=F32)
            for s, uu, j, rws, (_, h) in zip(state, u, it, ck, chains)]
    for s, (si, h) in zip(state, chains):
        st_ref[si, h] = s

    o = osc_ref[...]
    mean = _seg_sum(o, seg) * (1.0 / dh)
    d = o - mean
    var = _seg_sum(d * d, seg) * (1.0 / dh)
    o = d * lax.rsqrt(var + RW_GN_EPS) * gng_ref[...] + gnb_ref[...]
    o = ((o + bonus) * gate).astype(o_ref.dtype)
    for si in range(n_seq):
        o_ref[si] = o[si * seq_rows:(si + 1) * seq_rows]


def _rwkv(y, p, bsz, seq):
    n = y.shape[0]
    half = RW_W // 2
    seg = (_iota((half, half), 0) // RW_DH == _iota((half, half), 1) // RW_DH).astype(BF16)

    def two_terms(w):
        hi = w.astype(BF16)
        return jnp.concatenate([hi, hi], axis=0), (w - hi.astype(F32)).astype(BF16)

    zeros = jnp.zeros_like(p["w_up"])
    wa2, wa_lo = two_terms(jnp.concatenate([jnp.concatenate([p["w_up"], zeros], axis=1),
                                            jnp.concatenate([zeros, p["a_up"]], axis=1)], axis=0))
    gu2, gu_lo = two_terms(p["g_up"])
    rows = [p["mu"], p["w0"], p["a0"], wa2, wa_lo, gu2, gu_lo, p["k_k"], p["k_a"], p["r_k"],
            p["gn_g"], p["gn_b"], seg]
    full = lambda b, c: (0, 0)
    step = RW_STEP_CHUNKS * RW_CHUNK
    n_seq = RW_STEP_SEQS if bsz % RW_STEP_SEQS == 0 else 1
    out = pl.pallas_call(
        _rwkv_body,
        grid=(bsz // n_seq, seq // step),
        in_specs=[pl.BlockSpec((n_seq, step, RW_IN_W), lambda b, c: (b, c, RW_COL // RW_IN_W))]
        + [pl.BlockSpec(a.shape, full) for a in rows],
        out_specs=pl.BlockSpec((n_seq, step, RW_W), lambda b, c: (b, c, 0)),
        out_shape=jax.ShapeDtypeStruct((bsz, seq, RW_W), BF16),
        scratch_shapes=[pltpu.VMEM((n_seq, RW_HEADS, RW_DH, RW_DH), F32),
                        pltpu.VMEM((n_seq, RW_IN_W), F32),
                        pltpu.VMEM((n_seq * step, RW_W), F32)],
        compiler_params=_params("arbitrary", "arbitrary"),
        name="rwkv7",
    )(y.reshape(bsz, seq, -1), *rows)
    return out.reshape(n, RW_W)


def _first_argmax(vals, row):
    top = jnp.max(vals, axis=0, keepdims=True)
    idx = jnp.min(jnp.where(vals == top, row, N_EXPERTS), axis=0, keepdims=True)
    return top, idx


def _merge_body(ohg_ref, oda_ref, orw_ref, gt_ref, x_ref, mod_ref, wb_ref, wo_ref, lng_ref, lnb_ref,
                wrt_ref, rb_ref, tri_ref, tri16_ref, x1_ref, u2_ref, route_ref, cnt_ref):
    d = D_MODEL
    sub_rows = tri_ref.shape[0]

    def route(rows):
        merged = (gt_ref[rows, 0:d].astype(F32)
                  * jnp.dot(ohg_ref[rows, :], wb_ref[0:HG_W, :], preferred_element_type=F32)
                  + gt_ref[rows, d:2 * d].astype(F32)
                  * jnp.dot(oda_ref[rows, :], wb_ref[HG_W:HG_W + DA_W, :], preferred_element_type=F32)
                  + gt_ref[rows, 2 * d:3 * d].astype(F32)
                  * jnp.dot(orw_ref[rows, :], wb_ref[HG_W + DA_W:, :], preferred_element_type=F32))
        mix = _mm(merged, wo_ref[...])
        x1 = _layer_norm(ALPHA * x_ref[rows, :] + (1.0 + mod_ref[0, 2:3, :]) * mix, lng_ref[...], lnb_ref[...])
        x1_ref[rows, :] = x1
        u2 = x1 * (1.0 + mod_ref[0, 4:5, :]) + mod_ref[0, 3:4, :]
        u2_hi = u2.astype(BF16)
        u2_ref[rows, :] = u2_hi

        u2_lo = (u2 - u2_hi.astype(F32)).astype(BF16)
        two = lax.dot_general(wrt_ref[...], u2_hi, _NT, preferred_element_type=F32)
        logits = (two[:N_EXPERTS] + two[N_EXPERTS:]
                  + lax.dot_general(wrt_ref[0:N_EXPERTS, :], u2_lo, _NT, preferred_element_type=F32))
        ex = jnp.exp(logits - jnp.max(logits, axis=0, keepdims=True))
        scores = ex / jnp.sum(ex, axis=0, keepdims=True)
        sel = scores + rb_ref[...]
        row = _iota(sel.shape, 0)
        best = None
        for grp in range(N_GROUPS):
            a, b, c2, d2 = (sel[grp * EXPERTS_PER_GROUP + i:grp * EXPERTS_PER_GROUP + i + 1] for i in range(4))
            hi1, lo1, hi2, lo2 = jnp.maximum(a, b), jnp.minimum(a, b), jnp.maximum(c2, d2), jnp.minimum(c2, d2)
            top2 = jnp.maximum(hi1, hi2) + jnp.maximum(jnp.minimum(hi1, hi2), jnp.maximum(lo1, lo2))
            if best is None:
                best, best_grp = top2, jnp.zeros_like(top2, dtype=jnp.int32)
            else:
                better = top2 > best
                best = jnp.where(better, top2, best)
                best_grp = jnp.where(better, grp, best_grp)
        masked = jnp.where(row // EXPERTS_PER_GROUP == best_grp, sel, MASK_VALUE)
        _, idx1 = _first_argmax(masked, row)
        _, idx2 = _first_argmax(jnp.where(row == idx1, -jnp.inf, masked), row)
        pick1 = row == idx1
        pick2 = row == idx2
        w1 = jnp.sum(jnp.where(pick1, scores, 0.0), axis=0, keepdims=True)
        w2 = jnp.sum(jnp.where(pick2, scores, 0.0), axis=0, keepdims=True)
        onehot = jnp.where(pick1 | pick2, 1.0, 0.0)
        earlier = jnp.dot(onehot.astype(BF16), tri_ref[...], preferred_element_type=F32)
        return pick1, pick2, w1 / (w1 + w2), w2 / (w1 + w2), earlier, jnp.sum(onehot, axis=1, keepdims=True)

    parts = [route(slice(r0, r0 + sub_rows)) for r0 in range(0, x_ref.shape[0], sub_rows)]

    cnt = functools.reduce(jnp.add, [p[5] for p in parts])
    chunks = jnp.floor((cnt + (MOE_CHUNK - 1)) * (1.0 / MOE_CHUNK))
    seg_start = MOE_CHUNK * jnp.dot(tri16_ref[...], jnp.broadcast_to(chunks, (N_EXPERTS, LANES)).astype(BF16),
                                    preferred_element_type=F32)[:, 0:1]
    before = jnp.zeros_like(cnt)
    for k, (pick1, pick2, wn1, wn2, earlier, cnt_k) in enumerate(parts):
        pos = seg_start + before + earlier
        pos1 = jnp.sum(jnp.where(pick1, pos, 0.0), axis=0, keepdims=True)
        pos2 = jnp.sum(jnp.where(pick2, pos, 0.0), axis=0, keepdims=True)
        route_ref[:, k * sub_rows:(k + 1) * sub_rows] = jnp.concatenate(
            [pos1, pos2, wn1, wn2, jnp.zeros((4, sub_rows), F32)], axis=0)
        before = before + cnt_k
    cnt_ref[0] = jnp.broadcast_to(cnt, (N_EXPERTS, LANES))


def _merge(o_hg, o_da, o_rw, gates, x, mod, w_branch, w_out, ln_g, ln_b, w_router_t, router_bias, seq, tm):
    n, d = x.shape
    per_seq = seq // tm
    tile = lambda i: (i, 0)
    full = lambda i: (0, 0)
    sub = tm
    before =(_iota((sub, sub), 0) < _iota((sub, sub), 1)).astype(BF16)
    before16 = (_iota((N_EXPERTS, N_EXPERTS), 1) < _iota((N_EXPERTS, N_EXPERTS), 0)).astype(BF16)
    w_hi = w_router_t.astype(BF16)
    w_two = jnp.concatenate([w_hi, (w_router_t - w_hi.astype(F32)).astype(BF16)], axis=0)
    return pl.pallas_call(
        _merge_body,
        grid=(n // tm,),
        in_specs=[pl.BlockSpec((tm, HG_W), tile), pl.BlockSpec((tm, DA_W), tile), pl.BlockSpec((tm, RW_W), tile),
                  pl.BlockSpec((tm, 3 * d), tile), pl.BlockSpec((tm, d), tile),
                  pl.BlockSpec((1, 6, d), lambda i: (i // per_seq, 0, 0)),
                  pl.BlockSpec(w_branch.shape, full), pl.BlockSpec(w_out.shape, full),
                  pl.BlockSpec((1, d), full), pl.BlockSpec((1, d), full),
                  pl.BlockSpec((2 * N_EXPERTS, d), full), pl.BlockSpec((N_EXPERTS, 1), full),
                  pl.BlockSpec((sub, sub), full), pl.BlockSpec((N_EXPERTS, N_EXPERTS), full)],
        out_specs=[pl.BlockSpec((tm, d), tile), pl.BlockSpec((tm, d), tile), pl.BlockSpec((8, tm), lambda i: (0, i)),
                   pl.BlockSpec((1, N_EXPERTS, LANES), lambda i: (i, 0, 0))],
        out_shape=[jax.ShapeDtypeStruct((n, d), F32), jax.ShapeDtypeStruct((n, d), BF16),
                   jax.ShapeDtypeStruct((8, n), F32), jax.ShapeDtypeStruct((n // tm, N_EXPERTS, LANES), F32)],
        compiler_params=_params("arbitrary"),
        name="merge",
    )(o_hg, o_da, o_rw, gates, x, mod, w_branch, w_out, ln_g, ln_b, w_two, router_bias, before, before16)


def _local_rows(tm):
    return -(-(2 * tm + N_EXPERTS * (MOE_CHUNK - 1)) // LANES) * LANES


def _token_columns(route):
    return jnp.concatenate([route, jnp.zeros((LANES - route.shape[0], route.shape[1]), F32)], axis=0).T


def _segment_copies(i, nch_ref, loc_ref, glob_ref, local_buf, global_buf, sem, to_global):
    def run(action):
        for e in range(N_EXPERTS):
            seg = i * N_EXPERTS + e
            loc0, glob0 = loc_ref[seg], glob_ref[seg]

            def one(c, carry):
                loc = local_buf.at[pl.ds(pl.multiple_of(loc0 + c * MOE_CHUNK, MOE_CHUNK), MOE_CHUNK), :]
                glob = global_buf.at[pl.ds(pl.multiple_of(glob0 + c * MOE_CHUNK, MOE_CHUNK), MOE_CHUNK), :]
                copy = pltpu.make_async_copy(loc, glob, sem) if to_global else pltpu.make_async_copy(glob, loc, sem)
                getattr(copy, action)()
                return carry

            lax.fori_loop(0, nch_ref[seg], one, 0)
    return run


def _dispatch_body(nch_ref, loc_ref, glob_ref, u_ref, route_ref, xs_in_ref, xs_ref, stage2_ref, sems):
    del xs_in_ref
    i = pl.program_id(0)
    last = pl.num_programs(0) - 1
    tm, d = u_ref.shape
    slot = i % 2
    stage_ref = stage2_ref.at[slot]

    def tile_copies(tile):
        return _segment_copies(tile, nch_ref, loc_ref, glob_ref, stage2_ref.at[tile % 2], xs_ref,
                               sems.at[tile % 2], to_global=True)

    @pl.when(i >= 2)
    def _():
        tile_copies(i - 2)("wait")

    route = route_ref[...]
    local_row = _iota((stage_ref.shape[0], tm), 0)
    take1 = local_row == route[0:1].astype(jnp.int32)
    take2 = local_row == route[1:2].astype(jnp.int32)
    perm = jnp.where(take1 | take2, 1.0, 0.0).astype(BF16)
    stage_ref[:, 0:d] = jnp.dot(perm, u_ref[...], preferred_element_type=F32).astype(BF16)

    cols = _token_columns(route)
    lane = _iota((tm, LANES), 1)

    def weight_cols(w):
        hi = w.astype(BF16).astype(F32)
        return jnp.where(lane == 0, hi, jnp.where(lane == 1, w - hi, 0.0)).astype(BF16)

    stage_ref[:, d:d + LANES] = (
        jnp.dot(jnp.where(take1, 1.0, 0.0).astype(BF16), weight_cols(cols[:, 2:3]), preferred_element_type=F32)
        + jnp.dot(jnp.where(take2, 1.0, 0.0).astype(BF16), weight_cols(cols[:, 3:4]), preferred_element_type=F32)
    ).astype(BF16)

    tile_copies(i)("start")

    @pl.when(i == last)
    def _():

        @pl.when(i >= 1)
        def _():
            tile_copies(i - 1)("wait")

        tile_copies(i)("wait")


def _dispatch(u2, route, nch, loc, glob, rows, tm):
    n, d = u2.shape
    width = d + LANES
    return pl.pallas_call(
        _dispatch_body,
        grid_spec=pltpu.PrefetchScalarGridSpec(
            num_scalar_prefetch=3,
            grid=(n // tm,),
            in_specs=[pl.BlockSpec((tm, d), lambda i, *_: (i, 0)),
                      pl.BlockSpec((8, tm), lambda i, *_: (0, i)),
                      pl.BlockSpec(memory_space=pl.ANY)],
            out_specs=pl.BlockSpec(memory_space=pl.ANY),
            scratch_shapes=[pltpu.VMEM((2, _local_rows(tm), width), BF16), pltpu.SemaphoreType.DMA((2,))],
        ),
        out_shape=jax.ShapeDtypeStruct((rows, width), BF16),
        input_output_aliases={5: 0},
        compiler_params=_params("arbitrary"),
        name="moe_dispatch",
    )(nch, loc, glob, u2, route, jnp.zeros((rows, width), BF16))


def _experts_body(te_ref, x_ref, wg_ref, wu_ref, wd_ref, y_ref, wgu_bf, wd_bf):
    g = pl.program_id(0)
    expert = te_ref[g]
    used = expert < N_EXPERTS

    @pl.when(jnp.logical_and(used, jnp.logical_or(g == 0, expert != te_ref[jnp.maximum(g - 1, 0)])))
    def _():
        wgu_bf[:, :D_EXPERT] = wg_ref[0, 0].astype(BF16)
        wgu_bf[:, D_EXPERT:] = wu_ref[0, 0].astype(BF16)
        wd_bf[...] = wd_ref[0, 0].astype(BF16)

    @pl.when(used)
    def _():
        d = wgu_bf.shape[0]
        weight = x_ref[:, d:d + 1].astype(F32) + x_ref[:, d + 1:d + 2].astype(F32)
        hidden = jnp.dot(x_ref[:, 0:d], wgu_bf[...], preferred_element_type=F32)
        hg = hidden[:, :D_EXPERT]
        act = hg * _sigmoid(hg) * hidden[:, D_EXPERT:] * weight
        y_ref[...] = _mm(act, wd_bf[...]).astype(BF16)

    @pl.when(jnp.logical_not(used))
    def _():
        y_ref[...] = jnp.zeros_like(y_ref)


def _experts(xs, tile_expert, w_gate, w_up, w_down, layer):
    rows, width = xs.shape
    d = w_down.shape[3]
    expert = lambda g, te: (layer, jnp.minimum(te[g], N_EXPERTS - 1), 0, 0)
    return pl.pallas_call(
        _experts_body,
        grid_spec=pltpu.PrefetchScalarGridSpec(
            num_scalar_prefetch=1,
            grid=(rows // MOE_TM,),
            in_specs=[pl.BlockSpec((MOE_TM, width), lambda g, te: (g, 0)),
                      pl.BlockSpec((1, 1, d, D_EXPERT), expert),
                      pl.BlockSpec((1, 1, d, D_EXPERT), expert),
                      pl.BlockSpec((1, 1, D_EXPERT, d), expert)],
            out_specs=pl.BlockSpec((MOE_TM, d), lambda g, te: (g, 0)),
            scratch_shapes=[pltpu.VMEM((d, 2 * D_EXPERT), BF16), pltpu.VMEM((D_EXPERT, d), BF16)],
        ),
        out_shape=jax.ShapeDtypeStruct((rows, d), BF16),
        compiler_params=_params("arbitrary"),
        name="moe_experts",
    )(tile_expert, xs, w_gate, w_up, w_down)


def _combine_body(nch_ref, loc_ref, glob_ref, route_ref, x1_ref, mod_ref, lng_ref, lnb_ref, ys_ref, o_ref,
                  back2_ref, sems):
    i = pl.program_id(0)

    def tile_copies(tile):
        return _segment_copies(tile, nch_ref, loc_ref, glob_ref, back2_ref.at[tile % 2], ys_ref,
                               sems.at[tile % 2], to_global=False)

    @pl.when(i == 0)
    def _():
        back2_ref[...] = jnp.zeros_like(back2_ref)
        tile_copies(i)("start")

    @pl.when(i + 1 < pl.num_programs(0))
    def _():
        tile_copies(i + 1)("start")

    cols = _token_columns(route_ref[...]).astype(jnp.int32)
    local_row = _iota((x1_ref.shape[0], back2_ref.shape[1]), 1)
    unperm = jnp.where((local_row == cols[:, 0:1]) | (local_row == cols[:, 1:2]), 1.0, 0.0).astype(BF16)
    tile_copies(i)("wait")
    ffn = jnp.dot(unperm, back2_ref[i % 2], preferred_element_type=F32)
    y = ALPHA * x1_ref[...] + (1.0 + mod_ref[0, 5:6, :]) * ffn
    o_ref[...] = _layer_norm(y, lng_ref[...], lnb_ref[...])


def _combine(ys, route, nch, loc, glob, x1, mod, ln_g, ln_b, seq, tm):
    n, d = x1.shape
    per_seq = seq // tm
    full = lambda i, *_: (0, 0)
    return pl.pallas_call(
        _combine_body,
        grid_spec=pltpu.PrefetchScalarGridSpec(
            num_scalar_prefetch=3,
            grid=(n // tm,),
            in_specs=[pl.BlockSpec((8, tm), lambda i, *_: (0, i)),
                      pl.BlockSpec((tm, d), lambda i, *_: (i, 0)),
                      pl.BlockSpec((1, 6, d), lambda i, *_: (i // per_seq, 0, 0)),
                      pl.BlockSpec((1, d), full), pl.BlockSpec((1, d), full),
                      pl.BlockSpec(memory_space=pl.ANY)],
            out_specs=pl.BlockSpec((tm, d), lambda i, *_: (i, 0)),
            scratch_shapes=[pltpu.VMEM((2, _local_rows(tm), d), BF16), pltpu.SemaphoreType.DMA((2,))],
        ),
        out_shape=jax.ShapeDtypeStruct((n, d), F32),
        compiler_params=_params("arbitrary"),
        name="moe_combine",
    )(nch, loc, glob, route, x1, mod, ln_g, ln_b, ys)


def _moe(u2, route, counts, expert_weights, layer, x1, mod, ln_g, ln_b, seq, tm):
    n = u2.shape[0]
    n_tiles = n // tm
    seg_rows = (counts + MOE_CHUNK - 1) // MOE_CHUNK * MOE_CHUNK
    loc = jnp.cumsum(seg_rows, axis=1) - seg_rows
    region = (jnp.sum(seg_rows, axis=0) + MOE_TM - 1) // MOE_TM * MOE_TM
    region_end = jnp.cumsum(region)
    glob = (region_end - region)[None, :] + jnp.cumsum(seg_rows, axis=0) - seg_rows
    rows = -(-(2 * n + n_tiles * N_EXPERTS * (MOE_CHUNK - 1) + N_EXPERTS * (MOE_TM - 1)) // MOE_TM) * MOE_TM
    tile_expert = jnp.sum(jnp.arange(rows // MOE_TM, dtype=jnp.int32)[:, None] * MOE_TM >= region_end[None, :],
                          axis=1).astype(jnp.int32)
    flat = lambda a: a.reshape(-1).astype(jnp.int32)
    nch, loc, glob = flat(seg_rows // MOE_CHUNK), flat(loc), flat(glob)
    xs = _dispatch(u2, route, nch, loc, glob, rows, tm)
    ys = _experts(xs, tile_expert, *expert_weights, layer)
    return _combine(ys, route, nch, loc, glob, x1, mod, ln_g, ln_b, seq, tm)


def _tiles(seq):
    return min(512, seq), min(256, seq // 2)


def kernel(x, c, w_ada, b_ada, w_in, hg_lb_logits, hg_norm_g, da_lambda, da_subln_g, rw_mu, rw_w0, rw_w_up,
           rw_a0, rw_a_up, rw_g_up, rw_k_k, rw_k_a, rw_r_k, rw_gn_g, rw_gn_b, w_merge, b_merge, w_branch, w_out,
           ln_g, ln_b, w_router, router_bias, w_exp_gate, w_exp_up, w_exp_down):
    bsz, seq, d = x.shape
    depth = w_in.shape[0]
    n = bsz * seq
    tm, blk = _tiles(seq)

    sm = jax.nn.softmax(hg_lb_logits.astype(F32), axis=0)
    hg_lb = jnp.cumsum(sm, axis=0) - sm[0:1]
    slopes = jnp.asarray([2.0 ** (-8.0 * (h + 1) / DA_HEADS) for h in range(DA_HEADS)], F32)

    mod_all = _ada(c, w_ada, b_ada).reshape(depth, bsz, 6, d)
    w_router_t = w_router.T
    router_bias = router_bias.reshape(N_EXPERTS, 1)

    xf = x.reshape(n, d)
    for l in range(depth):
        mod = mod_all[l]
        lq1, lk1, lq2, lk2 = da_lambda[l].astype(F32)
        lam_init = 0.8 - 0.6 * math.exp(-0.3 * l)
        lam = jnp.exp(jnp.sum(lq1 * lk1)) - jnp.exp(jnp.sum(lq2 * lk2)) + lam_init
        scal = jnp.concatenate([jnp.stack([lam, jnp.asarray(1.0 - lam_init, F32)]), slopes])

        w_gates = jnp.concatenate([w_merge[l, br] for br in range(3)], axis=1).astype(BF16)
        y, gates, v_t = _proj(xf, mod, w_in[l].astype(BF16), w_gates, b_merge[l].reshape(1, 3 * d), seq, tm, blk)

        o_hg = _hgrn2(y, hg_lb[l].reshape(1, HG_W), hg_norm_g[l].reshape(1, HG_DV), bsz, seq)
        o_da = _attn(y, v_t, scal, da_subln_g[l].reshape(1, DA_DV), bsz, seq, blk)
        rw = dict(mu=rw_mu[l].reshape(1, -1), w0=rw_w0[l].reshape(1, -1), w_up=rw_w_up[l],
                  a0=rw_a0[l].reshape(1, -1), a_up=rw_a_up[l], g_up=rw_g_up[l],
                  k_k=rw_k_k[l].reshape(1, -1), k_a=rw_k_a[l].reshape(1, -1), r_k=rw_r_k[l].reshape(1, -1),
                  gn_g=rw_gn_g[l].reshape(1, -1), gn_b=rw_gn_b[l].reshape(1, -1))
        o_rw = _rwkv(y, rw, bsz, seq)

        x1, u2, route, counts = _merge(o_hg, o_da, o_rw, gates, xf, mod, w_branch[l].astype(BF16),
                                       w_out[l].astype(BF16), ln_g[l, 0].reshape(1, d), ln_b[l, 0].reshape(1, d),
                                       w_router_t, router_bias, seq, tm)
        xf = _moe(u2, route, counts[:, :, 0].astype(jnp.int32), (w_exp_gate, w_exp_up, w_exp_down), l, x1, mod,
                  ln_g[l, 1].reshape(1, d), ln_b[l, 1].reshape(1, d), seq, tm)
    return xf.reshape(bsz, seq, d)
```
